```python
import jax, jax.numpy as jnp
from jax import lax
import numpy as np

D_MODEL = 1024
BATCH = 4
SEQ = 4096
DEPTH = 1
DEC_BATCH = 32
DEC_SEQ = 1
PAST_LEN = 8192
PAGE_SIZE = 128

D_RNN = D_MODEL
N_RNN_BLOCKS = 16
RNN_BLOCK = D_RNN // N_RNN_BLOCKS
CONV_WIDTH = 4
LRU_C = 8.0
HEAD_DIM = 64
H_G = 8
ATTN_GROUPS = ((128, 1), (512, 4), (2048, 16))
N_GROUPS = 3
D_ATTN = H_G * HEAD_DIM
Q_BLOCK = 128
ALIBI_MAX = 8.0
EPS = 1e-6
NEG_INF = -1e30
SPLIT_SIZES = (D_RNN, D_RNN, N_GROUPS * D_ATTN, N_GROUPS * D_ATTN, N_GROUPS * D_ATTN, D_ATTN, D_MODEL, D_MODEL)
D_IN = 2 * D_RNN + 3 * N_GROUPS * D_ATTN + D_ATTN + 2 * D_MODEL

kernel_name = "hawk_dilated_alibi_hybrid_step"


def split_points():
    pts, acc = [], 0
    for s in SPLIT_SIZES[:-1]:
        acc += s
        pts.append(acc)
    return pts


def rms_norm(x, gain):
    x32 = x.astype(jnp.float32)
    y = x32 * lax.rsqrt(jnp.mean(x32 * x32, axis=-1, keepdims=True) + EPS)
    return (y * gain.astype(jnp.float32)).astype(x.dtype)


def alibi_slopes():
    n = N_GROUPS * H_G
    s = 2.0 ** (-ALIBI_MAX * jnp.arange(1, n + 1, dtype=jnp.float32) / n)
    return s.reshape(N_GROUPS, H_G)


def causal_conv(u, buf, conv_w, conv_b):
    T = u.shape[1]
    ext = jnp.concatenate([buf.astype(u.dtype), u], axis=1)
    out = conv_b
    for tap in range(CONV_WIDTH):
        out = out + ext[:, tap:tap + T] * conv_w[tap]
    return out, ext[:, -(CONV_WIDTH - 1):]


def rg_lru(xc, h0, w_a, b_a, w_x, b_x, lam):
    B, T, C = xc.shape
    xb = xc.reshape(B, T, N_RNN_BLOCKS, RNN_BLOCK)
    r = jax.nn.sigmoid((jnp.einsum('btnc,ncd->btnd', xb, w_a).reshape(B, T, C) + b_a).astype(jnp.float32))
    i = jax.nn.sigmoid((jnp.einsum('btnc,ncd->btnd', xb, w_x).reshape(B, T, C) + b_x).astype(jnp.float32))
    log_a = -LRU_C * r * jax.nn.softplus(-lam.astype(jnp.float32))
    a = jnp.exp(log_a)
    b = jnp.sqrt(-jnp.expm1(2.0 * log_a)) * i * xc.astype(jnp.float32)
    b = b.at[:, 0].add(a[:, 0] * h0.astype(jnp.float32))

    def combine(left, right):
        a_l, b_l = left
        a_r, b_r = right
        return a_l * a_r, a_r * b_l + b_r

    _, h = lax.associative_scan(combine, (a, b), axis=1)
    return h.astype(xc.dtype), h[:, -1].astype(h0.dtype)


def dilated_attention_prompt(q, k, v, window, dilation, slopes):
    B, S, H, E = q.shape
    L = S // dilation
    nb = -(-L // Q_BLOCK)
    Lp = nb * Q_BLOCK
    max_steps = window // dilation

    def to_blocks(t):
        t = t.reshape(B, L, dilation, H, E).transpose(0, 2, 1, 3, 4)
        t = jnp.pad(t, ((0, 0), (0, 0), (0, Lp - L), (0, 0), (0, 0)))
        return t.reshape(B, dilation, nb, Q_BLOCK, H, E)

    def with_prev(t):
        prev = jnp.pad(t, ((0, 0), (0, 0), (1, 0), (0, 0), (0, 0), (0, 0)))[:, :, :-1]
        return jnp.concatenate([prev, t], axis=3)

    qb = to_blocks(q)
    kc = with_prev(to_blocks(k))
    vc = with_prev(to_blocks(v))
    s = jnp.einsum('brnqhe,brnkhe->brnhqk', qb, kc, preferred_element_type=jnp.float32) * (HEAD_DIM ** -0.5)
    qi = jnp.arange(Q_BLOCK)
    kj = jnp.arange(2 * Q_BLOCK)
    steps = Q_BLOCK + qi[:, None] - kj[None, :]
    key_sub = jnp.arange(nb)[:, None] * Q_BLOCK - Q_BLOCK + kj[None, :]
    mask = ((steps >= 0) & (steps <= max_steps))[None] & (key_sub >= 0)[:, None, :]
    bias = -slopes[:, None, None] * (steps * dilation).astype(jnp.float32)[None]
    s = jnp.where(mask[None, None, :, None], s + bias[None, None, None], NEG_INF)
    lse = jax.nn.logsumexp(s, axis=-1)
    p = jnp.exp(s - lse[..., None])
    o = jnp.einsum('brnhqk,brnkhe->brnqhe', p.astype(vc.dtype), vc)

    def from_blocks(t):
        t = t.reshape((B, dilation, Lp) + t.shape[4:])[:, :, :L]
        t = jnp.moveaxis(t, 1, 2)
        return t.reshape((B, S) + t.shape[3:])

    return from_blocks(o), from_blocks(jnp.swapaxes(lse, -1, -2))


def dilated_attention_sample(q, k, v, kv_buf, window, dilation, slopes):
    T = q.shape[1]
    Wb = kv_buf.shape[1]
    k_all = jnp.concatenate([kv_buf[:, :, 0].astype(k.dtype), k], axis=1)
    v_all = jnp.concatenate([kv_buf[:, :, 1].astype(v.dtype), v], axis=1)
    steps = jnp.arange(window // dilation + 1)
    idx = Wb + jnp.arange(T)[:, None] - steps[None, :] * dilation
    valid = idx >= 0
    idx_c = jnp.maximum(idx, 0)
    kg = jnp.take(k_all, idx_c, axis=1)
    vg = jnp.take(v_all, idx_c, axis=1)
    s = jnp.einsum('bthe,btkhe->bhtk', q, kg, preferred_element_type=jnp.float32) * (HEAD_DIM ** -0.5)
    bias = -slopes[:, None, None] * (steps * dilation).astype(jnp.float32)[None, None, :]
    s = jnp.where(valid[None, None], s + bias[None], NEG_INF)
    lse = jax.nn.logsumexp(s, axis=-1)
    p = jnp.exp(s - lse[..., None])
    o = jnp.einsum('bhtk,btkhe->bthe', p.astype(vg.dtype), vg)
    return o, jnp.swapaxes(lse, 1, 2)


def merge_dilations(outs, lses):
    o = jnp.stack(outs)
    w = jax.nn.softmax(jnp.stack(lses), axis=0)
    return jnp.sum(o * w[..., None].astype(o.dtype), axis=0)


def hybrid_layer(x, conv_buf, h0, kv_bufs, norm_pre, norm_post, w_in, conv_w, conv_b,
                 lru_w_a, lru_b_a, lru_w_x, lru_b_x, lru_lambda, w_rnn_out, w_attn_out, w_out):
    B, T, _ = x.shape
    xn = rms_norm(x, norm_pre)
    proj = xn @ w_in
    u, z_rnn, q, k, v, z_attn, g_rnn, g_attn = jnp.split(proj, split_points(), axis=-1)
    uc, conv_new = causal_conv(u, conv_buf, conv_w, conv_b)
    h, h_last = rg_lru(uc, h0, lru_w_a, lru_b_a, lru_w_x, lru_b_x, lru_lambda)
    y_rnn = (h * jax.nn.silu(z_rnn)) @ w_rnn_out
    q = q.reshape(B, T, N_GROUPS, H_G, HEAD_DIM)
    k = k.reshape(B, T, N_GROUPS, H_G, HEAD_DIM)
    v = v.reshape(B, T, N_GROUPS, H_G, HEAD_DIM)
    slopes = alibi_slopes()
    outs, lses, kv_new = [], [], []
    for g, (window, dilation) in enumerate(ATTN_GROUPS):
        if kv_bufs is None:
            o, lse = dilated_attention_prompt(q[:, :, g], k[:, :, g], v[:, :, g], window, dilation, slopes[g])
            keep = min(window, T)
            kv_new.append(jnp.stack([k[:, -keep:, g], v[:, -keep:, g]], axis=2))
        else:
            o, lse = dilated_attention_sample(q[:, :, g], k[:, :, g], v[:, :, g], kv_bufs[g], window, dilation, slopes[g])
            kv_new.append(jnp.stack([k[:, :, g], v[:, :, g]], axis=2))
        outs.append(o)
        lses.append(lse)
    y_attn = merge_dilations(outs, lses).reshape(B, T, D_ATTN)
    y_attn = (y_attn * jax.nn.silu(z_attn)) @ w_attn_out
    mixed = jax.nn.sigmoid(g_rnn) * y_rnn + jax.nn.sigmoid(g_attn) * y_attn
    out = rms_norm(mixed @ w_out, norm_post)
    return x + out, conv_new, h_last, kv_new


def setup_inputs(seed: int = 0) -> dict:
    key = jax.random.key(seed)
    ks = jax.random.split(key, 24)
    f32 = jnp.float32
    nrm = lambda k, shape, scale: jax.random.normal(k, shape, f32) * scale
    a_c = jax.random.uniform(ks[0], (DEPTH, D_RNN), f32, 0.9, 0.999)
    a = a_c ** (1.0 / LRU_C)
    lam = jnp.log(a) - jnp.log1p(-a)

    def kv_cache(k, window):
        return nrm(k, (DEPTH, DEC_BATCH, min(window, PAST_LEN), 2, H_G, HEAD_DIM), 1.0)

    return {
        "x_prompt": nrm(ks[1], (BATCH, SEQ, D_MODEL), 1.0),
        "x_sample": nrm(ks[2], (DEC_BATCH, DEC_SEQ, D_MODEL), 1.0),
        "state_conv": nrm(ks[3], (DEPTH, DEC_BATCH, CONV_WIDTH - 1, D_RNN), 1.0),
        "state_h": nrm(ks[4], (DEPTH, DEC_BATCH, D_RNN), 0.5),
        "cache_kv_w128": kv_cache(ks[5], ATTN_GROUPS[0][0]),
        "cache_kv_w512": kv_cache(ks[6], ATTN_GROUPS[1][0]),
        "cache_kv_w2048": kv_cache(ks[7], ATTN_GROUPS[2][0]),
        "norm_pre": 1.0 + nrm(ks[8], (DEPTH, D_MODEL), 0.05),
        "norm_post": 1.0 + nrm(ks[9], (DEPTH, D_MODEL), 0.05),
        "w_in": nrm(ks[10], (DEPTH, D_MODEL, D_IN), D_MODEL ** -0.5),
        "conv_w": nrm(ks[11], (DEPTH, CONV_WIDTH, D_RNN), CONV_WIDTH ** -0.5),
        "conv_b": nrm(ks[12], (DEPTH, D_RNN), 0.01),
        "lru_w_a": nrm(ks[13], (DEPTH, N_RNN_BLOCKS, RNN_BLOCK, RNN_BLOCK), RNN_BLOCK ** -0.5),
        "lru_b_a": nrm(ks[14], (DEPTH, D_RNN), 0.01),
        "lru_w_x": nrm(ks[15], (DEPTH, N_RNN_BLOCKS, RNN_BLOCK, RNN_BLOCK), RNN_BLOCK ** -0.5),
        "lru_b_x": nrm(ks[16], (DEPTH, D_RNN), 0.01),
        "lru_lambda": lam,
        "w_rnn_out": nrm(ks[17], (DEPTH, D_RNN, D_MODEL), D_RNN ** -0.5),
        "w_attn_out": nrm(ks[18], (DEPTH, D_ATTN, D_MODEL), D_ATTN ** -0.5),
        "w_out": nrm(ks[19], (DEPTH, D_MODEL, D_MODEL), D_MODEL ** -0.5),
    }


def reference(x_prompt, x_sample, state_conv, state_h, cache_kv_w128, cache_kv_w512, cache_kv_w2048,
              norm_pre, norm_post, w_in, conv_w, conv_b, lru_w_a, lru_b_a, lru_w_x, lru_b_x, lru_lambda,
              w_rnn_out, w_attn_out, w_out):
    yp, ys = x_prompt, x_sample
    conv_p, conv_s, h_p, h_s = [], [], [], []
    kvp = ([], [], [])
    kvs = ([], [], [])
    for l in range(DEPTH):
        params = (norm_pre[l], norm_post[l], w_in[l], conv_w[l], conv_b[l], lru_w_a[l], lru_b_a[l],
                  lru_w_x[l], lru_b_x[l], lru_lambda[l], w_rnn_out[l], w_attn_out[l], w_out[l])
        zeros_conv = jnp.zeros((yp.shape[0], CONV_WIDTH - 1, D_RNN), yp.dtype)
        zeros_h = jnp.zeros((yp.shape[0], D_RNN), state_h.dtype)
        yp, cp, hp, kv_p = hybrid_layer(yp, zeros_conv, zeros_h, None, *params)
        ys, cs, hs, kv_s = hybrid_layer(ys, state_conv[l], state_h[l],
                                        (cache_kv_w128[l], cache_kv_w512[l], cache_kv_w2048[l]), *params)
        conv_p.append(cp)
        conv_s.append(cs)
        h_p.append(hp)
        h_s.append(hs)
        for g in range(N_GROUPS):
            kvp[g].append(kv_p[g])
            kvs[g].append(kv_s[g])
    return (yp, ys, jnp.stack(conv_p), jnp.stack(conv_s), jnp.stack(h_p), jnp.stack(h_s),
            jnp.stack(kvp[0]), jnp.stack(kvs[0]), jnp.stack(kvp[1]), jnp.stack(kvs[1]),
            jnp.stack(kvp[2]), jnp.stack(kvs[2]))
```

```python
import functools

import numpy as np
import jax
import jax.numpy as jnp
from jax import lax
from jax.experimental import pallas as pl
from jax.experimental.pallas import tpu as pltpu

F32 = jnp.float32
BF16 = jnp.bfloat16

D_MODEL = 1024
D_RNN = 1024
N_RNN_BLOCKS = 16
RNN_BLOCK = D_RNN // N_RNN_BLOCKS
CONV_WIDTH = 4
LRU_C = 8.0
HEAD_DIM = 64
H_G = 8
ATTN_GROUPS = ((128, 1), (512, 4), (2048, 16))
N_GROUPS = 3
D_ATTN = H_G * HEAD_DIM
Q_BLOCK = 128
ALIBI_MAX = 8.0
EPS = 1e-6
NEG_INF = -1e30
D_IN = 2 * D_RNN + 3 * N_GROUPS * D_ATTN + D_ATTN + 2 * D_MODEL

TN = 1024
N_COL_BLOCKS = D_IN // TN
MID_FIRST, MID_BLOCKS = 2, 5
D_MID = MID_BLOCKS * TN
MID_COL_BLOCKS = D_MID // D_ATTN
GATE_BLOCK = 256
N_GATE_BLOCKS = D_RNN // GATE_BLOCK
STAT_LANES = 128
LANES_PER_HEAD_STAT = STAT_LANES // H_G

TM_PROMPT = 512
TM_OUT = 512
ROW_CHUNK = 64
GATE_ROWS = 128
SAMPLE_BATCH_BLOCK = 8
VMEM_LIMIT = 48 * 1024 * 1024


def _alibi_slopes():
    n = N_GROUPS * H_G
    s = np.float32(2.0) ** (np.float32(-ALIBI_MAX) * np.arange(1, n + 1, dtype=np.float32) / np.float32(n))
    return s.reshape(N_GROUPS, H_G)


_SLOPES = _alibi_slopes()


def _softplus(y):
    return jnp.maximum(y, 0.0) + jnp.log1p(jnp.exp(-jnp.abs(y)))


def _sigmoid(x):
    return 1.0 / (1.0 + jnp.exp(-x))


def _for_rows(n_rows, chunk, fn):
    if n_rows <= chunk:
        fn(0)
        return

    def body(c, carry):
        fn(pl.multiple_of(c * chunk, chunk))
        return carry

    lax.fori_loop(0, n_rows // chunk, body, 0)


def _gate_coef(lam_ref, c0):
    return _softplus(-lam_ref[:, c0:c0 + GATE_BLOCK])


def _lru_gates(xc, c, wa_ref, wx_ref, ba_ref, bx_ref, lam_ref):
    c0 = c * GATE_BLOCK
    xcb = xc.astype(BF16)
    r = _sigmoid(jnp.dot(xcb, wa_ref[c], preferred_element_type=F32) + ba_ref[:, c0:c0 + GATE_BLOCK])
    i = _sigmoid(jnp.dot(xcb, wx_ref[c], preferred_element_type=F32) + bx_ref[:, c0:c0 + GATE_BLOCK])
    log_a = (-LRU_C * r) * _gate_coef(lam_ref, c0)
    a = jnp.exp(log_a)
    b = jnp.sqrt(1.0 - jnp.exp(2.0 * log_a)) * i * xc
    return a, b


def _proj_rnn_kernel(x_ref, w_ref, gpre_ref, cw_ref, cb_ref, wa_ref, wx_ref, ba_ref, bx_ref, lam_ref, wro_ref,
                     mid_ref, mixr_ref, sga_ref, kv0_ref, kv1_ref, kv2_ref, conv_ref, h_ref,
                     xn_s, ext_s, a_s, b_s, h_s, hz_s, yr_s, acc_s, carry_s, *, tm, n_tiles):
    m = pl.program_id(1)
    j = pl.program_id(2)
    last_tile = m == n_tiles - 1
    kv2_tiles = ATTN_GROUPS[2][0] // tm
    in_kv2 = m >= n_tiles - kv2_tiles

    def project():
        return jnp.dot(xn_s[...], w_ref[...], preferred_element_type=F32)

    @pl.when(j == 0)
    def _u_block():
        def norm(r0):
            xv = x_ref[pl.ds(r0, ROW_CHUNK), :]
            ms = jnp.mean(xv * xv, axis=-1, keepdims=True)
            xn_s[pl.ds(r0, ROW_CHUNK), :] = ((xv * lax.rsqrt(ms + EPS)) * gpre_ref[...]).astype(BF16)

        _for_rows(tm, ROW_CHUNK, norm)

        @pl.when(m == 0)
        def _reset():
            ext_s[0:8, :] = jnp.zeros((8, D_RNN), F32)
            carry_s[...] = jnp.zeros((1, D_RNN), F32)

        ext_s[8:8 + tm, :] = project()

        for c in range(N_GATE_BLOCKS):
            c0 = c * GATE_BLOCK
            for rc in range(tm // GATE_ROWS):
                r0 = rc * GATE_ROWS
                xc = cb_ref[:, c0:c0 + GATE_BLOCK]
                for tap in range(CONV_WIDTH):
                    shift = CONV_WIDTH - 1 - tap
                    xc = xc + ext_s[8 + r0 - shift:8 + r0 - shift + GATE_ROWS, c0:c0 + GATE_BLOCK] * \
                        cw_ref[tap:tap + 1, c0:c0 + GATE_BLOCK]
                a, b = _lru_gates(xc, c, wa_ref, wx_ref, ba_ref, bx_ref, lam_ref)
                a_s[r0:r0 + GATE_ROWS, c0:c0 + GATE_BLOCK] = a
                b_s[r0:r0 + GATE_ROWS, c0:c0 + GATE_BLOCK] = b

        row = lax.broadcasted_iota(jnp.int32, (8, D_RNN), 0)

        def scan(g, carry):
            r0 = pl.multiple_of(g * 8, 8)
            av = a_s[pl.ds(r0, 8), :]
            bv = b_s[pl.ds(r0, 8), :]
            for s in (1, 2, 4):
                keep = row >= s
                bv = jnp.where(keep, av * pltpu.roll(bv, s, axis=0) + bv, bv)
                av = jnp.where(keep, av * pltpu.roll(av, s, axis=0), av)
            h = av * carry + bv
            h_s[pl.ds(r0, 8), :] = h
            return h[7:8, :]

        carry = lax.fori_loop(0, tm // 8, scan, carry_s[...])
        carry_s[...] = carry
        h_ref[...] = carry
        tail = ext_s[tm + 5:tm + 8, :]
        conv_ref[...] = tail
        ext_s[5:8, :] = tail

    @pl.when(j == 1)
    def _z_rnn_block():
        acc_s[...] = project()

        def gate(r0):
            z = acc_s[pl.ds(r0, ROW_CHUNK), :]
            hz_s[pl.ds(r0, ROW_CHUNK), :] = (h_s[pl.ds(r0, ROW_CHUNK), :] * (z * _sigmoid(z))).astype(BF16)

        _for_rows(tm, ROW_CHUNK, gate)
        yr_s[...] = jnp.dot(hz_s[...], wro_ref[...], preferred_element_type=F32)

    @pl.when((j >= MID_FIRST) & (j < MID_FIRST + MID_BLOCKS))
    def _mid_blocks():
        acc_s[...] = project()

        def cast(r0):
            mid_ref[pl.ds(r0, ROW_CHUNK), :] = acc_s[pl.ds(r0, ROW_CHUNK), :].astype(BF16)

        _for_rows(tm, ROW_CHUNK, cast)

        @pl.when((j == 3) & last_tile)
        def _():
            kv0_ref[...] = acc_s[tm - 128:tm, 512:1024]

        @pl.when((j == 4) & last_tile)
        def _():
            kv1_ref[...] = acc_s[tm - 512:tm, 0:512]

        @pl.when((j == 4) & in_kv2)
        def _():
            kv2_ref[...] = acc_s[:, 512:1024]

        @pl.when((j == 5) & last_tile)
        def _():
            kv0_ref[...] = acc_s[tm - 128:tm, 0:512]
            kv1_ref[...] = acc_s[tm - 512:tm, 512:1024]

        @pl.when((j == 6) & in_kv2)
        def _():
            kv2_ref[...] = acc_s[:, 0:512]

    @pl.when(j == 7)
    def _g_rnn_block():
        acc_s[...] = project()

        def gate(r0):
            g = acc_s[pl.ds(r0, ROW_CHUNK), :]
            mixr_ref[pl.ds(r0, ROW_CHUNK), :] = (_sigmoid(g) * yr_s[pl.ds(r0, ROW_CHUNK), :]).astype(BF16)

        _for_rows(tm, ROW_CHUNK, gate)

    @pl.when(j == 8)
    def _g_attn_block():
        acc_s[...] = project()

        def gate(r0):
            sga_ref[pl.ds(r0, ROW_CHUNK), :] = _sigmoid(acc_s[pl.ds(r0, ROW_CHUNK), :]).astype(BF16)

        _for_rows(tm, ROW_CHUNK, gate)


def _proj_rnn_prompt(x, w_in, gpre, cw, cb, wa, wx, ba, bx, lam, wro):
    B, S, D = x.shape
    tm = TM_PROMPT
    n_tiles = S // tm
    kv2_tiles = ATTN_GROUPS[2][0] // tm
    m0 = n_tiles - kv2_tiles
    assert S % tm == 0 and tm >= ATTN_GROUPS[1][0] and ATTN_GROUPS[2][0] % tm == 0 and S >= ATTN_GROUPS[2][0]

    const2 = lambda b, m, j: (0, 0)
    const3 = lambda b, m, j: (0, 0, 0)

    def kv_col(first_v_block, tile_ok):
        return lambda b, m, j: jnp.where(tile_ok(m) & (j >= first_v_block), 1, 0)

    is_last = lambda m: m == n_tiles - 1
    in_kv2 = lambda m: m >= m0
    kv0_col, kv1_col, kv2_col = kv_col(5, is_last), kv_col(5, is_last), kv_col(6, in_kv2)

    out_shape = (
        jax.ShapeDtypeStruct((B, S, D_MID), BF16),
        jax.ShapeDtypeStruct((B, S, D), BF16),
        jax.ShapeDtypeStruct((B, S, D), BF16),
        jax.ShapeDtypeStruct((B, ATTN_GROUPS[0][0], 2 * D_ATTN), F32),
        jax.ShapeDtypeStruct((B, ATTN_GROUPS[1][0], 2 * D_ATTN), F32),
        jax.ShapeDtypeStruct((B, ATTN_GROUPS[2][0], 2 * D_ATTN), F32),
        jax.ShapeDtypeStruct((B, CONV_WIDTH - 1, D_RNN), F32),
        jax.ShapeDtypeStruct((B, 1, D_RNN), F32),
    )
    out_specs = (
        pl.BlockSpec((None, tm, TN), lambda b, m, j: (b, m, jnp.clip(j - MID_FIRST, 0, MID_BLOCKS - 1))),
        pl.BlockSpec((None, tm, D), lambda b, m, j: (b, m, 0)),
        pl.BlockSpec((None, tm, D), lambda b, m, j: (b, m, 0)),
        pl.BlockSpec((None, 128, D_ATTN), lambda b, m, j: (b, 0, kv0_col(b, m, j))),
        pl.BlockSpec((None, 512, D_ATTN), lambda b, m, j: (b, 0, kv1_col(b, m, j))),
        pl.BlockSpec((None, tm, D_ATTN), lambda b, m, j: (b, jnp.maximum(m - m0, 0), kv2_col(b, m, j))),
        pl.BlockSpec((None, CONV_WIDTH - 1, D_RNN), lambda b, m, j: (b, 0, 0)),
        pl.BlockSpec((None, 1, D_RNN), lambda b, m, j: (b, 0, 0)),
    )
    in_specs = [
        pl.BlockSpec((None, tm, D), lambda b, m, j: (b, m, 0)),
        pl.BlockSpec((D, TN), lambda b, m, j: (0, j)),
        pl.BlockSpec((1, D), const2),
        pl.BlockSpec((CONV_WIDTH, D_RNN), const2),
        pl.BlockSpec((1, D_RNN), const2),
        pl.BlockSpec((N_GATE_BLOCKS, GATE_BLOCK, GATE_BLOCK), const3),
        pl.BlockSpec((N_GATE_BLOCKS, GATE_BLOCK, GATE_BLOCK), const3),
        pl.BlockSpec((1, D_RNN), const2),
        pl.BlockSpec((1, D_RNN), const2),
        pl.BlockSpec((1, D_RNN), const2),
        pl.BlockSpec((D_RNN, D), const2),
    ]
    scratch = [
        pltpu.VMEM((tm, D), BF16),
        pltpu.VMEM((tm + 8, D_RNN), F32),
        pltpu.VMEM((tm, D_RNN), F32),
        pltpu.VMEM((tm, D_RNN), F32),
        pltpu.VMEM((tm, D_RNN), F32),
        pltpu.VMEM((tm, D_RNN), BF16),
        pltpu.VMEM((tm, D), F32),
        pltpu.VMEM((tm, TN), F32),
        pltpu.VMEM((1, D_RNN), F32),
    ]
    return pl.pallas_call(
        functools.partial(_proj_rnn_kernel, tm=tm, n_tiles=n_tiles),
        grid=(B, n_tiles, N_COL_BLOCKS),
        in_specs=in_specs,
        out_specs=out_specs,
        out_shape=out_shape,
        scratch_shapes=scratch,
        compiler_params=pltpu.CompilerParams(
            dimension_semantics=("arbitrary", "arbitrary", "arbitrary"), vmem_limit_bytes=VMEM_LIMIT),
        name="proj_rnn_prompt",
    )(x, w_in, gpre, cw, cb, wa, wx, ba, bx, lam, wro)


def _proj_rnn_sample_kernel(x_ref, w_ref, gpre_ref, cw_ref, cb_ref, wa_ref, wx_ref, ba_ref, bx_ref, lam_ref, wro_ref,
                            sc_ref, h0_ref,
                            mid_ref, mixr_ref, sga_ref, conv_ref, h_ref,
                            xn_s, h_s, yr_s):
    j = pl.program_id(0)

    def project():
        return jnp.dot(xn_s[...], w_ref[...], preferred_element_type=F32)

    @pl.when(j == 0)
    def _u_block():
        xv = x_ref[...]
        ms = jnp.mean(xv * xv, axis=-1, keepdims=True)
        xn_s[...] = ((xv * lax.rsqrt(ms + EPS)) * gpre_ref[...]).astype(BF16)
        u = project()
        taps = [sc_ref[:, k * D_RNN:(k + 1) * D_RNN] for k in range(CONV_WIDTH - 1)] + [u]
        for c in range(N_GATE_BLOCKS):
            c0 = c * GATE_BLOCK
            xc = cb_ref[:, c0:c0 + GATE_BLOCK]
            for tap in range(CONV_WIDTH):
                xc = xc + taps[tap][:, c0:c0 + GATE_BLOCK] * cw_ref[tap:tap + 1, c0:c0 + GATE_BLOCK]
            a, b = _lru_gates(xc, c, wa_ref, wx_ref, ba_ref, bx_ref, lam_ref)
            h = a * h0_ref[:, c0:c0 + GATE_BLOCK] + b
            h_s[:, c0:c0 + GATE_BLOCK] = h
            h_ref[:, c0:c0 + GATE_BLOCK] = h
        for k in range(1, CONV_WIDTH):
            conv_ref[:, (k - 1) * D_RNN:k * D_RNN] = taps[k]

    @pl.when(j == 1)
    def _z_rnn_block():
        z = project()
        hz = (h_s[...] * (z * _sigmoid(z))).astype(BF16)
        yr_s[...] = jnp.dot(hz, wro_ref[...], preferred_element_type=F32)

    @pl.when((j >= MID_FIRST) & (j < MID_FIRST + MID_BLOCKS))
    def _mid_blocks():
        mid_ref[...] = project()

    @pl.when(j == 7)
    def _g_rnn_block():
        mixr_ref[...] = _sigmoid(project()) * yr_s[...]

    @pl.when(j == 8)
    def _g_attn_block():
        sga_ref[...] = _sigmoid(project())


def _proj_rnn_sample(x, w_in, gpre, cw, cb, wa, wx, ba, bx, lam, wro, sconv, h0):
    N, D = x.shape
    const2 = lambda j: (0, 0)
    const3 = lambda j: (0, 0, 0)
    out_shape = (
        jax.ShapeDtypeStruct((N, D_MID), F32),
        jax.ShapeDtypeStruct((N, D), F32),
        jax.ShapeDtypeStruct((N, D), F32),
        jax.ShapeDtypeStruct((N, (CONV_WIDTH - 1) * D_RNN), F32),
        jax.ShapeDtypeStruct((N, D_RNN), F32),
    )
    out_specs = (
        pl.BlockSpec((N, TN), lambda j: (0, jnp.clip(j - MID_FIRST, 0, MID_BLOCKS - 1))),
        pl.BlockSpec((N, D), const2),
        pl.BlockSpec((N, D), const2),
        pl.BlockSpec((N, (CONV_WIDTH - 1) * D_RNN), const2),
        pl.BlockSpec((N, D_RNN), const2),
    )
    in_specs = [
        pl.BlockSpec((N, D), const2),
        pl.BlockSpec((D, TN), lambda j: (0, j)),
        pl.BlockSpec((1, D), const2),
        pl.BlockSpec((CONV_WIDTH, D_RNN), const2),
        pl.BlockSpec((1, D_RNN), const2),
        pl.BlockSpec((N_GATE_BLOCKS, GATE_BLOCK, GATE_BLOCK), const3),
        pl.BlockSpec((N_GATE_BLOCKS, GATE_BLOCK, GATE_BLOCK), const3),
        pl.BlockSpec((1, D_RNN), const2),
        pl.BlockSpec((1, D_RNN), const2),
        pl.BlockSpec((1, D_RNN), const2),
        pl.BlockSpec((D_RNN, D), const2),
        pl.BlockSpec((N, (CONV_WIDTH - 1) * D_RNN), const2),
        pl.BlockSpec((N, D_RNN), const2),
    ]
    scratch = [pltpu.VMEM((N, D), BF16), pltpu.VMEM((N, D_RNN), F32), pltpu.VMEM((N, D), F32)]
    return pl.pallas_call(
        _proj_rnn_sample_kernel,
        grid=(N_COL_BLOCKS,),
        in_specs=in_specs,
        out_specs=out_specs,
        out_shape=out_shape,
        scratch_shapes=scratch,
        compiler_params=pltpu.CompilerParams(dimension_semantics=("arbitrary",), vmem_limit_bytes=VMEM_LIMIT),
        name="proj_rnn_sample",
    )(x, w_in, gpre, cw, cb, wa, wx, ba, bx, lam, wro, sconv, h0)


def _attn_kernel(*refs, group, first, last):
    q_ref, kc_ref, kp_ref, vc_ref, vp_ref = refs[:5]
    pos = 5
    if not first:
        o_in_ref, st_in_ref = refs[pos:pos + 2]
        pos += 2
    o_ref = refs[pos]
    pos += 1
    if not last:
        st_ref = refs[pos]
        pos += 1
    bias_s = refs[pos]

    dilation = ATTN_GROUPS[group][1]
    n = pl.program_id(2)
    nt = (((1,), (1,)), ((), ()))

    @pl.when((pl.program_id(0) == 0) & (pl.program_id(1) == 0) & (n == 0))
    def _init_bias():
        qi = lax.broadcasted_iota(jnp.int32, (Q_BLOCK, Q_BLOCK), 0)
        kj = lax.broadcasted_iota(jnp.int32, (Q_BLOCK, Q_BLOCK), 1)
        steps_prev = Q_BLOCK + qi - kj
        steps_cur = qi - kj
        dist_prev = (steps_prev * dilation).astype(F32)
        dist_cur = (steps_cur * dilation).astype(F32)
        for h in range(H_G):
            slope = float(_SLOPES[group, h])
            bias_s[2 * h] = jnp.where(steps_prev <= Q_BLOCK, -slope * dist_prev, NEG_INF)
            bias_s[2 * h + 1] = jnp.where(steps_cur >= 0, -slope * dist_cur, NEG_INF)

    pen = jnp.where(n == 0, NEG_INF, 0.0).astype(F32)
    lane = lax.broadcasted_iota(jnp.int32, (Q_BLOCK, 128), 1)
    low = lane < HEAD_DIM
    stat_lane = lax.broadcasted_iota(jnp.int32, (Q_BLOCK, STAT_LANES), 1) // LANES_PER_HEAD_STAT
    stats = jnp.zeros((Q_BLOCK, STAT_LANES), F32)

    for p in range(H_G // 2):
        sl = slice(128 * p, 128 * (p + 1))
        qp = q_ref[:, sl] * (HEAD_DIM ** -0.5)
        kcp, kpp, vcp, vpp = kc_ref[:, sl], kp_ref[:, sl], vc_ref[:, sl], vp_ref[:, sl]
        o_pair = jnp.zeros((Q_BLOCK, 128), F32)
        prev_scale = []
        for hh in range(2):
            h = 2 * p + hh
            msk = low if hh == 0 else jnp.logical_not(low)
            qm = jnp.where(msk, qp, jnp.zeros_like(qp))
            s_c = lax.dot_general(qm, kcp, nt, preferred_element_type=F32) + bias_s[2 * h + 1]
            s_p = lax.dot_general(qm, kpp, nt, preferred_element_type=F32) + (bias_s[2 * h] + pen)
            mx = jnp.maximum(jnp.max(s_c, axis=-1, keepdims=True), jnp.max(s_p, axis=-1, keepdims=True))
            e_c = jnp.exp(s_c - mx)
            e_p = jnp.exp(s_p - mx)
            l = jnp.sum(e_c, axis=-1, keepdims=True) + jnp.sum(e_p, axis=-1, keepdims=True)
            vcm = jnp.where(msk, vcp, jnp.zeros_like(vcp))
            vpm = jnp.where(msk, vpp, jnp.zeros_like(vpp))
            acc = jnp.dot(e_c.astype(BF16), vcm, preferred_element_type=F32) + \
                jnp.dot(e_p.astype(BF16), vpm, preferred_element_type=F32)
            lse = mx + jnp.log(l)
            if first:
                scale = 1.0 / l
                lse_out = lse
            else:
                lse_prev = st_in_ref[:, LANES_PER_HEAD_STAT * h:LANES_PER_HEAD_STAT * h + 1]
                mm = jnp.maximum(lse_prev, lse)
                w_prev = jnp.exp(lse_prev - mm)
                w_new = jnp.exp(lse - mm)
                den = w_prev + w_new
                scale = w_new / (l * den)
                prev_scale.append(w_prev / den)
                lse_out = mm + jnp.log(den)
            o_pair = o_pair + acc * scale
            if not last:
                stats = jnp.where(stat_lane == h, lse_out, stats)
        if not first:
            o_pair = o_pair + o_in_ref[:, sl].astype(F32) * jnp.where(low, prev_scale[0], prev_scale[1])
        o_ref[:, sl] = o_pair.astype(o_ref.dtype)
    if not last:
        st_ref[...] = stats


def _attn_group(mid, group, prev):
    B, S, _ = mid.shape
    dilation = ATTN_GROUPS[group][1]
    L = S // dilation
    nb = L // Q_BLOCK
    assert S % dilation == 0 and L % Q_BLOCK == 0
    first = prev is None
    last = group == 0
    mid_v = mid.reshape(B, L, dilation * D_MID)

    def col(block):
        return lambda b, r, n: (b, n, r * MID_COL_BLOCKS + block)

    def col_prev(block):
        return lambda b, r, n: (b, jnp.maximum(n - 1, 0), r * MID_COL_BLOCKS + block)

    blk = (None, Q_BLOCK, D_ATTN)
    in_specs = [
        pl.BlockSpec(blk, col(group)),
        pl.BlockSpec(blk, col(N_GROUPS + group)),
        pl.BlockSpec(blk, col_prev(N_GROUPS + group)),
        pl.BlockSpec(blk, col(2 * N_GROUPS + group)),
        pl.BlockSpec(blk, col_prev(2 * N_GROUPS + group)),
    ]
    args = [mid_v, mid_v, mid_v, mid_v, mid_v]
    res = lambda b, r, n: (b, n, r)
    if not first:
        o_prev, st_prev = prev
        in_specs += [pl.BlockSpec(blk, res), pl.BlockSpec((None, Q_BLOCK, STAT_LANES), res)]
        args += [o_prev.reshape(B, L, dilation * D_ATTN), st_prev.reshape(B, L, dilation * STAT_LANES)]
    out_shape = [jax.ShapeDtypeStruct((B, L, dilation * D_ATTN), BF16)]
    out_specs = [pl.BlockSpec(blk, res)]
    if not last:
        out_shape.append(jax.ShapeDtypeStruct((B, L, dilation * STAT_LANES), F32))
        out_specs.append(pl.BlockSpec((None, Q_BLOCK, STAT_LANES), res))
    outs = pl.pallas_call(
        functools.partial(_attn_kernel, group=group, first=first, last=last),
        grid=(B, dilation, nb),
        in_specs=in_specs,
        out_specs=out_specs,
        out_shape=out_shape,
        scratch_shapes=[pltpu.VMEM((2 * H_G, Q_BLOCK, Q_BLOCK), F32)],
        compiler_params=pltpu.CompilerParams(
            dimension_semantics=("arbitrary", "arbitrary", "arbitrary"), vmem_limit_bytes=VMEM_LIMIT),
        name=f"attn_group{group}",
    )(*args)
    o = outs[0].reshape(B, S, D_ATTN)
    if last:
        return o, None
    return o, outs[1].reshape(B, S, STAT_LANES)


def _attn_sample_kernel(mid_ref, c0_ref, c1_ref, c2_ref, o_ref, *, bb):
    caches = (c0_ref, c1_ref, c2_ref)
    n_keys = ATTN_GROUPS[0][0]
    lane_head = lax.broadcasted_iota(jnp.int32, (H_G, D_ATTN), 1) // HEAD_DIM
    head = lax.broadcasted_iota(jnp.int32, (H_G, D_ATTN), 0)
    own = lane_head == head
    key_i = lax.broadcasted_iota(jnp.int32, (H_G, n_keys), 1)
    head_k = lax.broadcasted_iota(jnp.int32, (H_G, n_keys), 0)
    nt = (((1,), (1,)), ((), ()))
    scale = HEAD_DIM ** -0.5

    def body(bi, carry):
        row = mid_ref[pl.ds(bi, 1), :]
        outs, lses = [], []
        for g, (window, dilation) in enumerate(ATTN_GROUPS):
            slope = jnp.zeros((H_G, n_keys), F32)
            for h in range(H_G):
                slope = jnp.where(head_k == h, float(_SLOPES[g, h]), slope)
            bias = -slope * ((window // dilation - key_i) * dilation).astype(F32)
            qg = row[:, D_ATTN * g:D_ATTN * (g + 1)] * scale
            kn = row[:, D_ATTN * (N_GROUPS + g):D_ATTN * (N_GROUPS + g + 1)].astype(BF16).astype(F32)
            vn = row[:, D_ATTN * (2 * N_GROUPS + g):D_ATTN * (2 * N_GROUPS + g + 1)].astype(BF16).astype(F32)
            q_rows = jnp.where(own, jnp.broadcast_to(qg, (H_G, D_ATTN)), 0.0).astype(BF16)
            kv = caches[g][bi]
            kc = kv[:, 0:D_ATTN].astype(BF16)
            vc = kv[:, D_ATTN:2 * D_ATTN].astype(BF16)
            s = lax.dot_general(q_rows, kc, nt, preferred_element_type=F32) + bias
            s_new = jnp.sum(q_rows.astype(F32) * kn, axis=-1, keepdims=True)
            mx = jnp.maximum(jnp.max(s, axis=-1, keepdims=True), s_new)
            e = jnp.exp(s - mx)
            e_new = jnp.exp(s_new - mx)
            l = jnp.sum(e, axis=-1, keepdims=True) + e_new
            o = jnp.dot(e.astype(BF16), vc, preferred_element_type=F32) + e_new.astype(BF16).astype(F32) * vn
            outs.append(o / l)
            lses.append(mx + jnp.log(l))
        mm = jnp.maximum(jnp.maximum(lses[0], lses[1]), lses[2])
        ws = [jnp.exp(x - mm) for x in lses]
        den = ws[0] + ws[1] + ws[2]
        merged = (outs[0] * ws[0] + outs[1] * ws[1] + outs[2] * ws[2]) / den
        o_ref[pl.ds(bi, 1), :] = jnp.sum(jnp.where(own, merged, 0.0), axis=0, keepdims=True)
        return carry

    lax.fori_loop(0, bb, body, 0)


def _attn_sample(mid_s, caches):
    N = mid_s.shape[0]
    bb = SAMPLE_BATCH_BLOCK
    n_keys = ATTN_GROUPS[0][0]
    views = []
    for (window, dilation), c in zip(ATTN_GROUPS, caches):
        assert c.shape[1] == window and window // dilation == n_keys
        views.append(c.reshape(N, n_keys, dilation * 2 * D_ATTN))
    in_specs = [pl.BlockSpec((bb, D_MID), lambda i: (i, 0))]
    in_specs += [pl.BlockSpec((bb, n_keys, 2 * D_ATTN), lambda i: (i, 0, 0)) for _ in views]
    return pl.pallas_call(
        functools.partial(_attn_sample_kernel, bb=bb),
        grid=(N // bb,),
        in_specs=in_specs,
        out_specs=pl.BlockSpec((bb, D_ATTN), lambda i: (i, 0)),
        out_shape=jax.ShapeDtypeStruct((N, D_ATTN), F32),
        compiler_params=pltpu.CompilerParams(dimension_semantics=("arbitrary",), vmem_limit_bytes=VMEM_LIMIT),
        name="attn_sample",
    )(mid_s, *views)


def _out_kernel(att_ref, z_ref, sga_ref, mixr_ref, x_ref, wao_ref, wo_ref, gpost_ref, y_ref):
    z = z_ref[...].astype(F32)
    ya_in = (att_ref[...].astype(F32) * (z * _sigmoid(z))).astype(BF16)
    ya = jnp.dot(ya_in, wao_ref[...], preferred_element_type=F32)
    mixed = mixr_ref[...].astype(F32) + sga_ref[...].astype(F32) * ya
    out = jnp.dot(mixed.astype(BF16), wo_ref[...], preferred_element_type=F32)
    ms = jnp.mean(out * out, axis=-1, keepdims=True)
    y_ref[...] = x_ref[...] + (out * lax.rsqrt(ms + EPS)) * gpost_ref[...]


def _out_stage(att, mid, sga, mixr, x, wao, wo, gpost, tm):
    R, D = x.shape
    z_block = (D_MID - D_ATTN) // D_ATTN
    rows = lambda i: (i, 0)
    const = lambda i: (0, 0)
    return pl.pallas_call(
        _out_kernel,
        grid=(R // tm,),
        in_specs=[
            pl.BlockSpec((tm, D_ATTN), rows),
            pl.BlockSpec((tm, D_ATTN), lambda i: (i, z_block)),
            pl.BlockSpec((tm, D), rows),
            pl.BlockSpec((tm, D), rows),
            pl.BlockSpec((tm, D), rows),
            pl.BlockSpec((D_ATTN, D), const),
            pl.BlockSpec((D, D), const),
            pl.BlockSpec((1, D), const),
        ],
        out_specs=pl.BlockSpec((tm, D), rows),
        out_shape=jax.ShapeDtypeStruct((R, D), F32),
        compiler_params=pltpu.CompilerParams(dimension_semantics=("arbitrary",), vmem_limit_bytes=VMEM_LIMIT),
        name=f"out_stage_{R}",
    )(att, mid, sga, mixr, x, wao, wo, gpost)


def _block_diag_chunks(w):
    per = GATE_BLOCK // RNN_BLOCK
    w = w.reshape(N_GATE_BLOCKS, per, RNN_BLOCK, RNN_BLOCK)
    eye = jnp.eye(per, dtype=w.dtype)
    dense = w[:, :, :, None, :] * eye[None, :, None, :, None]
    return dense.reshape(N_GATE_BLOCKS, GATE_BLOCK, GATE_BLOCK)


def _layer(yp, ys, sconv, h0, caches, norm_pre, norm_post, w_in, conv_w, conv_b, lru_w_a, lru_b_a, lru_w_x,
           lru_b_x, lru_lambda, w_rnn_out, w_attn_out, w_out):
    B, S, D = yp.shape
    N = ys.shape[0]
    row = lambda v: v.reshape(1, -1)
    w_in_b = w_in.astype(BF16)
    wa = _block_diag_chunks(lru_w_a).astype(BF16)
    wx = _block_diag_chunks(lru_w_x).astype(BF16)
    wro = w_rnn_out.astype(BF16)
    wao = w_attn_out.astype(BF16)
    wo = w_out.astype(BF16)
    shared = (w_in_b, row(norm_pre), conv_w, row(conv_b), wa, wx, row(lru_b_a), row(lru_b_x), row(lru_lambda), wro)

    mid, mixr, sga, kv0, kv1, kv2, conv_p, h_p = _proj_rnn_prompt(yp, *shared)
    merged = None
    for group in (2, 1, 0):
        o, st = _attn_group(mid, group, merged)
        merged = (o, st)
    y_p = _out_stage(merged[0].reshape(B * S, D_ATTN), mid.reshape(B * S, D_MID), sga.reshape(B * S, D),
                     mixr.reshape(B * S, D), yp.reshape(B * S, D), wao, wo, row(norm_post), TM_OUT)
    kv_p = [kv.reshape(B, kv.shape[1], 2, H_G, HEAD_DIM) for kv in (kv0, kv1, kv2)]

    xs = ys.reshape(N, D)
    mid_s, mixr_s, sga_s, conv_s, h_s = _proj_rnn_sample(
        xs, *shared, sconv.reshape(N, (CONV_WIDTH - 1) * D_RNN), h0)
    att_s = _attn_sample(mid_s, caches)
    y_s = _out_stage(att_s, mid_s, sga_s, mixr_s, xs, wao, wo, row(norm_post), N)
    kv_s = []
    for g in range(N_GROUPS):
        k = mid_s[:, D_ATTN * (N_GROUPS + g):D_ATTN * (N_GROUPS + g + 1)]
        v = mid_s[:, D_ATTN * (2 * N_GROUPS + g):D_ATTN * (2 * N_GROUPS + g + 1)]
        kv_s.append(jnp.concatenate([k, v], axis=1).reshape(N, 1, 2, H_G, HEAD_DIM))

    return (y_p.reshape(B, S, D), y_s.reshape(N, 1, D), conv_p, conv_s.reshape(N, CONV_WIDTH - 1, D_RNN),
            h_p.reshape(B, D_RNN), h_s, kv_p, kv_s)


def kernel(x_prompt, x_sample, state_conv, state_h, cache_kv_w128, cache_kv_w512, cache_kv_w2048, norm_pre, norm_post, w_in, conv_w, conv_b, lru_w_a, lru_b_a, lru_w_x, lru_b_x, lru_lambda, w_rnn_out, w_attn_out, w_out):
    depth = norm_pre.shape[0]
    yp, ys = x_prompt, x_sample
    conv_p, conv_s, h_p, h_s = [], [], [], []
    kvp = ([], [], [])
    kvs = ([], [], [])
    for l in range(depth):
        yp, ys, cp, cs, hp, hs, kv_p, kv_s = _layer(
            yp, ys, state_conv[l], state_h[l], (cache_kv_w128[l], cache_kv_w512[l], cache_kv_w2048[l]),
            norm_pre[l], norm_post[l], w_in[l], conv_w[l], conv_b[l], lru_w_a[l], lru_b_a[l], lru_w_x[l],
            lru_b_x[l], lru_lambda[l], w_rnn_out[l], w_attn_out[l], w_out[l])
        conv_p.append(cp)
        conv_s.append(cs)
        h_p.append(hp)
        h_s.append(hs)
        for g in range(N_GROUPS):
            kvp[g].append(kv_p[g])
            kvs[g].append(kv_s[g])
    return (yp, ys, jnp.stack(conv_p), jnp.stack(conv_s), jnp.stack(h_p), jnp.stack(h_s),
            jnp.stack(kvp[0]), jnp.stack(kvs[0]), jnp.stack(kvp[1]), jnp.stack(kvs[1]),
            jnp.stack(kvp[2]), jnp.stack(kvs[2]))
```

```python
import functools

import numpy as np
import jax
import jax.numpy as jnp
from jax import lax
from jax.experimental import pallas as pl
from jax.experimental.pallas import tpu as pltpu

F32 = jnp.float32
BF16 = jnp.bfloat16

D_MODEL = 1024
D_RNN = 1024
N_RNN_BLOCKS = 16
RNN_BLOCK = D_RNN // N_RNN_BLOCKS
CONV_WIDTH = 4
LRU_C = 8.0
HEAD_DIM = 64
H_G = 8
ATTN_GROUPS = ((128, 1), (512, 4), (2048, 16))
N_GROUPS = 3
D_ATTN = H_G * HEAD_DIM
Q_BLOCK = 128
ALIBI_MAX = 8.0
EPS = 1e-6
NEG_INF = -1e30
D_IN = 2 * D_RNN + 3 * N_GROUPS * D_ATTN + D_ATTN + 2 * D_MODEL

LANES = 128
TN = 1024
N_COL_BLOCKS = D_IN // TN
GATE_BLOCK = 256
N_GATE_BLOCKS = D_RNN // GATE_BLOCK
STAT_LANES = LANES
STAT_LANES_PER_HEAD = STAT_LANES // H_G

TM_PROMPT = 512
TM_OUT = 512
ATTN_CHUNK = 512
ROW_CHUNK = 64
GATE_ROWS = 128
SAMPLE_BATCH_BLOCK = 4
VMEM_LIMIT = 56 * 1024 * 1024


def _alibi_slopes():
    n = N_GROUPS * H_G
    s = np.float32(2.0) ** (np.float32(-ALIBI_MAX) * np.arange(1, n + 1, dtype=np.float32) / np.float32(n))
    return s.reshape(N_GROUPS, H_G)


_SLOPES = _alibi_slopes()


def _stat_lane(h):
    return (h % 2) * (STAT_LANES // 2) + (h // 2) * STAT_LANES_PER_HEAD


def _softplus(y):
    return jnp.maximum(y, 0.0) + jnp.log1p(jnp.exp(-jnp.abs(y)))


def _sigmoid(x):
    return 1.0 / (1.0 + jnp.exp(-x))


def _for_rows(n_rows, chunk, fn):
    if n_rows <= chunk:
        fn(0)
        return

    def body(c, carry):
        fn(pl.multiple_of(c * chunk, chunk))
        return carry

    lax.fori_loop(0, n_rows // chunk, body, 0)


def _lru_gates(xc, c, wa_ref, wx_ref, ba_ref, bx_ref, lam_ref):
    c0 = c * GATE_BLOCK
    xcb = xc.astype(BF16)
    r = _sigmoid(jnp.dot(xcb, wa_ref[c], preferred_element_type=F32) + ba_ref[:, c0:c0 + GATE_BLOCK])
    i = _sigmoid(jnp.dot(xcb, wx_ref[c], preferred_element_type=F32) + bx_ref[:, c0:c0 + GATE_BLOCK])
    log_a = (-LRU_C * r) * _softplus(-lam_ref[:, c0:c0 + GATE_BLOCK])
    a = jnp.exp(log_a)
    b = jnp.sqrt(1.0 - jnp.exp(2.0 * log_a)) * i * xc
    return a, b


def _proj_rnn_kernel(x_ref, w_ref, gpre_ref, cw_ref, cb_ref, wa_ref, wx_ref, ba_ref, bx_ref, lam_ref, wro_ref,
                     q0_ref, k0_ref, v0_ref, q1_ref, k1_ref, v1_ref, q2_ref, k2_ref, v2_ref,
                     zg_ref, mixr_ref, sga_ref, kv0_ref, kv1_ref, kv2_ref, conv_ref, h_ref,
                     xn_s, xnf_s, xn4_s, xn16_s, ext_s, a_s, b_s, hz_s, yr_s, acc_s, tail_s, carry_s,
                     *, tm, n_tiles):
    m = pl.program_id(1)
    j = pl.program_id(2)
    last_tile = m == n_tiles - 1
    in_kv2 = m >= n_tiles - ATTN_GROUPS[2][0] // tm
    half = TN // 2
    d1, d2 = ATTN_GROUPS[1][1], ATTN_GROUPS[2][1]

    def project(lhs_s):
        return jnp.dot(lhs_s[...], w_ref[...], preferred_element_type=F32)

    def emit(out_ref, c0, dilation):
        p = tm // dilation
        for r in range(dilation):
            out_ref[r] = acc_s[r * p:(r + 1) * p, c0:c0 + half].astype(BF16)

    def natural_tail(out_ref, out_c0, c0, dilation):
        p = tm // dilation
        for c in range(half // LANES):
            for r in range(dilation):
                tail_s[c, pl.ds(r, p, stride=dilation), :] = \
                    acc_s[r * p:(r + 1) * p, c0 + c * LANES:c0 + (c + 1) * LANES]
            out_ref[:, out_c0 + c * LANES:out_c0 + (c + 1) * LANES] = tail_s[c]

    @pl.when(j == 0)
    def _u_block():
        def norm(r0):
            xv = x_ref[pl.ds(r0, ROW_CHUNK), :]
            ms = jnp.mean(xv * xv, axis=-1, keepdims=True)
            xn = (xv * lax.rsqrt(ms + EPS)) * gpre_ref[...]
            xn_s[pl.ds(r0, ROW_CHUNK), :] = xn.astype(BF16)
            for c in range(D_MODEL // LANES):
                xnf_s[c, pl.ds(r0, ROW_CHUNK), :] = xn[:, c * LANES:(c + 1) * LANES]

        _for_rows(tm, ROW_CHUNK, norm)

        for dil, dst in ((d1, xn4_s), (d2, xn16_s)):
            p = tm // dil
            for r in range(dil):
                for c in range(D_MODEL // LANES):
                    dst[r * p:(r + 1) * p, c * LANES:(c + 1) * LANES] = \
                        xnf_s[c, pl.ds(r, p, stride=dil), :].astype(BF16)

        @pl.when(m == 0)
        def _reset():
            ext_s[0:8, :] = jnp.zeros((8, D_RNN), F32)
            carry_s[...] = jnp.zeros((1, D_RNN), F32)

        ext_s[8:8 + tm, :] = project(xn_s)

        for c in range(N_GATE_BLOCKS):
            c0 = c * GATE_BLOCK
            for rc in range(tm // GATE_ROWS):
                r0 = rc * GATE_ROWS
                xc = cb_ref[:, c0:c0 + GATE_BLOCK]
                for tap in range(CONV_WIDTH):
                    shift = CONV_WIDTH - 1 - tap
                    xc = xc + ext_s[8 + r0 - shift:8 + r0 - shift + GATE_ROWS, c0:c0 + GATE_BLOCK] * \
                        cw_ref[tap:tap + 1, c0:c0 + GATE_BLOCK]
                a, b = _lru_gates(xc, c, wa_ref, wx_ref, ba_ref, bx_ref, lam_ref)
                a_s[r0:r0 + GATE_ROWS, c0:c0 + GATE_BLOCK] = a
                b_s[r0:r0 + GATE_ROWS, c0:c0 + GATE_BLOCK] = b

        row = lax.broadcasted_iota(jnp.int32, (8, D_RNN), 0)

        def scan(g, carry):
            r0 = pl.multiple_of(g * 8, 8)
            av = a_s[pl.ds(r0, 8), :]
            bv = b_s[pl.ds(r0, 8), :]
            for s in (1, 2, 4):
                keep = row >= s
                bv = jnp.where(keep, av * pltpu.roll(bv, s, axis=0) + bv, bv)
                av = jnp.where(keep, av * pltpu.roll(av, s, axis=0), av)
            h = av * carry + bv
            b_s[pl.ds(r0, 8), :] = h
            return h[7:8, :]

        carry = lax.fori_loop(0, tm // 8, scan, carry_s[...])
        carry_s[...] = carry
        h_ref[...] = carry
        tail = ext_s[tm + 5:tm + 8, :]
        conv_ref[...] = tail
        ext_s[5:8, :] = tail

    @pl.when(j == 1)
    def _z_rnn_block():
        acc_s[...] = project(xn_s)

        def gate(r0):
            z = acc_s[pl.ds(r0, ROW_CHUNK), :]
            hz_s[pl.ds(r0, ROW_CHUNK), :] = (b_s[pl.ds(r0, ROW_CHUNK), :] * (z * _sigmoid(z))).astype(BF16)

        _for_rows(tm, ROW_CHUNK, gate)
        yr_s[...] = jnp.dot(hz_s[...], wro_ref[...], preferred_element_type=F32)

    @pl.when(j == 2)
    def _g_rnn_block():
        acc_s[...] = project(xn_s)

        def gate(r0):
            g = acc_s[pl.ds(r0, ROW_CHUNK), :]
            mixr_ref[pl.ds(r0, ROW_CHUNK), :] = (_sigmoid(g) * yr_s[pl.ds(r0, ROW_CHUNK), :]).astype(BF16)

        _for_rows(tm, ROW_CHUNK, gate)

    @pl.when(j == 3)
    def _g_attn_block():
        acc_s[...] = project(xn_s)

        def gate(r0):
            sga_ref[pl.ds(r0, ROW_CHUNK), :] = _sigmoid(acc_s[pl.ds(r0, ROW_CHUNK), :]).astype(BF16)

        _for_rows(tm, ROW_CHUNK, gate)

    @pl.when(j == 4)
    def _zattn_q0_block():
        acc_s[...] = project(xn_s)
        zg_ref[...] = acc_s[:, 0:half].astype(BF16)
        emit(q0_ref, half, 1)

    @pl.when(j == 5)
    def _k0_v0_block():
        acc_s[...] = project(xn_s)
        emit(k0_ref, 0, 1)
        emit(v0_ref, half, 1)

        @pl.when(last_tile)
        def _():
            kv0_ref[...] = acc_s[tm - ATTN_GROUPS[0][0]:tm, :]

    @pl.when(j == 6)
    def _q1_k1_block():
        acc_s[...] = project(xn4_s)
        emit(q1_ref, 0, d1)
        emit(k1_ref, half, d1)

        @pl.when(last_tile)
        def _():
            natural_tail(kv1_ref, 0, half, d1)

    @pl.when(j == 7)
    def _v1_q2_block():
        acc_s[:, 0:half] = jnp.dot(xn4_s[...], w_ref[:, 0:half], preferred_element_type=F32)
        acc_s[:, half:TN] = jnp.dot(xn16_s[...], w_ref[:, half:TN], preferred_element_type=F32)
        emit(v1_ref, 0, d1)
        emit(q2_ref, half, d2)

        @pl.when(last_tile)
        def _():
            natural_tail(kv1_ref, half, 0, d1)

    @pl.when(j == 8)
    def _k2_v2_block():
        acc_s[...] = project(xn16_s)
        emit(k2_ref, 0, d2)
        emit(v2_ref, half, d2)

        @pl.when(in_kv2)
        def _():
            natural_tail(kv2_ref, 0, 0, d2)
            natural_tail(kv2_ref, half, half, d2)


def _proj_rnn_prompt(x, w_in, gpre, cw, cb, wa, wx, ba, bx, lam, wro):
    B, S, D = x.shape
    tm = TM_PROMPT
    n_tiles = S // tm
    m0 = n_tiles - ATTN_GROUPS[2][0] // tm
    assert S % tm == 0 and tm == ATTN_GROUPS[1][0] and ATTN_GROUPS[2][0] % tm == 0 and S >= ATTN_GROUPS[2][0]

    const2 = lambda b, m, j: (0, 0)
    const3 = lambda b, m, j: (0, 0, 0)
    tile3 = lambda b, m, j: (b, m, 0)
    tile4 = lambda b, m, j: (b, 0, m, 0)
    per_seq = lambda b, m, j: (b, 0, 0)

    qkv_shapes, qkv_specs = [], []
    for _, dil in ATTN_GROUPS:
        for _ in range(3):
            qkv_shapes.append(jax.ShapeDtypeStruct((B, dil, S // dil, D_ATTN), BF16))
            qkv_specs.append(pl.BlockSpec((None, dil, tm // dil, D_ATTN), tile4))
    out_shape = tuple(qkv_shapes) + (
        jax.ShapeDtypeStruct((B, S, D_ATTN), BF16),
        jax.ShapeDtypeStruct((B, S, D), BF16),
        jax.ShapeDtypeStruct((B, S, D), BF16),
        jax.ShapeDtypeStruct((B, ATTN_GROUPS[0][0], 2 * D_ATTN), F32),
        jax.ShapeDtypeStruct((B, ATTN_GROUPS[1][0], 2 * D_ATTN), F32),
        jax.ShapeDtypeStruct((B, ATTN_GROUPS[2][0], 2 * D_ATTN), F32),
        jax.ShapeDtypeStruct((B, CONV_WIDTH - 1, D_RNN), F32),
        jax.ShapeDtypeStruct((B, 1, D_RNN), F32),
    )
    out_specs = tuple(qkv_specs) + (
        pl.BlockSpec((None, tm, D_ATTN), tile3),
        pl.BlockSpec((None, tm, D), tile3),
        pl.BlockSpec((None, tm, D), tile3),
        pl.BlockSpec((None, ATTN_GROUPS[0][0], 2 * D_ATTN), per_seq),
        pl.BlockSpec((None, ATTN_GROUPS[1][0], 2 * D_ATTN), per_seq),
        pl.BlockSpec((None, tm, 2 * D_ATTN), lambda b, m, j: (b, jnp.maximum(m - m0, 0), 0)),
        pl.BlockSpec((None, CONV_WIDTH - 1, D_RNN), per_seq),
        pl.BlockSpec((None, 1, D_RNN), per_seq),
    )
    in_specs = [
        pl.BlockSpec((None, tm, D), tile3),
        pl.BlockSpec((D, TN), lambda b, m, j: (0, j)),
        pl.BlockSpec((1, D), const2),
        pl.BlockSpec((CONV_WIDTH, D_RNN), const2),
        pl.BlockSpec((1, D_RNN), const2),
        pl.BlockSpec((N_GATE_BLOCKS, GATE_BLOCK, GATE_BLOCK), const3),
        pl.BlockSpec((N_GATE_BLOCKS, GATE_BLOCK, GATE_BLOCK), const3),
        pl.BlockSpec((1, D_RNN), const2),
        pl.BlockSpec((1, D_RNN), const2),
        pl.BlockSpec((1, D_RNN), const2),
        pl.BlockSpec((D_RNN, D), const2),
    ]
    scratch = [
        pltpu.VMEM((tm, D), BF16),
        pltpu.VMEM((D // LANES, tm, LANES), F32),
        pltpu.VMEM((tm, D), BF16),
        pltpu.VMEM((tm, D), BF16),
        pltpu.VMEM((tm + 8, D_RNN), F32),
        pltpu.VMEM((tm, D_RNN), F32),
        pltpu.VMEM((tm, D_RNN), F32),
        pltpu.VMEM((tm, D_RNN), BF16),
        pltpu.VMEM((tm, D), F32),
        pltpu.VMEM((tm, TN), F32),
        pltpu.VMEM((D_ATTN // LANES, tm, LANES), F32),
        pltpu.VMEM((1, D_RNN), F32),
    ]
    return pl.pallas_call(
        functools.partial(_proj_rnn_kernel, tm=tm, n_tiles=n_tiles),
        grid=(B, n_tiles, N_COL_BLOCKS),
        in_specs=in_specs,
        out_specs=out_specs,
        out_shape=out_shape,
        scratch_shapes=scratch,
        compiler_params=pltpu.CompilerParams(
            dimension_semantics=("arbitrary", "arbitrary", "arbitrary"), vmem_limit_bytes=VMEM_LIMIT),
        name="proj_rnn_prompt",
    )(x, w_in, gpre, cw, cb, wa, wx, ba, bx, lam, wro)


def _proj_rnn_sample_kernel(x_ref, w_ref, gpre_ref, cw_ref, cb_ref, wa_ref, wx_ref, ba_ref, bx_ref, lam_ref, wro_ref,
                            sc_ref, h0_ref,
                            att_ref, mixr_ref, sga_ref, conv_ref, h_ref,
                            xn_s, h_s, yr_s):
    j = pl.program_id(0)

    def project():
        return jnp.dot(xn_s[...], w_ref[...], preferred_element_type=F32)

    @pl.when(j == 0)
    def _u_block():
        xv = x_ref[...]
        ms = jnp.mean(xv * xv, axis=-1, keepdims=True)
        xn_s[...] = ((xv * lax.rsqrt(ms + EPS)) * gpre_ref[...]).astype(BF16)
        u = project()
        taps = [sc_ref[:, k * D_RNN:(k + 1) * D_RNN] for k in range(CONV_WIDTH - 1)] + [u]
        for c in range(N_GATE_BLOCKS):
            c0 = c * GATE_BLOCK
            xc = cb_ref[:, c0:c0 + GATE_BLOCK]
            for tap in range(CONV_WIDTH):
                xc = xc + taps[tap][:, c0:c0 + GATE_BLOCK] * cw_ref[tap:tap + 1, c0:c0 + GATE_BLOCK]
            a, b = _lru_gates(xc, c, wa_ref, wx_ref, ba_ref, bx_ref, lam_ref)
            h = a * h0_ref[:, c0:c0 + GATE_BLOCK] + b
            h_s[:, c0:c0 + GATE_BLOCK] = h
            h_ref[:, c0:c0 + GATE_BLOCK] = h
        for k in range(1, CONV_WIDTH):
            conv_ref[:, (k - 1) * D_RNN:k * D_RNN] = taps[k]

    @pl.when(j == 1)
    def _z_rnn_block():
        z = project()
        hz = (h_s[...] * (z * _sigmoid(z))).astype(BF16)
        yr_s[...] = jnp.dot(hz, wro_ref[...], preferred_element_type=F32)

    @pl.when(j == 2)
    def _g_rnn_block():
        mixr_ref[...] = _sigmoid(project()) * yr_s[...]

    @pl.when(j == 3)
    def _g_attn_block():
        sga_ref[...] = _sigmoid(project())

    @pl.when(j >= 4)
    def _attn_blocks():
        att_ref[...] = project()


def _proj_rnn_sample(x, w_in, gpre, cw, cb, wa, wx, ba, bx, lam, wro, sconv, h0):
    N, D = x.shape
    const2 = lambda j: (0, 0)
    const3 = lambda j: (0, 0, 0)
    n_att = N_COL_BLOCKS - 4
    out_shape = (
        jax.ShapeDtypeStruct((N, n_att * TN), F32),
        jax.ShapeDtypeStruct((N, D), F32),
        jax.ShapeDtypeStruct((N, D), F32),
        jax.ShapeDtypeStruct((N, (CONV_WIDTH - 1) * D_RNN), F32),
        jax.ShapeDtypeStruct((N, D_RNN), F32),
    )
    out_specs = (
        pl.BlockSpec((N, TN), lambda j: (0, jnp.maximum(j - 4, 0))),
        pl.BlockSpec((N, D), const2),
        pl.BlockSpec((N, D), const2),
        pl.BlockSpec((N, (CONV_WIDTH - 1) * D_RNN), const2),
        pl.BlockSpec((N, D_RNN), const2),
    )
    in_specs = [
        pl.BlockSpec((N, D), const2),
        pl.BlockSpec((D, TN), lambda j: (0, j)),
        pl.BlockSpec((1, D), const2),
        pl.BlockSpec((CONV_WIDTH, D_RNN), const2),
        pl.BlockSpec((1, D_RNN), const2),
        pl.BlockSpec((N_GATE_BLOCKS, GATE_BLOCK, GATE_BLOCK), const3),
        pl.BlockSpec((N_GATE_BLOCKS, GATE_BLOCK, GATE_BLOCK), const3),
        pl.BlockSpec((1, D_RNN), const2),
        pl.BlockSpec((1, D_RNN), const2),
        pl.BlockSpec((1, D_RNN), const2),
        pl.BlockSpec((D_RNN, D), const2),
        pl.BlockSpec((N, (CONV_WIDTH - 1) * D_RNN), const2),
        pl.BlockSpec((N, D_RNN), const2),
    ]
    scratch = [pltpu.VMEM((N, D), BF16), pltpu.VMEM((N, D_RNN), F32), pltpu.VMEM((N, D), F32)]
    return pl.pallas_call(
        _proj_rnn_sample_kernel,
        grid=(N_COL_BLOCKS,),
        in_specs=in_specs,
        out_specs=out_specs,
        out_shape=out_shape,
        scratch_shapes=scratch,
        compiler_params=pltpu.CompilerParams(dimension_semantics=("arbitrary",), vmem_limit_bytes=VMEM_LIMIT),
        name="proj_rnn_sample",
    )(x, w_in, gpre, cw, cb, wa, wx, ba, bx, lam, wro, sconv, h0)


def _attn_kernel(q_ref, kc_ref, vc_ref, kp_ref, vp_ref, o_ref, st_ref, kcat_s, vcat_s, bias_s, *, group, chunk):
    dilation = ATTN_GROUPS[group][1]
    c = pl.program_id(2)
    nt = (((1,), (1,)), ((), ()))

    @pl.when((pl.program_id(0) == 0) & (pl.program_id(1) == 0) & (c == 0))
    def _init_bias():
        qi = lax.broadcasted_iota(jnp.int32, (Q_BLOCK, Q_BLOCK), 0)
        kj = lax.broadcasted_iota(jnp.int32, (Q_BLOCK, Q_BLOCK), 1)
        steps_prev = Q_BLOCK + qi - kj
        steps_cur = qi - kj
        dist_prev = (steps_prev * dilation).astype(F32)
        dist_cur = (steps_cur * dilation).astype(F32)
        for h in range(H_G):
            slope = float(_SLOPES[group, h])
            bias_s[h] = jnp.where(steps_cur >= 0, -slope * dist_cur, NEG_INF)
            bias_s[H_G + h] = jnp.where(steps_prev <= Q_BLOCK, -slope * dist_prev, NEG_INF)
            bias_s[2 * H_G + h] = jnp.full((Q_BLOCK, Q_BLOCK), NEG_INF, F32)

    kcat_s[0:Q_BLOCK, :] = kp_ref[...]
    kcat_s[Q_BLOCK:Q_BLOCK + chunk, :] = kc_ref[...]
    vcat_s[0:Q_BLOCK, :] = vp_ref[...]
    vcat_s[Q_BLOCK:Q_BLOCK + chunk, :] = vc_ref[...]

    lane = lax.broadcasted_iota(jnp.int32, (Q_BLOCK, LANES), 1)
    low = lane < HEAD_DIM
    lane2 = lax.broadcasted_iota(jnp.int32, (2 * Q_BLOCK, LANES), 1)
    low2 = lane2 < HEAD_DIM
    ones_lo = jnp.where(low2, 1.0, 0.0).astype(BF16)
    ones_hi = jnp.where(low2, 0.0, 1.0).astype(BF16)
    stat_lane = lax.broadcasted_iota(jnp.int32, (Q_BLOCK, STAT_LANES), 1)

    def block(i, carry):
        r0 = pl.multiple_of(i * Q_BLOCK, Q_BLOCK)
        first = jnp.where((c == 0) & (i == 0), H_G, 0)
        stats = jnp.zeros((Q_BLOCK, STAT_LANES), F32)
        for p in range(H_G // 2):
            sl = slice(LANES * p, LANES * (p + 1))
            qp = q_ref[pl.ds(r0, Q_BLOCK), sl] * (HEAD_DIM ** -0.5)
            kp = kcat_s[pl.ds(r0, 2 * Q_BLOCK), sl]
            vp = vcat_s[pl.ds(r0, 2 * Q_BLOCK), sl]
            es, ms = [], []
            for hh in range(2):
                h = 2 * p + hh
                msk = low if hh == 0 else jnp.logical_not(low)
                qm = jnp.where(msk, qp, jnp.zeros_like(qp))
                s = lax.dot_general(qm, kp, nt, preferred_element_type=F32)
                s_p = s[:, 0:Q_BLOCK] + bias_s[H_G + h + first]
                s_c = s[:, Q_BLOCK:2 * Q_BLOCK] + bias_s[h]
                mx = jnp.maximum(jnp.max(s_p, axis=-1, keepdims=True), jnp.max(s_c, axis=-1, keepdims=True))
                es.append(jnp.exp(s_p - mx).astype(BF16))
                es.append(jnp.exp(s_c - mx).astype(BF16))
                ms.append(mx)
            vm0 = jnp.where(low2, vp, jnp.zeros_like(vp))
            vm1 = jnp.where(low2, jnp.zeros_like(vp), vp)
            w = jnp.concatenate([jnp.concatenate([vm0, ones_lo], axis=1),
                                 jnp.concatenate([vm1, ones_hi], axis=1)], axis=0)
            acc = jnp.dot(jnp.concatenate(es, axis=1), w, preferred_element_type=F32)
            l_pair = acc[:, LANES:2 * LANES]
            o_ref[pl.ds(r0, Q_BLOCK), sl] = (acc[:, 0:LANES] / l_pair).astype(o_ref.dtype)
            lse_pair = jnp.where(low, ms[0], ms[1]) + jnp.log(l_pair)
            keep = ((stat_lane % (STAT_LANES // 2)) // STAT_LANES_PER_HEAD) == p
            stats = jnp.where(keep, lse_pair, stats)
        st_ref[pl.ds(r0, Q_BLOCK), :] = stats
        return carry

    lax.fori_loop(0, chunk // Q_BLOCK, block, 0)


def _attn_group(q, k, v, group):
    B, dil, L, _ = q.shape
    chunk = min(ATTN_CHUNK, L)
    assert L % chunk == 0 and chunk % Q_BLOCK == 0
    cur = lambda b, r, c: (b, r, c, 0)
    prev = lambda b, r, c: (b, r, jnp.maximum(c * (chunk // Q_BLOCK) - 1, 0), 0)
    blk = (None, None, chunk, D_ATTN)
    pblk = (None, None, Q_BLOCK, D_ATTN)
    return pl.pallas_call(
        functools.partial(_attn_kernel, group=group, chunk=chunk),
        grid=(B, dil, L // chunk),
        in_specs=[pl.BlockSpec(blk, cur), pl.BlockSpec(blk, cur), pl.BlockSpec(blk, cur),
                  pl.BlockSpec(pblk, prev), pl.BlockSpec(pblk, prev)],
        out_specs=[pl.BlockSpec(blk, cur), pl.BlockSpec((None, None, chunk, STAT_LANES), cur)],
        out_shape=[jax.ShapeDtypeStruct((B, dil, L, D_ATTN), BF16),
                   jax.ShapeDtypeStruct((B, dil, L, STAT_LANES), F32)],
        scratch_shapes=[pltpu.VMEM((Q_BLOCK + chunk, D_ATTN), BF16),
                        pltpu.VMEM((Q_BLOCK + chunk, D_ATTN), BF16),
                        pltpu.VMEM((3 * H_G, Q_BLOCK, Q_BLOCK), F32)],
        compiler_params=pltpu.CompilerParams(
            dimension_semantics=("arbitrary", "arbitrary", "arbitrary"), vmem_limit_bytes=VMEM_LIMIT),
        name=f"attn_group{group}",
    )(q, k, v, k, v)


def _attn_sample_kernel(qkv_ref, c0_ref, c1_ref, c2_ref, o_ref, *, bb):
    caches = (c0_ref, c1_ref, c2_ref)
    n_keys = ATTN_GROUPS[0][0]
    scale = HEAD_DIM ** -0.5
    key_i = lax.broadcasted_iota(jnp.int32, (n_keys, H_G, 1), 0)
    head = lax.broadcasted_iota(jnp.int32, (n_keys, H_G, 1), 1)

    def body(bi, carry):
        outs, lses = [], []
        for g, (window, dilation) in enumerate(ATTN_GROUPS):
            slope = jnp.zeros((n_keys, H_G, 1), F32)
            for h in range(H_G):
                slope = jnp.where(head == h, float(_SLOPES[g, h]), slope)
            bias = -slope * ((window // dilation - key_i) * dilation).astype(F32)
            q = qkv_ref[bi, 3 * g] * scale
            kn = qkv_ref[bi, 3 * g + 1]
            vn = qkv_ref[bi, 3 * g + 2]
            kc = caches[g][bi, :, 0]
            vc = caches[g][bi, :, 1]
            s = jnp.sum(kc * q[None], axis=-1, keepdims=True) + bias
            s_new = jnp.sum(kn * q, axis=-1, keepdims=True)
            mx = jnp.maximum(jnp.max(s, axis=0), s_new)
            e = jnp.exp(s - mx[None])
            e_new = jnp.exp(s_new - mx)
            l = jnp.sum(e, axis=0) + e_new
            o = jnp.sum(e * vc, axis=0) + e_new * vn
            outs.append(o / l)
            lses.append(mx + jnp.log(l))
        mm = jnp.maximum(jnp.maximum(lses[0], lses[1]), lses[2])
        ws = [jnp.exp(x - mm) for x in lses]
        den = ws[0] + ws[1] + ws[2]
        o_ref[bi] = (outs[0] * ws[0] + outs[1] * ws[1] + outs[2] * ws[2]) / den
        return carry

    lax.fori_loop(0, bb, body, 0)


def _attn_sample(qkv_s, caches, layer):
    N = qkv_s.shape[0]
    bb = SAMPLE_BATCH_BLOCK
    n_keys = ATTN_GROUPS[0][0]
    views = []
    for (window, dilation), c in zip(ATTN_GROUPS, caches):
        assert c.shape[2] == window and window // dilation == n_keys
        views.append(c.reshape(c.shape[0], N, n_keys, dilation, 2, H_G, HEAD_DIM))
    in_specs = [pl.BlockSpec((bb, 3 * N_GROUPS, H_G, HEAD_DIM), lambda i: (i, 0, 0, 0))]
    in_specs += [pl.BlockSpec((None, bb, n_keys, None, 2, H_G, HEAD_DIM), lambda i: (layer, i, 0, 0, 0, 0, 0))
                 for _ in views]
    return pl.pallas_call(
        functools.partial(_attn_sample_kernel, bb=bb),
        grid=(N // bb,),
        in_specs=in_specs,
        out_specs=pl.BlockSpec((bb, H_G, HEAD_DIM), lambda i: (i, 0, 0)),
        out_shape=jax.ShapeDtypeStruct((N, H_G, HEAD_DIM), F32),
        compiler_params=pltpu.CompilerParams(dimension_semantics=("arbitrary",), vmem_limit_bytes=VMEM_LIMIT),
        name="attn_sample",
    )(qkv_s, *views)


def _finish(att, z_ref, sga_ref, mixr_ref, x_ref, wao_ref, wo_ref, gpost_ref, y_ref):
    z = z_ref[...].astype(F32)
    ya_in = (att * (z * _sigmoid(z))).astype(BF16)
    ya = jnp.dot(ya_in, wao_ref[...], preferred_element_type=F32)
    mixed = mixr_ref[...].astype(F32) + sga_ref[...].astype(F32) * ya
    out = jnp.dot(mixed.astype(BF16), wo_ref[...], preferred_element_type=F32)
    ms = jnp.mean(out * out, axis=-1, keepdims=True)
    y_ref[...] = x_ref[...] + (out * lax.rsqrt(ms + EPS)) * gpost_ref[...]


def _out_kernel(att_ref, z_ref, sga_ref, mixr_ref, x_ref, wao_ref, wo_ref, gpost_ref, y_ref):
    _finish(att_ref[...].astype(F32), z_ref, sga_ref, mixr_ref, x_ref, wao_ref, wo_ref, gpost_ref, y_ref)


def _merge_out_kernel(o0_ref, o1_ref, o2_ref, s0_ref, s1_ref, s2_ref, z_ref, sga_ref, mixr_ref, x_ref,
                      wao_ref, wo_ref, gpost_ref, y_ref, nat1_s, nat2_s, st1_s, st2_s, *, tm):
    for o_ref, s_ref, nat_s, stn_s, dil in ((o1_ref, s1_ref, nat1_s, st1_s, ATTN_GROUPS[1][1]),
                                            (o2_ref, s2_ref, nat2_s, st2_s, ATTN_GROUPS[2][1])):
        p = tm // dil
        for r in range(dil):
            stn_s[pl.ds(r, p, stride=dil), :] = s_ref[r]
            for c in range(D_ATTN // LANES):
                nat_s[c, pl.ds(r, p, stride=dil), :] = o_ref[r, :, c * LANES:(c + 1) * LANES].astype(F32)

    lse = [s0_ref[...], st1_s[...], st2_s[...]]
    mm = jnp.maximum(jnp.maximum(lse[0], lse[1]), lse[2])
    ws = [jnp.exp(x - mm) for x in lse]
    den = ws[0] + ws[1] + ws[2]
    src = lax.broadcasted_iota(jnp.int32, (STAT_LANES, D_ATTN), 0)
    dst_head = lax.broadcasted_iota(jnp.int32, (STAT_LANES, D_ATTN), 1) // HEAD_DIM
    expand = jnp.where(src == (dst_head % 2) * (STAT_LANES // 2) + (dst_head // 2) * STAT_LANES_PER_HEAD,
                       1.0, 0.0).astype(BF16)
    outs = [o0_ref[...].astype(F32),
            jnp.concatenate([nat1_s[c] for c in range(D_ATTN // LANES)], axis=1),
            jnp.concatenate([nat2_s[c] for c in range(D_ATTN // LANES)], axis=1)]
    att = jnp.zeros((tm, D_ATTN), F32)
    for g in range(N_GROUPS):
        wexp = jnp.dot((ws[g] / den).astype(BF16), expand, preferred_element_type=F32)
        att = att + wexp * outs[g]
    _finish(att, z_ref, sga_ref, mixr_ref, x_ref, wao_ref, wo_ref, gpost_ref, y_ref)


def _merge_out_prompt(os, sts, zg, sga, mixr, x, wao, wo, gpost):
    B, S, D = x.shape
    tm = TM_OUT
    tile3 = lambda b, m: (b, m, 0)
    tile4 = lambda b, m: (b, 0, m, 0)
    const = lambda b, m: (0, 0)
    in_specs = []
    for (_, dil), width in [(g, D_ATTN) for g in ATTN_GROUPS] + [(g, STAT_LANES) for g in ATTN_GROUPS]:
        if dil == 1:
            in_specs.append(pl.BlockSpec((None, None, tm, width), tile4))
        else:
            in_specs.append(pl.BlockSpec((None, dil, tm // dil, width), tile4))
    in_specs += [
        pl.BlockSpec((None, tm, D_ATTN), tile3),
        pl.BlockSpec((None, tm, D), tile3),
        pl.BlockSpec((None, tm, D), tile3),
        pl.BlockSpec((None, tm, D), tile3),
        pl.BlockSpec((D_ATTN, D), const),
        pl.BlockSpec((D, D), const),
        pl.BlockSpec((1, D), const),
    ]
    scratch = [
        pltpu.VMEM((D_ATTN // LANES, tm, LANES), F32),
        pltpu.VMEM((D_ATTN // LANES, tm, LANES), F32),
        pltpu.VMEM((tm, STAT_LANES), F32),
        pltpu.VMEM((tm, STAT_LANES), F32),
    ]
    return pl.pallas_call(
        functools.partial(_merge_out_kernel, tm=tm),
        grid=(B, S // tm),
        in_specs=in_specs,
        out_specs=pl.BlockSpec((None, tm, D), tile3),
        out_shape=jax.ShapeDtypeStruct((B, S, D), F32),
        scratch_shapes=scratch,
        compiler_params=pltpu.CompilerParams(
            dimension_semantics=("arbitrary", "arbitrary"), vmem_limit_bytes=VMEM_LIMIT),
        name="merge_out_prompt",
    )(*os, *sts, zg, sga, mixr, x, wao, wo, gpost)


def _out_sample(att, z, sga, mixr, x, wao, wo, gpost):
    N, D = x.shape
    full = lambda shape: pl.BlockSpec(shape, lambda i: (0, 0))
    return pl.pallas_call(
        _out_kernel,
        grid=(1,),
        in_specs=[full((N, D_ATTN)), full((N, D_ATTN)), full((N, D)), full((N, D)), full((N, D)),
                  full((D_ATTN, D)), full((D, D)), full((1, D))],
        out_specs=full((N, D)),
        out_shape=jax.ShapeDtypeStruct((N, D), F32),
        compiler_params=pltpu.CompilerParams(dimension_semantics=("arbitrary",), vmem_limit_bytes=VMEM_LIMIT),
        name="out_sample",
    )(att, z, sga, mixr, x, wao, wo, gpost)


def _block_diag_chunks(w):
    per = GATE_BLOCK // RNN_BLOCK
    w = w.reshape(N_GATE_BLOCKS, per, RNN_BLOCK, RNN_BLOCK)
    eye = jnp.eye(per, dtype=w.dtype)
    dense = w[:, :, :, None, :] * eye[None, :, None, :, None]
    return dense.reshape(N_GATE_BLOCKS, GATE_BLOCK, GATE_BLOCK)


def _reorder_w_in(w_in):
    sizes = (D_RNN, D_RNN, N_GROUPS * D_ATTN, N_GROUPS * D_ATTN, N_GROUPS * D_ATTN, D_ATTN, D_MODEL, D_MODEL)
    starts = np.concatenate([[0], np.cumsum(sizes)[:-1]])
    u0, z0, q0, k0, v0, za0, gr0, ga0 = (int(s) for s in starts)
    pieces = [(u0, D_RNN), (z0, D_RNN), (gr0, D_MODEL), (ga0, D_MODEL), (za0, D_ATTN)]
    for g in range(N_GROUPS):
        pieces += [(q0 + g * D_ATTN, D_ATTN), (k0 + g * D_ATTN, D_ATTN), (v0 + g * D_ATTN, D_ATTN)]
    return jnp.concatenate([w_in[:, s:s + n] for s, n in pieces], axis=1).astype(BF16)


def _layer(layer, yp, ys, sconv, h0, caches, norm_pre, norm_post, w_in, conv_w, conv_b, lru_w_a, lru_b_a, lru_w_x,
           lru_b_x, lru_lambda, w_rnn_out, w_attn_out, w_out):
    B, S, D = yp.shape
    N = ys.shape[0]
    row = lambda v: v.reshape(1, -1)
    w_in_b = _reorder_w_in(w_in)
    wa = _block_diag_chunks(lru_w_a).astype(BF16)
    wx = _block_diag_chunks(lru_w_x).astype(BF16)
    wro = w_rnn_out.astype(BF16)
    wao = w_attn_out.astype(BF16)
    wo = w_out.astype(BF16)
    shared = (w_in_b, row(norm_pre), conv_w, row(conv_b), wa, wx, row(lru_b_a), row(lru_b_x), row(lru_lambda), wro)

    outs = _proj_rnn_prompt(yp, *shared)
    qkv, (zg, mixr, sga, kv0, kv1, kv2, conv_p, h_p) = outs[:9], outs[9:]
    os, sts = [], []
    for g in range(N_GROUPS):
        o, st = _attn_group(qkv[3 * g], qkv[3 * g + 1], qkv[3 * g + 2], g)
        os.append(o)
        sts.append(st)
    y_p = _merge_out_prompt(os, sts, zg, sga, mixr, yp, wao, wo, row(norm_post))
    kv_p = [kv.reshape(B, kv.shape[1], 2, H_G, HEAD_DIM) for kv in (kv0, kv1, kv2)]

    xs = ys.reshape(N, D)
    att_in, mixr_s, sga_s, conv_s, h_s = _proj_rnn_sample(
        xs, *shared, sconv.reshape(N, (CONV_WIDTH - 1) * D_RNN), h0)
    z_s = att_in[:, 0:D_ATTN]
    qkv_s = att_in[:, D_ATTN:].reshape(N, 3 * N_GROUPS, H_G, HEAD_DIM)
    att_s = _attn_sample(qkv_s, caches, layer).reshape(N, D_ATTN)
    y_s = _out_sample(att_s, z_s, sga_s, mixr_s, xs, wao, wo, row(norm_post))
    kv_s = [qkv_s[:, 3 * g + 1:3 * g + 3].reshape(N, 1, 2, H_G, HEAD_DIM) for g in range(N_GROUPS)]

    return (y_p, y_s.reshape(N, 1, D), conv_p, conv_s.reshape(N, CONV_WIDTH - 1, D_RNN),
            h_p.reshape(B, D_RNN), h_s, kv_p, kv_s)


def kernel(x_prompt, x_sample, state_conv, state_h, cache_kv_w128, cache_kv_w512, cache_kv_w2048, norm_pre, norm_post, w_in, conv_w, conv_b, lru_w_a, lru_b_a, lru_w_x, lru_b_x, lru_lambda, w_rnn_out, w_attn_out, w_out):
    depth = norm_pre.shape[0]
    caches = (cache_kv_w128, cache_kv_w512, cache_kv_w2048)
    yp, ys = x_prompt, x_sample
    conv_p, conv_s, h_p, h_s = [], [], [], []
    kvp = ([], [], [])
    kvs = ([], [], [])
    for l in range(depth):
        yp, ys, cp, cs, hp, hs, kv_p, kv_s = _layer(
            l, yp, ys, state_conv[l], state_h[l], caches,
            norm_pre[l], norm_post[l], w_in[l], conv_w[l], conv_b[l], lru_w_a[l], lru_b_a[l], lru_w_x[l],
            lru_b_x[l], lru_lambda[l], w_rnn_out[l], w_attn_out[l], w_out[l])
        conv_p.append(cp)
        conv_s.append(cs)
        h_p.append(hp)
        h_s.append(hs)
        for g in range(N_GROUPS):
            kvp[g].append(kv_p[g])
            kvs[g].append(kv_s[g])
    return (yp, ys, jnp.stack(conv_p), jnp.stack(conv_s), jnp.stack(h_p), jnp.stack(h_s),
            jnp.stack(kvp[0]), jnp.stack(kvs[0]), jnp.stack(kvp[1]), jnp.stack(kvs[1]),
            jnp.stack(kvp[2]), jnp.stack(kvs[2]))
```

```python
import functools

import numpy as np
import jax
import jax.numpy as jnp
from jax import lax
from jax.experimental import pallas as pl
from jax.experimental.pallas import tpu as pltpu

F32 = jnp.float32
BF16 = jnp.bfloat16

D_MODEL = 1024
D_RNN = 1024
N_RNN_BLOCKS = 16
RNN_BLOCK = D_RNN // N_RNN_BLOCKS
CONV_WIDTH = 4
LRU_C = 8.0
HEAD_DIM = 64
H_G = 8
ATTN_GROUPS = ((128, 1), (512, 4), (2048, 16))
N_GROUPS = 3
D_ATTN = H_G * HEAD_DIM
Q_BLOCK = 128
ALIBI_MAX = 8.0
EPS = 1e-6
NEG_INF = -1e30
D_IN = 2 * D_RNN + 3 * N_GROUPS * D_ATTN + D_ATTN + 2 * D_MODEL

LANES = 128
TN = 1024
N_COL_BLOCKS = D_IN // TN
GATE_BLOCK = 256
N_GATE_BLOCKS = D_RNN // GATE_BLOCK
STAT_LANES = LANES
STAT_LANES_PER_HEAD = STAT_LANES // H_G

TM_PROMPT = 512
TM_OUT = 512
ATTN_CHUNK = 512
ROW_CHUNK = 64
GATE_ROWS = 128
VMEM_LIMIT = 56 * 1024 * 1024


def _alibi_slopes():
    n = N_GROUPS * H_G
    s = np.float32(2.0) ** (np.float32(-ALIBI_MAX) * np.arange(1, n + 1, dtype=np.float32) / np.float32(n))
    return s.reshape(N_GROUPS, H_G)


_SLOPES = _alibi_slopes()


def _stat_lane(h):
    return (h % 2) * (STAT_LANES // 2) + (h // 2) * STAT_LANES_PER_HEAD


def _softplus(y):
    return jnp.maximum(y, 0.0) + jnp.log1p(jnp.exp(-jnp.abs(y)))


def _sigmoid(x):
    return 1.0 / (1.0 + jnp.exp(-x))


def _for_rows(n_rows, chunk, fn):
    if n_rows <= chunk:
        fn(0)
        return

    def body(c, carry):
        fn(pl.multiple_of(c * chunk, chunk))
        return carry

    lax.fori_loop(0, n_rows // chunk, body, 0)


def _lru_gates(xc, c, wa_ref, wx_ref, ba_ref, bx_ref, lam_ref):
    c0 = c * GATE_BLOCK
    xcb = xc.astype(BF16)
    r = _sigmoid(jnp.dot(xcb, wa_ref[c], preferred_element_type=F32) + ba_ref[:, c0:c0 + GATE_BLOCK])
    i = _sigmoid(jnp.dot(xcb, wx_ref[c], preferred_element_type=F32) + bx_ref[:, c0:c0 + GATE_BLOCK])
    log_a = (-LRU_C * r) * _softplus(-lam_ref[:, c0:c0 + GATE_BLOCK])
    a = jnp.exp(log_a)
    b = jnp.sqrt(1.0 - jnp.exp(2.0 * log_a)) * i * xc
    return a, b


def _proj_rnn_kernel(x_ref, w_ref, gpre_ref, cw_ref, cb_ref, wa_ref, wx_ref, ba_ref, bx_ref, lam_ref, wro_ref,
                     q0_ref, k0_ref, v0_ref, q1_ref, k1_ref, v1_ref, q2_ref, k2_ref, v2_ref,
                     zg_ref, mixr_ref, sga_ref, kv0_ref, kv1_ref, kv2_ref, conv_ref, h_ref,
                     xn_s, xnf_s, xn4_s, xn16_s, ext_s, a_s, b_s, hz_s, yr_s, acc_s, tail_s, carry_s,
                     *, tm, n_tiles):
    m = pl.program_id(1)
    j = pl.program_id(2)
    last_tile = m == n_tiles - 1
    in_kv2 = m >= n_tiles - ATTN_GROUPS[2][0] // tm
    half = TN // 2
    d1, d2 = ATTN_GROUPS[1][1], ATTN_GROUPS[2][1]

    def project(lhs_s):
        return jnp.dot(lhs_s[...], w_ref[...], preferred_element_type=F32)

    def emit(out_ref, c0, dilation):
        p = tm // dilation
        for r in range(dilation):
            out_ref[r] = acc_s[r * p:(r + 1) * p, c0:c0 + half].astype(BF16)

    def natural_tail(out_ref, out_c0, c0, dilation):
        p = tm // dilation
        for c in range(half // LANES):
            for r in range(dilation):
                tail_s[c, pl.ds(r, p, stride=dilation), :] = \
                    acc_s[r * p:(r + 1) * p, c0 + c * LANES:c0 + (c + 1) * LANES]
            out_ref[:, out_c0 + c * LANES:out_c0 + (c + 1) * LANES] = tail_s[c]

    @pl.when(j == 0)
    def _u_block():
        def norm(r0):
            xv = x_ref[pl.ds(r0, ROW_CHUNK), :]
            ms = jnp.mean(xv * xv, axis=-1, keepdims=True)
            xn = (xv * lax.rsqrt(ms + EPS)) * gpre_ref[...]
            xn_s[pl.ds(r0, ROW_CHUNK), :] = xn.astype(BF16)
            for c in range(D_MODEL // LANES):
                xnf_s[c, pl.ds(r0, ROW_CHUNK), :] = xn[:, c * LANES:(c + 1) * LANES]

        _for_rows(tm, ROW_CHUNK, norm)

        for dil, dst in ((d1, xn4_s), (d2, xn16_s)):
            p = tm // dil
            for r in range(dil):
                for c in range(D_MODEL // LANES):
                    dst[r * p:(r + 1) * p, c * LANES:(c + 1) * LANES] = \
                        xnf_s[c, pl.ds(r, p, stride=dil), :].astype(BF16)

        @pl.when(m == 0)
        def _reset():
            ext_s[0:8, :] = jnp.zeros((8, D_RNN), F32)
            carry_s[...] = jnp.zeros((1, D_RNN), F32)

        ext_s[8:8 + tm, :] = project(xn_s)

        for c in range(N_GATE_BLOCKS):
            c0 = c * GATE_BLOCK
            for rc in range(tm // GATE_ROWS):
                r0 = rc * GATE_ROWS
                xc = cb_ref[:, c0:c0 + GATE_BLOCK]
                for tap in range(CONV_WIDTH):
                    shift = CONV_WIDTH - 1 - tap
                    xc = xc + ext_s[8 + r0 - shift:8 + r0 - shift + GATE_ROWS, c0:c0 + GATE_BLOCK] * \
                        cw_ref[tap:tap + 1, c0:c0 + GATE_BLOCK]
                a, b = _lru_gates(xc, c, wa_ref, wx_ref, ba_ref, bx_ref, lam_ref)
                a_s[r0:r0 + GATE_ROWS, c0:c0 + GATE_BLOCK] = a
                b_s[r0:r0 + GATE_ROWS, c0:c0 + GATE_BLOCK] = b

        row = lax.broadcasted_iota(jnp.int32, (8, D_RNN), 0)

        def scan(g, carry):
            r0 = pl.multiple_of(g * 8, 8)
            av = a_s[pl.ds(r0, 8), :]
            bv = b_s[pl.ds(r0, 8), :]
            for s in (1, 2, 4):
                keep = row >= s
                bv = jnp.where(keep, av * pltpu.roll(bv, s, axis=0) + bv, bv)
                av = jnp.where(keep, av * pltpu.roll(av, s, axis=0), av)
            h = av * carry + bv
            b_s[pl.ds(r0, 8), :] = h
            return h[7:8, :]

        carry = lax.fori_loop(0, tm // 8, scan, carry_s[...])
        carry_s[...] = carry
        h_ref[...] = carry
        tail = ext_s[tm + 5:tm + 8, :]
        conv_ref[...] = tail
        ext_s[5:8, :] = tail

    @pl.when(j == 1)
    def _z_rnn_block():
        acc_s[...] = project(xn_s)

        def gate(r0):
            z = acc_s[pl.ds(r0, ROW_CHUNK), :]
            hz_s[pl.ds(r0, ROW_CHUNK), :] = (b_s[pl.ds(r0, ROW_CHUNK), :] * (z * _sigmoid(z))).astype(BF16)

        _for_rows(tm, ROW_CHUNK, gate)
        yr_s[...] = jnp.dot(hz_s[...], wro_ref[...], preferred_element_type=F32)

    @pl.when(j == 2)
    def _g_rnn_block():
        acc_s[...] = project(xn_s)

        def gate(r0):
            g = acc_s[pl.ds(r0, ROW_CHUNK), :]
            mixr_ref[pl.ds(r0, ROW_CHUNK), :] = (_sigmoid(g) * yr_s[pl.ds(r0, ROW_CHUNK), :]).astype(BF16)

        _for_rows(tm, ROW_CHUNK, gate)

    @pl.when(j == 3)
    def _g_attn_block():
        acc_s[...] = project(xn_s)

        def gate(r0):
            sga_ref[pl.ds(r0, ROW_CHUNK), :] = _sigmoid(acc_s[pl.ds(r0, ROW_CHUNK), :]).astype(BF16)

        _for_rows(tm, ROW_CHUNK, gate)

    @pl.when(j == 4)
    def _zattn_q0_block():
        acc_s[...] = project(xn_s)
        zg_ref[...] = acc_s[:, 0:half].astype(BF16)
        emit(q0_ref, half, 1)

    @pl.when(j == 5)
    def _k0_v0_block():
        acc_s[...] = project(xn_s)
        emit(k0_ref, 0, 1)
        emit(v0_ref, half, 1)

        @pl.when(last_tile)
        def _():
            kv0_ref[...] = acc_s[tm - ATTN_GROUPS[0][0]:tm, :]

    @pl.when(j == 6)
    def _q1_k1_block():
        acc_s[...] = project(xn4_s)
        emit(q1_ref, 0, d1)
        emit(k1_ref, half, d1)

        @pl.when(last_tile)
        def _():
            natural_tail(kv1_ref, 0, half, d1)

    @pl.when(j == 7)
    def _v1_q2_block():
        acc_s[:, 0:half] = jnp.dot(xn4_s[...], w_ref[:, 0:half], preferred_element_type=F32)
        acc_s[:, half:TN] = jnp.dot(xn16_s[...], w_ref[:, half:TN], preferred_element_type=F32)
        emit(v1_ref, 0, d1)
        emit(q2_ref, half, d2)

        @pl.when(last_tile)
        def _():
            natural_tail(kv1_ref, half, 0, d1)

    @pl.when(j == 8)
    def _k2_v2_block():
        acc_s[...] = project(xn16_s)
        emit(k2_ref, 0, d2)
        emit(v2_ref, half, d2)

        @pl.when(in_kv2)
        def _():
            natural_tail(kv2_ref, 0, 0, d2)
            natural_tail(kv2_ref, half, half, d2)


def _proj_rnn_prompt(x, w_in, gpre, cw, cb, wa, wx, ba, bx, lam, wro):
    B, S, D = x.shape
    tm = TM_PROMPT
    n_tiles = S // tm
    m0 = n_tiles - ATTN_GROUPS[2][0] // tm
    assert S % tm == 0 and tm == ATTN_GROUPS[1][0] and ATTN_GROUPS[2][0] % tm == 0 and S >= ATTN_GROUPS[2][0]

    const2 = lambda b, m, j: (0, 0)
    const3 = lambda b, m, j: (0, 0, 0)
    tile3 = lambda b, m, j: (b, m, 0)
    tile4 = lambda b, m, j: (b, 0, m, 0)
    per_seq = lambda b, m, j: (b, 0, 0)

    qkv_shapes, qkv_specs = [], []
    for _, dil in ATTN_GROUPS:
        for _ in range(3):
            qkv_shapes.append(jax.ShapeDtypeStruct((B, dil, S // dil, D_ATTN), BF16))
            qkv_specs.append(pl.BlockSpec((None, dil, tm // dil, D_ATTN), tile4))
    out_shape = tuple(qkv_shapes) + (
        jax.ShapeDtypeStruct((B, S, D_ATTN), BF16),
        jax.ShapeDtypeStruct((B, S, D), BF16),
        jax.ShapeDtypeStruct((B, S, D), BF16),
        jax.ShapeDtypeStruct((B, ATTN_GROUPS[0][0], 2 * D_ATTN), F32),
        jax.ShapeDtypeStruct((B, ATTN_GROUPS[1][0], 2 * D_ATTN), F32),
        jax.ShapeDtypeStruct((B, ATTN_GROUPS[2][0], 2 * D_ATTN), F32),
        jax.ShapeDtypeStruct((B, CONV_WIDTH - 1, D_RNN), F32),
        jax.ShapeDtypeStruct((B, 1, D_RNN), F32),
    )
    out_specs = tuple(qkv_specs) + (
        pl.BlockSpec((None, tm, D_ATTN), tile3),
        pl.BlockSpec((None, tm, D), tile3),
        pl.BlockSpec((None, tm, D), tile3),
        pl.BlockSpec((None, ATTN_GROUPS[0][0], 2 * D_ATTN), per_seq),
        pl.BlockSpec((None, ATTN_GROUPS[1][0], 2 * D_ATTN), per_seq),
        pl.BlockSpec((None, tm, 2 * D_ATTN), lambda b, m, j: (b, jnp.maximum(m - m0, 0), 0)),
        pl.BlockSpec((None, CONV_WIDTH - 1, D_RNN), per_seq),
        pl.BlockSpec((None, 1, D_RNN), per_seq),
    )
    in_specs = [
        pl.BlockSpec((None, tm, D), tile3),
        pl.BlockSpec((D, TN), lambda b, m, j: (0, j)),
        pl.BlockSpec((1, D), const2),
        pl.BlockSpec((CONV_WIDTH, D_RNN), const2),
        pl.BlockSpec((1, D_RNN), const2),
        pl.BlockSpec((N_GATE_BLOCKS, GATE_BLOCK, GATE_BLOCK), const3),
        pl.BlockSpec((N_GATE_BLOCKS, GATE_BLOCK, GATE_BLOCK), const3),
        pl.BlockSpec((1, D_RNN), const2),
        pl.BlockSpec((1, D_RNN), const2),
        pl.BlockSpec((1, D_RNN), const2),
        pl.BlockSpec((D_RNN, D), const2),
    ]
    scratch = [
        pltpu.VMEM((tm, D), BF16),
        pltpu.VMEM((D // LANES, tm, LANES), F32),
        pltpu.VMEM((tm, D), BF16),
        pltpu.VMEM((tm, D), BF16),
        pltpu.VMEM((tm + 8, D_RNN), F32),
        pltpu.VMEM((tm, D_RNN), F32),
        pltpu.VMEM((tm, D_RNN), F32),
        pltpu.VMEM((tm, D_RNN), BF16),
        pltpu.VMEM((tm, D), F32),
        pltpu.VMEM((tm, TN), F32),
        pltpu.VMEM((D_ATTN // LANES, tm, LANES), F32),
        pltpu.VMEM((1, D_RNN), F32),
    ]
    return pl.pallas_call(
        functools.partial(_proj_rnn_kernel, tm=tm, n_tiles=n_tiles),
        grid=(B, n_tiles, N_COL_BLOCKS),
        in_specs=in_specs,
        out_specs=out_specs,
        out_shape=out_shape,
        scratch_shapes=scratch,
        compiler_params=pltpu.CompilerParams(
            dimension_semantics=("arbitrary", "arbitrary", "arbitrary"), vmem_limit_bytes=VMEM_LIMIT),
        name="proj_rnn_prompt",
    )(x, w_in, gpre, cw, cb, wa, wx, ba, bx, lam, wro)


def _proj_rnn_sample_kernel(x_ref, w_ref, gpre_ref, cw_ref, cb_ref, wa_ref, wx_ref, ba_ref, bx_ref, lam_ref, wro_ref,
                            sc_ref, h0_ref,
                            att_ref, mixr_ref, sga_ref, conv_ref, h_ref,
                            xn_s, h_s, yr_s):
    j = pl.program_id(0)

    def project():
        return jnp.dot(xn_s[...], w_ref[...], preferred_element_type=F32)

    @pl.when(j == 0)
    def _u_block():
        xv = x_ref[...]
        ms = jnp.mean(xv * xv, axis=-1, keepdims=True)
        xn_s[...] = ((xv * lax.rsqrt(ms + EPS)) * gpre_ref[...]).astype(BF16)
        u = project()
        taps = [sc_ref[:, k * D_RNN:(k + 1) * D_RNN] for k in range(CONV_WIDTH - 1)] + [u]
        for c in range(N_GATE_BLOCKS):
            c0 = c * GATE_BLOCK
            xc = cb_ref[:, c0:c0 + GATE_BLOCK]
            for tap in range(CONV_WIDTH):
                xc = xc + taps[tap][:, c0:c0 + GATE_BLOCK] * cw_ref[tap:tap + 1, c0:c0 + GATE_BLOCK]
            a, b = _lru_gates(xc, c, wa_ref, wx_ref, ba_ref, bx_ref, lam_ref)
            h = a * h0_ref[:, c0:c0 + GATE_BLOCK] + b
            h_s[:, c0:c0 + GATE_BLOCK] = h
            h_ref[:, c0:c0 + GATE_BLOCK] = h
        for k in range(1, CONV_WIDTH):
            conv_ref[:, (k - 1) * D_RNN:k * D_RNN] = taps[k]

    @pl.when(j == 1)
    def _z_rnn_block():
        z = project()
        hz = (h_s[...] * (z * _sigmoid(z))).astype(BF16)
        yr_s[...] = jnp.dot(hz, wro_ref[...], preferred_element_type=F32)

    @pl.when(j == 2)
    def _g_rnn_block():
        mixr_ref[...] = _sigmoid(project()) * yr_s[...]

    @pl.when(j == 3)
    def _g_attn_block():
        sga_ref[...] = _sigmoid(project())

    @pl.when(j >= 4)
    def _attn_blocks():
        att_ref[...] = project()


def _proj_rnn_sample(x, w_in, gpre, cw, cb, wa, wx, ba, bx, lam, wro, sconv, h0):
    N, D = x.shape
    const2 = lambda j: (0, 0)
    const3 = lambda j: (0, 0, 0)
    n_att = N_COL_BLOCKS - 4
    out_shape = (
        jax.ShapeDtypeStruct((N, n_att * TN), F32),
        jax.ShapeDtypeStruct((N, D), F32),
        jax.ShapeDtypeStruct((N, D), F32),
        jax.ShapeDtypeStruct((N, (CONV_WIDTH - 1) * D_RNN), F32),
        jax.ShapeDtypeStruct((N, D_RNN), F32),
    )
    out_specs = (
        pl.BlockSpec((N, TN), lambda j: (0, jnp.maximum(j - 4, 0))),
        pl.BlockSpec((N, D), const2),
        pl.BlockSpec((N, D), const2),
        pl.BlockSpec((N, (CONV_WIDTH - 1) * D_RNN), const2),
        pl.BlockSpec((N, D_RNN), const2),
    )
    in_specs = [
        pl.BlockSpec((N, D), const2),
        pl.BlockSpec((D, TN), lambda j: (0, j)),
        pl.BlockSpec((1, D), const2),
        pl.BlockSpec((CONV_WIDTH, D_RNN), const2),
        pl.BlockSpec((1, D_RNN), const2),
        pl.BlockSpec((N_GATE_BLOCKS, GATE_BLOCK, GATE_BLOCK), const3),
        pl.BlockSpec((N_GATE_BLOCKS, GATE_BLOCK, GATE_BLOCK), const3),
        pl.BlockSpec((1, D_RNN), const2),
        pl.BlockSpec((1, D_RNN), const2),
        pl.BlockSpec((1, D_RNN), const2),
        pl.BlockSpec((D_RNN, D), const2),
        pl.BlockSpec((N, (CONV_WIDTH - 1) * D_RNN), const2),
        pl.BlockSpec((N, D_RNN), const2),
    ]
    scratch = [pltpu.VMEM((N, D), BF16), pltpu.VMEM((N, D_RNN), F32), pltpu.VMEM((N, D), F32)]
    return pl.pallas_call(
        _proj_rnn_sample_kernel,
        grid=(N_COL_BLOCKS,),
        in_specs=in_specs,
        out_specs=out_specs,
        out_shape=out_shape,
        scratch_shapes=scratch,
        compiler_params=pltpu.CompilerParams(dimension_semantics=("arbitrary",), vmem_limit_bytes=VMEM_LIMIT),
        name="proj_rnn_sample",
    )(x, w_in, gpre, cw, cb, wa, wx, ba, bx, lam, wro, sconv, h0)


def _attn_kernel(q_ref, kc_ref, vc_ref, kp_ref, vp_ref, o_ref, st_ref, kcat_s, vcat_s, bias_s, *, group, chunk):
    dilation = ATTN_GROUPS[group][1]
    c = pl.program_id(2)
    nt = (((1,), (1,)), ((), ()))

    @pl.when((pl.program_id(0) == 0) & (pl.program_id(1) == 0) & (c == 0))
    def _init_bias():
        qi = lax.broadcasted_iota(jnp.int32, (Q_BLOCK, Q_BLOCK), 0)
        kj = lax.broadcasted_iota(jnp.int32, (Q_BLOCK, Q_BLOCK), 1)
        steps_prev = Q_BLOCK + qi - kj
        steps_cur = qi - kj
        dist_prev = (steps_prev * dilation).astype(F32)
        dist_cur = (steps_cur * dilation).astype(F32)
        for h in range(H_G):
            slope = float(_SLOPES[group, h])
            bias_s[h] = jnp.where(steps_cur >= 0, -slope * dist_cur, NEG_INF)
            bias_s[H_G + h] = jnp.where(steps_prev <= Q_BLOCK, -slope * dist_prev, NEG_INF)
            bias_s[2 * H_G + h] = jnp.full((Q_BLOCK, Q_BLOCK), NEG_INF, F32)

    kcat_s[0:Q_BLOCK, :] = kp_ref[...]
    kcat_s[Q_BLOCK:Q_BLOCK + chunk, :] = kc_ref[...]
    vcat_s[0:Q_BLOCK, :] = vp_ref[...]
    vcat_s[Q_BLOCK:Q_BLOCK + chunk, :] = vc_ref[...]

    lane = lax.broadcasted_iota(jnp.int32, (Q_BLOCK, LANES), 1)
    low = lane < HEAD_DIM
    lane2 = lax.broadcasted_iota(jnp.int32, (2 * Q_BLOCK, LANES), 1)
    low2 = lane2 < HEAD_DIM
    ones_lo = jnp.where(low2, 1.0, 0.0).astype(BF16)
    ones_hi = jnp.where(low2, 0.0, 1.0).astype(BF16)
    stat_lane = lax.broadcasted_iota(jnp.int32, (Q_BLOCK, STAT_LANES), 1)

    def block(i, carry):
        r0 = pl.multiple_of(i * Q_BLOCK, Q_BLOCK)
        first = jnp.where((c == 0) & (i == 0), H_G, 0)
        stats = jnp.zeros((Q_BLOCK, STAT_LANES), F32)
        for p in range(H_G // 2):
            sl = slice(LANES * p, LANES * (p + 1))
            qp = q_ref[pl.ds(r0, Q_BLOCK), sl] * (HEAD_DIM ** -0.5)
            kp = kcat_s[pl.ds(r0, 2 * Q_BLOCK), sl]
            vp = vcat_s[pl.ds(r0, 2 * Q_BLOCK), sl]
            es, ms = [], []
            for hh in range(2):
                h = 2 * p + hh
                msk = low if hh == 0 else jnp.logical_not(low)
                qm = jnp.where(msk, qp, jnp.zeros_like(qp))
                s = lax.dot_general(qm, kp, nt, preferred_element_type=F32)
                s_p = s[:, 0:Q_BLOCK] + bias_s[H_G + h + first]
                s_c = s[:, Q_BLOCK:2 * Q_BLOCK] + bias_s[h]
                mx = jnp.maximum(jnp.max(s_p, axis=-1, keepdims=True), jnp.max(s_c, axis=-1, keepdims=True))
                es.append(jnp.exp(s_p - mx).astype(BF16))
                es.append(jnp.exp(s_c - mx).astype(BF16))
                ms.append(mx)
            vm0 = jnp.where(low2, vp, jnp.zeros_like(vp))
            vm1 = jnp.where(low2, jnp.zeros_like(vp), vp)
            w = jnp.concatenate([jnp.concatenate([vm0, ones_lo], axis=1),
                                 jnp.concatenate([vm1, ones_hi], axis=1)], axis=0)
            acc = jnp.dot(jnp.concatenate(es, axis=1), w, preferred_element_type=F32)
            l_pair = acc[:, LANES:2 * LANES]
            o_ref[pl.ds(r0, Q_BLOCK), sl] = (acc[:, 0:LANES] / l_pair).astype(o_ref.dtype)
            lse_pair = jnp.where(low, ms[0], ms[1]) + jnp.log(l_pair)
            keep = ((stat_lane % (STAT_LANES // 2)) // STAT_LANES_PER_HEAD) == p
            stats = jnp.where(keep, lse_pair, stats)
        st_ref[pl.ds(r0, Q_BLOCK), :] = stats
        return carry

    lax.fori_loop(0, chunk // Q_BLOCK, block, 0)


def _attn_group(q, k, v, group):
    B, dil, L, _ = q.shape
    chunk = min(ATTN_CHUNK, L)
    assert L % chunk == 0 and chunk % Q_BLOCK == 0
    cur = lambda b, r, c: (b, r, c, 0)
    prev = lambda b, r, c: (b, r, jnp.maximum(c * (chunk // Q_BLOCK) - 1, 0), 0)
    blk = (None, None, chunk, D_ATTN)
    pblk = (None, None, Q_BLOCK, D_ATTN)
    return pl.pallas_call(
        functools.partial(_attn_kernel, group=group, chunk=chunk),
        grid=(B, dil, L // chunk),
        in_specs=[pl.BlockSpec(blk, cur), pl.BlockSpec(blk, cur), pl.BlockSpec(blk, cur),
                  pl.BlockSpec(pblk, prev), pl.BlockSpec(pblk, prev)],
        out_specs=[pl.BlockSpec(blk, cur), pl.BlockSpec((None, None, chunk, STAT_LANES), cur)],
        out_shape=[jax.ShapeDtypeStruct((B, dil, L, D_ATTN), BF16),
                   jax.ShapeDtypeStruct((B, dil, L, STAT_LANES), F32)],
        scratch_shapes=[pltpu.VMEM((Q_BLOCK + chunk, D_ATTN), BF16),
                        pltpu.VMEM((Q_BLOCK + chunk, D_ATTN), BF16),
                        pltpu.VMEM((3 * H_G, Q_BLOCK, Q_BLOCK), F32)],
        compiler_params=pltpu.CompilerParams(
            dimension_semantics=("arbitrary", "arbitrary", "arbitrary"), vmem_limit_bytes=VMEM_LIMIT),
        name=f"attn_group{group}",
    )(q, k, v, k, v)


def _attn_sample_kernel(qkv_ref, c0_ref, c1_ref, c2_ref, o_ref):
    caches = (c0_ref, c1_ref, c2_ref)
    scale = HEAD_DIM ** -0.5
    ri = lax.broadcasted_iota(jnp.int32, (HEAD_DIM, HEAD_DIM), 0)
    ci = lax.broadcasted_iota(jnp.int32, (HEAD_DIM, HEAD_DIM), 1)
    eye = ri == ci

    def to_col(row):
        return jnp.sum(jnp.where(eye, jnp.broadcast_to(row, (HEAD_DIM, HEAD_DIM)), 0.0), axis=-1, keepdims=True)

    def to_row(col):
        return jnp.sum(jnp.where(eye, jnp.broadcast_to(col, (HEAD_DIM, HEAD_DIM)), 0.0), axis=0, keepdims=True)

    outs = [[None] * N_GROUPS for _ in range(H_G)]
    lses = [[None] * N_GROUPS for _ in range(H_G)]
    for g, (window, dilation) in enumerate(ATTN_GROUPS):
        c_ref = caches[g]
        pos = lax.broadcasted_iota(jnp.int32, (1, window), 1)
        on_stride = (pos % dilation) == 0
        dist = (window - pos).astype(F32)
        q = qkv_ref[3 * g] * scale
        s_new_all = jnp.sum(qkv_ref[3 * g + 1] * q, axis=-1, keepdims=True)
        for h in range(H_G):
            bias = jnp.where(on_stride, -float(_SLOPES[g, h]) * dist, NEG_INF)
            q_col = to_col(q[h:h + 1, :])
            v_col = to_col(qkv_ref[3 * g + 2, h:h + 1, :])
            s = jnp.sum(c_ref[0, h] * q_col, axis=0, keepdims=True) + bias
            s_new = s_new_all[h:h + 1, :]
            mx = jnp.maximum(jnp.max(s, axis=-1, keepdims=True), s_new)
            e = jnp.exp(s - mx)
            e_new = jnp.exp(s_new - mx)
            l = jnp.sum(e, axis=-1, keepdims=True) + e_new
            o = jnp.sum(c_ref[1, h] * e, axis=-1, keepdims=True) + e_new * v_col
            outs[h][g] = o / l
            lses[h][g] = mx + jnp.log(l)
    for h in range(H_G):
        mm = jnp.maximum(jnp.maximum(lses[h][0], lses[h][1]), lses[h][2])
        ws = [jnp.exp(x - mm) for x in lses[h]]
        den = ws[0] + ws[1] + ws[2]
        col = (outs[h][0] * ws[0] + outs[h][1] * ws[1] + outs[h][2] * ws[2]) / den
        o_ref[h:h + 1, :] = to_row(col)


def _attn_sample(qkv_s, caches, layer):
    N = qkv_s.shape[0]
    views = []
    for (window, _), c in zip(ATTN_GROUPS, caches):
        assert c.shape[2] == window
        views.append(jnp.transpose(c, (0, 1, 3, 4, 5, 2)))
    in_specs = [pl.BlockSpec((None, 3 * N_GROUPS, H_G, HEAD_DIM), lambda i: (i, 0, 0, 0))]
    in_specs += [pl.BlockSpec((None, None, 2, H_G, HEAD_DIM, window), lambda i: (layer, i, 0, 0, 0, 0))
                 for window, _ in ATTN_GROUPS]
    return pl.pallas_call(
        _attn_sample_kernel,
        grid=(N,),
        in_specs=in_specs,
        out_specs=pl.BlockSpec((None, H_G, HEAD_DIM), lambda i: (i, 0, 0)),
        out_shape=jax.ShapeDtypeStruct((N, H_G, HEAD_DIM), F32),
        compiler_params=pltpu.CompilerParams(dimension_semantics=("arbitrary",), vmem_limit_bytes=VMEM_LIMIT),
        name="attn_sample",
    )(qkv_s, *views)


def _finish(att, z_ref, sga_ref, mixr_ref, x_ref, wao_ref, wo_ref, gpost_ref, y_ref):
    z = z_ref[...].astype(F32)
    ya_in = (att * (z * _sigmoid(z))).astype(BF16)
    ya = jnp.dot(ya_in, wao_ref[...], preferred_element_type=F32)
    mixed = mixr_ref[...].astype(F32) + sga_ref[...].astype(F32) * ya
    out = jnp.dot(mixed.astype(BF16), wo_ref[...], preferred_element_type=F32)
    ms = jnp.mean(out * out, axis=-1, keepdims=True)
    y_ref[...] = x_ref[...] + (out * lax.rsqrt(ms + EPS)) * gpost_ref[...]


def _out_kernel(att_ref, z_ref, sga_ref, mixr_ref, x_ref, wao_ref, wo_ref, gpost_ref, y_ref):
    _finish(att_ref[...].astype(F32), z_ref, sga_ref, mixr_ref, x_ref, wao_ref, wo_ref, gpost_ref, y_ref)


def _merge_out_kernel(o0_ref, o1_ref, o2_ref, s0_ref, s1_ref, s2_ref, z_ref, sga_ref, mixr_ref, x_ref,
                      wao_ref, wo_ref, gpost_ref, y_ref, nat1_s, nat2_s, st1_s, st2_s, *, tm):
    for o_ref, s_ref, nat_s, stn_s, dil in ((o1_ref, s1_ref, nat1_s, st1_s, ATTN_GROUPS[1][1]),
                                            (o2_ref, s2_ref, nat2_s, st2_s, ATTN_GROUPS[2][1])):
        p = tm // dil
        for r in range(dil):
            stn_s[pl.ds(r, p, stride=dil), :] = s_ref[r]
            for c in range(D_ATTN // LANES):
                nat_s[c, pl.ds(r, p, stride=dil), :] = o_ref[r, :, c * LANES:(c + 1) * LANES].astype(F32)

    lse = [s0_ref[...], st1_s[...], st2_s[...]]
    mm = jnp.maximum(jnp.maximum(lse[0], lse[1]), lse[2])
    ws = [jnp.exp(x - mm) for x in lse]
    den = ws[0] + ws[1] + ws[2]
    src = lax.broadcasted_iota(jnp.int32, (STAT_LANES, D_ATTN), 0)
    dst_head = lax.broadcasted_iota(jnp.int32, (STAT_LANES, D_ATTN), 1) // HEAD_DIM
    expand = jnp.where(src == (dst_head % 2) * (STAT_LANES // 2) + (dst_head // 2) * STAT_LANES_PER_HEAD,
                       1.0, 0.0).astype(BF16)
    outs = [o0_ref[...].astype(F32),
            jnp.concatenate([nat1_s[c] for c in range(D_ATTN // LANES)], axis=1),
            jnp.concatenate([nat2_s[c] for c in range(D_ATTN // LANES)], axis=1)]
    att = jnp.zeros((tm, D_ATTN), F32)
    for g in range(N_GROUPS):
        wexp = jnp.dot((ws[g] / den).astype(BF16), expand, preferred_element_type=F32)
        att = att + wexp * outs[g]
    _finish(att, z_ref, sga_ref, mixr_ref, x_ref, wao_ref, wo_ref, gpost_ref, y_ref)


def _merge_out_prompt(os, sts, zg, sga, mixr, x, wao, wo, gpost):
    B, S, D = x.shape
    tm = TM_OUT
    tile3 = lambda b, m: (b, m, 0)
    tile4 = lambda b, m: (b, 0, m, 0)
    const = lambda b, m: (0, 0)
    in_specs = []
    for (_, dil), width in [(g, D_ATTN) for g in ATTN_GROUPS] + [(g, STAT_LANES) for g in ATTN_GROUPS]:
        if dil == 1:
            in_specs.append(pl.BlockSpec((None, None, tm, width), tile4))
        else:
            in_specs.append(pl.BlockSpec((None, dil, tm // dil, width), tile4))
    in_specs += [
        pl.BlockSpec((None, tm, D_ATTN), tile3),
        pl.BlockSpec((None, tm, D), tile3),
        pl.BlockSpec((None, tm, D), tile3),
        pl.BlockSpec((None, tm, D), tile3),
        pl.BlockSpec((D_ATTN, D), const),
        pl.BlockSpec((D, D), const),
        pl.BlockSpec((1, D), const),
    ]
    scratch = [
        pltpu.VMEM((D_ATTN // LANES, tm, LANES), F32),
        pltpu.VMEM((D_ATTN // LANES, tm, LANES), F32),
        pltpu.VMEM((tm, STAT_LANES), F32),
        pltpu.VMEM((tm, STAT_LANES), F32),
    ]
    return pl.pallas_call(
        functools.partial(_merge_out_kernel, tm=tm),
        grid=(B, S // tm),
        in_specs=in_specs,
        out_specs=pl.BlockSpec((None, tm, D), tile3),
        out_shape=jax.ShapeDtypeStruct((B, S, D), F32),
        scratch_shapes=scratch,
        compiler_params=pltpu.CompilerParams(
            dimension_semantics=("arbitrary", "arbitrary"), vmem_limit_bytes=VMEM_LIMIT),
        name="merge_out_prompt",
    )(*os, *sts, zg, sga, mixr, x, wao, wo, gpost)


def _out_sample(att, z, sga, mixr, x, wao, wo, gpost):
    N, D = x.shape
    full = lambda shape: pl.BlockSpec(shape, lambda i: (0, 0))
    return pl.pallas_call(
        _out_kernel,
        grid=(1,),
        in_specs=[full((N, D_ATTN)), full((N, D_ATTN)), full((N, D)), full((N, D)), full((N, D)),
                  full((D_ATTN, D)), full((D, D)), full((1, D))],
        out_specs=full((N, D)),
        out_shape=jax.ShapeDtypeStruct((N, D), F32),
        compiler_params=pltpu.CompilerParams(dimension_semantics=("arbitrary",), vmem_limit_bytes=VMEM_LIMIT),
        name="out_sample",
    )(att, z, sga, mixr, x, wao, wo, gpost)


def _block_diag_chunks(w):
    per = GATE_BLOCK // RNN_BLOCK
    w = w.reshape(N_GATE_BLOCKS, per, RNN_BLOCK, RNN_BLOCK)
    eye = jnp.eye(per, dtype=w.dtype)
    dense = w[:, :, :, None, :] * eye[None, :, None, :, None]
    return dense.reshape(N_GATE_BLOCKS, GATE_BLOCK, GATE_BLOCK)


def _reorder_w_in(w_in):
    sizes = (D_RNN, D_RNN, N_GROUPS * D_ATTN, N_GROUPS * D_ATTN, N_GROUPS * D_ATTN, D_ATTN, D_MODEL, D_MODEL)
    starts = np.concatenate([[0], np.cumsum(sizes)[:-1]])
    u0, z0, q0, k0, v0, za0, gr0, ga0 = (int(s) for s in starts)
    pieces = [(u0, D_RNN), (z0, D_RNN), (gr0, D_MODEL), (ga0, D_MODEL), (za0, D_ATTN)]
    for g in range(N_GROUPS):
        pieces += [(q0 + g * D_ATTN, D_ATTN), (k0 + g * D_ATTN, D_ATTN), (v0 + g * D_ATTN, D_ATTN)]
    return jnp.concatenate([w_in[:, s:s + n] for s, n in pieces], axis=1).astype(BF16)


def _layer(layer, yp, ys, sconv, h0, caches, norm_pre, norm_post, w_in, conv_w, conv_b, lru_w_a, lru_b_a, lru_w_x,
           lru_b_x, lru_lambda, w_rnn_out, w_attn_out, w_out):
    B, S, D = yp.shape
    N = ys.shape[0]
    row = lambda v: v.reshape(1, -1)
    w_in_b = _reorder_w_in(w_in)
    wa = _block_diag_chunks(lru_w_a).astype(BF16)
    wx = _block_diag_chunks(lru_w_x).astype(BF16)
    wro = w_rnn_out.astype(BF16)
    wao = w_attn_out.astype(BF16)
    wo = w_out.astype(BF16)
    shared = (w_in_b, row(norm_pre), conv_w, row(conv_b), wa, wx, row(lru_b_a), row(lru_b_x), row(lru_lambda), wro)

    outs = _proj_rnn_prompt(yp, *shared)
    qkv, (zg, mixr, sga, kv0, kv1, kv2, conv_p, h_p) = outs[:9], outs[9:]
    os, sts = [], []
    for g in range(N_GROUPS):
        o, st = _attn_group(qkv[3 * g], qkv[3 * g + 1], qkv[3 * g + 2], g)
        os.append(o)
        sts.append(st)
    y_p = _merge_out_prompt(os, sts, zg, sga, mixr, yp, wao, wo, row(norm_post))
    kv_p = [kv.reshape(B, kv.shape[1], 2, H_G, HEAD_DIM) for kv in (kv0, kv1, kv2)]

    xs = ys.reshape(N, D)
    att_in, mixr_s, sga_s, conv_s, h_s = _proj_rnn_sample(
        xs, *shared, sconv.reshape(N, (CONV_WIDTH - 1) * D_RNN), h0)
    z_s = att_in[:, 0:D_ATTN]
    qkv_s = att_in[:, D_ATTN:].reshape(N, 3 * N_GROUPS, H_G, HEAD_DIM)
    att_s = _attn_sample(qkv_s, caches, layer).reshape(N, D_ATTN)
    y_s = _out_sample(att_s, z_s, sga_s, mixr_s, xs, wao, wo, row(norm_post))
    kv_s = [qkv_s[:, 3 * g + 1:3 * g + 3].reshape(N, 1, 2, H_G, HEAD_DIM) for g in range(N_GROUPS)]

    return (y_p, y_s.reshape(N, 1, D), conv_p, conv_s.reshape(N, CONV_WIDTH - 1, D_RNN),
            h_p.reshape(B, D_RNN), h_s, kv_p, kv_s)


def kernel(x_prompt, x_sample, state_conv, state_h, cache_kv_w128, cache_kv_w512, cache_kv_w2048, norm_pre, norm_post, w_in, conv_w, conv_b, lru_w_a, lru_b_a, lru_w_x, lru_b_x, lru_lambda, w_rnn_out, w_attn_out, w_out):
    depth = norm_pre.shape[0]
    caches = (cache_kv_w128, cache_kv_w512, cache_kv_w2048)
    yp, ys = x_prompt, x_sample
    conv_p, conv_s, h_p, h_s = [], [], [], []
    kvp = ([], [], [])
    kvs = ([], [], [])
    for l in range(depth):
        yp, ys, cp, cs, hp, hs, kv_p, kv_s = _layer(
            l, yp, ys, state_conv[l], state_h[l], caches,
            norm_pre[l], norm_post[l], w_in[l], conv_w[l], conv_b[l], lru_w_a[l], lru_b_a[l], lru_w_x[l],
            lru_b_x[l], lru_lambda[l], w_rnn_out[l], w_attn_out[l], w_out[l])
        conv_p.append(cp)
        conv_s.append(cs)
        h_p.append(hp)
        h_s.append(hs)
        for g in range(N_GROUPS):
            kvp[g].append(kv_p[g])
            kvs[g].append(kv_s[g])
    return (yp, ys, jnp.stack(conv_p), jnp.stack(conv_s), jnp.stack(h_p), jnp.stack(h_s),
            jnp.stack(kvp[0]), jnp.stack(kvs[0]), jnp.stack(kvp[1]), jnp.stack(kvs[1]),
            jnp.stack(kvp[2]), jnp.stack(kvs[2]))
```

```python
import functools

import numpy as np
import jax
import jax.numpy as jnp
from jax import lax
from jax.experimental import pallas as pl
from jax.experimental.pallas import tpu as pltpu

F32 = jnp.float32
BF16 = jnp.bfloat16

D_MODEL = 1024
D_RNN = 1024
N_RNN_BLOCKS = 16
RNN_BLOCK = D_RNN // N_RNN_BLOCKS
CONV_WIDTH = 4
LRU_C = 8.0
HEAD_DIM = 64
H_G = 8
ATTN_GROUPS = ((128, 1), (512, 4), (2048, 16))
N_GROUPS = 3
D_ATTN = H_G * HEAD_DIM
Q_BLOCK = 128
ALIBI_MAX = 8.0
EPS = 1e-6
NEG_INF = -1e30
D_IN = 2 * D_RNN + 3 * N_GROUPS * D_ATTN + D_ATTN + 2 * D_MODEL

LANES = 128
TN = 1024
N_COL_BLOCKS = D_IN // TN
GATE_BLOCK = 256
N_GATE_BLOCKS = D_RNN // GATE_BLOCK
STAT_LANES = LANES
STAT_LANES_PER_HEAD = STAT_LANES // H_G

TM_PROMPT = 512
SEGMENTS = 8
TM_OUT = 512
ATTN_CHUNK = 512
ROW_CHUNK = 64
GATE_ROWS = 128
VMEM_LIMIT = 56 * 1024 * 1024


def _alibi_slopes():
    n = N_GROUPS * H_G
    s = np.float32(2.0) ** (np.float32(-ALIBI_MAX) * np.arange(1, n + 1, dtype=np.float32) / np.float32(n))
    return s.reshape(N_GROUPS, H_G)


_SLOPES = _alibi_slopes()


def _stat_lane(h):
    return (h % 2) * (STAT_LANES // 2) + (h // 2) * STAT_LANES_PER_HEAD


def _softplus(y):
    return jnp.maximum(y, 0.0) + jnp.log1p(jnp.exp(-jnp.abs(y)))


def _sigmoid(x):
    return 0.5 * jnp.tanh(0.5 * x) + 0.5


def _for_rows(n_rows, chunk, fn):
    if n_rows <= chunk:
        fn(0)
        return

    def body(c, carry):
        fn(pl.multiple_of(c * chunk, chunk))
        return carry

    lax.fori_loop(0, n_rows // chunk, body, 0)


def _lru_gates(xc, c, wa_ref, wx_ref, ba_ref, bx_ref, lam_ref):
    c0 = c * GATE_BLOCK
    xcb = xc.astype(BF16)
    r = _sigmoid(jnp.dot(xcb, wa_ref[c], preferred_element_type=F32) + ba_ref[:, c0:c0 + GATE_BLOCK])
    i = _sigmoid(jnp.dot(xcb, wx_ref[c], preferred_element_type=F32) + bx_ref[:, c0:c0 + GATE_BLOCK])
    log_a = (-LRU_C * r) * _softplus(-lam_ref[:, c0:c0 + GATE_BLOCK])
    a = jnp.exp(log_a)
    b = jnp.sqrt(jnp.maximum(1.0 - a * a, 0.0)) * i * xc
    return a, b


def _proj_rnn_kernel(x_ref, w_ref, gpre_ref, cw_ref, cb_ref, wa_ref, wx_ref, ba_ref, bx_ref, lam_ref, wro_ref,
                     q0_ref, k0_ref, v0_ref, q1_ref, k1_ref, v1_ref, q2_ref, k2_ref, v2_ref,
                     zg_ref, mixr_ref, sga_ref, kv0_ref, kv1_ref, kv2_ref, conv_ref, h_ref,
                     xn_s, xnf_s, xseg_s, xnseg_s, xn4_s, xn16_s, u_s, a_s, b_s, yr_s, acc_s,
                     cprev_s, cin_s, carry_s,
                     *, tm, n_tiles):
    m = pl.program_id(1)
    j = pl.program_id(2)
    last_tile = m == n_tiles - 1
    in_kv2 = m >= n_tiles - ATTN_GROUPS[2][0] // tm
    half = TN // 2
    d1, d2 = ATTN_GROUPS[1][1], ATTN_GROUPS[2][1]

    def project(lhs_s):
        return jnp.dot(lhs_s[...], w_ref[...], preferred_element_type=F32)

    def emit(out_ref, c0, dilation):
        p = tm // dilation
        for r in range(dilation):
            out_ref[r] = acc_s[r * p:(r + 1) * p, c0:c0 + half].astype(BF16)

    def natural_tail(out_ref, out_c0, c0, dilation):
        p = tm // dilation
        for c in range(half // LANES):
            for r in range(dilation):
                xnf_s[c, pl.ds(r, p, stride=dilation), :] = \
                    acc_s[r * p:(r + 1) * p, c0 + c * LANES:c0 + (c + 1) * LANES]
            out_ref[:, out_c0 + c * LANES:out_c0 + (c + 1) * LANES] = xnf_s[c]

    seg = tm // SEGMENTS
    n_slab = D_MODEL // LANES

    @pl.when(j == 0)
    def _u_block():
        def norm(s, carry):
            r0 = pl.multiple_of(s * seg, seg)
            xv = x_ref[pl.ds(r0, seg), :]
            ms = jnp.mean(xv * xv, axis=-1, keepdims=True)
            xn = (xv * lax.rsqrt(ms + EPS)) * gpre_ref[...]
            xn_s[pl.ds(r0, seg), :] = xn.astype(BF16)
            for c in range(n_slab):
                piece = xn[:, c * LANES:(c + 1) * LANES]
                xnf_s[c, pl.ds(r0, seg), :] = piece
                xseg_s[c, pl.ds(s, seg, stride=SEGMENTS), :] = piece
            return carry

        lax.fori_loop(0, SEGMENTS, norm, 0)

        for c in range(n_slab):
            xnseg_s[:, c * LANES:(c + 1) * LANES] = xseg_s[c].astype(BF16)

        for dil, dst in ((d1, xn4_s), (d2, xn16_s)):
            p = tm // dil
            for r in range(dil):
                for c in range(n_slab):
                    dst[r * p:(r + 1) * p, c * LANES:(c + 1) * LANES] = \
                        xnf_s[c, pl.ds(r, p, stride=dil), :].astype(BF16)

        @pl.when(m == 0)
        def _reset():
            cprev_s[...] = jnp.zeros((8, D_RNN), F32)
            carry_s[...] = jnp.zeros((1, D_RNN), F32)

        u_s[...] = project(xnseg_s)

        sub = lax.broadcasted_iota(jnp.int32, (SEGMENTS, GATE_BLOCK), 0)
        for c in range(N_GATE_BLOCKS):
            cols = slice(c * GATE_BLOCK, (c + 1) * GATE_BLOCK)

            def wrapped(t):
                v = pltpu.roll(u_s[SEGMENTS * (seg + t):SEGMENTS * (seg + t + 1), cols], 1, axis=0)
                return jnp.where(sub == 0, cprev_s[8 + t:9 + t, cols], v)

            wrap = {t: wrapped(t) for t in range(1 - CONV_WIDTH, 0)}
            for rc in range(tm // GATE_ROWS):
                r0 = rc * GATE_ROWS
                xc = cb_ref[:, cols] + u_s[r0:r0 + GATE_ROWS, cols] * cw_ref[CONV_WIDTH - 1:CONV_WIDTH, cols]
                for sh in range(1, CONV_WIDTH):
                    if r0 == 0:
                        head = [wrap[t - sh] for t in range(sh)]
                        ush = jnp.concatenate(head + [u_s[0:GATE_ROWS - SEGMENTS * sh, cols]], axis=0)
                    else:
                        ush = u_s[r0 - SEGMENTS * sh:r0 - SEGMENTS * sh + GATE_ROWS, cols]
                    xc = xc + ush * cw_ref[CONV_WIDTH - 1 - sh:CONV_WIDTH - sh, cols]
                a, b = _lru_gates(xc, c, wa_ref, wx_ref, ba_ref, bx_ref, lam_ref)
                a_s[r0:r0 + GATE_ROWS, cols] = a
                b_s[r0:r0 + GATE_ROWS, cols] = b

        def scan(t, carry):
            hloc, cum = carry
            r0 = pl.multiple_of(t * SEGMENTS, SEGMENTS)
            av = a_s[pl.ds(r0, SEGMENTS), :]
            hloc = av * hloc + b_s[pl.ds(r0, SEGMENTS), :]
            cum = av * cum
            b_s[pl.ds(r0, SEGMENTS), :] = hloc
            a_s[pl.ds(r0, SEGMENTS), :] = cum
            return hloc, cum

        h_end, a_end = lax.fori_loop(
            0, seg, scan, (jnp.zeros((SEGMENTS, D_RNN), F32), jnp.ones((SEGMENTS, D_RNN), F32)), unroll=4)
        state = carry_s[...]
        for s in range(SEGMENTS):
            cin_s[s:s + 1, :] = state
            state = h_end[s:s + 1, :] + a_end[s:s + 1, :] * state
        carry_s[...] = state
        h_ref[...] = state
        for k in range(1, CONV_WIDTH):
            last = u_s[tm - SEGMENTS * (k - 1) - 1:tm - SEGMENTS * (k - 1), :]
            conv_ref[CONV_WIDTH - 1 - k:CONV_WIDTH - k, :] = last
            cprev_s[8 - k:9 - k, :] = last

    @pl.when(j == 1)
    def _z_rnn_block():
        z = project(xnseg_s)
        split = (seg, SEGMENTS, D_RNN)
        h = b_s[...].reshape(split) + a_s[...].reshape(split) * cin_s[...][None]
        hz = (h.reshape(tm, D_RNN) * (z * _sigmoid(z))).astype(BF16)
        yr_s[...] = jnp.dot(hz, wro_ref[...], preferred_element_type=F32)

    @pl.when(j == 2)
    def _g_rnn_block():
        mix = _sigmoid(project(xnseg_s)) * yr_s[...]
        for c in range(n_slab):
            xseg_s[c] = mix[:, c * LANES:(c + 1) * LANES]
        for s in range(SEGMENTS):
            for c in range(n_slab):
                mixr_ref[s * seg:(s + 1) * seg, c * LANES:(c + 1) * LANES] = \
                    xseg_s[c, pl.ds(s, seg, stride=SEGMENTS), :].astype(BF16)

    @pl.when(j == 3)
    def _g_attn_block():
        sga_ref[...] = _sigmoid(project(xn_s)).astype(BF16)

    @pl.when(j == 4)
    def _zattn_q0_block():
        acc_s[...] = project(xn_s)
        zg_ref[...] = acc_s[:, 0:half].astype(BF16)
        emit(q0_ref, half, 1)

    @pl.when(j == 5)
    def _k0_v0_block():
        acc_s[...] = project(xn_s)
        emit(k0_ref, 0, 1)
        emit(v0_ref, half, 1)

        @pl.when(last_tile)
        def _():
            kv0_ref[...] = acc_s[tm - ATTN_GROUPS[0][0]:tm, :]

    @pl.when(j == 6)
    def _q1_k1_block():
        acc_s[...] = project(xn4_s)
        emit(q1_ref, 0, d1)
        emit(k1_ref, half, d1)

        @pl.when(last_tile)
        def _():
            natural_tail(kv1_ref, 0, half, d1)

    @pl.when(j == 7)
    def _v1_q2_block():
        acc_s[:, 0:half] = jnp.dot(xn4_s[...], w_ref[:, 0:half], preferred_element_type=F32)
        acc_s[:, half:TN] = jnp.dot(xn16_s[...], w_ref[:, half:TN], preferred_element_type=F32)
        emit(v1_ref, 0, d1)
        emit(q2_ref, half, d2)

        @pl.when(last_tile)
        def _():
            natural_tail(kv1_ref, half, 0, d1)

    @pl.when(j == 8)
    def _k2_v2_block():
        acc_s[...] = project(xn16_s)
        emit(k2_ref, 0, d2)
        emit(v2_ref, half, d2)

        @pl.when(in_kv2)
        def _():
            natural_tail(kv2_ref, 0, 0, d2)
            natural_tail(kv2_ref, half, half, d2)


def _proj_rnn_prompt(x, w_in, gpre, cw, cb, wa, wx, ba, bx, lam, wro):
    B, S, D = x.shape
    tm = TM_PROMPT
    n_tiles = S // tm
    m0 = n_tiles - ATTN_GROUPS[2][0] // tm
    assert S % tm == 0 and tm == ATTN_GROUPS[1][0] and ATTN_GROUPS[2][0] % tm == 0 and S >= ATTN_GROUPS[2][0]

    const2 = lambda b, m, j: (0, 0)
    const3 = lambda b, m, j: (0, 0, 0)
    tile3 = lambda b, m, j: (b, m, 0)
    tile4 = lambda b, m, j: (b, 0, m, 0)
    per_seq = lambda b, m, j: (b, 0, 0)

    qkv_shapes, qkv_specs = [], []
    for _, dil in ATTN_GROUPS:
        for _ in range(3):
            qkv_shapes.append(jax.ShapeDtypeStruct((B, dil, S // dil, D_ATTN), BF16))
            qkv_specs.append(pl.BlockSpec((None, dil, tm // dil, D_ATTN), tile4))
    out_shape = tuple(qkv_shapes) + (
        jax.ShapeDtypeStruct((B, S, D_ATTN), BF16),
        jax.ShapeDtypeStruct((B, S, D), BF16),
        jax.ShapeDtypeStruct((B, S, D), BF16),
        jax.ShapeDtypeStruct((B, ATTN_GROUPS[0][0], 2 * D_ATTN), F32),
        jax.ShapeDtypeStruct((B, ATTN_GROUPS[1][0], 2 * D_ATTN), F32),
        jax.ShapeDtypeStruct((B, ATTN_GROUPS[2][0], 2 * D_ATTN), F32),
        jax.ShapeDtypeStruct((B, CONV_WIDTH - 1, D_RNN), F32),
        jax.ShapeDtypeStruct((B, 1, D_RNN), F32),
    )
    out_specs = tuple(qkv_specs) + (
        pl.BlockSpec((None, tm, D_ATTN), tile3),
        pl.BlockSpec((None, tm, D), tile3),
        pl.BlockSpec((None, tm, D), tile3),
        pl.BlockSpec((None, ATTN_GROUPS[0][0], 2 * D_ATTN), per_seq),
        pl.BlockSpec((None, ATTN_GROUPS[1][0], 2 * D_ATTN), per_seq),
        pl.BlockSpec((None, tm, 2 * D_ATTN), lambda b, m, j: (b, jnp.maximum(m - m0, 0), 0)),
        pl.BlockSpec((None, CONV_WIDTH - 1, D_RNN), per_seq),
        pl.BlockSpec((None, 1, D_RNN), per_seq),
    )
    in_specs = [
        pl.BlockSpec((None, tm, D), tile3),
        pl.BlockSpec((D, TN), lambda b, m, j: (0, j)),
        pl.BlockSpec((1, D), const2),
        pl.BlockSpec((CONV_WIDTH, D_RNN), const2),
        pl.BlockSpec((1, D_RNN), const2),
        pl.BlockSpec((N_GATE_BLOCKS, GATE_BLOCK, GATE_BLOCK), const3),
        pl.BlockSpec((N_GATE_BLOCKS, GATE_BLOCK, GATE_BLOCK), const3),
        pl.BlockSpec((1, D_RNN), const2),
        pl.BlockSpec((1, D_RNN), const2),
        pl.BlockSpec((1, D_RNN), const2),
        pl.BlockSpec((D_RNN, D), const2),
    ]
    scratch = [
        pltpu.VMEM((tm, D), BF16),
        pltpu.VMEM((D // LANES, tm, LANES), F32),
        pltpu.VMEM((D // LANES, tm, LANES), F32),
        pltpu.VMEM((tm, D), BF16),
        pltpu.VMEM((tm, D), BF16),
        pltpu.VMEM((tm, D), BF16),
        pltpu.VMEM((tm, D_RNN), F32),
        pltpu.VMEM((tm, D_RNN), F32),
        pltpu.VMEM((tm, D_RNN), F32),
        pltpu.VMEM((tm, D), F32),
        pltpu.VMEM((tm, TN), F32),
        pltpu.VMEM((8, D_RNN), F32),
        pltpu.VMEM((SEGMENTS, D_RNN), F32),
        pltpu.VMEM((1, D_RNN), F32),
    ]
    return pl.pallas_call(
        functools.partial(_proj_rnn_kernel, tm=tm, n_tiles=n_tiles),
        grid=(B, n_tiles, N_COL_BLOCKS),
        in_specs=in_specs,
        out_specs=out_specs,
        out_shape=out_shape,
        scratch_shapes=scratch,
        compiler_params=pltpu.CompilerParams(
            dimension_semantics=("arbitrary", "arbitrary", "arbitrary"), vmem_limit_bytes=VMEM_LIMIT),
        name="proj_rnn_prompt",
    )(x, w_in, gpre, cw, cb, wa, wx, ba, bx, lam, wro)


def _proj_rnn_sample_kernel(x_ref, w_ref, gpre_ref, cw_ref, cb_ref, wa_ref, wx_ref, ba_ref, bx_ref, lam_ref, wro_ref,
                            sc_ref, h0_ref,
                            att_ref, mixr_ref, sga_ref, conv_ref, h_ref,
                            xn_s, h_s, yr_s):
    j = pl.program_id(0)

    def project():
        return jnp.dot(xn_s[...], w_ref[...], preferred_element_type=F32)

    @pl.when(j == 0)
    def _u_block():
        xv = x_ref[...]
        ms = jnp.mean(xv * xv, axis=-1, keepdims=True)
        xn_s[...] = ((xv * lax.rsqrt(ms + EPS)) * gpre_ref[...]).astype(BF16)
        u = project()
        taps = [sc_ref[:, k * D_RNN:(k + 1) * D_RNN] for k in range(CONV_WIDTH - 1)] + [u]
        for c in range(N_GATE_BLOCKS):
            c0 = c * GATE_BLOCK
            xc = cb_ref[:, c0:c0 + GATE_BLOCK]
            for tap in range(CONV_WIDTH):
                xc = xc + taps[tap][:, c0:c0 + GATE_BLOCK] * cw_ref[tap:tap + 1, c0:c0 + GATE_BLOCK]
            a, b = _lru_gates(xc, c, wa_ref, wx_ref, ba_ref, bx_ref, lam_ref)
            h = a * h0_ref[:, c0:c0 + GATE_BLOCK] + b
            h_s[:, c0:c0 + GATE_BLOCK] = h
            h_ref[:, c0:c0 + GATE_BLOCK] = h
        for k in range(1, CONV_WIDTH):
            conv_ref[:, (k - 1) * D_RNN:k * D_RNN] = taps[k]

    @pl.when(j == 1)
    def _z_rnn_block():
        z = project()
        hz = (h_s[...] * (z * _sigmoid(z))).astype(BF16)
        yr_s[...] = jnp.dot(hz, wro_ref[...], preferred_element_type=F32)

    @pl.when(j == 2)
    def _g_rnn_block():
        mixr_ref[...] = _sigmoid(project()) * yr_s[...]

    @pl.when(j == 3)
    def _g_attn_block():
        sga_ref[...] = _sigmoid(project())

    @pl.when(j >= 4)
    def _attn_blocks():
        att_ref[...] = project()


def _proj_rnn_sample(x, w_in, gpre, cw, cb, wa, wx, ba, bx, lam, wro, sconv, h0):
    N, D = x.shape
    const2 = lambda j: (0, 0)
    const3 = lambda j: (0, 0, 0)
    n_att = N_COL_BLOCKS - 4
    out_shape = (
        jax.ShapeDtypeStruct((N, n_att * TN), F32),
        jax.ShapeDtypeStruct((N, D), F32),
        jax.ShapeDtypeStruct((N, D), F32),
        jax.ShapeDtypeStruct((N, (CONV_WIDTH - 1) * D_RNN), F32),
        jax.ShapeDtypeStruct((N, D_RNN), F32),
    )
    out_specs = (
        pl.BlockSpec((N, TN), lambda j: (0, jnp.maximum(j - 4, 0))),
        pl.BlockSpec((N, D), const2),
        pl.BlockSpec((N, D), const2),
        pl.BlockSpec((N, (CONV_WIDTH - 1) * D_RNN), const2),
        pl.BlockSpec((N, D_RNN), const2),
    )
    in_specs = [
        pl.BlockSpec((N, D), const2),
        pl.BlockSpec((D, TN), lambda j: (0, j)),
        pl.BlockSpec((1, D), const2),
        pl.BlockSpec((CONV_WIDTH, D_RNN), const2),
        pl.BlockSpec((1, D_RNN), const2),
        pl.BlockSpec((N_GATE_BLOCKS, GATE_BLOCK, GATE_BLOCK), const3),
        pl.BlockSpec((N_GATE_BLOCKS, GATE_BLOCK, GATE_BLOCK), const3),
        pl.BlockSpec((1, D_RNN), const2),
        pl.BlockSpec((1, D_RNN), const2),
        pl.BlockSpec((1, D_RNN), const2),
        pl.BlockSpec((D_RNN, D), const2),
        pl.BlockSpec((N, (CONV_WIDTH - 1) * D_RNN), const2),
        pl.BlockSpec((N, D_RNN), const2),
    ]
    scratch = [pltpu.VMEM((N, D), BF16), pltpu.VMEM((N, D_RNN), F32), pltpu.VMEM((N, D), F32)]
    return pl.pallas_call(
        _proj_rnn_sample_kernel,
        grid=(N_COL_BLOCKS,),
        in_specs=in_specs,
        out_specs=out_specs,
        out_shape=out_shape,
        scratch_shapes=scratch,
        compiler_params=pltpu.CompilerParams(dimension_semantics=("arbitrary",), vmem_limit_bytes=VMEM_LIMIT),
        name="proj_rnn_sample",
    )(x, w_in, gpre, cw, cb, wa, wx, ba, bx, lam, wro, sconv, h0)


def _attn_kernel(q_ref, kc_ref, vc_ref, kp_ref, vp_ref, o_ref, st_ref, kcat_s, vcat_s, bias_s, *, group, chunk):
    dilation = ATTN_GROUPS[group][1]
    c = pl.program_id(2)
    nt = (((1,), (1,)), ((), ()))

    @pl.when((pl.program_id(0) == 0) & (pl.program_id(1) == 0) & (c == 0))
    def _init_bias():
        qi = lax.broadcasted_iota(jnp.int32, (Q_BLOCK, Q_BLOCK), 0)
        kj = lax.broadcasted_iota(jnp.int32, (Q_BLOCK, Q_BLOCK), 1)
        steps_prev = Q_BLOCK + qi - kj
        steps_cur = qi - kj
        dist_prev = (steps_prev * dilation).astype(F32)
        dist_cur = (steps_cur * dilation).astype(F32)
        for h in range(H_G):
            slope = float(_SLOPES[group, h])
            bias_s[h] = jnp.where(steps_cur >= 0, -slope * dist_cur, NEG_INF)
            bias_s[H_G + h] = jnp.where(steps_prev <= Q_BLOCK, -slope * dist_prev, NEG_INF)
            bias_s[2 * H_G + h] = jnp.full((Q_BLOCK, Q_BLOCK), NEG_INF, F32)

    kcat_s[0:Q_BLOCK, :] = kp_ref[...]
    kcat_s[Q_BLOCK:Q_BLOCK + chunk, :] = kc_ref[...]
    vcat_s[0:Q_BLOCK, :] = vp_ref[...]
    vcat_s[Q_BLOCK:Q_BLOCK + chunk, :] = vc_ref[...]

    lane = lax.broadcasted_iota(jnp.int32, (Q_BLOCK, LANES), 1)
    low = lane < HEAD_DIM
    lane2 = lax.broadcasted_iota(jnp.int32, (2 * Q_BLOCK, LANES), 1)
    low2 = lane2 < HEAD_DIM
    ones_lo = jnp.where(low2, 1.0, 0.0).astype(BF16)
    ones_hi = jnp.where(low2, 0.0, 1.0).astype(BF16)
    stat_lane = lax.broadcasted_iota(jnp.int32, (Q_BLOCK, STAT_LANES), 1)

    def block(i, carry):
        r0 = pl.multiple_of(i * Q_BLOCK, Q_BLOCK)
        first = jnp.where((c == 0) & (i == 0), H_G, 0)
        stats = jnp.zeros((Q_BLOCK, STAT_LANES), F32)
        for p in range(H_G // 2):
            sl = slice(LANES * p, LANES * (p + 1))
            qp = q_ref[pl.ds(r0, Q_BLOCK), sl] * (HEAD_DIM ** -0.5)
            kp = kcat_s[pl.ds(r0, 2 * Q_BLOCK), sl]
            vp = vcat_s[pl.ds(r0, 2 * Q_BLOCK), sl]
            es, ms = [], []
            for hh in range(2):
                h = 2 * p + hh
                msk = low if hh == 0 else jnp.logical_not(low)
                qm = jnp.where(msk, qp, jnp.zeros_like(qp))
                s = lax.dot_general(qm, kp, nt, preferred_element_type=F32)
                s_p = s[:, 0:Q_BLOCK] + bias_s[H_G + h + first]
                s_c = s[:, Q_BLOCK:2 * Q_BLOCK] + bias_s[h]
                mx = jnp.maximum(jnp.max(s_p, axis=-1, keepdims=True), jnp.max(s_c, axis=-1, keepdims=True))
                es.append(jnp.exp(s_p - mx).astype(BF16))
                es.append(jnp.exp(s_c - mx).astype(BF16))
                ms.append(mx)
            vm0 = jnp.where(low2, vp, jnp.zeros_like(vp))
            vm1 = jnp.where(low2, jnp.zeros_like(vp), vp)
            w = jnp.concatenate([jnp.concatenate([vm0, ones_lo], axis=1),
                                 jnp.concatenate([vm1, ones_hi], axis=1)], axis=0)
            acc = jnp.dot(jnp.concatenate(es, axis=1), w, preferred_element_type=F32)
            l_pair = acc[:, LANES:2 * LANES]
            o_ref[pl.ds(r0, Q_BLOCK), sl] = (acc[:, 0:LANES] / l_pair).astype(o_ref.dtype)
            lse_pair = jnp.where(low, ms[0], ms[1]) + jnp.log(l_pair)
            keep = ((stat_lane % (STAT_LANES // 2)) // STAT_LANES_PER_HEAD) == p
            stats = jnp.where(keep, lse_pair, stats)
        st_ref[pl.ds(r0, Q_BLOCK), :] = stats
        return carry

    lax.fori_loop(0, chunk // Q_BLOCK, block, 0)


def _attn_group(q, k, v, group):
    B, dil, L, _ = q.shape
    chunk = min(ATTN_CHUNK, L)
    assert L % chunk == 0 and chunk % Q_BLOCK == 0
    cur = lambda b, r, c: (b, r, c, 0)
    prev = lambda b, r, c: (b, r, jnp.maximum(c * (chunk // Q_BLOCK) - 1, 0), 0)
    blk = (None, None, chunk, D_ATTN)
    pblk = (None, None, Q_BLOCK, D_ATTN)
    return pl.pallas_call(
        functools.partial(_attn_kernel, group=group, chunk=chunk),
        grid=(B, dil, L // chunk),
        in_specs=[pl.BlockSpec(blk, cur), pl.BlockSpec(blk, cur), pl.BlockSpec(blk, cur),
                  pl.BlockSpec(pblk, prev), pl.BlockSpec(pblk, prev)],
        out_specs=[pl.BlockSpec(blk, cur), pl.BlockSpec((None, None, chunk, STAT_LANES), cur)],
        out_shape=[jax.ShapeDtypeStruct((B, dil, L, D_ATTN), BF16),
                   jax.ShapeDtypeStruct((B, dil, L, STAT_LANES), F32)],
        scratch_shapes=[pltpu.VMEM((Q_BLOCK + chunk, D_ATTN), BF16),
                        pltpu.VMEM((Q_BLOCK + chunk, D_ATTN), BF16),
                        pltpu.VMEM((3 * H_G, Q_BLOCK, Q_BLOCK), F32)],
        compiler_params=pltpu.CompilerParams(
            dimension_semantics=("arbitrary", "arbitrary", "arbitrary"), vmem_limit_bytes=VMEM_LIMIT),
        name=f"attn_group{group}",
    )(q, k, v, k, v)


def _attn_sample_kernel(qkv_ref, c0_ref, c1_ref, c2_ref, o_ref):
    caches = (c0_ref, c1_ref, c2_ref)
    scale = HEAD_DIM ** -0.5
    ri = lax.broadcasted_iota(jnp.int32, (HEAD_DIM, HEAD_DIM), 0)
    ci = lax.broadcasted_iota(jnp.int32, (HEAD_DIM, HEAD_DIM), 1)
    eye = ri == ci

    def to_col(row):
        return jnp.sum(jnp.where(eye, jnp.broadcast_to(row, (HEAD_DIM, HEAD_DIM)), 0.0), axis=-1, keepdims=True)

    def to_row(col):
        return jnp.sum(jnp.where(eye, jnp.broadcast_to(col, (HEAD_DIM, HEAD_DIM)), 0.0), axis=0, keepdims=True)

    outs = [[None] * N_GROUPS for _ in range(H_G)]
    lses = [[None] * N_GROUPS for _ in range(H_G)]
    for g, (window, dilation) in enumerate(ATTN_GROUPS):
        c_ref = caches[g]
        pos = lax.broadcasted_iota(jnp.int32, (1, window), 1)
        on_stride = (pos % dilation) == 0
        dist = (window - pos).astype(F32)
        q = qkv_ref[3 * g] * scale
        s_new_all = jnp.sum(qkv_ref[3 * g + 1] * q, axis=-1, keepdims=True)
        for h in range(H_G):
            bias = jnp.where(on_stride, -float(_SLOPES[g, h]) * dist, NEG_INF)
            q_col = to_col(q[h:h + 1, :])
            v_col = to_col(qkv_ref[3 * g + 2, h:h + 1, :])
            s = jnp.sum(c_ref[0, h] * q_col, axis=0, keepdims=True) + bias
            s_new = s_new_all[h:h + 1, :]
            mx = jnp.maximum(jnp.max(s, axis=-1, keepdims=True), s_new)
            e = jnp.exp(s - mx)
            e_new = jnp.exp(s_new - mx)
            l = jnp.sum(e, axis=-1, keepdims=True) + e_new
            o = jnp.sum(c_ref[1, h] * e, axis=-1, keepdims=True) + e_new * v_col
            outs[h][g] = o / l
            lses[h][g] = mx + jnp.log(l)
    for h in range(H_G):
        mm = jnp.maximum(jnp.maximum(lses[h][0], lses[h][1]), lses[h][2])
        ws = [jnp.exp(x - mm) for x in lses[h]]
        den = ws[0] + ws[1] + ws[2]
        col = (outs[h][0] * ws[0] + outs[h][1] * ws[1] + outs[h][2] * ws[2]) / den
        o_ref[h:h + 1, :] = to_row(col)


def _attn_sample(qkv_s, caches, layer):
    N = qkv_s.shape[0]
    views = []
    for (window, _), c in zip(ATTN_GROUPS, caches):
        assert c.shape[2] == window
        views.append(jnp.transpose(c, (0, 1, 3, 4, 5, 2)))
    in_specs = [pl.BlockSpec((None, 3 * N_GROUPS, H_G, HEAD_DIM), lambda i: (i, 0, 0, 0))]
    in_specs += [pl.BlockSpec((None, None, 2, H_G, HEAD_DIM, window), lambda i: (layer, i, 0, 0, 0, 0))
                 for window, _ in ATTN_GROUPS]
    return pl.pallas_call(
        _attn_sample_kernel,
        grid=(N,),
        in_specs=in_specs,
        out_specs=pl.BlockSpec((None, H_G, HEAD_DIM), lambda i: (i, 0, 0)),
        out_shape=jax.ShapeDtypeStruct((N, H_G, HEAD_DIM), F32),
        compiler_params=pltpu.CompilerParams(dimension_semantics=("arbitrary",), vmem_limit_bytes=VMEM_LIMIT),
        name="attn_sample",
    )(qkv_s, *views)


def _finish(att, z_ref, sga_ref, mixr_ref, x_ref, wao_ref, wo_ref, gpost_ref, y_ref):
    z = z_ref[...].astype(F32)
    ya_in = (att * (z * _sigmoid(z))).astype(BF16)
    ya = jnp.dot(ya_in, wao_ref[...], preferred_element_type=F32)
    mixed = mixr_ref[...].astype(F32) + sga_ref[...].astype(F32) * ya
    out = jnp.dot(mixed.astype(BF16), wo_ref[...], preferred_element_type=F32)
    ms = jnp.mean(out * out, axis=-1, keepdims=True)
    y_ref[...] = x_ref[...] + (out * lax.rsqrt(ms + EPS)) * gpost_ref[...]


def _out_kernel(att_ref, z_ref, sga_ref, mixr_ref, x_ref, wao_ref, wo_ref, gpost_ref, y_ref):
    _finish(att_ref[...].astype(F32), z_ref, sga_ref, mixr_ref, x_ref, wao_ref, wo_ref, gpost_ref, y_ref)


def _merge_out_kernel(o0_ref, o1_ref, o2_ref, s0_ref, s1_ref, s2_ref, z_ref, sga_ref, mixr_ref, x_ref,
                      wao_ref, wo_ref, gpost_ref, y_ref, nat1_s, nat2_s, st1_s, st2_s, *, tm):
    for o_ref, s_ref, nat_s, stn_s, dil in ((o1_ref, s1_ref, nat1_s, st1_s, ATTN_GROUPS[1][1]),
                                            (o2_ref, s2_ref, nat2_s, st2_s, ATTN_GROUPS[2][1])):
        p = tm // dil
        for r in range(dil):
            stn_s[pl.ds(r, p, stride=dil), :] = s_ref[r]
            for c in range(D_ATTN // LANES):
                nat_s[c, pl.ds(r, p, stride=dil), :] = o_ref[r, :, c * LANES:(c + 1) * LANES].astype(F32)

    lse = [s0_ref[...], st1_s[...], st2_s[...]]
    mm = jnp.maximum(jnp.maximum(lse[0], lse[1]), lse[2])
    ws = [jnp.exp(x - mm) for x in lse]
    den = ws[0] + ws[1] + ws[2]
    src = lax.broadcasted_iota(jnp.int32, (STAT_LANES, D_ATTN), 0)
    dst_head = lax.broadcasted_iota(jnp.int32, (STAT_LANES, D_ATTN), 1) // HEAD_DIM
    expand = jnp.where(src == (dst_head % 2) * (STAT_LANES // 2) + (dst_head // 2) * STAT_LANES_PER_HEAD,
                       1.0, 0.0).astype(BF16)
    outs = [o0_ref[...].astype(F32),
            jnp.concatenate([nat1_s[c] for c in range(D_ATTN // LANES)], axis=1),
            jnp.concatenate([nat2_s[c] for c in range(D_ATTN // LANES)], axis=1)]
    att = jnp.zeros((tm, D_ATTN), F32)
    for g in range(N_GROUPS):
        wexp = jnp.dot((ws[g] / den).astype(BF16), expand, preferred_element_type=F32)
        att = att + wexp * outs[g]
    _finish(att, z_ref, sga_ref, mixr_ref, x_ref, wao_ref, wo_ref, gpost_ref, y_ref)


def _merge_out_prompt(os, sts, zg, sga, mixr, x, wao, wo, gpost):
    B, S, D = x.shape
    tm = TM_OUT
    tile3 = lambda b, m: (b, m, 0)
    tile4 = lambda b, m: (b, 0, m, 0)
    const = lambda b, m: (0, 0)
    in_specs = []
    for (_, dil), width in [(g, D_ATTN) for g in ATTN_GROUPS] + [(g, STAT_LANES) for g in ATTN_GROUPS]:
        if dil == 1:
            in_specs.append(pl.BlockSpec((None, None, tm, width), tile4))
        else:
            in_specs.append(pl.BlockSpec((None, dil, tm // dil, width), tile4))
    in_specs += [
        pl.BlockSpec((None, tm, D_ATTN), tile3),
        pl.BlockSpec((None, tm, D), tile3),
        pl.BlockSpec((None, tm, D), tile3),
        pl.BlockSpec((None, tm, D), tile3),
        pl.BlockSpec((D_ATTN, D), const),
        pl.BlockSpec((D, D), const),
        pl.BlockSpec((1, D), const),
    ]
    scratch = [
        pltpu.VMEM((D_ATTN // LANES, tm, LANES), F32),
        pltpu.VMEM((D_ATTN // LANES, tm, LANES), F32),
        pltpu.VMEM((tm, STAT_LANES), F32),
        pltpu.VMEM((tm, STAT_LANES), F32),
    ]
    return pl.pallas_call(
        functools.partial(_merge_out_kernel, tm=tm),
        grid=(B, S // tm),
        in_specs=in_specs,
        out_specs=pl.BlockSpec((None, tm, D), tile3),
        out_shape=jax.ShapeDtypeStruct((B, S, D), F32),
        scratch_shapes=scratch,
        compiler_params=pltpu.CompilerParams(
            dimension_semantics=("arbitrary", "arbitrary"), vmem_limit_bytes=VMEM_LIMIT),
        name="merge_out_prompt",
    )(*os, *sts, zg, sga, mixr, x, wao, wo, gpost)


def _out_sample(att, z, sga, mixr, x, wao, wo, gpost):
    N, D = x.shape
    full = lambda shape: pl.BlockSpec(shape, lambda i: (0, 0))
    return pl.pallas_call(
        _out_kernel,
        grid=(1,),
        in_specs=[full((N, D_ATTN)), full((N, D_ATTN)), full((N, D)), full((N, D)), full((N, D)),
                  full((D_ATTN, D)), full((D, D)), full((1, D))],
        out_specs=full((N, D)),
        out_shape=jax.ShapeDtypeStruct((N, D), F32),
        compiler_params=pltpu.CompilerParams(dimension_semantics=("arbitrary",), vmem_limit_bytes=VMEM_LIMIT),
        name="out_sample",
    )(att, z, sga, mixr, x, wao, wo, gpost)


def _block_diag_chunks(w):
    per = GATE_BLOCK // RNN_BLOCK
    w = w.reshape(N_GATE_BLOCKS, per, RNN_BLOCK, RNN_BLOCK)
    eye = jnp.eye(per, dtype=w.dtype)
    dense = w[:, :, :, None, :] * eye[None, :, None, :, None]
    return dense.reshape(N_GATE_BLOCKS, GATE_BLOCK, GATE_BLOCK)


def _reorder_w_in(w_in):
    sizes = (D_RNN, D_RNN, N_GROUPS * D_ATTN, N_GROUPS * D_ATTN, N_GROUPS * D_ATTN, D_ATTN, D_MODEL, D_MODEL)
    starts = np.concatenate([[0], np.cumsum(sizes)[:-1]])
    u0, z0, q0, k0, v0, za0, gr0, ga0 = (int(s) for s in starts)
    pieces = [(u0, D_RNN), (z0, D_RNN), (gr0, D_MODEL), (ga0, D_MODEL), (za0, D_ATTN)]
    for g in range(N_GROUPS):
        pieces += [(q0 + g * D_ATTN, D_ATTN), (k0 + g * D_ATTN, D_ATTN), (v0 + g * D_ATTN, D_ATTN)]
    return jnp.concatenate([w_in[:, s:s + n] for s, n in pieces], axis=1).astype(BF16)


def _layer(layer, yp, ys, sconv, h0, caches, norm_pre, norm_post, w_in, conv_w, conv_b, lru_w_a, lru_b_a, lru_w_x,
           lru_b_x, lru_lambda, w_rnn_out, w_attn_out, w_out):
    B, S, D = yp.shape
    N = ys.shape[0]
    row = lambda v: v.reshape(1, -1)
    w_in_b = _reorder_w_in(w_in)
    wa = _block_diag_chunks(lru_w_a).astype(BF16)
    wx = _block_diag_chunks(lru_w_x).astype(BF16)
    wro = w_rnn_out.astype(BF16)
    wao = w_attn_out.astype(BF16)
    wo = w_out.astype(BF16)
    shared = (w_in_b, row(norm_pre), conv_w, row(conv_b), wa, wx, row(lru_b_a), row(lru_b_x), row(lru_lambda), wro)

    outs = _proj_rnn_prompt(yp, *shared)
    qkv, (zg, mixr, sga, kv0, kv1, kv2, conv_p, h_p) = outs[:9], outs[9:]
    os, sts = [], []
    for g in range(N_GROUPS):
        o, st = _attn_group(qkv[3 * g], qkv[3 * g + 1], qkv[3 * g + 2], g)
        os.append(o)
        sts.append(st)
    y_p = _merge_out_prompt(os, sts, zg, sga, mixr, yp, wao, wo, row(norm_post))
    kv_p = [kv.reshape(B, kv.shape[1], 2, H_G, HEAD_DIM) for kv in (kv0, kv1, kv2)]

    xs = ys.reshape(N, D)
    att_in, mixr_s, sga_s, conv_s, h_s = _proj_rnn_sample(
        xs, *shared, sconv.reshape(N, (CONV_WIDTH - 1) * D_RNN), h0)
    z_s = att_in[:, 0:D_ATTN]
    qkv_s = att_in[:, D_ATTN:].reshape(N, 3 * N_GROUPS, H_G, HEAD_DIM)
    att_s = _attn_sample(qkv_s, caches, layer).reshape(N, D_ATTN)
    y_s = _out_sample(att_s, z_s, sga_s, mixr_s, xs, wao, wo, row(norm_post))
    kv_s = [qkv_s[:, 3 * g + 1:3 * g + 3].reshape(N, 1, 2, H_G, HEAD_DIM) for g in range(N_GROUPS)]

    return (y_p, y_s.reshape(N, 1, D), conv_p, conv_s.reshape(N, CONV_WIDTH - 1, D_RNN),
            h_p.reshape(B, D_RNN), h_s, kv_p, kv_s)


def kernel(x_prompt, x_sample, state_conv, state_h, cache_kv_w128, cache_kv_w512, cache_kv_w2048, norm_pre, norm_post, w_in, conv_w, conv_b, lru_w_a, lru_b_a, lru_w_x, lru_b_x, lru_lambda, w_rnn_out, w_attn_out, w_out):
    depth = norm_pre.shape[0]
    caches = (cache_kv_w128, cache_kv_w512, cache_kv_w2048)
    yp, ys = x_prompt, x_sample
    conv_p, conv_s, h_p, h_s = [], [], [], []
    kvp = ([], [], [])
    kvs = ([], [], [])
    for l in range(depth):
        yp, ys, cp, cs, hp, hs, kv_p, kv_s = _layer(
            l, yp, ys, state_conv[l], state_h[l], caches,
            norm_pre[l], norm_post[l], w_in[l], conv_w[l], conv_b[l], lru_w_a[l], lru_b_a[l], lru_w_x[l],
            lru_b_x[l], lru_lambda[l], w_rnn_out[l], w_attn_out[l], w_out[l])
        conv_p.append(cp)
        conv_s.append(cs)
        h_p.append(hp)
        h_s.append(hs)
        for g in range(N_GROUPS):
            kvp[g].append(kv_p[g])
            kvs[g].append(kv_s[g])
    return (yp, ys, jnp.stack(conv_p), jnp.stack(conv_s), jnp.stack(h_p), jnp.stack(h_s),
            jnp.stack(kvp[0]), jnp.stack(kvs[0]), jnp.stack(kvp[1]), jnp.stack(kvs[1]),
            jnp.stack(kvp[2]), jnp.stack(kvs[2]))
```

```python
import functools

import numpy as np
import jax
import jax.numpy as jnp
from jax import lax
from jax.experimental import pallas as pl
from jax.experimental.pallas import tpu as pltpu

F32 = jnp.float32
BF16 = jnp.bfloat16

D_MODEL = 1024
D_RNN = 1024
N_RNN_BLOCKS = 16
RNN_BLOCK = D_RNN // N_RNN_BLOCKS
CONV_WIDTH = 4
LRU_C = 8.0
HEAD_DIM = 64
H_G = 8
ATTN_GROUPS = ((128, 1), (512, 4), (2048, 16))
N_GROUPS = 3
D_ATTN = H_G * HEAD_DIM
Q_BLOCK = 128
ALIBI_MAX = 8.0
EPS = 1e-6
NEG_INF = -1e30
D_IN = 2 * D_RNN + 3 * N_GROUPS * D_ATTN + D_ATTN + 2 * D_MODEL

LANES = 128
TN = 1024
N_COL_BLOCKS = D_IN // TN
GATE_BLOCK = 256
N_GATE_BLOCKS = D_RNN // GATE_BLOCK
STAT_LANES = LANES
STAT_LANES_PER_HEAD = STAT_LANES // H_G

TM_PROMPT = 512
SEGMENTS = 8
TM_OUT = 512
ATTN_CHUNK = 512
ROW_CHUNK = 64
GATE_ROWS = 128
VMEM_LIMIT = 56 * 1024 * 1024


def _alibi_slopes():
    n = N_GROUPS * H_G
    s = np.float32(2.0) ** (np.float32(-ALIBI_MAX) * np.arange(1, n + 1, dtype=np.float32) / np.float32(n))
    return s.reshape(N_GROUPS, H_G)


_SLOPES = _alibi_slopes()


def _stat_lane(h):
    return (h % 2) * (STAT_LANES // 2) + (h // 2) * STAT_LANES_PER_HEAD


def _softplus(y):
    return jnp.maximum(y, 0.0) + jnp.log1p(jnp.exp(-jnp.abs(y)))


def _sigmoid(x):
    return 0.5 * jnp.tanh(0.5 * x) + 0.5


def _for_rows(n_rows, chunk, fn):
    if n_rows <= chunk:
        fn(0)
        return

    def body(c, carry):
        fn(pl.multiple_of(c * chunk, chunk))
        return carry

    lax.fori_loop(0, n_rows // chunk, body, 0)


def _lru_gates(xc, c, wa_ref, wx_ref, ba_ref, bx_ref, lam_ref):
    c0 = c * GATE_BLOCK
    xcb = xc.astype(BF16)
    r = _sigmoid(jnp.dot(xcb, wa_ref[c], preferred_element_type=F32) + ba_ref[:, c0:c0 + GATE_BLOCK])
    i = _sigmoid(jnp.dot(xcb, wx_ref[c], preferred_element_type=F32) + bx_ref[:, c0:c0 + GATE_BLOCK])
    log_a = (-LRU_C * r) * _softplus(-lam_ref[:, c0:c0 + GATE_BLOCK])
    a = jnp.exp(log_a)
    v = 1.0 - a * a
    b = jnp.where(v > 0.0, v * lax.rsqrt(v), 0.0) * i * xc
    return a, b


def _proj_rnn_kernel(x_ref, w_ref, gpre_ref, cw_ref, cb_ref, wa_ref, wx_ref, ba_ref, bx_ref, lam_ref, wro_ref,
                     q0_ref, k0_ref, v0_ref, q1_ref, k1_ref, v1_ref, q2_ref, k2_ref, v2_ref,
                     zg_ref, mixr_ref, sga_ref, kv0_ref, kv1_ref, kv2_ref, conv_ref, h_ref,
                     xn_s, xnf_s, xseg_s, xnseg_s, xn4_s, xn16_s, u_s, a_s, b_s, yr_s, acc_s,
                     cprev_s, cin_s, carry_s,
                     *, tm, n_tiles):
    m = pl.program_id(1)
    j = pl.program_id(2)
    last_tile = m == n_tiles - 1
    in_kv2 = m >= n_tiles - ATTN_GROUPS[2][0] // tm
    half = TN // 2
    d1, d2 = ATTN_GROUPS[1][1], ATTN_GROUPS[2][1]

    def project(lhs_s):
        return jnp.dot(lhs_s[...], w_ref[...], preferred_element_type=F32)

    def emit(out_ref, c0, dilation):
        p = tm // dilation
        for r in range(dilation):
            out_ref[r] = acc_s[r * p:(r + 1) * p, c0:c0 + half].astype(BF16)

    def natural_tail(out_ref, out_c0, c0, dilation):
        p = tm // dilation
        for c in range(half // LANES):
            for r in range(dilation):
                xnf_s[c, pl.ds(r, p, stride=dilation), :] = \
                    acc_s[r * p:(r + 1) * p, c0 + c * LANES:c0 + (c + 1) * LANES]
            out_ref[:, out_c0 + c * LANES:out_c0 + (c + 1) * LANES] = xnf_s[c]

    seg = tm // SEGMENTS
    n_slab = D_MODEL // LANES

    @pl.when(j == 0)
    def _u_block():
        def norm(s, carry):
            r0 = pl.multiple_of(s * seg, seg)
            xv = x_ref[pl.ds(r0, seg), :]
            ms = jnp.mean(xv * xv, axis=-1, keepdims=True)
            xn = (xv * lax.rsqrt(ms + EPS)) * gpre_ref[...]
            xn_s[pl.ds(r0, seg), :] = xn.astype(BF16)
            for c in range(n_slab):
                piece = xn[:, c * LANES:(c + 1) * LANES]
                xnf_s[c, pl.ds(r0, seg), :] = piece
                xseg_s[c, pl.ds(s, seg, stride=SEGMENTS), :] = piece
            return carry

        lax.fori_loop(0, SEGMENTS, norm, 0)

        for c in range(n_slab):
            xnseg_s[:, c * LANES:(c + 1) * LANES] = xseg_s[c].astype(BF16)

        for dil, dst in ((d1, xn4_s), (d2, xn16_s)):
            p = tm // dil
            for r in range(dil):
                for c in range(n_slab):
                    dst[r * p:(r + 1) * p, c * LANES:(c + 1) * LANES] = \
                        xnf_s[c, pl.ds(r, p, stride=dil), :].astype(BF16)

        @pl.when(m == 0)
        def _reset():
            cprev_s[...] = jnp.zeros((8, D_RNN), F32)
            carry_s[...] = jnp.zeros((1, D_RNN), F32)

        u_s[...] = project(xnseg_s)

        sub = lax.broadcasted_iota(jnp.int32, (SEGMENTS, GATE_BLOCK), 0)
        for c in range(N_GATE_BLOCKS):
            cols = slice(c * GATE_BLOCK, (c + 1) * GATE_BLOCK)

            def wrapped(t):
                v = pltpu.roll(u_s[SEGMENTS * (seg + t):SEGMENTS * (seg + t + 1), cols], 1, axis=0)
                return jnp.where(sub == 0, cprev_s[8 + t:9 + t, cols], v)

            wrap = {t: wrapped(t) for t in range(1 - CONV_WIDTH, 0)}
            for rc in range(tm // GATE_ROWS):
                r0 = rc * GATE_ROWS
                xc = cb_ref[:, cols] + u_s[r0:r0 + GATE_ROWS, cols] * cw_ref[CONV_WIDTH - 1:CONV_WIDTH, cols]
                for sh in range(1, CONV_WIDTH):
                    if r0 == 0:
                        head = [wrap[t - sh] for t in range(sh)]
                        ush = jnp.concatenate(head + [u_s[0:GATE_ROWS - SEGMENTS * sh, cols]], axis=0)
                    else:
                        ush = u_s[r0 - SEGMENTS * sh:r0 - SEGMENTS * sh + GATE_ROWS, cols]
                    xc = xc + ush * cw_ref[CONV_WIDTH - 1 - sh:CONV_WIDTH - sh, cols]
                a, b = _lru_gates(xc, c, wa_ref, wx_ref, ba_ref, bx_ref, lam_ref)
                a_s[r0:r0 + GATE_ROWS, cols] = a
                b_s[r0:r0 + GATE_ROWS, cols] = b

        def scan(t, carry):
            hloc, cum = carry
            r0 = pl.multiple_of(t * SEGMENTS, SEGMENTS)
            av = a_s[pl.ds(r0, SEGMENTS), :]
            hloc = av * hloc + b_s[pl.ds(r0, SEGMENTS), :]
            cum = av * cum
            b_s[pl.ds(r0, SEGMENTS), :] = hloc
            a_s[pl.ds(r0, SEGMENTS), :] = cum
            return hloc, cum

        h_end, a_end = lax.fori_loop(
            0, seg, scan, (jnp.zeros((SEGMENTS, D_RNN), F32), jnp.ones((SEGMENTS, D_RNN), F32)), unroll=4)
        state = carry_s[...]
        for s in range(SEGMENTS):
            cin_s[s:s + 1, :] = state
            state = h_end[s:s + 1, :] + a_end[s:s + 1, :] * state
        carry_s[...] = state
        h_ref[...] = state
        for k in range(1, CONV_WIDTH):
            last = u_s[tm - SEGMENTS * (k - 1) - 1:tm - SEGMENTS * (k - 1), :]
            conv_ref[CONV_WIDTH - 1 - k:CONV_WIDTH - k, :] = last
            cprev_s[8 - k:9 - k, :] = last

    @pl.when(j == 1)
    def _z_rnn_block():
        z = project(xnseg_s)
        split = (seg, SEGMENTS, D_RNN)
        h = b_s[...].reshape(split) + a_s[...].reshape(split) * cin_s[...][None]
        hz = (h.reshape(tm, D_RNN) * (z * _sigmoid(z))).astype(BF16)
        yr_s[...] = jnp.dot(hz, wro_ref[...], preferred_element_type=F32)

    @pl.when(j == 2)
    def _g_rnn_block():
        mix = _sigmoid(project(xnseg_s)) * yr_s[...]
        for c in range(n_slab):
            xseg_s[c] = mix[:, c * LANES:(c + 1) * LANES]
        for s in range(SEGMENTS):
            for c in range(n_slab):
                mixr_ref[s * seg:(s + 1) * seg, c * LANES:(c + 1) * LANES] = \
                    xseg_s[c, pl.ds(s, seg, stride=SEGMENTS), :].astype(BF16)

    @pl.when(j == 3)
    def _g_attn_block():
        sga_ref[...] = _sigmoid(project(xn_s)).astype(BF16)

    @pl.when(j == 4)
    def _zattn_q0_block():
        acc_s[...] = project(xn_s)
        zg_ref[...] = acc_s[:, 0:half].astype(BF16)
        emit(q0_ref, half, 1)

    @pl.when(j == 5)
    def _k0_v0_block():
        acc_s[...] = project(xn_s)
        emit(k0_ref, 0, 1)
        emit(v0_ref, half, 1)

        @pl.when(last_tile)
        def _():
            kv0_ref[...] = acc_s[tm - ATTN_GROUPS[0][0]:tm, :]

    @pl.when(j == 6)
    def _q1_k1_block():
        acc_s[...] = project(xn4_s)
        emit(q1_ref, 0, d1)
        emit(k1_ref, half, d1)

        @pl.when(last_tile)
        def _():
            natural_tail(kv1_ref, 0, half, d1)

    @pl.when(j == 7)
    def _v1_q2_block():
        acc_s[:, 0:half] = jnp.dot(xn4_s[...], w_ref[:, 0:half], preferred_element_type=F32)
        acc_s[:, half:TN] = jnp.dot(xn16_s[...], w_ref[:, half:TN], preferred_element_type=F32)
        emit(v1_ref, 0, d1)
        emit(q2_ref, half, d2)

        @pl.when(last_tile)
        def _():
            natural_tail(kv1_ref, half, 0, d1)

    @pl.when(j == 8)
    def _k2_v2_block():
        acc_s[...] = project(xn16_s)
        emit(k2_ref, 0, d2)
        emit(v2_ref, half, d2)

        @pl.when(in_kv2)
        def _():
            natural_tail(kv2_ref, 0, 0, d2)
            natural_tail(kv2_ref, half, half, d2)


def _proj_rnn_prompt(x, w_in, gpre, cw, cb, wa, wx, ba, bx, lam, wro):
    B, S, D = x.shape
    tm = TM_PROMPT
    n_tiles = S // tm
    m0 = n_tiles - ATTN_GROUPS[2][0] // tm
    assert S % tm == 0 and tm == ATTN_GROUPS[1][0] and ATTN_GROUPS[2][0] % tm == 0 and S >= ATTN_GROUPS[2][0]

    const2 = lambda b, m, j: (0, 0)
    const3 = lambda b, m, j: (0, 0, 0)
    tile3 = lambda b, m, j: (b, m, 0)
    tile4 = lambda b, m, j: (b, 0, m, 0)
    per_seq = lambda b, m, j: (b, 0, 0)

    qkv_shapes, qkv_specs = [], []
    for _, dil in ATTN_GROUPS:
        for _ in range(3):
            qkv_shapes.append(jax.ShapeDtypeStruct((B, dil, S // dil, D_ATTN), BF16))
            qkv_specs.append(pl.BlockSpec((None, dil, tm // dil, D_ATTN), tile4))
    out_shape = tuple(qkv_shapes) + (
        jax.ShapeDtypeStruct((B, S, D_ATTN), BF16),
        jax.ShapeDtypeStruct((B, S, D), BF16),
        jax.ShapeDtypeStruct((B, S, D), BF16),
        jax.ShapeDtypeStruct((B, ATTN_GROUPS[0][0], 2 * D_ATTN), F32),
        jax.ShapeDtypeStruct((B, ATTN_GROUPS[1][0], 2 * D_ATTN), F32),
        jax.ShapeDtypeStruct((B, ATTN_GROUPS[2][0], 2 * D_ATTN), F32),
        jax.ShapeDtypeStruct((B, CONV_WIDTH - 1, D_RNN), F32),
        jax.ShapeDtypeStruct((B, 1, D_RNN), F32),
    )
    out_specs = tuple(qkv_specs) + (
        pl.BlockSpec((None, tm, D_ATTN), tile3),
        pl.BlockSpec((None, tm, D), tile3),
        pl.BlockSpec((None, tm, D), tile3),
        pl.BlockSpec((None, ATTN_GROUPS[0][0], 2 * D_ATTN), per_seq),
        pl.BlockSpec((None, ATTN_GROUPS[1][0], 2 * D_ATTN), per_seq),
        pl.BlockSpec((None, tm, 2 * D_ATTN), lambda b, m, j: (b, jnp.maximum(m - m0, 0), 0)),
        pl.BlockSpec((None, CONV_WIDTH - 1, D_RNN), per_seq),
        pl.BlockSpec((None, 1, D_RNN), per_seq),
    )
    in_specs = [
        pl.BlockSpec((None, tm, D), tile3),
        pl.BlockSpec((D, TN), lambda b, m, j: (0, j)),
        pl.BlockSpec((1, D), const2),
        pl.BlockSpec((CONV_WIDTH, D_RNN), const2),
        pl.BlockSpec((1, D_RNN), const2),
        pl.BlockSpec((N_GATE_BLOCKS, GATE_BLOCK, GATE_BLOCK), const3),
        pl.BlockSpec((N_GATE_BLOCKS, GATE_BLOCK, GATE_BLOCK), const3),
        pl.BlockSpec((1, D_RNN), const2),
        pl.BlockSpec((1, D_RNN), const2),
        pl.BlockSpec((1, D_RNN), const2),
        pl.BlockSpec((D_RNN, D), const2),
    ]
    scratch = [
        pltpu.VMEM((tm, D), BF16),
        pltpu.VMEM((D // LANES, tm, LANES), F32),
        pltpu.VMEM((D // LANES, tm, LANES), F32),
        pltpu.VMEM((tm, D), BF16),
        pltpu.VMEM((tm, D), BF16),
        pltpu.VMEM((tm, D), BF16),
        pltpu.VMEM((tm, D_RNN), F32),
        pltpu.VMEM((tm, D_RNN), F32),
        pltpu.VMEM((tm, D_RNN), F32),
        pltpu.VMEM((tm, D), F32),
        pltpu.VMEM((tm, TN), F32),
        pltpu.VMEM((8, D_RNN), F32),
        pltpu.VMEM((SEGMENTS, D_RNN), F32),
        pltpu.VMEM((1, D_RNN), F32),
    ]
    return pl.pallas_call(
        functools.partial(_proj_rnn_kernel, tm=tm, n_tiles=n_tiles),
        grid=(B, n_tiles, N_COL_BLOCKS),
        in_specs=in_specs,
        out_specs=out_specs,
        out_shape=out_shape,
        scratch_shapes=scratch,
        compiler_params=pltpu.CompilerParams(
            dimension_semantics=("arbitrary", "arbitrary", "arbitrary"), vmem_limit_bytes=VMEM_LIMIT),
        name="proj_rnn_prompt",
    )(x, w_in, gpre, cw, cb, wa, wx, ba, bx, lam, wro)


def _proj_rnn_sample_kernel(x_ref, w_ref, gpre_ref, cw_ref, cb_ref, wa_ref, wx_ref, ba_ref, bx_ref, lam_ref, wro_ref,
                            sc_ref, h0_ref,
                            att_ref, mixr_ref, sga_ref, conv_ref, h_ref,
                            xn_s, h_s, yr_s):
    j = pl.program_id(0)

    def project():
        return jnp.dot(xn_s[...], w_ref[...], preferred_element_type=F32)

    @pl.when(j == 0)
    def _u_block():
        xv = x_ref[...]
        ms = jnp.mean(xv * xv, axis=-1, keepdims=True)
        xn_s[...] = ((xv * lax.rsqrt(ms + EPS)) * gpre_ref[...]).astype(BF16)
        u = project()
        taps = [sc_ref[:, k * D_RNN:(k + 1) * D_RNN] for k in range(CONV_WIDTH - 1)] + [u]
        for c in range(N_GATE_BLOCKS):
            c0 = c * GATE_BLOCK
            xc = cb_ref[:, c0:c0 + GATE_BLOCK]
            for tap in range(CONV_WIDTH):
                xc = xc + taps[tap][:, c0:c0 + GATE_BLOCK] * cw_ref[tap:tap + 1, c0:c0 + GATE_BLOCK]
            a, b = _lru_gates(xc, c, wa_ref, wx_ref, ba_ref, bx_ref, lam_ref)
            h = a * h0_ref[:, c0:c0 + GATE_BLOCK] + b
            h_s[:, c0:c0 + GATE_BLOCK] = h
            h_ref[:, c0:c0 + GATE_BLOCK] = h
        for k in range(1, CONV_WIDTH):
            conv_ref[:, (k - 1) * D_RNN:k * D_RNN] = taps[k]

    @pl.when(j == 1)
    def _z_rnn_block():
        z = project()
        hz = (h_s[...] * (z * _sigmoid(z))).astype(BF16)
        yr_s[...] = jnp.dot(hz, wro_ref[...], preferred_element_type=F32)

    @pl.when(j == 2)
    def _g_rnn_block():
        mixr_ref[...] = _sigmoid(project()) * yr_s[...]

    @pl.when(j == 3)
    def _g_attn_block():
        sga_ref[...] = _sigmoid(project())

    @pl.when(j >= 4)
    def _attn_blocks():
        att_ref[...] = project()


def _proj_rnn_sample(x, w_in, gpre, cw, cb, wa, wx, ba, bx, lam, wro, sconv, h0):
    N, D = x.shape
    const2 = lambda j: (0, 0)
    const3 = lambda j: (0, 0, 0)
    n_att = N_COL_BLOCKS - 4
    out_shape = (
        jax.ShapeDtypeStruct((N, n_att * TN), F32),
        jax.ShapeDtypeStruct((N, D), F32),
        jax.ShapeDtypeStruct((N, D), F32),
        jax.ShapeDtypeStruct((N, (CONV_WIDTH - 1) * D_RNN), F32),
        jax.ShapeDtypeStruct((N, D_RNN), F32),
    )
    out_specs = (
        pl.BlockSpec((N, TN), lambda j: (0, jnp.maximum(j - 4, 0))),
        pl.BlockSpec((N, D), const2),
        pl.BlockSpec((N, D), const2),
        pl.BlockSpec((N, (CONV_WIDTH - 1) * D_RNN), const2),
        pl.BlockSpec((N, D_RNN), const2),
    )
    in_specs = [
        pl.BlockSpec((N, D), const2),
        pl.BlockSpec((D, TN), lambda j: (0, j)),
        pl.BlockSpec((1, D), const2),
        pl.BlockSpec((CONV_WIDTH, D_RNN), const2),
        pl.BlockSpec((1, D_RNN), const2),
        pl.BlockSpec((N_GATE_BLOCKS, GATE_BLOCK, GATE_BLOCK), const3),
        pl.BlockSpec((N_GATE_BLOCKS, GATE_BLOCK, GATE_BLOCK), const3),
        pl.BlockSpec((1, D_RNN), const2),
        pl.BlockSpec((1, D_RNN), const2),
        pl.BlockSpec((1, D_RNN), const2),
        pl.BlockSpec((D_RNN, D), const2),
        pl.BlockSpec((N, (CONV_WIDTH - 1) * D_RNN), const2),
        pl.BlockSpec((N, D_RNN), const2),
    ]
    scratch = [pltpu.VMEM((N, D), BF16), pltpu.VMEM((N, D_RNN), F32), pltpu.VMEM((N, D), F32)]
    return pl.pallas_call(
        _proj_rnn_sample_kernel,
        grid=(N_COL_BLOCKS,),
        in_specs=in_specs,
        out_specs=out_specs,
        out_shape=out_shape,
        scratch_shapes=scratch,
        compiler_params=pltpu.CompilerParams(dimension_semantics=("arbitrary",), vmem_limit_bytes=VMEM_LIMIT),
        name="proj_rnn_sample",
    )(x, w_in, gpre, cw, cb, wa, wx, ba, bx, lam, wro, sconv, h0)


def _attn_kernel(q_ref, kc_ref, vc_ref, kp_ref, vp_ref, o_ref, st_ref, bias_s, *, group, chunk):
    dilation = ATTN_GROUPS[group][1]
    c = pl.program_id(2)
    nt = (((1,), (1,)), ((), ()))

    @pl.when((pl.program_id(0) == 0) & (pl.program_id(1) == 0) & (c == 0))
    def _init_bias():
        qi = lax.broadcasted_iota(jnp.int32, (Q_BLOCK, Q_BLOCK), 0)
        kj = lax.broadcasted_iota(jnp.int32, (Q_BLOCK, Q_BLOCK), 1)
        steps_prev = Q_BLOCK + qi - kj
        steps_cur = qi - kj
        dist_prev = (steps_prev * dilation).astype(F32)
        dist_cur = (steps_cur * dilation).astype(F32)
        for h in range(H_G):
            slope = float(_SLOPES[group, h])
            bias_s[h] = jnp.where(steps_cur >= 0, -slope * dist_cur, NEG_INF)
            bias_s[H_G + h] = jnp.where(steps_prev <= Q_BLOCK, -slope * dist_prev, NEG_INF)
            bias_s[2 * H_G + h] = jnp.full((Q_BLOCK, Q_BLOCK), NEG_INF, F32)

    lane = lax.broadcasted_iota(jnp.int32, (Q_BLOCK, LANES), 1)
    low = lane < HEAD_DIM
    lane2 = lax.broadcasted_iota(jnp.int32, (2 * Q_BLOCK, LANES), 1)
    low2 = lane2 < HEAD_DIM
    ones_lo = jnp.where(low2, 1.0, 0.0).astype(BF16)
    ones_hi = jnp.where(low2, 0.0, 1.0).astype(BF16)
    stat_lane = lax.broadcasted_iota(jnp.int32, (Q_BLOCK, STAT_LANES), 1)

    def keys_of(block, cur_ref, prev_ref, sl):
        if block == 0:
            return jnp.concatenate([prev_ref[:, sl], cur_ref[0:Q_BLOCK, sl]], axis=0)
        return cur_ref[(block - 1) * Q_BLOCK:(block + 1) * Q_BLOCK, sl]

    for i in range(chunk // Q_BLOCK):
        r0 = i * Q_BLOCK
        first = jnp.where(c == 0, H_G, 0) if i == 0 else 0
        stats = jnp.zeros((Q_BLOCK, STAT_LANES), F32)
        for p in range(H_G // 2):
            sl = slice(LANES * p, LANES * (p + 1))
            qp = q_ref[r0:r0 + Q_BLOCK, sl] * (HEAD_DIM ** -0.5)
            kp = keys_of(i, kc_ref, kp_ref, sl)
            vp = keys_of(i, vc_ref, vp_ref, sl)
            es, ms = [], []
            for hh in range(2):
                h = 2 * p + hh
                msk = low if hh == 0 else jnp.logical_not(low)
                qm = jnp.where(msk, qp, jnp.zeros_like(qp))
                s = lax.dot_general(qm, kp, nt, preferred_element_type=F32)
                s_p = s[:, 0:Q_BLOCK] + bias_s[H_G + h + first]
                s_c = s[:, Q_BLOCK:2 * Q_BLOCK] + bias_s[h]
                mx = jnp.maximum(jnp.max(s_p, axis=-1, keepdims=True), jnp.max(s_c, axis=-1, keepdims=True))
                es.append(jnp.exp(s_p - mx).astype(BF16))
                es.append(jnp.exp(s_c - mx).astype(BF16))
                ms.append(mx)
            vm0 = jnp.where(low2, vp, jnp.zeros_like(vp))
            vm1 = jnp.where(low2, jnp.zeros_like(vp), vp)
            w = jnp.concatenate([jnp.concatenate([vm0, ones_lo], axis=1),
                                 jnp.concatenate([vm1, ones_hi], axis=1)], axis=0)
            acc = jnp.dot(jnp.concatenate(es, axis=1), w, preferred_element_type=F32)
            l_pair = acc[:, LANES:2 * LANES]
            o_ref[r0:r0 + Q_BLOCK, sl] = (acc[:, 0:LANES] / l_pair).astype(o_ref.dtype)
            lse_pair = jnp.where(low, ms[0], ms[1]) + jnp.log(l_pair)
            keep = ((stat_lane % (STAT_LANES // 2)) // STAT_LANES_PER_HEAD) == p
            stats = jnp.where(keep, lse_pair, stats)
        st_ref[r0:r0 + Q_BLOCK, :] = stats


def _attn_group(q, k, v, group):
    B, dil, L, _ = q.shape
    chunk = min(ATTN_CHUNK, L)
    assert L % chunk == 0 and chunk % Q_BLOCK == 0
    cur = lambda b, r, c: (b, r, c, 0)
    prev = lambda b, r, c: (b, r, jnp.maximum(c * (chunk // Q_BLOCK) - 1, 0), 0)
    blk = (None, None, chunk, D_ATTN)
    pblk = (None, None, Q_BLOCK, D_ATTN)
    return pl.pallas_call(
        functools.partial(_attn_kernel, group=group, chunk=chunk),
        grid=(B, dil, L // chunk),
        in_specs=[pl.BlockSpec(blk, cur), pl.BlockSpec(blk, cur), pl.BlockSpec(blk, cur),
                  pl.BlockSpec(pblk, prev), pl.BlockSpec(pblk, prev)],
        out_specs=[pl.BlockSpec(blk, cur), pl.BlockSpec((None, None, chunk, STAT_LANES), cur)],
        out_shape=[jax.ShapeDtypeStruct((B, dil, L, D_ATTN), BF16),
                   jax.ShapeDtypeStruct((B, dil, L, STAT_LANES), F32)],
        scratch_shapes=[pltpu.VMEM((3 * H_G, Q_BLOCK, Q_BLOCK), F32)],
        compiler_params=pltpu.CompilerParams(
            dimension_semantics=("arbitrary", "arbitrary", "arbitrary"), vmem_limit_bytes=VMEM_LIMIT),
        name=f"attn_group{group}",
    )(q, k, v, k, v)


def _attn_sample_kernel(qkv_ref, c0_ref, c1_ref, c2_ref, o_ref):
    caches = (c0_ref, c1_ref, c2_ref)
    scale = HEAD_DIM ** -0.5
    ri = lax.broadcasted_iota(jnp.int32, (HEAD_DIM, HEAD_DIM), 0)
    ci = lax.broadcasted_iota(jnp.int32, (HEAD_DIM, HEAD_DIM), 1)
    eye = ri == ci

    def to_col(row):
        return jnp.sum(jnp.where(eye, jnp.broadcast_to(row, (HEAD_DIM, HEAD_DIM)), 0.0), axis=-1, keepdims=True)

    def to_row(col):
        return jnp.sum(jnp.where(eye, jnp.broadcast_to(col, (HEAD_DIM, HEAD_DIM)), 0.0), axis=0, keepdims=True)

    outs = [[None] * N_GROUPS for _ in range(H_G)]
    lses = [[None] * N_GROUPS for _ in range(H_G)]
    for g, (window, dilation) in enumerate(ATTN_GROUPS):
        c_ref = caches[g]
        pos = lax.broadcasted_iota(jnp.int32, (1, window), 1)
        on_stride = (pos % dilation) == 0
        dist = (window - pos).astype(F32)
        q = qkv_ref[3 * g] * scale
        s_new_all = jnp.sum(qkv_ref[3 * g + 1] * q, axis=-1, keepdims=True)
        for h in range(H_G):
            bias = jnp.where(on_stride, -float(_SLOPES[g, h]) * dist, NEG_INF)
            q_col = to_col(q[h:h + 1, :])
            v_col = to_col(qkv_ref[3 * g + 2, h:h + 1, :])
            s = jnp.sum(c_ref[0, h] * q_col, axis=0, keepdims=True) + bias
            s_new = s_new_all[h:h + 1, :]
            mx = jnp.maximum(jnp.max(s, axis=-1, keepdims=True), s_new)
            e = jnp.exp(s - mx)
            e_new = jnp.exp(s_new - mx)
            l = jnp.sum(e, axis=-1, keepdims=True) + e_new
            o = jnp.sum(c_ref[1, h] * e, axis=-1, keepdims=True) + e_new * v_col
            outs[h][g] = o / l
            lses[h][g] = mx + jnp.log(l)
    for h in range(H_G):
        mm = jnp.maximum(jnp.maximum(lses[h][0], lses[h][1]), lses[h][2])
        ws = [jnp.exp(x - mm) for x in lses[h]]
        den = ws[0] + ws[1] + ws[2]
        col = (outs[h][0] * ws[0] + outs[h][1] * ws[1] + outs[h][2] * ws[2]) / den
        o_ref[h:h + 1, :] = to_row(col)


def _attn_sample(qkv_s, caches, layer):
    N = qkv_s.shape[0]
    views = []
    for (window, _), c in zip(ATTN_GROUPS, caches):
        assert c.shape[2] == window
        views.append(jnp.transpose(c, (0, 1, 3, 4, 5, 2)))
    in_specs = [pl.BlockSpec((None, 3 * N_GROUPS, H_G, HEAD_DIM), lambda i: (i, 0, 0, 0))]
    in_specs += [pl.BlockSpec((None, None, 2, H_G, HEAD_DIM, window), lambda i: (layer, i, 0, 0, 0, 0))
                 for window, _ in ATTN_GROUPS]
    return pl.pallas_call(
        _attn_sample_kernel,
        grid=(N,),
        in_specs=in_specs,
        out_specs=pl.BlockSpec((None, H_G, HEAD_DIM), lambda i: (i, 0, 0)),
        out_shape=jax.ShapeDtypeStruct((N, H_G, HEAD_DIM), F32),
        compiler_params=pltpu.CompilerParams(dimension_semantics=("arbitrary",), vmem_limit_bytes=VMEM_LIMIT),
        name="attn_sample",
    )(qkv_s, *views)


def _finish(att, z_ref, sga_ref, mixr_ref, x_ref, wao_ref, wo_ref, gpost_ref, y_ref):
    z = z_ref[...].astype(F32)
    ya_in = (att * (z * _sigmoid(z))).astype(BF16)
    ya = jnp.dot(ya_in, wao_ref[...], preferred_element_type=F32)
    mixed = mixr_ref[...].astype(F32) + sga_ref[...].astype(F32) * ya
    out = jnp.dot(mixed.astype(BF16), wo_ref[...], preferred_element_type=F32)
    ms = jnp.mean(out * out, axis=-1, keepdims=True)
    y_ref[...] = x_ref[...] + (out * lax.rsqrt(ms + EPS)) * gpost_ref[...]


def _out_kernel(att_ref, z_ref, sga_ref, mixr_ref, x_ref, wao_ref, wo_ref, gpost_ref, y_ref):
    _finish(att_ref[...].astype(F32), z_ref, sga_ref, mixr_ref, x_ref, wao_ref, wo_ref, gpost_ref, y_ref)


def _merge_out_kernel(o0_ref, o1_ref, o2_ref, s0_ref, s1_ref, s2_ref, z_ref, sga_ref, mixr_ref, x_ref,
                      wao_ref, wo_ref, gpost_ref, y_ref, nat1_s, nat2_s, st1_s, st2_s, *, tm):
    for o_ref, s_ref, nat_s, stn_s, dil in ((o1_ref, s1_ref, nat1_s, st1_s, ATTN_GROUPS[1][1]),
                                            (o2_ref, s2_ref, nat2_s, st2_s, ATTN_GROUPS[2][1])):
        p = tm // dil
        for r in range(dil):
            stn_s[pl.ds(r, p, stride=dil), :] = s_ref[r]
            for c in range(D_ATTN // LANES):
                nat_s[c, pl.ds(r, p, stride=dil), :] = o_ref[r, :, c * LANES:(c + 1) * LANES].astype(F32)

    lse = [s0_ref[...], st1_s[...], st2_s[...]]
    mm = jnp.maximum(jnp.maximum(lse[0], lse[1]), lse[2])
    ws = [jnp.exp(x - mm) for x in lse]
    den = ws[0] + ws[1] + ws[2]
    src = lax.broadcasted_iota(jnp.int32, (STAT_LANES, D_ATTN), 0)
    dst_head = lax.broadcasted_iota(jnp.int32, (STAT_LANES, D_ATTN), 1) // HEAD_DIM
    expand = jnp.where(src == (dst_head % 2) * (STAT_LANES // 2) + (dst_head // 2) * STAT_LANES_PER_HEAD,
                       1.0, 0.0).astype(BF16)
    outs = [o0_ref[...].astype(F32),
            jnp.concatenate([nat1_s[c] for c in range(D_ATTN // LANES)], axis=1),
            jnp.concatenate([nat2_s[c] for c in range(D_ATTN // LANES)], axis=1)]
    att = jnp.zeros((tm, D_ATTN), F32)
    for g in range(N_GROUPS):
        wexp = jnp.dot((ws[g] / den).astype(BF16), expand, preferred_element_type=F32)
        att = att + wexp * outs[g]
    _finish(att, z_ref, sga_ref, mixr_ref, x_ref, wao_ref, wo_ref, gpost_ref, y_ref)


def _merge_out_prompt(os, sts, zg, sga, mixr, x, wao, wo, gpost):
    B, S, D = x.shape
    tm = TM_OUT
    tile3 = lambda b, m: (b, m, 0)
    tile4 = lambda b, m: (b, 0, m, 0)
    const = lambda b, m: (0, 0)
    in_specs = []
    for (_, dil), width in [(g, D_ATTN) for g in ATTN_GROUPS] + [(g, STAT_LANES) for g in ATTN_GROUPS]:
        if dil == 1:
            in_specs.append(pl.BlockSpec((None, None, tm, width), tile4))
        else:
            in_specs.append(pl.BlockSpec((None, dil, tm // dil, width), tile4))
    in_specs += [
        pl.BlockSpec((None, tm, D_ATTN), tile3),
        pl.BlockSpec((None, tm, D), tile3),
        pl.BlockSpec((None, tm, D), tile3),
        pl.BlockSpec((None, tm, D), tile3),
        pl.BlockSpec((D_ATTN, D), const),
        pl.BlockSpec((D, D), const),
        pl.BlockSpec((1, D), const),
    ]
    scratch = [
        pltpu.VMEM((D_ATTN // LANES, tm, LANES), F32),
        pltpu.VMEM((D_ATTN // LANES, tm, LANES), F32),
        pltpu.VMEM((tm, STAT_LANES), F32),
        pltpu.VMEM((tm, STAT_LANES), F32),
    ]
    return pl.pallas_call(
        functools.partial(_merge_out_kernel, tm=tm),
        grid=(B, S // tm),
        in_specs=in_specs,
        out_specs=pl.BlockSpec((None, tm, D), tile3),
        out_shape=jax.ShapeDtypeStruct((B, S, D), F32),
        scratch_shapes=scratch,
        compiler_params=pltpu.CompilerParams(
            dimension_semantics=("arbitrary", "arbitrary"), vmem_limit_bytes=VMEM_LIMIT),
        name="merge_out_prompt",
    )(*os, *sts, zg, sga, mixr, x, wao, wo, gpost)


def _out_sample(att, z, sga, mixr, x, wao, wo, gpost):
    N, D = x.shape
    full = lambda shape: pl.BlockSpec(shape, lambda i: (0, 0))
    return pl.pallas_call(
        _out_kernel,
        grid=(1,),
        in_specs=[full((N, D_ATTN)), full((N, D_ATTN)), full((N, D)), full((N, D)), full((N, D)),
                  full((D_ATTN, D)), full((D, D)), full((1, D))],
        out_specs=full((N, D)),
        out_shape=jax.ShapeDtypeStruct((N, D), F32),
        compiler_params=pltpu.CompilerParams(dimension_semantics=("arbitrary",), vmem_limit_bytes=VMEM_LIMIT),
        name="out_sample",
    )(att, z, sga, mixr, x, wao, wo, gpost)


def _block_diag_chunks(w):
    per = GATE_BLOCK // RNN_BLOCK
    w = w.reshape(N_GATE_BLOCKS, per, RNN_BLOCK, RNN_BLOCK)
    eye = jnp.eye(per, dtype=w.dtype)
    dense = w[:, :, :, None, :] * eye[None, :, None, :, None]
    return dense.reshape(N_GATE_BLOCKS, GATE_BLOCK, GATE_BLOCK)


def _reorder_w_in(w_in):
    sizes = (D_RNN, D_RNN, N_GROUPS * D_ATTN, N_GROUPS * D_ATTN, N_GROUPS * D_ATTN, D_ATTN, D_MODEL, D_MODEL)
    starts = np.concatenate([[0], np.cumsum(sizes)[:-1]])
    u0, z0, q0, k0, v0, za0, gr0, ga0 = (int(s) for s in starts)
    pieces = [(u0, D_RNN), (z0, D_RNN), (gr0, D_MODEL), (ga0, D_MODEL), (za0, D_ATTN)]
    for g in range(N_GROUPS):
        pieces += [(q0 + g * D_ATTN, D_ATTN), (k0 + g * D_ATTN, D_ATTN), (v0 + g * D_ATTN, D_ATTN)]
    return jnp.concatenate([w_in[:, s:s + n] for s, n in pieces], axis=1).astype(BF16)


def _layer(layer, yp, ys, sconv, h0, caches, norm_pre, norm_post, w_in, conv_w, conv_b, lru_w_a, lru_b_a, lru_w_x,
           lru_b_x, lru_lambda, w_rnn_out, w_attn_out, w_out):
    B, S, D = yp.shape
    N = ys.shape[0]
    row = lambda v: v.reshape(1, -1)
    w_in_b = _reorder_w_in(w_in)
    wa = _block_diag_chunks(lru_w_a).astype(BF16)
    wx = _block_diag_chunks(lru_w_x).astype(BF16)
    wro = w_rnn_out.astype(BF16)
    wao = w_attn_out.astype(BF16)
    wo = w_out.astype(BF16)
    shared = (w_in_b, row(norm_pre), conv_w, row(conv_b), wa, wx, row(lru_b_a), row(lru_b_x), row(lru_lambda), wro)

    outs = _proj_rnn_prompt(yp, *shared)
    qkv, (zg, mixr, sga, kv0, kv1, kv2, conv_p, h_p) = outs[:9], outs[9:]
    os, sts = [], []
    for g in range(N_GROUPS):
        o, st = _attn_group(qkv[3 * g], qkv[3 * g + 1], qkv[3 * g + 2], g)
        os.append(o)
        sts.append(st)
    y_p = _merge_out_prompt(os, sts, zg, sga, mixr, yp, wao, wo, row(norm_post))
    kv_p = [kv.reshape(B, kv.shape[1], 2, H_G, HEAD_DIM) for kv in (kv0, kv1, kv2)]

    xs = ys.reshape(N, D)
    att_in, mixr_s, sga_s, conv_s, h_s = _proj_rnn_sample(
        xs, *shared, sconv.reshape(N, (CONV_WIDTH - 1) * D_RNN), h0)
    z_s = att_in[:, 0:D_ATTN]
    qkv_s = att_in[:, D_ATTN:].reshape(N, 3 * N_GROUPS, H_G, HEAD_DIM)
    att_s = _attn_sample(qkv_s, caches, layer).reshape(N, D_ATTN)
    y_s = _out_sample(att_s, z_s, sga_s, mixr_s, xs, wao, wo, row(norm_post))
    kv_s = [qkv_s[:, 3 * g + 1:3 * g + 3].reshape(N, 1, 2, H_G, HEAD_DIM) for g in range(N_GROUPS)]

    return (y_p, y_s.reshape(N, 1, D), conv_p, conv_s.reshape(N, CONV_WIDTH - 1, D_RNN),
            h_p.reshape(B, D_RNN), h_s, kv_p, kv_s)


def kernel(x_prompt, x_sample, state_conv, state_h, cache_kv_w128, cache_kv_w512, cache_kv_w2048, norm_pre, norm_post, w_in, conv_w, conv_b, lru_w_a, lru_b_a, lru_w_x, lru_b_x, lru_lambda, w_rnn_out, w_attn_out, w_out):
    depth = norm_pre.shape[0]
    caches = (cache_kv_w128, cache_kv_w512, cache_kv_w2048)
    yp, ys = x_prompt, x_sample
    conv_p, conv_s, h_p, h_s = [], [], [], []
    kvp = ([], [], [])
    kvs = ([], [], [])
    for l in range(depth):
        yp, ys, cp, cs, hp, hs, kv_p, kv_s = _layer(
            l, yp, ys, state_conv[l], state_h[l], caches,
            norm_pre[l], norm_post[l], w_in[l], conv_w[l], conv_b[l], lru_w_a[l], lru_b_a[l], lru_w_x[l],
            lru_b_x[l], lru_lambda[l], w_rnn_out[l], w_attn_out[l], w_out[l])
        conv_p.append(cp)
        conv_s.append(cs)
        h_p.append(hp)
        h_s.append(hs)
        for g in range(N_GROUPS):
            kvp[g].append(kv_p[g])
            kvs[g].append(kv_s[g])
    return (yp, ys, jnp.stack(conv_p), jnp.stack(conv_s), jnp.stack(h_p), jnp.stack(h_s),
            jnp.stack(kvp[0]), jnp.stack(kvs[0]), jnp.stack(kvp[1]), jnp.stack(kvs[1]),
            jnp.stack(kvp[2]), jnp.stack(kvs[2]))
```

```python
import functools

import numpy as np
import jax
import jax.numpy as jnp
from jax import lax
from jax.experimental import pallas as pl
from jax.experimental.pallas import tpu as pltpu

F32 = jnp.float32
BF16 = jnp.bfloat16

D_MODEL = 1024
D_RNN = 1024
N_RNN_BLOCKS = 16
RNN_BLOCK = D_RNN // N_RNN_BLOCKS
CONV_WIDTH = 4
LRU_C = 8.0
HEAD_DIM = 64
H_G = 8
ATTN_GROUPS = ((128, 1), (512, 4), (2048, 16))
N_GROUPS = 3
D_ATTN = H_G * HEAD_DIM
Q_BLOCK = 128
ALIBI_MAX = 8.0
EPS = 1e-6
NEG_INF = -1e30
D_IN = 2 * D_RNN + 3 * N_GROUPS * D_ATTN + D_ATTN + 2 * D_MODEL

LANES = 128
TN = 1024
N_COL_BLOCKS = D_IN // TN
GATE_BLOCK = 256
N_GATE_BLOCKS = D_RNN // GATE_BLOCK
STAT_LANES = LANES
STAT_LANES_PER_HEAD = STAT_LANES // H_G

TM_PROMPT = 512
SEGMENTS = 8
TM_OUT = 512
ATTN_CHUNK = 512
ROW_CHUNK = 64
GATE_ROWS = 128
VMEM_LIMIT = 56 * 1024 * 1024


def _alibi_slopes():
    n = N_GROUPS * H_G
    s = np.float32(2.0) ** (np.float32(-ALIBI_MAX) * np.arange(1, n + 1, dtype=np.float32) / np.float32(n))
    return s.reshape(N_GROUPS, H_G)


_SLOPES = _alibi_slopes()


def _stat_lane(h):
    return (h % 2) * (STAT_LANES // 2) + (h // 2) * STAT_LANES_PER_HEAD


def _softplus(y):
    return jnp.maximum(y, 0.0) + jnp.log1p(jnp.exp(-jnp.abs(y)))


def _sigmoid(x):
    return 0.5 * jnp.tanh(0.5 * x) + 0.5


def _for_rows(n_rows, chunk, fn):
    if n_rows <= chunk:
        fn(0)
        return

    def body(c, carry):
        fn(pl.multiple_of(c * chunk, chunk))
        return carry

    lax.fori_loop(0, n_rows // chunk, body, 0)


def _lru_gates(xc, c, wa_ref, wx_ref, ba_ref, bx_ref, lam_ref):
    c0 = c * GATE_BLOCK
    xcb = xc.astype(BF16)
    r = _sigmoid(jnp.dot(xcb, wa_ref[c], preferred_element_type=F32) + ba_ref[:, c0:c0 + GATE_BLOCK])
    i = _sigmoid(jnp.dot(xcb, wx_ref[c], preferred_element_type=F32) + bx_ref[:, c0:c0 + GATE_BLOCK])
    log_a = (-LRU_C * r) * _softplus(-lam_ref[:, c0:c0 + GATE_BLOCK])
    a = jnp.exp(log_a)
    v = 1.0 - a * a
    b = jnp.where(v > 0.0, v * lax.rsqrt(v), 0.0) * i * xc
    return a, b


def _sample_head_attention(h, qs_ref, cache_refs, att_ref):
    scale = HEAD_DIM ** -0.5
    ri = lax.broadcasted_iota(jnp.int32, (HEAD_DIM, HEAD_DIM), 0)
    ci = lax.broadcasted_iota(jnp.int32, (HEAD_DIM, HEAD_DIM), 1)
    eye = ri == ci

    def to_col(row):
        return jnp.sum(jnp.where(eye, jnp.broadcast_to(row, (HEAD_DIM, HEAD_DIM)), 0.0), axis=-1, keepdims=True)

    def to_row(col):
        return jnp.sum(jnp.where(eye, jnp.broadcast_to(col, (HEAD_DIM, HEAD_DIM)), 0.0), axis=0, keepdims=True)

    outs, lses = [], []
    for g, (window, dilation) in enumerate(ATTN_GROUPS):
        c_ref = cache_refs[g]
        pos = lax.broadcasted_iota(jnp.int32, (1, window), 1)
        dist = (window - pos).astype(F32)
        bias = jnp.where((pos % dilation) == 0, -float(_SLOPES[g, h]) * dist, NEG_INF)
        q = qs_ref[3 * g, h:h + 1, :] * scale
        s_new = jnp.sum(qs_ref[3 * g + 1, h:h + 1, :] * q, axis=-1, keepdims=True)
        v_col = to_col(qs_ref[3 * g + 2, h:h + 1, :])
        s = jnp.sum(c_ref[0] * to_col(q), axis=0, keepdims=True) + bias
        mx = jnp.maximum(jnp.max(s, axis=-1, keepdims=True), s_new)
        e = jnp.exp(s - mx)
        e_new = jnp.exp(s_new - mx)
        l = jnp.sum(e, axis=-1, keepdims=True) + e_new
        o = jnp.sum(c_ref[1] * e, axis=-1, keepdims=True) + e_new * v_col
        outs.append(o / l)
        lses.append(mx + jnp.log(l))
    mm = jnp.maximum(jnp.maximum(lses[0], lses[1]), lses[2])
    ws = [jnp.exp(x - mm) for x in lses]
    col = (outs[0] * ws[0] + outs[1] * ws[1] + outs[2] * ws[2]) / (ws[0] + ws[1] + ws[2])
    att_ref[h:h + 1, :] = to_row(col)


def _proj_rnn_kernel(x_ref, w_ref, gpre_ref, cw_ref, cb_ref, wa_ref, wx_ref, ba_ref, bx_ref, lam_ref, wro_ref,
                     qs_ref, c0_ref, c1_ref, c2_ref,
                     q0_ref, k0_ref, v0_ref, q1_ref, k1_ref, v1_ref, q2_ref, k2_ref, v2_ref,
                     zg_ref, mixr_ref, sga_ref, kv0_ref, kv1_ref, kv2_ref, conv_ref, h_ref, att_ref,
                     xn_s, xnf_s, xseg_s, xnseg_s, xn4_s, xn16_s, u_s, a_s, b_s, yr_s,
                     cprev_s, cin_s, carry_s,
                     *, tm, n_tiles):
    m = pl.program_id(1)
    j = pl.program_id(2)
    acc_s = u_s

    def sample_head(h):
        _sample_head_attention(h, qs_ref, (c0_ref, c1_ref, c2_ref), att_ref)
    last_tile = m == n_tiles - 1
    in_kv2 = m >= n_tiles - ATTN_GROUPS[2][0] // tm
    half = TN // 2
    d1, d2 = ATTN_GROUPS[1][1], ATTN_GROUPS[2][1]

    def project(lhs_s):
        return jnp.dot(lhs_s[...], w_ref[...], preferred_element_type=F32)

    def emit(out_ref, c0, dilation):
        p = tm // dilation
        for r in range(dilation):
            out_ref[r] = acc_s[r * p:(r + 1) * p, c0:c0 + half].astype(BF16)

    def natural_tail(out_ref, out_c0, c0, dilation):
        p = tm // dilation
        for c in range(half // LANES):
            for r in range(dilation):
                xnf_s[c, pl.ds(r, p, stride=dilation), :] = \
                    acc_s[r * p:(r + 1) * p, c0 + c * LANES:c0 + (c + 1) * LANES]
            out_ref[:, out_c0 + c * LANES:out_c0 + (c + 1) * LANES] = xnf_s[c]

    seg = tm // SEGMENTS
    n_slab = D_MODEL // LANES

    @pl.when(j == 0)
    def _u_block():
        def norm(s, carry):
            r0 = pl.multiple_of(s * seg, seg)
            xv = x_ref[pl.ds(r0, seg), :]
            ms = jnp.mean(xv * xv, axis=-1, keepdims=True)
            xn = (xv * lax.rsqrt(ms + EPS)) * gpre_ref[...]
            xn_s[pl.ds(r0, seg), :] = xn.astype(BF16)
            for c in range(n_slab):
                piece = xn[:, c * LANES:(c + 1) * LANES]
                xnf_s[c, pl.ds(r0, seg), :] = piece
                xseg_s[c, pl.ds(s, seg, stride=SEGMENTS), :] = piece
            return carry

        lax.fori_loop(0, SEGMENTS, norm, 0)

        for c in range(n_slab):
            xnseg_s[:, c * LANES:(c + 1) * LANES] = xseg_s[c].astype(BF16)

        for dil, dst in ((d1, xn4_s), (d2, xn16_s)):
            p = tm // dil
            for r in range(dil):
                for c in range(n_slab):
                    dst[r * p:(r + 1) * p, c * LANES:(c + 1) * LANES] = \
                        xnf_s[c, pl.ds(r, p, stride=dil), :].astype(BF16)

        @pl.when(m == 0)
        def _reset():
            cprev_s[...] = jnp.zeros((8, D_RNN), F32)
            carry_s[...] = jnp.zeros((1, D_RNN), F32)

        u_s[...] = project(xnseg_s)

        sub = lax.broadcasted_iota(jnp.int32, (SEGMENTS, GATE_BLOCK), 0)
        for c in range(N_GATE_BLOCKS):
            cols = slice(c * GATE_BLOCK, (c + 1) * GATE_BLOCK)

            def wrapped(t):
                v = pltpu.roll(u_s[SEGMENTS * (seg + t):SEGMENTS * (seg + t + 1), cols], 1, axis=0)
                return jnp.where(sub == 0, cprev_s[8 + t:9 + t, cols], v)

            wrap = {t: wrapped(t) for t in range(1 - CONV_WIDTH, 0)}
            for rc in range(tm // GATE_ROWS):
                r0 = rc * GATE_ROWS
                xc = cb_ref[:, cols] + u_s[r0:r0 + GATE_ROWS, cols] * cw_ref[CONV_WIDTH - 1:CONV_WIDTH, cols]
                for sh in range(1, CONV_WIDTH):
                    if r0 == 0:
                        head = [wrap[t - sh] for t in range(sh)]
                        ush = jnp.concatenate(head + [u_s[0:GATE_ROWS - SEGMENTS * sh, cols]], axis=0)
                    else:
                        ush = u_s[r0 - SEGMENTS * sh:r0 - SEGMENTS * sh + GATE_ROWS, cols]
                    xc = xc + ush * cw_ref[CONV_WIDTH - 1 - sh:CONV_WIDTH - sh, cols]
                a, b = _lru_gates(xc, c, wa_ref, wx_ref, ba_ref, bx_ref, lam_ref)
                a_s[r0:r0 + GATE_ROWS, cols] = a
                b_s[r0:r0 + GATE_ROWS, cols] = b

        def scan(t, carry):
            hloc, cum = carry
            r0 = pl.multiple_of(t * SEGMENTS, SEGMENTS)
            av = a_s[pl.ds(r0, SEGMENTS), :]
            hloc = av * hloc + b_s[pl.ds(r0, SEGMENTS), :]
            cum = av * cum
            b_s[pl.ds(r0, SEGMENTS), :] = hloc
            a_s[pl.ds(r0, SEGMENTS), :] = cum
            return hloc, cum

        h_end, a_end = lax.fori_loop(
            0, seg, scan, (jnp.zeros((SEGMENTS, D_RNN), F32), jnp.ones((SEGMENTS, D_RNN), F32)), unroll=4)
        state = carry_s[...]
        for s in range(SEGMENTS):
            cin_s[s:s + 1, :] = state
            state = h_end[s:s + 1, :] + a_end[s:s + 1, :] * state
        carry_s[...] = state
        h_ref[...] = state
        for k in range(1, CONV_WIDTH):
            last = u_s[tm - SEGMENTS * (k - 1) - 1:tm - SEGMENTS * (k - 1), :]
            conv_ref[CONV_WIDTH - 1 - k:CONV_WIDTH - k, :] = last
            cprev_s[8 - k:9 - k, :] = last

    @pl.when(j == 1)
    def _z_rnn_block():
        sample_head(0)
        z = project(xnseg_s)
        split = (seg, SEGMENTS, D_RNN)
        h = b_s[...].reshape(split) + a_s[...].reshape(split) * cin_s[...][None]
        hz = (h.reshape(tm, D_RNN) * (z * _sigmoid(z))).astype(BF16)
        yr_s[...] = jnp.dot(hz, wro_ref[...], preferred_element_type=F32)

    @pl.when(j == 2)
    def _g_rnn_block():
        sample_head(1)
        mix = _sigmoid(project(xnseg_s)) * yr_s[...]
        for c in range(n_slab):
            xseg_s[c] = mix[:, c * LANES:(c + 1) * LANES]
        for s in range(SEGMENTS):
            for c in range(n_slab):
                mixr_ref[s * seg:(s + 1) * seg, c * LANES:(c + 1) * LANES] = \
                    xseg_s[c, pl.ds(s, seg, stride=SEGMENTS), :].astype(BF16)

    @pl.when(j == 3)
    def _g_attn_block():
        sample_head(2)
        sga_ref[...] = _sigmoid(project(xn_s)).astype(BF16)

    @pl.when(j == 4)
    def _zattn_q0_block():
        sample_head(3)
        acc_s[...] = project(xn_s)
        zg_ref[...] = acc_s[:, 0:half].astype(BF16)
        emit(q0_ref, half, 1)

    @pl.when(j == 5)
    def _k0_v0_block():
        sample_head(4)
        acc_s[...] = project(xn_s)
        emit(k0_ref, 0, 1)
        emit(v0_ref, half, 1)

        @pl.when(last_tile)
        def _():
            kv0_ref[...] = acc_s[tm - ATTN_GROUPS[0][0]:tm, :]

    @pl.when(j == 6)
    def _q1_k1_block():
        sample_head(5)
        acc_s[...] = project(xn4_s)
        emit(q1_ref, 0, d1)
        emit(k1_ref, half, d1)

        @pl.when(last_tile)
        def _():
            natural_tail(kv1_ref, 0, half, d1)

    @pl.when(j == 7)
    def _v1_q2_block():
        sample_head(6)
        acc_s[:, 0:half] =jnp.dot(xn4_s[...], w_ref[:, 0:half], preferred_element_type=F32)
        acc_s[:, half:TN] = jnp.dot(xn16_s[...], w_ref[:, half:TN], preferred_element_type=F32)
        emit(v1_ref, 0, d1)
        emit(q2_ref, half, d2)

        @pl.when(last_tile)
        def _():
            natural_tail(kv1_ref, half, 0, d1)

    @pl.when(j == 8)
    def _k2_v2_block():
        sample_head(7)
        acc_s[...] = project(xn16_s)
        emit(k2_ref, 0, d2)
        emit(v2_ref, half, d2)

        @pl.when(in_kv2)
        def _():
            natural_tail(kv2_ref, 0, 0, d2)
            natural_tail(kv2_ref, half, half, d2)


def _proj_rnn_prompt(x, w_in, gpre, cw, cb, wa, wx, ba, bx, lam, wro, qkv_s, caches, layer):
    B, S, D = x.shape
    tm = TM_PROMPT
    n_tiles = S // tm
    m0 = n_tiles - ATTN_GROUPS[2][0] // tm
    assert S % tm == 0 and tm == ATTN_GROUPS[1][0] and ATTN_GROUPS[2][0] % tm == 0 and S >= ATTN_GROUPS[2][0]
    N = qkv_s.shape[0]
    assert N == B * n_tiles and N_COL_BLOCKS == H_G + 1
    cache_views = [jnp.transpose(c, (0, 1, 3, 4, 5, 2)) for c in caches]
    sample_seq = lambda b, m, j: (b * n_tiles + m, 0, 0, 0)
    cache_head = lambda b, m, j: (layer, b * n_tiles + m, 0, jnp.clip(j - 1, 0, H_G - 1), 0, 0)

    const2 = lambda b, m, j: (0, 0)
    const3 = lambda b, m, j: (0, 0, 0)
    tile3 = lambda b, m, j: (b, m, 0)
    tile4 = lambda b, m, j: (b, 0, m, 0)
    per_seq = lambda b, m, j: (b, 0, 0)

    qkv_shapes, qkv_specs = [], []
    for _, dil in ATTN_GROUPS:
        for _ in range(3):
            qkv_shapes.append(jax.ShapeDtypeStruct((B, dil, S // dil, D_ATTN), BF16))
            qkv_specs.append(pl.BlockSpec((None, dil, tm // dil, D_ATTN), tile4))
    out_shape = tuple(qkv_shapes) + (
        jax.ShapeDtypeStruct((B, S, D_ATTN), BF16),
        jax.ShapeDtypeStruct((B, S, D), BF16),
        jax.ShapeDtypeStruct((B, S, D), BF16),
        jax.ShapeDtypeStruct((B, ATTN_GROUPS[0][0], 2 * D_ATTN), F32),
        jax.ShapeDtypeStruct((B, ATTN_GROUPS[1][0], 2 * D_ATTN), F32),
        jax.ShapeDtypeStruct((B, ATTN_GROUPS[2][0], 2 * D_ATTN), F32),
        jax.ShapeDtypeStruct((B, CONV_WIDTH - 1, D_RNN), F32),
        jax.ShapeDtypeStruct((B, 1, D_RNN), F32),
        jax.ShapeDtypeStruct((N, H_G, HEAD_DIM), F32),
    )
    out_specs = tuple(qkv_specs) + (
        pl.BlockSpec((None, tm, D_ATTN), tile3),
        pl.BlockSpec((None, tm, D), tile3),
        pl.BlockSpec((None, tm, D), tile3),
        pl.BlockSpec((None, ATTN_GROUPS[0][0], 2 * D_ATTN), per_seq),
        pl.BlockSpec((None, ATTN_GROUPS[1][0], 2 * D_ATTN), per_seq),
        pl.BlockSpec((None, tm, 2 * D_ATTN), lambda b, m, j: (b, jnp.maximum(m - m0, 0), 0)),
        pl.BlockSpec((None, CONV_WIDTH - 1, D_RNN), per_seq),
        pl.BlockSpec((None, 1, D_RNN), per_seq),
        pl.BlockSpec((None, H_G, HEAD_DIM), lambda b, m, j: (b * n_tiles + m, 0, 0)),
    )
    in_specs = [
        pl.BlockSpec((None, tm, D), tile3),
        pl.BlockSpec((D, TN), lambda b, m, j: (0, j)),
        pl.BlockSpec((1, D), const2),
        pl.BlockSpec((CONV_WIDTH, D_RNN), const2),
        pl.BlockSpec((1, D_RNN), const2),
        pl.BlockSpec((N_GATE_BLOCKS, GATE_BLOCK, GATE_BLOCK), const3),
        pl.BlockSpec((N_GATE_BLOCKS, GATE_BLOCK, GATE_BLOCK), const3),
        pl.BlockSpec((1, D_RNN), const2),
        pl.BlockSpec((1, D_RNN), const2),
        pl.BlockSpec((1, D_RNN), const2),
        pl.BlockSpec((D_RNN, D), const2),
        pl.BlockSpec((None, 3 * N_GROUPS, H_G, HEAD_DIM), sample_seq),
    ]
    in_specs += [pl.BlockSpec((None, None, 2, None, HEAD_DIM, window), cache_head) for window, _ in ATTN_GROUPS]
    scratch = [
        pltpu.VMEM((tm, D), BF16),
        pltpu.VMEM((D // LANES, tm, LANES), F32),
        pltpu.VMEM((D // LANES, tm, LANES), F32),
        pltpu.VMEM((tm, D), BF16),
        pltpu.VMEM((tm, D), BF16),
        pltpu.VMEM((tm, D), BF16),
        pltpu.VMEM((tm, D_RNN), F32),
        pltpu.VMEM((tm, D_RNN), F32),
        pltpu.VMEM((tm, D_RNN), F32),
        pltpu.VMEM((tm, D), F32),
        pltpu.VMEM((8, D_RNN), F32),
        pltpu.VMEM((SEGMENTS, D_RNN), F32),
        pltpu.VMEM((1, D_RNN), F32),
    ]
    return pl.pallas_call(
        functools.partial(_proj_rnn_kernel, tm=tm, n_tiles=n_tiles),
        grid=(B, n_tiles, N_COL_BLOCKS),
        in_specs=in_specs,
        out_specs=out_specs,
        out_shape=out_shape,
        scratch_shapes=scratch,
        compiler_params=pltpu.CompilerParams(
            dimension_semantics=("arbitrary", "arbitrary", "arbitrary"), vmem_limit_bytes=VMEM_LIMIT),
        name="proj_rnn_prompt",
    )(x, w_in, gpre, cw, cb, wa, wx, ba, bx, lam, wro, qkv_s, *cache_views)


def _proj_rnn_sample_kernel(x_ref, w_ref, gpre_ref, cw_ref, cb_ref, wa_ref, wx_ref, ba_ref, bx_ref, lam_ref, wro_ref,
                            sc_ref, h0_ref,
                            att_ref, mixr_ref, sga_ref, conv_ref, h_ref,
                            xn_s, h_s, yr_s):
    j = pl.program_id(0)

    def project():
        return jnp.dot(xn_s[...], w_ref[...], preferred_element_type=F32)

    @pl.when(j == 0)
    def _u_block():
        xv = x_ref[...]
        ms = jnp.mean(xv * xv, axis=-1, keepdims=True)
        xn_s[...] = ((xv * lax.rsqrt(ms + EPS)) * gpre_ref[...]).astype(BF16)
        u = project()
        taps = [sc_ref[:, k * D_RNN:(k + 1) * D_RNN] for k in range(CONV_WIDTH - 1)] + [u]
        for c in range(N_GATE_BLOCKS):
            c0 = c * GATE_BLOCK
            xc = cb_ref[:, c0:c0 + GATE_BLOCK]
            for tap in range(CONV_WIDTH):
                xc = xc + taps[tap][:, c0:c0 + GATE_BLOCK] * cw_ref[tap:tap + 1, c0:c0 + GATE_BLOCK]
            a, b = _lru_gates(xc, c, wa_ref, wx_ref, ba_ref, bx_ref, lam_ref)
            h = a * h0_ref[:, c0:c0 + GATE_BLOCK] + b
            h_s[:, c0:c0 + GATE_BLOCK] = h
            h_ref[:, c0:c0 + GATE_BLOCK] = h
        for k in range(1, CONV_WIDTH):
            conv_ref[:, (k - 1) * D_RNN:k * D_RNN] = taps[k]

    @pl.when(j == 1)
    def _z_rnn_block():
        z = project()
        hz = (h_s[...] * (z * _sigmoid(z))).astype(BF16)
        yr_s[...] = jnp.dot(hz, wro_ref[...], preferred_element_type=F32)

    @pl.when(j == 2)
    def _g_rnn_block():
        mixr_ref[...] = _sigmoid(project()) * yr_s[...]

    @pl.when(j == 3)
    def _g_attn_block():
        sga_ref[...] = _sigmoid(project())

    @pl.when(j >= 4)
    def _attn_blocks():
        att_ref[...] = project()


def _proj_rnn_sample(x, w_in, gpre, cw, cb, wa, wx, ba, bx, lam, wro, sconv, h0):
    N, D = x.shape
    const2 = lambda j: (0, 0)
    const3 = lambda j: (0, 0, 0)
    n_att = N_COL_BLOCKS - 4
    out_shape = (
        jax.ShapeDtypeStruct((N, n_att * TN), F32),
        jax.ShapeDtypeStruct((N, D), F32),
        jax.ShapeDtypeStruct((N, D), F32),
        jax.ShapeDtypeStruct((N, (CONV_WIDTH - 1) * D_RNN), F32),
        jax.ShapeDtypeStruct((N, D_RNN), F32),
    )
    out_specs = (
        pl.BlockSpec((N, TN), lambda j: (0, jnp.maximum(j - 4, 0))),
        pl.BlockSpec((N, D), const2),
        pl.BlockSpec((N, D), const2),
        pl.BlockSpec((N, (CONV_WIDTH - 1) * D_RNN), const2),
        pl.BlockSpec((N, D_RNN), const2),
    )
    in_specs = [
        pl.BlockSpec((N, D), const2),
        pl.BlockSpec((D, TN), lambda j: (0, j)),
        pl.BlockSpec((1, D), const2),
        pl.BlockSpec((CONV_WIDTH, D_RNN), const2),
        pl.BlockSpec((1, D_RNN), const2),
        pl.BlockSpec((N_GATE_BLOCKS, GATE_BLOCK, GATE_BLOCK), const3),
        pl.BlockSpec((N_GATE_BLOCKS, GATE_BLOCK, GATE_BLOCK), const3),
        pl.BlockSpec((1, D_RNN), const2),
        pl.BlockSpec((1, D_RNN), const2),
        pl.BlockSpec((1, D_RNN), const2),
        pl.BlockSpec((D_RNN, D), const2),
        pl.BlockSpec((N, (CONV_WIDTH - 1) * D_RNN), const2),
        pl.BlockSpec((N, D_RNN), const2),
    ]
    scratch = [pltpu.VMEM((N, D), BF16), pltpu.VMEM((N, D_RNN), F32), pltpu.VMEM((N, D), F32)]
    return pl.pallas_call(
        _proj_rnn_sample_kernel,
        grid=(N_COL_BLOCKS,),
        in_specs=in_specs,
        out_specs=out_specs,
        out_shape=out_shape,
        scratch_shapes=scratch,
        compiler_params=pltpu.CompilerParams(dimension_semantics=("arbitrary",), vmem_limit_bytes=VMEM_LIMIT),
        name="proj_rnn_sample",
    )(x, w_in, gpre, cw, cb, wa, wx, ba, bx, lam, wro, sconv, h0)


def _attn_kernel(q_ref, kc_ref, vc_ref, kp_ref, vp_ref, o_ref, st_ref, bias_s, *, group, chunk):
    dilation = ATTN_GROUPS[group][1]
    c = pl.program_id(2)
    nt = (((1,), (1,)), ((), ()))

    @pl.when((pl.program_id(0) == 0) & (pl.program_id(1) == 0) & (c == 0))
    def _init_bias():
        qi = lax.broadcasted_iota(jnp.int32, (Q_BLOCK, Q_BLOCK), 0)
        kj = lax.broadcasted_iota(jnp.int32, (Q_BLOCK, Q_BLOCK), 1)
        steps_prev = Q_BLOCK + qi - kj
        steps_cur = qi - kj
        dist_prev = (steps_prev * dilation).astype(F32)
        dist_cur = (steps_cur * dilation).astype(F32)
        for h in range(H_G):
            slope = float(_SLOPES[group, h])
            bias_s[h] = jnp.where(steps_cur >= 0, -slope * dist_cur, NEG_INF)
            bias_s[H_G + h] = jnp.where(steps_prev <= Q_BLOCK, -slope * dist_prev, NEG_INF)
            bias_s[2 * H_G + h] = jnp.full((Q_BLOCK, Q_BLOCK), NEG_INF, F32)

    lane = lax.broadcasted_iota(jnp.int32, (Q_BLOCK, LANES), 1)
    low = lane < HEAD_DIM
    lane2 = lax.broadcasted_iota(jnp.int32, (2 * Q_BLOCK, LANES), 1)
    low2 = lane2 < HEAD_DIM
    ones_lo = jnp.where(low2, 1.0, 0.0).astype(BF16)
    ones_hi = jnp.where(low2, 0.0, 1.0).astype(BF16)
    stat_lane = lax.broadcasted_iota(jnp.int32, (Q_BLOCK, STAT_LANES), 1)

    def keys_of(block, cur_ref, prev_ref, sl):
        if block == 0:
            return jnp.concatenate([prev_ref[:, sl], cur_ref[0:Q_BLOCK, sl]], axis=0)
        return cur_ref[(block - 1) * Q_BLOCK:(block + 1) * Q_BLOCK, sl]

    for i in range(chunk // Q_BLOCK):
        r0 = i * Q_BLOCK
        first = jnp.where(c == 0, H_G, 0) if i == 0 else 0
        stats = jnp.zeros((Q_BLOCK, STAT_LANES), F32)
        for p in range(H_G // 2):
            sl = slice(LANES * p, LANES * (p + 1))
            qp = q_ref[r0:r0 + Q_BLOCK, sl] * (HEAD_DIM ** -0.5)
            kp = keys_of(i, kc_ref, kp_ref, sl)
            vp = keys_of(i, vc_ref, vp_ref, sl)
            es, ms = [], []
            for hh in range(2):
                h = 2 * p + hh
                msk = low if hh == 0 else jnp.logical_not(low)
                qm = jnp.where(msk, qp, jnp.zeros_like(qp))
                s = lax.dot_general(qm, kp, nt, preferred_element_type=F32)
                s_p = s[:, 0:Q_BLOCK] + bias_s[H_G + h + first]
                s_c = s[:, Q_BLOCK:2 * Q_BLOCK] + bias_s[h]
                mx = jnp.maximum(jnp.max(s_p, axis=-1, keepdims=True), jnp.max(s_c, axis=-1, keepdims=True))
                es.append(jnp.exp(s_p - mx).astype(BF16))
                es.append(jnp.exp(s_c - mx).astype(BF16))
                ms.append(mx)
            vm0 = jnp.where(low2, vp, jnp.zeros_like(vp))
            vm1 = jnp.where(low2, jnp.zeros_like(vp), vp)
            w = jnp.concatenate([jnp.concatenate([vm0, ones_lo], axis=1),
                                 jnp.concatenate([vm1, ones_hi], axis=1)], axis=0)
            acc = jnp.dot(jnp.concatenate(es, axis=1), w, preferred_element_type=F32)
            l_pair = acc[:, LANES:2 * LANES]
            o_ref[r0:r0 + Q_BLOCK, sl] = (acc[:, 0:LANES] / l_pair).astype(o_ref.dtype)
            lse_pair = jnp.where(low, ms[0], ms[1]) + jnp.log(l_pair)
            keep = ((stat_lane % (STAT_LANES // 2)) // STAT_LANES_PER_HEAD) == p
            stats = jnp.where(keep, lse_pair, stats)
        st_ref[r0:r0 + Q_BLOCK, :] = stats


def _attn_group(q, k, v, group):
    B, dil, L, _ = q.shape
    chunk = min(ATTN_CHUNK, L)
    assert L % chunk == 0 and chunk % Q_BLOCK == 0
    cur = lambda b, r, c: (b, r, c, 0)
    prev = lambda b, r, c: (b, r, jnp.maximum(c * (chunk // Q_BLOCK) - 1, 0), 0)
    blk = (None, None, chunk, D_ATTN)
    pblk = (None, None, Q_BLOCK, D_ATTN)
    return pl.pallas_call(
        functools.partial(_attn_kernel, group=group, chunk=chunk),
        grid=(B, dil, L // chunk),
        in_specs=[pl.BlockSpec(blk, cur), pl.BlockSpec(blk, cur), pl.BlockSpec(blk, cur),
                  pl.BlockSpec(pblk, prev), pl.BlockSpec(pblk, prev)],
        out_specs=[pl.BlockSpec(blk, cur), pl.BlockSpec((None, None, chunk, STAT_LANES), cur)],
        out_shape=[jax.ShapeDtypeStruct((B, dil, L, D_ATTN), BF16),
                   jax.ShapeDtypeStruct((B, dil, L, STAT_LANES), F32)],
        scratch_shapes=[pltpu.VMEM((3 * H_G, Q_BLOCK, Q_BLOCK), F32)],
        compiler_params=pltpu.CompilerParams(
            dimension_semantics=("arbitrary", "arbitrary", "arbitrary"), vmem_limit_bytes=VMEM_LIMIT),
        name=f"attn_group{group}",
    )(q, k, v, k, v)


def _finish(att, z_ref, sga_ref, mixr_ref, x_ref, wao_ref, wo_ref, gpost_ref, y_ref):
    z = z_ref[...].astype(F32)
    ya_in = (att * (z * _sigmoid(z))).astype(BF16)
    ya = jnp.dot(ya_in, wao_ref[...], preferred_element_type=F32)
    mixed = mixr_ref[...].astype(F32) + sga_ref[...].astype(F32) * ya
    out = jnp.dot(mixed.astype(BF16), wo_ref[...], preferred_element_type=F32)
    ms = jnp.mean(out * out, axis=-1, keepdims=True)
    y_ref[...] = x_ref[...] + (out * lax.rsqrt(ms + EPS)) * gpost_ref[...]


def _out_kernel(att_ref, z_ref, sga_ref, mixr_ref, x_ref, wao_ref, wo_ref, gpost_ref, y_ref):
    _finish(att_ref[...].astype(F32), z_ref, sga_ref, mixr_ref, x_ref, wao_ref, wo_ref, gpost_ref, y_ref)


def _merge_out_kernel(o0_ref, o1_ref, o2_ref, s0_ref, s1_ref, s2_ref, z_ref, sga_ref, mixr_ref, x_ref,
                      wao_ref, wo_ref, gpost_ref, y_ref, nat1_s, nat2_s, st1_s, st2_s, *, tm):
    for o_ref, s_ref, nat_s, stn_s, dil in ((o1_ref, s1_ref, nat1_s, st1_s, ATTN_GROUPS[1][1]),
                                            (o2_ref, s2_ref, nat2_s, st2_s, ATTN_GROUPS[2][1])):
        p = tm // dil
        for r in range(dil):
            stn_s[pl.ds(r, p, stride=dil), :] = s_ref[r]
            for c in range(D_ATTN // LANES):
                nat_s[c, pl.ds(r, p, stride=dil), :] = o_ref[r, :, c * LANES:(c + 1) * LANES].astype(F32)

    lse = [s0_ref[...], st1_s[...], st2_s[...]]
    mm = jnp.maximum(jnp.maximum(lse[0], lse[1]), lse[2])
    ws = [jnp.exp(x - mm) for x in lse]
    den = ws[0] + ws[1] + ws[2]
    src = lax.broadcasted_iota(jnp.int32, (STAT_LANES, D_ATTN), 0)
    dst_head = lax.broadcasted_iota(jnp.int32, (STAT_LANES, D_ATTN), 1) // HEAD_DIM
    expand = jnp.where(src == (dst_head % 2) * (STAT_LANES // 2) + (dst_head // 2) * STAT_LANES_PER_HEAD,
                       1.0, 0.0).astype(BF16)
    outs = [o0_ref[...].astype(F32),
            jnp.concatenate([nat1_s[c] for c in range(D_ATTN // LANES)], axis=1),
            jnp.concatenate([nat2_s[c] for c in range(D_ATTN // LANES)], axis=1)]
    att = jnp.zeros((tm, D_ATTN), F32)
    for g in range(N_GROUPS):
        wexp = jnp.dot((ws[g] / den).astype(BF16), expand, preferred_element_type=F32)
        att = att + wexp * outs[g]
    _finish(att, z_ref, sga_ref, mixr_ref, x_ref, wao_ref, wo_ref, gpost_ref, y_ref)


def _merge_out_prompt(os, sts, zg, sga, mixr, x, wao, wo, gpost):
    B, S, D = x.shape
    tm = TM_OUT
    tile3 = lambda b, m: (b, m, 0)
    tile4 = lambda b, m: (b, 0, m, 0)
    const = lambda b, m: (0, 0)
    in_specs = []
    for (_, dil), width in [(g, D_ATTN) for g in ATTN_GROUPS] + [(g, STAT_LANES) for g in ATTN_GROUPS]:
        if dil == 1:
            in_specs.append(pl.BlockSpec((None, None, tm, width), tile4))
        else:
            in_specs.append(pl.BlockSpec((None, dil, tm // dil, width), tile4))
    in_specs += [
        pl.BlockSpec((None, tm, D_ATTN), tile3),
        pl.BlockSpec((None, tm, D), tile3),
        pl.BlockSpec((None, tm, D), tile3),
        pl.BlockSpec((None, tm, D), tile3),
        pl.BlockSpec((D_ATTN, D), const),
        pl.BlockSpec((D, D), const),
        pl.BlockSpec((1, D), const),
    ]
    scratch = [
        pltpu.VMEM((D_ATTN // LANES, tm, LANES), F32),
        pltpu.VMEM((D_ATTN // LANES, tm, LANES), F32),
        pltpu.VMEM((tm, STAT_LANES), F32),
        pltpu.VMEM((tm, STAT_LANES), F32),
    ]
    return pl.pallas_call(
        functools.partial(_merge_out_kernel, tm=tm),
        grid=(B, S // tm),
        in_specs=in_specs,
        out_specs=pl.BlockSpec((None, tm, D), tile3),
        out_shape=jax.ShapeDtypeStruct((B, S, D), F32),
        scratch_shapes=scratch,
        compiler_params=pltpu.CompilerParams(
            dimension_semantics=("arbitrary", "arbitrary"), vmem_limit_bytes=VMEM_LIMIT),
        name="merge_out_prompt",
    )(*os, *sts, zg, sga, mixr, x, wao, wo, gpost)


def _out_sample(att, z, sga, mixr, x, wao, wo, gpost):
    N, D = x.shape
    full = lambda shape: pl.BlockSpec(shape, lambda i: (0, 0))
    return pl.pallas_call(
        _out_kernel,
        grid=(1,),
        in_specs=[full((N, D_ATTN)), full((N, D_ATTN)), full((N, D)), full((N, D)), full((N, D)),
                  full((D_ATTN, D)), full((D, D)), full((1, D))],
        out_specs=full((N, D)),
        out_shape=jax.ShapeDtypeStruct((N, D), F32),
        compiler_params=pltpu.CompilerParams(dimension_semantics=("arbitrary",), vmem_limit_bytes=VMEM_LIMIT),
        name="out_sample",
    )(att, z, sga, mixr, x, wao, wo, gpost)


def _block_diag_chunks(w):
    per = GATE_BLOCK // RNN_BLOCK
    w = w.reshape(N_GATE_BLOCKS, per, RNN_BLOCK, RNN_BLOCK)
    eye = jnp.eye(per, dtype=w.dtype)
    dense = w[:, :, :, None, :] * eye[None, :, None, :, None]
    return dense.reshape(N_GATE_BLOCKS, GATE_BLOCK, GATE_BLOCK)


def _reorder_w_in(w_in):
    sizes = (D_RNN, D_RNN, N_GROUPS * D_ATTN, N_GROUPS * D_ATTN, N_GROUPS * D_ATTN, D_ATTN, D_MODEL, D_MODEL)
    starts = np.concatenate([[0], np.cumsum(sizes)[:-1]])
    u0, z0, q0, k0, v0, za0, gr0, ga0 = (int(s) for s in starts)
    pieces = [(u0, D_RNN), (z0, D_RNN), (gr0, D_MODEL), (ga0, D_MODEL), (za0, D_ATTN)]
    for g in range(N_GROUPS):
        pieces += [(q0 + g * D_ATTN, D_ATTN), (k0 + g * D_ATTN, D_ATTN), (v0 + g * D_ATTN, D_ATTN)]
    return jnp.concatenate([w_in[:, s:s + n] for s, n in pieces], axis=1).astype(BF16)


def _layer(layer, yp, ys, sconv, h0, caches, norm_pre, norm_post, w_in, conv_w, conv_b, lru_w_a, lru_b_a, lru_w_x,
           lru_b_x, lru_lambda, w_rnn_out, w_attn_out, w_out):
    B, S, D = yp.shape
    N = ys.shape[0]
    row = lambda v: v.reshape(1, -1)
    w_in_b = _reorder_w_in(w_in)
    wa = _block_diag_chunks(lru_w_a).astype(BF16)
    wx = _block_diag_chunks(lru_w_x).astype(BF16)
    wro = w_rnn_out.astype(BF16)
    wao = w_attn_out.astype(BF16)
    wo = w_out.astype(BF16)
    shared = (w_in_b, row(norm_pre), conv_w, row(conv_b), wa, wx, row(lru_b_a), row(lru_b_x), row(lru_lambda), wro)

    xs = ys.reshape(N, D)
    att_in, mixr_s, sga_s, conv_s, h_s = _proj_rnn_sample(
        xs, *shared, sconv.reshape(N, (CONV_WIDTH - 1) * D_RNN), h0)
    z_s = att_in[:, 0:D_ATTN]
    qkv_s = att_in[:, D_ATTN:].reshape(N, 3 * N_GROUPS, H_G, HEAD_DIM)

    outs = _proj_rnn_prompt(yp, *shared, qkv_s, caches, layer)
    qkv, (zg, mixr, sga, kv0, kv1, kv2, conv_p, h_p, att_s) = outs[:9], outs[9:]
    os, sts = [], []
    for g in range(N_GROUPS):
        o, st = _attn_group(qkv[3 * g], qkv[3 * g + 1], qkv[3 * g + 2], g)
        os.append(o)
        sts.append(st)
    y_p = _merge_out_prompt(os, sts, zg, sga, mixr, yp, wao, wo, row(norm_post))
    kv_p = [kv.reshape(B, kv.shape[1], 2, H_G, HEAD_DIM) for kv in (kv0, kv1, kv2)]

    y_s = _out_sample(att_s.reshape(N, D_ATTN), z_s, sga_s, mixr_s, xs, wao, wo, row(norm_post))
    kv_s = [qkv_s[:, 3 * g + 1:3 * g + 3].reshape(N, 1, 2, H_G, HEAD_DIM) for g in range(N_GROUPS)]

    return (y_p, y_s.reshape(N, 1, D), conv_p, conv_s.reshape(N, CONV_WIDTH - 1, D_RNN),
            h_p.reshape(B, D_RNN), h_s, kv_p, kv_s)


def kernel(x_prompt, x_sample, state_conv, state_h, cache_kv_w128, cache_kv_w512, cache_kv_w2048, norm_pre, norm_post, w_in, conv_w, conv_b, lru_w_a, lru_b_a, lru_w_x, lru_b_x, lru_lambda, w_rnn_out, w_attn_out, w_out):
    depth = norm_pre.shape[0]
    caches = (cache_kv_w128, cache_kv_w512, cache_kv_w2048)
    yp, ys = x_prompt, x_sample
    conv_p, conv_s, h_p, h_s = [], [], [], []
    kvp = ([], [], [])
    kvs = ([], [], [])
    for l in range(depth):
        yp, ys, cp, cs, hp, hs, kv_p, kv_s = _layer(
            l, yp, ys, state_conv[l], state_h[l], caches,
            norm_pre[l], norm_post[l], w_in[l], conv_w[l], conv_b[l], lru_w_a[l], lru_b_a[l], lru_w_x[l],
            lru_b_x[l], lru_lambda[l], w_rnn_out[l], w_attn_out[l], w_out[l])
        conv_p.append(cp)
        conv_s.append(cs)
        h_p.append(hp)
        h_s.append(hs)
        for g in range(N_GROUPS):
            kvp[g].append(kv_p[g])
            kvs[g].append(kv_s[g])
    return (yp, ys, jnp.stack(conv_p), jnp.stack(conv_s), jnp.stack(h_p), jnp.stack(h_s),
            jnp.stack(kvp[0]), jnp.stack(kvs[0]), jnp.stack(kvp[1]), jnp.stack(kvs[1]),
            jnp.stack(kvp[2]), jnp.stack(kvs[2]))
```

```python
import functools

import numpy as np
import jax
import jax.numpy as jnp
from jax import lax
from jax.experimental import pallas as pl
from jax.experimental.pallas import tpu as pltpu

F32 = jnp.float32
BF16 = jnp.bfloat16

D_MODEL = 1024
D_RNN = 1024
N_RNN_BLOCKS = 16
RNN_BLOCK = D_RNN // N_RNN_BLOCKS
CONV_WIDTH = 4
LRU_C = 8.0
HEAD_DIM = 64
H_G = 8
ATTN_GROUPS = ((128, 1), (512, 4), (2048, 16))
N_GROUPS = 3
D_ATTN = H_G * HEAD_DIM
Q_BLOCK = 128
ALIBI_MAX = 8.0
EPS = 1e-6
NEG_INF = -1e30
D_IN = 2 * D_RNN + 3 * N_GROUPS * D_ATTN + D_ATTN + 2 * D_MODEL

LANES = 128
TN = 1024
N_COL_BLOCKS = D_IN // TN
GATE_BLOCK = 256
N_GATE_BLOCKS = D_RNN // GATE_BLOCK
STAT_LANES = LANES
STAT_LANES_PER_HEAD = STAT_LANES // H_G

TM_PROMPT = 512
TM_PROMPT2 = 256
TILES_PER_SAMPLE = 2
SEGMENTS = 8
TM_OUT = 512
ATTN_CHUNK = 512
ROW_CHUNK = 64
GATE_ROWS = 128
VMEM_LIMIT = 56 * 1024 * 1024


def _alibi_slopes():
    n = N_GROUPS * H_G
    s = np.float32(2.0) ** (np.float32(-ALIBI_MAX) * np.arange(1, n + 1, dtype=np.float32) / np.float32(n))
    return s.reshape(N_GROUPS, H_G)


_SLOPES = _alibi_slopes()


def _stat_lane(h):
    return (h % 2) * (STAT_LANES // 2) + (h // 2) * STAT_LANES_PER_HEAD


def _softplus(y):
    return jnp.maximum(y, 0.0) + jnp.log1p(jnp.exp(-jnp.abs(y)))


def _sigmoid(x):
    return 0.5 * jnp.tanh(0.5 * x) + 0.5


def _for_rows(n_rows, chunk, fn):
    if n_rows <= chunk:
        fn(0)
        return

    def body(c, carry):
        fn(pl.multiple_of(c * chunk, chunk))
        return carry

    lax.fori_loop(0, n_rows // chunk, body, 0)


def _lru_gates(xc, c, wa_ref, wx_ref, ba_ref, bx_ref, lam_ref):
    c0 = c * GATE_BLOCK
    xcb = xc.astype(BF16)
    r = _sigmoid(jnp.dot(xcb, wa_ref[c], preferred_element_type=F32) + ba_ref[:, c0:c0 + GATE_BLOCK])
    i = _sigmoid(jnp.dot(xcb, wx_ref[c], preferred_element_type=F32) + bx_ref[:, c0:c0 + GATE_BLOCK])
    log_a = (-LRU_C * r) * _softplus(-lam_ref[:, c0:c0 + GATE_BLOCK])
    a = jnp.exp(log_a)
    v = 1.0 - a * a
    b = jnp.where(v > 0.0, v * lax.rsqrt(v), 0.0) * i * xc
    return a, b


def _sample_head_attention(h, qs_ref, cache_refs, att_ref):
    scale = HEAD_DIM ** -0.5
    ri = lax.broadcasted_iota(jnp.int32, (HEAD_DIM, HEAD_DIM), 0)
    ci = lax.broadcasted_iota(jnp.int32, (HEAD_DIM, HEAD_DIM), 1)
    eye = ri == ci

    def to_col(row):
        return jnp.sum(jnp.where(eye, jnp.broadcast_to(row, (HEAD_DIM, HEAD_DIM)), 0.0), axis=-1, keepdims=True)

    def to_row(col):
        return jnp.sum(jnp.where(eye, jnp.broadcast_to(col, (HEAD_DIM, HEAD_DIM)), 0.0), axis=0, keepdims=True)

    outs, lses = [], []
    for g, (window, dilation) in enumerate(ATTN_GROUPS):
        c_ref = cache_refs[g]
        pos = lax.broadcasted_iota(jnp.int32, (1, window), 1)
        dist = (window - pos).astype(F32)
        bias = jnp.where((pos % dilation) == 0, -float(_SLOPES[g, h]) * dist, NEG_INF)
        q = qs_ref[3 * g, h:h + 1, :] * scale
        s_new = jnp.sum(qs_ref[3 * g + 1, h:h + 1, :] * q, axis=-1, keepdims=True)
        v_col = to_col(qs_ref[3 * g + 2, h:h + 1, :])
        s = jnp.sum(c_ref[0] * to_col(q), axis=0, keepdims=True) + bias
        mx = jnp.maximum(jnp.max(s, axis=-1, keepdims=True), s_new)
        e = jnp.exp(s - mx)
        e_new = jnp.exp(s_new - mx)
        l = jnp.sum(e, axis=-1, keepdims=True) + e_new
        o = jnp.sum(c_ref[1] * e, axis=-1, keepdims=True) + e_new * v_col
        outs.append(o / l)
        lses.append(mx + jnp.log(l))
    mm = jnp.maximum(jnp.maximum(lses[0], lses[1]), lses[2])
    ws = [jnp.exp(x - mm) for x in lses]
    col = (outs[0] * ws[0] + outs[1] * ws[1] + outs[2] * ws[2]) / (ws[0] + ws[1] + ws[2])
    att_ref[h:h + 1, :] = to_row(col)


def _proj_rnn_kernel(x_ref, w_ref, gpre_ref, cw_ref, cb_ref, wa_ref, wx_ref, ba_ref, bx_ref, lam_ref, wro_ref,
                     qs_ref, c0_ref, c1_ref, c2_ref,
                     q0_ref, k0_ref, v0_ref, q1_ref, k1_ref, v1_ref, q2_ref, k2_ref, v2_ref,
                     zg_ref, mixr_ref, sga_ref, kv0_ref, kv1_ref, kv2_ref, conv_ref, h_ref, att_ref,
                     xn_s, xnf_s, xseg_s, xnseg_s, xn4_s, xn16_s, u_s, a_s, b_s, yr_s,
                     cprev_s, cin_s, carry_s,
                     *, tm, n_tiles):
    m = pl.program_id(1)
    j = pl.program_id(2)
    acc_s = u_s

    def sample_head(h):
        _sample_head_attention(h, qs_ref, (c0_ref, c1_ref, c2_ref), att_ref)
    last_tile = m == n_tiles - 1
    in_kv2 = m >= n_tiles - ATTN_GROUPS[2][0] // tm
    half = TN // 2
    d1, d2 = ATTN_GROUPS[1][1], ATTN_GROUPS[2][1]

    def project(lhs_s):
        return jnp.dot(lhs_s[...], w_ref[...], preferred_element_type=F32)

    def emit(out_ref, c0, dilation):
        p = tm // dilation
        for r in range(dilation):
            out_ref[r] = acc_s[r * p:(r + 1) * p, c0:c0 + half].astype(BF16)

    def natural_tail(out_ref, out_c0, c0, dilation):
        p = tm // dilation
        for c in range(half // LANES):
            for r in range(dilation):
                xnf_s[c, pl.ds(r, p, stride=dilation), :] = \
                    acc_s[r * p:(r + 1) * p, c0 + c * LANES:c0 + (c + 1) * LANES]
            out_ref[:, out_c0 + c * LANES:out_c0 + (c + 1) * LANES] = xnf_s[c]

    seg = tm // SEGMENTS
    n_slab = D_MODEL // LANES

    @pl.when(j == 0)
    def _u_block():
        def norm(s, carry):
            r0 = pl.multiple_of(s * seg, seg)
            xv = x_ref[pl.ds(r0, seg), :]
            ms = jnp.mean(xv * xv, axis=-1, keepdims=True)
            xn = (xv * lax.rsqrt(ms + EPS)) * gpre_ref[...]
            xn_s[pl.ds(r0, seg), :] = xn.astype(BF16)
            for c in range(n_slab):
                piece = xn[:, c * LANES:(c + 1) * LANES]
                xnf_s[c, pl.ds(r0, seg), :] = piece
                xseg_s[c, pl.ds(s, seg, stride=SEGMENTS), :] = piece
            return carry

        lax.fori_loop(0, SEGMENTS, norm, 0)

        for c in range(n_slab):
            xnseg_s[:, c * LANES:(c + 1) * LANES] = xseg_s[c].astype(BF16)

        for dil, dst in ((d1, xn4_s), (d2, xn16_s)):
            p = tm // dil
            for r in range(dil):
                for c in range(n_slab):
                    dst[r * p:(r + 1) * p, c * LANES:(c + 1) * LANES] = \
                        xnf_s[c, pl.ds(r, p, stride=dil), :].astype(BF16)

        @pl.when(m == 0)
        def _reset():
            cprev_s[...] = jnp.zeros((8, D_RNN), F32)
            carry_s[...] = jnp.zeros((1, D_RNN), F32)

        u_s[...] = project(xnseg_s)

        sub = lax.broadcasted_iota(jnp.int32, (SEGMENTS, GATE_BLOCK), 0)
        for c in range(N_GATE_BLOCKS):
            cols = slice(c * GATE_BLOCK, (c + 1) * GATE_BLOCK)

            def wrapped(t):
                v = pltpu.roll(u_s[SEGMENTS * (seg + t):SEGMENTS * (seg + t + 1), cols], 1, axis=0)
                return jnp.where(sub == 0, cprev_s[8 + t:9 + t, cols], v)

            wrap = {t: wrapped(t) for t in range(1 - CONV_WIDTH, 0)}
            for rc in range(tm // GATE_ROWS):
                r0 = rc * GATE_ROWS
                xc = cb_ref[:, cols] + u_s[r0:r0 + GATE_ROWS, cols] * cw_ref[CONV_WIDTH - 1:CONV_WIDTH, cols]
                for sh in range(1, CONV_WIDTH):
                    if r0 == 0:
                        head = [wrap[t - sh] for t in range(sh)]
                        ush = jnp.concatenate(head + [u_s[0:GATE_ROWS - SEGMENTS * sh, cols]], axis=0)
                    else:
                        ush = u_s[r0 - SEGMENTS * sh:r0 - SEGMENTS * sh + GATE_ROWS, cols]
                    xc = xc + ush * cw_ref[CONV_WIDTH - 1 - sh:CONV_WIDTH - sh, cols]
                a, b = _lru_gates(xc, c, wa_ref, wx_ref, ba_ref, bx_ref, lam_ref)
                a_s[r0:r0 + GATE_ROWS, cols] = a
                b_s[r0:r0 + GATE_ROWS, cols] = b

        def scan(t, carry):
            hloc, cum = carry
            r0 = pl.multiple_of(t * SEGMENTS, SEGMENTS)
            av = a_s[pl.ds(r0, SEGMENTS), :]
            hloc = av * hloc + b_s[pl.ds(r0, SEGMENTS), :]
            cum = av * cum
            b_s[pl.ds(r0, SEGMENTS), :] = hloc
            a_s[pl.ds(r0, SEGMENTS), :] = cum
            return hloc, cum

        h_end, a_end = lax.fori_loop(
            0, seg, scan, (jnp.zeros((SEGMENTS, D_RNN), F32), jnp.ones((SEGMENTS, D_RNN), F32)), unroll=4)
        state = carry_s[...]
        for s in range(SEGMENTS):
            cin_s[s:s + 1, :] = state
            state = h_end[s:s + 1, :] + a_end[s:s + 1, :] * state
        carry_s[...] = state
        h_ref[...] = state
        for k in range(1, CONV_WIDTH):
            last = u_s[tm - SEGMENTS * (k - 1) - 1:tm - SEGMENTS * (k - 1), :]
            conv_ref[CONV_WIDTH - 1 - k:CONV_WIDTH - k, :] = last
            cprev_s[8 - k:9 - k, :] = last

    @pl.when(j == 1)
    def _z_rnn_block():
        sample_head(0)
        z = project(xnseg_s)
        split = (seg, SEGMENTS, D_RNN)
        h = b_s[...].reshape(split) + a_s[...].reshape(split) * cin_s[...][None]
        hz = (h.reshape(tm, D_RNN) * (z * _sigmoid(z))).astype(BF16)
        yr_s[...] = jnp.dot(hz, wro_ref[...], preferred_element_type=F32)

    @pl.when(j == 2)
    def _g_rnn_block():
        sample_head(1)
        mix = _sigmoid(project(xnseg_s)) * yr_s[...]
        for c in range(n_slab):
            xseg_s[c] = mix[:, c * LANES:(c + 1) * LANES]
        for s in range(SEGMENTS):
            for c in range(n_slab):
                mixr_ref[s * seg:(s + 1) * seg, c * LANES:(c + 1) * LANES] = \
                    xseg_s[c, pl.ds(s, seg, stride=SEGMENTS), :].astype(BF16)

    @pl.when(j == 3)
    def _g_attn_block():
        sample_head(2)
        sga_ref[...] = _sigmoid(project(xn_s)).astype(BF16)

    @pl.when(j == 4)
    def _zattn_q0_block():
        sample_head(3)
        acc_s[...] = project(xn_s)
        zg_ref[...] = acc_s[:, 0:half].astype(BF16)
        emit(q0_ref, half, 1)

    @pl.when(j == 5)
    def _k0_v0_block():
        sample_head(4)
        acc_s[...] = project(xn_s)
        emit(k0_ref, 0, 1)
        emit(v0_ref, half, 1)

        @pl.when(last_tile)
        def _():
            kv0_ref[...] = acc_s[tm - ATTN_GROUPS[0][0]:tm, :]

    @pl.when(j == 6)
    def _q1_k1_block():
        sample_head(5)
        acc_s[...] = project(xn4_s)
        emit(q1_ref, 0, d1)
        emit(k1_ref, half, d1)

        @pl.when(last_tile)
        def _():
            natural_tail(kv1_ref, 0, half, d1)

    @pl.when(j == 7)
    def _v1_q2_block():
        sample_head(6)
        acc_s[:, 0:half] =jnp.dot(xn4_s[...], w_ref[:, 0:half], preferred_element_type=F32)
        acc_s[:, half:TN] = jnp.dot(xn16_s[...], w_ref[:, half:TN], preferred_element_type=F32)
        emit(v1_ref, 0, d1)
        emit(q2_ref, half, d2)

        @pl.when(last_tile)
        def _():
            natural_tail(kv1_ref, half, 0, d1)

    @pl.when(j == 8)
    def _k2_v2_block():
        sample_head(7)
        acc_s[...] = project(xn16_s)
        emit(k2_ref, 0, d2)
        emit(v2_ref, half, d2)

        @pl.when(in_kv2)
        def _():
            natural_tail(kv2_ref, 0, 0, d2)
            natural_tail(kv2_ref, half, half, d2)


def _proj_rnn_prompt(x, w_in, gpre, cw, cb, wa, wx, ba, bx, lam, wro, qkv_s, caches, layer):
    B, S, D = x.shape
    tm = TM_PROMPT
    n_tiles = S // tm
    m0 = n_tiles - ATTN_GROUPS[2][0] // tm
    assert S % tm == 0 and tm == ATTN_GROUPS[1][0] and ATTN_GROUPS[2][0] % tm == 0 and S >= ATTN_GROUPS[2][0]
    N = qkv_s.shape[0]
    assert N == B * n_tiles and N_COL_BLOCKS == H_G + 1
    cache_views = [jnp.transpose(c, (0, 1, 3, 4, 5, 2)) for c in caches]
    sample_seq = lambda b, m, j: (b * n_tiles + m, 0, 0, 0)
    cache_head = lambda b, m, j: (layer, b * n_tiles + m, 0, jnp.clip(j - 1, 0, H_G - 1), 0, 0)

    const2 = lambda b, m, j: (0, 0)
    const3 = lambda b, m, j: (0, 0, 0)
    tile3 = lambda b, m, j: (b, m, 0)
    tile4 = lambda b, m, j: (b, 0, m, 0)
    per_seq = lambda b, m, j: (b, 0, 0)

    qkv_shapes, qkv_specs = [], []
    for _, dil in ATTN_GROUPS:
        for _ in range(3):
            qkv_shapes.append(jax.ShapeDtypeStruct((B, dil, S // dil, D_ATTN), BF16))
            qkv_specs.append(pl.BlockSpec((None, dil, tm // dil, D_ATTN), tile4))
    out_shape = tuple(qkv_shapes) + (
        jax.ShapeDtypeStruct((B, S, D_ATTN), BF16),
        jax.ShapeDtypeStruct((B, S, D), BF16),
        jax.ShapeDtypeStruct((B, S, D), BF16),
        jax.ShapeDtypeStruct((B, ATTN_GROUPS[0][0], 2 * D_ATTN), F32),
        jax.ShapeDtypeStruct((B, ATTN_GROUPS[1][0], 2 * D_ATTN), F32),
        jax.ShapeDtypeStruct((B, ATTN_GROUPS[2][0], 2 * D_ATTN), F32),
        jax.ShapeDtypeStruct((B, CONV_WIDTH - 1, D_RNN), F32),
        jax.ShapeDtypeStruct((B, 1, D_RNN), F32),
        jax.ShapeDtypeStruct((N, H_G, HEAD_DIM), F32),
    )
    out_specs = tuple(qkv_specs) + (
        pl.BlockSpec((None, tm, D_ATTN), tile3),
        pl.BlockSpec((None, tm, D), tile3),
        pl.BlockSpec((None, tm, D), tile3),
        pl.BlockSpec((None, ATTN_GROUPS[0][0], 2 * D_ATTN), per_seq),
        pl.BlockSpec((None, ATTN_GROUPS[1][0], 2 * D_ATTN), per_seq),
        pl.BlockSpec((None, tm, 2 * D_ATTN), lambda b, m, j: (b, jnp.maximum(m - m0, 0), 0)),
        pl.BlockSpec((None, CONV_WIDTH - 1, D_RNN), per_seq),
        pl.BlockSpec((None, 1, D_RNN), per_seq),
        pl.BlockSpec((None, H_G, HEAD_DIM), lambda b, m, j: (b * n_tiles + m, 0, 0)),
    )
    in_specs = [
        pl.BlockSpec((None, tm, D), tile3),
        pl.BlockSpec((D, TN), lambda b, m, j: (0, j)),
        pl.BlockSpec((1, D), const2),
        pl.BlockSpec((CONV_WIDTH, D_RNN), const2),
        pl.BlockSpec((1, D_RNN), const2),
        pl.BlockSpec((N_GATE_BLOCKS, GATE_BLOCK, GATE_BLOCK), const3),
        pl.BlockSpec((N_GATE_BLOCKS, GATE_BLOCK, GATE_BLOCK), const3),
        pl.BlockSpec((1, D_RNN), const2),
        pl.BlockSpec((1, D_RNN), const2),
        pl.BlockSpec((1, D_RNN), const2),
        pl.BlockSpec((D_RNN, D), const2),
        pl.BlockSpec((None, 3 * N_GROUPS, H_G, HEAD_DIM), sample_seq),
    ]
    in_specs += [pl.BlockSpec((None, None, 2, None, HEAD_DIM, window), cache_head) for window, _ in ATTN_GROUPS]
    scratch = [
        pltpu.VMEM((tm, D), BF16),
        pltpu.VMEM((D // LANES, tm, LANES), F32),
        pltpu.VMEM((D // LANES, tm, LANES), F32),
        pltpu.VMEM((tm, D), BF16),
        pltpu.VMEM((tm, D), BF16),
        pltpu.VMEM((tm, D), BF16),
        pltpu.VMEM((tm, D_RNN), F32),
        pltpu.VMEM((tm, D_RNN), F32),
        pltpu.VMEM((tm, D_RNN), F32),
        pltpu.VMEM((tm, D), F32),
        pltpu.VMEM((8, D_RNN), F32),
        pltpu.VMEM((SEGMENTS, D_RNN), F32),
        pltpu.VMEM((1, D_RNN), F32),
    ]
    return pl.pallas_call(
        functools.partial(_proj_rnn_kernel, tm=tm, n_tiles=n_tiles),
        grid=(B, n_tiles, N_COL_BLOCKS),
        in_specs=in_specs,
        out_specs=out_specs,
        out_shape=out_shape,
        scratch_shapes=scratch,
        compiler_params=pltpu.CompilerParams(
            dimension_semantics=("arbitrary", "arbitrary", "arbitrary"), vmem_limit_bytes=VMEM_LIMIT),
        name="proj_rnn_prompt",
    )(x, w_in, gpre, cw, cb, wa, wx, ba, bx, lam, wro, qkv_s, *cache_views)


def _sample_heads(local_heads, first_head, slope_of, qs_ref, cache_refs, att_ref):
    scale = HEAD_DIM ** -0.5
    ri = lax.broadcasted_iota(jnp.int32, (HEAD_DIM, HEAD_DIM), 0)
    ci = lax.broadcasted_iota(jnp.int32, (HEAD_DIM, HEAD_DIM), 1)
    eye = ri == ci

    def to_col(row):
        return jnp.sum(jnp.where(eye, jnp.broadcast_to(row, (HEAD_DIM, HEAD_DIM)), 0.0), axis=-1, keepdims=True)

    def to_row(col):
        return jnp.sum(jnp.where(eye, jnp.broadcast_to(col, (HEAD_DIM, HEAD_DIM)), 0.0), axis=0, keepdims=True)

    for i in local_heads:
        h = first_head + i
        outs, lses = [], []
        for g, (window, dilation) in enumerate(ATTN_GROUPS):
            c_ref = cache_refs[g]
            pos = lax.broadcasted_iota(jnp.int32, (1, window), 1)
            dist = (window - pos).astype(F32)
            bias = jnp.where((pos % dilation) == 0, -slope_of(g, i) * dist, NEG_INF)
            q = qs_ref[3 * g, pl.ds(h, 1), :] * scale
            s_new = jnp.sum(qs_ref[3 * g + 1, pl.ds(h, 1), :] * q, axis=-1, keepdims=True)
            v_col = to_col(qs_ref[3 * g + 2, pl.ds(h, 1), :])
            s = jnp.sum(c_ref[0, i] * to_col(q), axis=0, keepdims=True) + bias
            mx = jnp.maximum(jnp.max(s, axis=-1, keepdims=True), s_new)
            e = jnp.exp(s - mx)
            e_new = jnp.exp(s_new - mx)
            l = jnp.sum(e, axis=-1, keepdims=True) + e_new
            o = jnp.sum(c_ref[1, i] * e, axis=-1, keepdims=True) + e_new * v_col
            outs.append(o / l)
            lses.append(mx + jnp.log(l))
        mm = jnp.maximum(jnp.maximum(lses[0], lses[1]), lses[2])
        ws = [jnp.exp(x - mm) for x in lses]
        col = (outs[0] * ws[0] + outs[1] * ws[1] + outs[2] * ws[2]) / (ws[0] + ws[1] + ws[2])
        att_ref[pl.ds(h, 1), :] = to_row(col)


def _proj_rnn2_kernel(x_ref, w_ref, gpre_ref, cw_ref, cb_ref, wa_ref, wx_ref, ba_ref, bx_ref, lam_ref, wro_ref,
                      qs_ref, c0_ref, c1_ref, c2_ref,
                      q0_ref, k0_ref, v0_ref, q1_ref, k1_ref, v1_ref, q2_ref, k2_ref, v2_ref,
                      zg_ref, mixr_ref, sga_ref, kv0_ref, kv1_ref, kv2_ref, conv_ref, h_ref, att_ref,
                      xn_s, xnf_s, xseg_s, xnseg_s, xn4_s, xn16_s, u_s, a_s, b_s, cprev_s, carry_s,
                      *, tm, n_tiles):
    m = pl.program_id(1)
    j = pl.program_id(2)
    tile = pl.program_id(0) * n_tiles + m
    half = TN // 2
    d1, d2 = ATTN_GROUPS[1][1], ATTN_GROUPS[2][1]
    seg = tm // SEGMENTS
    n_slab = D_MODEL // LANES
    heads_per_step = H_G // (TILES_PER_SAMPLE * 2)

    def project(lhs_s, block, lo=0, hi=TN):
        return jnp.dot(lhs_s[...], w_ref[:, block * TN + lo:block * TN + hi], preferred_element_type=F32)

    def emit(out_ref, val, c0, dilation):
        p = tm // dilation
        for r in range(dilation):
            out_ref[r] = val[r * p:(r + 1) * p, c0:c0 + half].astype(BF16)

    def natural_tail(out_ref, out_c0, val, c0, dilation):
        p = tm // dilation
        for c in range(half // LANES):
            for r in range(dilation):
                xnf_s[c, pl.ds(r, p, stride=dilation), :] = val[r * p:(r + 1) * p, c0 + c * LANES:c0 + (c + 1) * LANES]
            out_ref[:, out_c0 + c * LANES:out_c0 + (c + 1) * LANES] = xnf_s[c]

    def sample_heads(step):
        part = tile % TILES_PER_SAMPLE
        first = part * (heads_per_step * 2) + step * heads_per_step

        def slope_of(g, i):
            table = [float(_SLOPES[g, t * heads_per_step * 2 + step * heads_per_step + i])
                     for t in range(TILES_PER_SAMPLE)]
            out = jnp.float32(table[0])
            for t in range(1, TILES_PER_SAMPLE):
                out = jnp.where(part == t, jnp.float32(table[t]), out)
            return out

        _sample_heads(range(heads_per_step), first, slope_of, qs_ref, (c0_ref, c1_ref, c2_ref), att_ref)

    @pl.when(j == 0)
    def _recurrent_blocks():
        def norm(s, carry):
            r0 = pl.multiple_of(s * seg, seg)
            xv = x_ref[pl.ds(r0, seg), :]
            ms = jnp.mean(xv * xv, axis=-1, keepdims=True)
            xn = (xv * lax.rsqrt(ms + EPS)) * gpre_ref[...]
            xn_s[pl.ds(r0, seg), :] = xn.astype(BF16)
            for c in range(n_slab):
                piece = xn[:, c * LANES:(c + 1) * LANES]
                xnf_s[c, pl.ds(r0, seg), :] = piece
                xseg_s[c, pl.ds(s, seg, stride=SEGMENTS), :] = piece
            return carry

        lax.fori_loop(0, SEGMENTS, norm, 0)

        @pl.when(m == 0)
        def _reset():
            cprev_s[...] = jnp.zeros((8, D_RNN), F32)
            carry_s[...] = jnp.zeros((1, D_RNN), F32)

        for c in range(n_slab):
            xnseg_s[:, c * LANES:(c + 1) * LANES] = xseg_s[c].astype(BF16)
        for dil, dst in ((d1, xn4_s), (d2, xn16_s)):
            p = tm // dil
            for r in range(dil):
                for c in range(n_slab):
                    dst[r * p:(r + 1) * p, c * LANES:(c + 1) * LANES] = \
                        xnf_s[c, pl.ds(r, p, stride=dil), :].astype(BF16)

        u_s[...] = project(xnseg_s, 0)

        sub = lax.broadcasted_iota(jnp.int32, (SEGMENTS, GATE_BLOCK), 0)
        gate_rows = min(GATE_ROWS, tm)
        for c in range(N_GATE_BLOCKS):
            cols = slice(c * GATE_BLOCK, (c + 1) * GATE_BLOCK)

            def wrapped(t):
                v = pltpu.roll(u_s[SEGMENTS * (seg + t):SEGMENTS * (seg + t + 1), cols], 1, axis=0)
                return jnp.where(sub == 0, cprev_s[8 + t:9 + t, cols], v)

            wrap = {t: wrapped(t) for t in range(1 - CONV_WIDTH, 0)}
            for rc in range(tm // gate_rows):
                r0 = rc * gate_rows
                xc = cb_ref[:, cols] + u_s[r0:r0 + gate_rows, cols] * cw_ref[CONV_WIDTH - 1:CONV_WIDTH, cols]
                for sh in range(1, CONV_WIDTH):
                    if r0 == 0:
                        head = [wrap[t - sh] for t in range(sh)]
                        ush = jnp.concatenate(head + [u_s[0:gate_rows - SEGMENTS * sh, cols]], axis=0)
                    else:
                        ush = u_s[r0 - SEGMENTS * sh:r0 - SEGMENTS * sh + gate_rows, cols]
                    xc = xc + ush * cw_ref[CONV_WIDTH - 1 - sh:CONV_WIDTH - sh, cols]
                a, b = _lru_gates(xc, c, wa_ref, wx_ref, ba_ref, bx_ref, lam_ref)
                a_s[r0:r0 + gate_rows, cols] = a
                b_s[r0:r0 + gate_rows, cols] = b

        hloc = jnp.zeros((SEGMENTS, D_RNN), F32)
        cum = jnp.ones((SEGMENTS, D_RNN), F32)
        for t in range(seg):
            rows = slice(t * SEGMENTS, (t + 1) * SEGMENTS)
            av = a_s[rows, :]
            hloc = av * hloc + b_s[rows, :]
            cum = av * cum
            b_s[rows, :] = hloc
            a_s[rows, :] = cum
        state = carry_s[...]
        enter = []
        for s in range(SEGMENTS):
            enter.append(state)
            state = hloc[s:s + 1, :] + cum[s:s + 1, :] * state
        enter = jnp.concatenate(enter, axis=0)
        carry_s[...] = state
        h_ref[...] = state
        for k in range(1, CONV_WIDTH):
            last = u_s[tm - SEGMENTS * (k - 1) - 1:tm - SEGMENTS * (k - 1), :]
            conv_ref[CONV_WIDTH - 1 - k:CONV_WIDTH - k, :] = last
            cprev_s[8 - k:9 - k, :] = last

        z = project(xnseg_s, 1)
        split = (seg, SEGMENTS, D_RNN)
        h = b_s[...].reshape(split) + a_s[...].reshape(split) * enter[None]
        hz = (h.reshape(tm, D_RNN) * (z * _sigmoid(z))).astype(BF16)
        yr = jnp.dot(hz, wro_ref[...], preferred_element_type=F32)

        mix = _sigmoid(project(xnseg_s, 2)) * yr
        for c in range(n_slab):
            xseg_s[c] = mix[:, c * LANES:(c + 1) * LANES]
        for s in range(SEGMENTS):
            for c in range(n_slab):
                mixr_ref[s * seg:(s + 1) * seg, c * LANES:(c + 1) * LANES] = \
                    xseg_s[c, pl.ds(s, seg, stride=SEGMENTS), :].astype(BF16)

        sga_ref[...] = _sigmoid(project(xn_s, 3)).astype(BF16)
        sample_heads(0)

    @pl.when(j == 1)
    def _attention_blocks():
        r = project(xn_s, 4)
        zg_ref[...] = r[:, 0:half].astype(BF16)
        emit(q0_ref, r, half, 1)

        r = project(xn_s, 5)
        emit(k0_ref, r, 0, 1)
        emit(v0_ref, r, half, 1)
        kv0_ref[...] = r[tm - ATTN_GROUPS[0][0]:tm, :]

        r = project(xn4_s, 6)
        emit(q1_ref, r, 0, d1)
        emit(k1_ref, r, half, d1)
        natural_tail(kv1_ref, 0, r, half, d1)

        r = project(xn4_s, 7, 0, half)
        emit(v1_ref, r, 0, d1)
        natural_tail(kv1_ref, half, r, 0, d1)
        emit(q2_ref, project(xn16_s, 7, half, TN), 0, d2)

        r = project(xn16_s, 8)
        emit(k2_ref, r, 0, d2)
        emit(v2_ref, r, half, d2)
        natural_tail(kv2_ref, 0, r, 0, d2)
        natural_tail(kv2_ref, half, r, half, d2)
        sample_heads(1)


def _proj_rnn_prompt2(x, w_in, gpre, cw, cb, wa, wx, ba, bx, lam, wro, qkv_s, caches, layer):
    B, S, D = x.shape
    tm = TM_PROMPT2
    n_tiles = S // tm
    N = qkv_s.shape[0]
    heads_per_step = H_G // (TILES_PER_SAMPLE * 2)
    assert S % tm == 0 and tm % (SEGMENTS * 16) == 0 and tm >= ATTN_GROUPS[0][0]
    assert all(w % tm == 0 and S >= w for w, _ in ATTN_GROUPS[1:])
    assert B * n_tiles == N * TILES_PER_SAMPLE and n_tiles % TILES_PER_SAMPLE == 0
    cache_views = [jnp.transpose(c, (0, 1, 3, 4, 5, 2)) for c in caches]

    const2 = lambda b, m, j: (0, 0)
    const3 = lambda b, m, j: (0, 0, 0)
    tile3 = lambda b, m, j: (b, m, 0)
    tile4 = lambda b, m, j: (b, 0, m, 0)
    per_seq = lambda b, m, j: (b, 0, 0)
    sample_of = lambda b, m: (b * n_tiles + m) // TILES_PER_SAMPLE
    head_block = lambda b, m, j: ((b * n_tiles + m) % TILES_PER_SAMPLE) * 2 + j

    def last_rows(window):
        first_tile = n_tiles - window // tm
        return lambda b, m, j: (b, jnp.maximum(m - first_tile, 0), 0)

    resident = dict(pipeline_mode=pl.Buffered(1))
    qkv_shapes, qkv_specs = [], []
    for _, dil in ATTN_GROUPS:
        for _ in range(3):
            qkv_shapes.append(jax.ShapeDtypeStruct((B, dil, S // dil, D_ATTN), BF16))
            qkv_specs.append(pl.BlockSpec((None, dil, tm // dil, D_ATTN), tile4))
    out_shape = tuple(qkv_shapes) + (
        jax.ShapeDtypeStruct((B, S, D_ATTN), BF16),
        jax.ShapeDtypeStruct((B, S, D), BF16),
        jax.ShapeDtypeStruct((B, S, D), BF16),
        jax.ShapeDtypeStruct((B, ATTN_GROUPS[0][0], 2 * D_ATTN), F32),
        jax.ShapeDtypeStruct((B, ATTN_GROUPS[1][0], 2 * D_ATTN), F32),
        jax.ShapeDtypeStruct((B, ATTN_GROUPS[2][0], 2 * D_ATTN), F32),
        jax.ShapeDtypeStruct((B, CONV_WIDTH - 1, D_RNN), F32),
        jax.ShapeDtypeStruct((B, 1, D_RNN), F32),
        jax.ShapeDtypeStruct((N, H_G, HEAD_DIM), F32),
    )
    out_specs = tuple(qkv_specs) + (
        pl.BlockSpec((None, tm, D_ATTN), tile3),
        pl.BlockSpec((None, tm, D), tile3),
        pl.BlockSpec((None, tm, D), tile3),
        pl.BlockSpec((None, ATTN_GROUPS[0][0], 2 * D_ATTN), per_seq),
        pl.BlockSpec((None, tm, 2 * D_ATTN), last_rows(ATTN_GROUPS[1][0])),
        pl.BlockSpec((None, tm, 2 * D_ATTN), last_rows(ATTN_GROUPS[2][0])),
        pl.BlockSpec((None, CONV_WIDTH - 1, D_RNN), per_seq),
        pl.BlockSpec((None, 1, D_RNN), per_seq),
        pl.BlockSpec((None, H_G, HEAD_DIM), lambda b, m, j: (sample_of(b, m), 0, 0)),
    )
    in_specs = [
        pl.BlockSpec((None, tm, D), tile3),
        pl.BlockSpec((D, D_IN), const2, **resident),
        pl.BlockSpec((1, D), const2),
        pl.BlockSpec((CONV_WIDTH, D_RNN), const2),
        pl.BlockSpec((1, D_RNN), const2),
        pl.BlockSpec((N_GATE_BLOCKS, GATE_BLOCK, GATE_BLOCK), const3),
        pl.BlockSpec((N_GATE_BLOCKS, GATE_BLOCK, GATE_BLOCK), const3),
        pl.BlockSpec((1, D_RNN), const2),
        pl.BlockSpec((1, D_RNN), const2),
        pl.BlockSpec((1, D_RNN), const2),
        pl.BlockSpec((D_RNN, D), const2, **resident),
        pl.BlockSpec((None, 3 * N_GROUPS, H_G, HEAD_DIM), lambda b, m, j: (sample_of(b, m), 0, 0, 0)),
    ]
    in_specs += [pl.BlockSpec((None, None, 2, heads_per_step, HEAD_DIM, window),
                              lambda b, m, j: (layer, sample_of(b, m), 0, head_block(b, m, j), 0, 0))
                 for window, _ in ATTN_GROUPS]
    scratch = [
        pltpu.VMEM((tm, D), BF16),
        pltpu.VMEM((D // LANES, tm, LANES), F32),
        pltpu.VMEM((D // LANES, tm, LANES), F32),
        pltpu.VMEM((tm, D), BF16),
        pltpu.VMEM((tm, D), BF16),
        pltpu.VMEM((tm, D), BF16),
        pltpu.VMEM((tm, D_RNN), F32),
        pltpu.VMEM((tm, D_RNN), F32),
        pltpu.VMEM((tm, D_RNN), F32),
        pltpu.VMEM((8, D_RNN), F32),
        pltpu.VMEM((1, D_RNN), F32),
    ]
    return pl.pallas_call(
        functools.partial(_proj_rnn2_kernel, tm=tm, n_tiles=n_tiles),
        grid=(B, n_tiles, 2),
        in_specs=in_specs,
        out_specs=out_specs,
        out_shape=out_shape,
        scratch_shapes=scratch,
        compiler_params=pltpu.CompilerParams(
            dimension_semantics=("arbitrary", "arbitrary", "arbitrary"), vmem_limit_bytes=VMEM_LIMIT),
        name="proj_rnn_prompt",
    )(x, w_in, gpre, cw, cb, wa, wx, ba, bx, lam, wro, qkv_s, *cache_views)


def _proj_rnn_sample_kernel(x_ref, w_ref, gpre_ref, cw_ref, cb_ref, wa_ref, wx_ref, ba_ref, bx_ref, lam_ref, wro_ref,
                            sc_ref, h0_ref,
                            att_ref, mixr_ref, sga_ref, conv_ref, h_ref,
                            xn_s, h_s, yr_s):
    j = pl.program_id(0)

    def project():
        return jnp.dot(xn_s[...], w_ref[...], preferred_element_type=F32)

    @pl.when(j == 0)
    def _u_block():
        xv = x_ref[...]
        ms = jnp.mean(xv * xv, axis=-1, keepdims=True)
        xn_s[...] = ((xv * lax.rsqrt(ms + EPS)) * gpre_ref[...]).astype(BF16)
        u = project()
        taps = [sc_ref[:, k * D_RNN:(k + 1) * D_RNN] for k in range(CONV_WIDTH - 1)] + [u]
        for c in range(N_GATE_BLOCKS):
            c0 = c * GATE_BLOCK
            xc = cb_ref[:, c0:c0 + GATE_BLOCK]
            for tap in range(CONV_WIDTH):
                xc = xc + taps[tap][:, c0:c0 + GATE_BLOCK] * cw_ref[tap:tap + 1, c0:c0 + GATE_BLOCK]
            a, b = _lru_gates(xc, c, wa_ref, wx_ref, ba_ref, bx_ref, lam_ref)
            h = a * h0_ref[:, c0:c0 + GATE_BLOCK] + b
            h_s[:, c0:c0 + GATE_BLOCK] = h
            h_ref[:, c0:c0 + GATE_BLOCK] = h
        for k in range(1, CONV_WIDTH):
            conv_ref[:, (k - 1) * D_RNN:k * D_RNN] = taps[k]

    @pl.when(j == 1)
    def _z_rnn_block():
        z = project()
        hz = (h_s[...] * (z * _sigmoid(z))).astype(BF16)
        yr_s[...] = jnp.dot(hz, wro_ref[...], preferred_element_type=F32)

    @pl.when(j == 2)
    def _g_rnn_block():
        mixr_ref[...] = _sigmoid(project()) * yr_s[...]

    @pl.when(j == 3)
    def _g_attn_block():
        sga_ref[...] = _sigmoid(project())

    @pl.when(j >= 4)
    def _attn_blocks():
        att_ref[...] = project()


def _proj_rnn_sample(x, w_in, gpre, cw, cb, wa, wx, ba, bx, lam, wro, sconv, h0):
    N, D = x.shape
    const2 = lambda j: (0, 0)
    const3 = lambda j: (0, 0, 0)
    n_att = N_COL_BLOCKS - 4
    out_shape = (
        jax.ShapeDtypeStruct((N, n_att * TN), F32),
        jax.ShapeDtypeStruct((N, D), F32),
        jax.ShapeDtypeStruct((N, D), F32),
        jax.ShapeDtypeStruct((N, (CONV_WIDTH - 1) * D_RNN), F32),
        jax.ShapeDtypeStruct((N, D_RNN), F32),
    )
    out_specs = (
        pl.BlockSpec((N, TN), lambda j: (0, jnp.maximum(j - 4, 0))),
        pl.BlockSpec((N, D), const2),
        pl.BlockSpec((N, D), const2),
        pl.BlockSpec((N, (CONV_WIDTH - 1) * D_RNN), const2),
        pl.BlockSpec((N, D_RNN), const2),
    )
    in_specs = [
        pl.BlockSpec((N, D), const2),
        pl.BlockSpec((D, TN), lambda j: (0, j)),
        pl.BlockSpec((1, D), const2),
        pl.BlockSpec((CONV_WIDTH, D_RNN), const2),
        pl.BlockSpec((1, D_RNN), const2),
        pl.BlockSpec((N_GATE_BLOCKS, GATE_BLOCK, GATE_BLOCK), const3),
        pl.BlockSpec((N_GATE_BLOCKS, GATE_BLOCK, GATE_BLOCK), const3),
        pl.BlockSpec((1, D_RNN), const2),
        pl.BlockSpec((1, D_RNN), const2),
        pl.BlockSpec((1, D_RNN), const2),
        pl.BlockSpec((D_RNN, D), const2),
        pl.BlockSpec((N, (CONV_WIDTH - 1) * D_RNN), const2),
        pl.BlockSpec((N, D_RNN), const2),
    ]
    scratch = [pltpu.VMEM((N, D), BF16), pltpu.VMEM((N, D_RNN), F32), pltpu.VMEM((N, D), F32)]
    return pl.pallas_call(
        _proj_rnn_sample_kernel,
        grid=(N_COL_BLOCKS,),
        in_specs=in_specs,
        out_specs=out_specs,
        out_shape=out_shape,
        scratch_shapes=scratch,
        compiler_params=pltpu.CompilerParams(dimension_semantics=("arbitrary",), vmem_limit_bytes=VMEM_LIMIT),
        name="proj_rnn_sample",
    )(x, w_in, gpre, cw, cb, wa, wx, ba, bx, lam, wro, sconv, h0)


def _attn_kernel(q_ref, kc_ref, vc_ref, kp_ref, vp_ref, o_ref, st_ref, bias_s, *, group, chunk):
    dilation = ATTN_GROUPS[group][1]
    c = pl.program_id(2)
    nt = (((1,), (1,)), ((), ()))

    @pl.when((pl.program_id(0) == 0) & (pl.program_id(1) == 0) & (c == 0))
    def _init_bias():
        qi = lax.broadcasted_iota(jnp.int32, (Q_BLOCK, Q_BLOCK), 0)
        kj = lax.broadcasted_iota(jnp.int32, (Q_BLOCK, Q_BLOCK), 1)
        steps_prev = Q_BLOCK + qi - kj
        steps_cur = qi - kj
        dist_prev = (steps_prev * dilation).astype(F32)
        dist_cur = (steps_cur * dilation).astype(F32)
        for h in range(H_G):
            slope = float(_SLOPES[group, h])
            bias_s[h] = jnp.where(steps_cur >= 0, -slope * dist_cur, NEG_INF)
            bias_s[H_G + h] = jnp.where(steps_prev <= Q_BLOCK, -slope * dist_prev, NEG_INF)
            bias_s[2 * H_G + h] = jnp.full((Q_BLOCK, Q_BLOCK), NEG_INF, F32)

    lane = lax.broadcasted_iota(jnp.int32, (Q_BLOCK, LANES), 1)
    low = lane < HEAD_DIM
    lane2 = lax.broadcasted_iota(jnp.int32, (2 * Q_BLOCK, LANES), 1)
    low2 = lane2 < HEAD_DIM
    ones_lo = jnp.where(low2, 1.0, 0.0).astype(BF16)
    ones_hi = jnp.where(low2, 0.0, 1.0).astype(BF16)
    stat_lane = lax.broadcasted_iota(jnp.int32, (Q_BLOCK, STAT_LANES), 1)

    def keys_of(block, cur_ref, prev_ref, sl):
        if block == 0:
            return jnp.concatenate([prev_ref[:, sl], cur_ref[0:Q_BLOCK, sl]], axis=0)
        return cur_ref[(block - 1) * Q_BLOCK:(block + 1) * Q_BLOCK, sl]

    for i in range(chunk // Q_BLOCK):
        r0 = i * Q_BLOCK
        first = jnp.where(c == 0, H_G, 0) if i == 0 else 0
        stats = jnp.zeros((Q_BLOCK, STAT_LANES), F32)
        for p in range(H_G // 2):
            sl = slice(LANES * p, LANES * (p + 1))
            qp = q_ref[r0:r0 + Q_BLOCK, sl] * (HEAD_DIM ** -0.5)
            kp = keys_of(i, kc_ref, kp_ref, sl)
            vp = keys_of(i, vc_ref, vp_ref, sl)
            es, ms = [], []
            for hh in range(2):
                h = 2 * p + hh
                msk = low if hh == 0 else jnp.logical_not(low)
                qm = jnp.where(msk, qp, jnp.zeros_like(qp))
                s = lax.dot_general(qm, kp, nt, preferred_element_type=F32)
                s_p = s[:, 0:Q_BLOCK] + bias_s[H_G + h + first]
                s_c = s[:, Q_BLOCK:2 * Q_BLOCK] + bias_s[h]
                mx = jnp.maximum(jnp.max(s_p, axis=-1, keepdims=True), jnp.max(s_c, axis=-1, keepdims=True))
                es.append(jnp.exp(s_p - mx).astype(BF16))
                es.append(jnp.exp(s_c - mx).astype(BF16))
                ms.append(mx)
            vm0 = jnp.where(low2, vp, jnp.zeros_like(vp))
            vm1 = jnp.where(low2, jnp.zeros_like(vp), vp)
            w = jnp.concatenate([jnp.concatenate([vm0, ones_lo], axis=1),
                                 jnp.concatenate([vm1, ones_hi], axis=1)], axis=0)
            acc = jnp.dot(jnp.concatenate(es, axis=1), w, preferred_element_type=F32)
            l_pair = acc[:, LANES:2 * LANES]
            o_ref[r0:r0 + Q_BLOCK, sl] = (acc[:, 0:LANES] / l_pair).astype(o_ref.dtype)
            lse_pair = jnp.where(low, ms[0], ms[1]) + jnp.log(l_pair)
            keep = ((stat_lane % (STAT_LANES // 2)) // STAT_LANES_PER_HEAD) == p
            stats = jnp.where(keep, lse_pair, stats)
        st_ref[r0:r0 + Q_BLOCK, :] = stats


def _attn_group(q, k, v, group):
    B, dil, L, _ = q.shape
    chunk = min(ATTN_CHUNK, L)
    assert L % chunk == 0 and chunk % Q_BLOCK == 0
    cur = lambda b, r, c: (b, r, c, 0)
    prev = lambda b, r, c: (b, r, jnp.maximum(c * (chunk // Q_BLOCK) - 1, 0), 0)
    blk = (None, None, chunk, D_ATTN)
    pblk = (None, None, Q_BLOCK, D_ATTN)
    return pl.pallas_call(
        functools.partial(_attn_kernel, group=group, chunk=chunk),
        grid=(B, dil, L // chunk),
        in_specs=[pl.BlockSpec(blk, cur), pl.BlockSpec(blk, cur), pl.BlockSpec(blk, cur),
                  pl.BlockSpec(pblk, prev), pl.BlockSpec(pblk, prev)],
        out_specs=[pl.BlockSpec(blk, cur), pl.BlockSpec((None, None, chunk, STAT_LANES), cur)],
        out_shape=[jax.ShapeDtypeStruct((B, dil, L, D_ATTN), BF16),
                   jax.ShapeDtypeStruct((B, dil, L, STAT_LANES), F32)],
        scratch_shapes=[pltpu.VMEM((3 * H_G, Q_BLOCK, Q_BLOCK), F32)],
        compiler_params=pltpu.CompilerParams(
            dimension_semantics=("arbitrary", "arbitrary", "arbitrary"), vmem_limit_bytes=VMEM_LIMIT),
        name=f"attn_group{group}",
    )(q, k, v, k, v)


def _finish(att, z_ref, sga_ref, mixr_ref, x_ref, wao_ref, wo_ref, gpost_ref, y_ref):
    z = z_ref[...].astype(F32)
    ya_in = (att * (z * _sigmoid(z))).astype(BF16)
    ya = jnp.dot(ya_in, wao_ref[...], preferred_element_type=F32)
    mixed = mixr_ref[...].astype(F32) + sga_ref[...].astype(F32) * ya
    out = jnp.dot(mixed.astype(BF16), wo_ref[...], preferred_element_type=F32)
    ms = jnp.mean(out * out, axis=-1, keepdims=True)
    y_ref[...] = x_ref[...] + (out * lax.rsqrt(ms + EPS)) * gpost_ref[...]


def _out_kernel(att_ref, z_ref, sga_ref, mixr_ref, x_ref, wao_ref, wo_ref, gpost_ref, y_ref):
    _finish(att_ref[...].astype(F32), z_ref, sga_ref, mixr_ref, x_ref, wao_ref, wo_ref, gpost_ref, y_ref)


def _merge_out_kernel(o0_ref, o1_ref, o2_ref, s0_ref, s1_ref, s2_ref, z_ref, sga_ref, mixr_ref, x_ref,
                      wao_ref, wo_ref, gpost_ref, y_ref, nat1_s, nat2_s, st1_s, st2_s, *, tm):
    for o_ref, s_ref, nat_s, stn_s, dil in ((o1_ref, s1_ref, nat1_s, st1_s, ATTN_GROUPS[1][1]),
                                            (o2_ref, s2_ref, nat2_s, st2_s, ATTN_GROUPS[2][1])):
        p = tm // dil
        for r in range(dil):
            stn_s[pl.ds(r, p, stride=dil), :] = s_ref[r]
            for c in range(D_ATTN // LANES):
                nat_s[c, pl.ds(r, p, stride=dil), :] = o_ref[r, :, c * LANES:(c + 1) * LANES].astype(F32)

    lse = [s0_ref[...], st1_s[...], st2_s[...]]
    mm = jnp.maximum(jnp.maximum(lse[0], lse[1]), lse[2])
    ws = [jnp.exp(x - mm) for x in lse]
    den = ws[0] + ws[1] + ws[2]
    src = lax.broadcasted_iota(jnp.int32, (STAT_LANES, D_ATTN), 0)
    dst_head = lax.broadcasted_iota(jnp.int32, (STAT_LANES, D_ATTN), 1) // HEAD_DIM
    expand = jnp.where(src == (dst_head % 2) * (STAT_LANES // 2) + (dst_head // 2) * STAT_LANES_PER_HEAD,
                       1.0, 0.0).astype(BF16)
    outs = [o0_ref[...].astype(F32),
            jnp.concatenate([nat1_s[c] for c in range(D_ATTN // LANES)], axis=1),
            jnp.concatenate([nat2_s[c] for c in range(D_ATTN // LANES)], axis=1)]
    att = jnp.zeros((tm, D_ATTN), F32)
    for g in range(N_GROUPS):
        wexp = jnp.dot((ws[g] / den).astype(BF16), expand, preferred_element_type=F32)
        att = att + wexp * outs[g]
    _finish(att, z_ref, sga_ref, mixr_ref, x_ref, wao_ref, wo_ref, gpost_ref, y_ref)


def _merge_out_prompt(os, sts, zg, sga, mixr, x, wao, wo, gpost):
    B, S, D = x.shape
    tm = TM_OUT
    tile3 = lambda b, m: (b, m, 0)
    tile4 = lambda b, m: (b, 0, m, 0)
    const = lambda b, m: (0, 0)
    in_specs = []
    for (_, dil), width in [(g, D_ATTN) for g in ATTN_GROUPS] + [(g, STAT_LANES) for g in ATTN_GROUPS]:
        if dil == 1:
            in_specs.append(pl.BlockSpec((None, None, tm, width), tile4))
        else:
            in_specs.append(pl.BlockSpec((None, dil, tm // dil, width), tile4))
    in_specs += [
        pl.BlockSpec((None, tm, D_ATTN), tile3),
        pl.BlockSpec((None, tm, D), tile3),
        pl.BlockSpec((None, tm, D), tile3),
        pl.BlockSpec((None, tm, D), tile3),
        pl.BlockSpec((D_ATTN, D), const),
        pl.BlockSpec((D, D), const),
        pl.BlockSpec((1, D), const),
    ]
    scratch = [
        pltpu.VMEM((D_ATTN // LANES, tm, LANES), F32),
        pltpu.VMEM((D_ATTN // LANES, tm, LANES), F32),
        pltpu.VMEM((tm, STAT_LANES), F32),
        pltpu.VMEM((tm, STAT_LANES), F32),
    ]
    return pl.pallas_call(
        functools.partial(_merge_out_kernel, tm=tm),
        grid=(B, S // tm),
        in_specs=in_specs,
        out_specs=pl.BlockSpec((None, tm, D), tile3),
        out_shape=jax.ShapeDtypeStruct((B, S, D), F32),
        scratch_shapes=scratch,
        compiler_params=pltpu.CompilerParams(
            dimension_semantics=("arbitrary", "arbitrary"), vmem_limit_bytes=VMEM_LIMIT),
        name="merge_out_prompt",
    )(*os, *sts, zg, sga, mixr, x, wao, wo, gpost)


def _out_sample(att, z, sga, mixr, x, wao, wo, gpost):
    N, D = x.shape
    full = lambda shape: pl.BlockSpec(shape, lambda i: (0, 0))
    return pl.pallas_call(
        _out_kernel,
        grid=(1,),
        in_specs=[full((N, D_ATTN)), full((N, D_ATTN)), full((N, D)), full((N, D)), full((N, D)),
                  full((D_ATTN, D)), full((D, D)), full((1, D))],
        out_specs=full((N, D)),
        out_shape=jax.ShapeDtypeStruct((N, D), F32),
        compiler_params=pltpu.CompilerParams(dimension_semantics=("arbitrary",), vmem_limit_bytes=VMEM_LIMIT),
        name="out_sample",
    )(att, z, sga, mixr, x, wao, wo, gpost)


def _block_diag_chunks(w):
    per = GATE_BLOCK // RNN_BLOCK
    w = w.reshape(N_GATE_BLOCKS, per, RNN_BLOCK, RNN_BLOCK)
    eye = jnp.eye(per, dtype=w.dtype)
    dense = w[:, :, :, None, :] * eye[None, :, None, :, None]
    return dense.reshape(N_GATE_BLOCKS, GATE_BLOCK, GATE_BLOCK)


def _reorder_w_in(w_in):
    sizes = (D_RNN, D_RNN, N_GROUPS * D_ATTN, N_GROUPS * D_ATTN, N_GROUPS * D_ATTN, D_ATTN, D_MODEL, D_MODEL)
    starts = np.concatenate([[0], np.cumsum(sizes)[:-1]])
    u0, z0, q0, k0, v0, za0, gr0, ga0 = (int(s) for s in starts)
    pieces = [(u0, D_RNN), (z0, D_RNN), (gr0, D_MODEL), (ga0, D_MODEL), (za0, D_ATTN)]
    for g in range(N_GROUPS):
        pieces += [(q0 + g * D_ATTN, D_ATTN), (k0 + g * D_ATTN, D_ATTN), (v0 + g * D_ATTN, D_ATTN)]
    return jnp.concatenate([w_in[:, s:s + n] for s, n in pieces], axis=1).astype(BF16)


def _layer(layer, yp, ys, sconv, h0, caches, norm_pre, norm_post, w_in, conv_w, conv_b, lru_w_a, lru_b_a, lru_w_x,
           lru_b_x, lru_lambda, w_rnn_out, w_attn_out, w_out):
    B, S, D = yp.shape
    N = ys.shape[0]
    row = lambda v: v.reshape(1, -1)
    w_in_b = _reorder_w_in(w_in)
    wa = _block_diag_chunks(lru_w_a).astype(BF16)
    wx = _block_diag_chunks(lru_w_x).astype(BF16)
    wro = w_rnn_out.astype(BF16)
    wao = w_attn_out.astype(BF16)
    wo = w_out.astype(BF16)
    shared = (w_in_b, row(norm_pre), conv_w, row(conv_b), wa, wx, row(lru_b_a), row(lru_b_x), row(lru_lambda), wro)

    xs = ys.reshape(N, D)
    att_in, mixr_s, sga_s, conv_s, h_s = _proj_rnn_sample(
        xs, *shared, sconv.reshape(N, (CONV_WIDTH - 1) * D_RNN), h0)
    z_s = att_in[:, 0:D_ATTN]
    qkv_s = att_in[:, D_ATTN:].reshape(N, 3 * N_GROUPS, H_G, HEAD_DIM)

    outs = _proj_rnn_prompt2(yp, *shared, qkv_s, caches, layer)
    qkv, (zg, mixr, sga, kv0, kv1, kv2, conv_p, h_p, att_s) = outs[:9], outs[9:]
    os, sts = [], []
    for g in range(N_GROUPS):
        o, st = _attn_group(qkv[3 * g], qkv[3 * g + 1], qkv[3 * g + 2], g)
        os.append(o)
        sts.append(st)
    y_p = _merge_out_prompt(os, sts, zg, sga, mixr, yp, wao, wo, row(norm_post))
    kv_p = [kv.reshape(B, kv.shape[1], 2, H_G, HEAD_DIM) for kv in (kv0, kv1, kv2)]

    y_s = _out_sample(att_s.reshape(N, D_ATTN), z_s, sga_s, mixr_s, xs, wao, wo, row(norm_post))
    kv_s = [qkv_s[:, 3 * g + 1:3 * g + 3].reshape(N, 1, 2, H_G, HEAD_DIM) for g in range(N_GROUPS)]

    return (y_p, y_s.reshape(N, 1, D), conv_p, conv_s.reshape(N, CONV_WIDTH - 1, D_RNN),
            h_p.reshape(B, D_RNN), h_s, kv_p, kv_s)


def kernel(x_prompt, x_sample, state_conv, state_h, cache_kv_w128, cache_kv_w512, cache_kv_w2048, norm_pre, norm_post, w_in, conv_w, conv_b, lru_w_a, lru_b_a, lru_w_x, lru_b_x, lru_lambda, w_rnn_out, w_attn_out, w_out):
    depth = norm_pre.shape[0]
    caches = (cache_kv_w128, cache_kv_w512, cache_kv_w2048)
    yp, ys = x_prompt, x_sample
    conv_p, conv_s, h_p, h_s = [], [], [], []
    kvp = ([], [], [])
    kvs = ([], [], [])
    for l in range(depth):
        yp, ys, cp, cs, hp, hs, kv_p, kv_s = _layer(
            l, yp, ys, state_conv[l], state_h[l], caches,
            norm_pre[l], norm_post[l], w_in[l], conv_w[l], conv_b[l], lru_w_a[l], lru_b_a[l], lru_w_x[l],
            lru_b_x[l], lru_lambda[l], w_rnn_out[l], w_attn_out[l], w_out[l])
        conv_p.append(cp)
        conv_s.append(cs)
        h_p.append(hp)
        h_s.append(hs)
        for g in range(N_GROUPS):
            kvp[g].append(kv_p[g])
            kvs[g].append(kv_s[g])
    return (yp, ys, jnp.stack(conv_p), jnp.stack(conv_s), jnp.stack(h_p), jnp.stack(h_s),
            jnp.stack(kvp[0]), jnp.stack(kvs[0]), jnp.stack(kvp[1]), jnp.stack(kvs[1]),
            jnp.stack(kvp[2]), jnp.stack(kvs[2]))
```

```python
import functools

import numpy as np
import jax
import jax.numpy as jnp
from jax import lax
from jax.experimental import pallas as pl
from jax.experimental.pallas import tpu as pltpu

F32 = jnp.float32
BF16 = jnp.bfloat16

D_MODEL = 1024
D_RNN = 1024
N_RNN_BLOCKS = 16
RNN_BLOCK = D_RNN // N_RNN_BLOCKS
CONV_WIDTH = 4
LRU_C = 8.0
HEAD_DIM = 64
H_G = 8
ATTN_GROUPS = ((128, 1), (512, 4), (2048, 16))
N_GROUPS = 3
D_ATTN = H_G * HEAD_DIM
Q_BLOCK = 128
ALIBI_MAX = 8.0
EPS = 1e-6
NEG_INF = -1e30
D_IN = 2 * D_RNN + 3 * N_GROUPS * D_ATTN + D_ATTN + 2 * D_MODEL

LANES = 128
TN = 1024
N_COL_BLOCKS = D_IN // TN
GATE_BLOCK = 256
N_GATE_BLOCKS = D_RNN // GATE_BLOCK
STAT_LANES = LANES
STAT_LANES_PER_HEAD = STAT_LANES // H_G

TM_PROMPT = 512
TM_PROMPT2 = 256
TILES_PER_SAMPLE = 2
SEGMENTS = 8
TM_OUT = 512
OUT_ROWS = TM_OUT
ATTN_CHUNK = 512
ROW_CHUNK = 64
GATE_ROWS = 128
VMEM_LIMIT = 56 * 1024 * 1024


def _alibi_slopes():
    n = N_GROUPS * H_G
    s = np.float32(2.0) ** (np.float32(-ALIBI_MAX) * np.arange(1, n + 1, dtype=np.float32) / np.float32(n))
    return s.reshape(N_GROUPS, H_G)


_SLOPES = _alibi_slopes()


def _stat_lane(h):
    return (h % 2) * (STAT_LANES // 2) + (h // 2) * STAT_LANES_PER_HEAD


def _softplus(y):
    return jnp.maximum(y, 0.0) + jnp.log1p(jnp.exp(-jnp.abs(y)))


def _sigmoid(x):
    return 0.5 * jnp.tanh(0.5 * x) + 0.5


def _for_rows(n_rows, chunk, fn):
    if n_rows <= chunk:
        fn(0)
        return

    def body(c, carry):
        fn(pl.multiple_of(c * chunk, chunk))
        return carry

    lax.fori_loop(0, n_rows // chunk, body, 0)


def _lru_gates(xc, c, wa_ref, wx_ref, ba_ref, bx_ref, lam_ref):
    c0 = c * GATE_BLOCK
    xcb = xc.astype(BF16)
    r = _sigmoid(jnp.dot(xcb, wa_ref[c], preferred_element_type=F32) + ba_ref[:, c0:c0 + GATE_BLOCK])
    i = _sigmoid(jnp.dot(xcb, wx_ref[c], preferred_element_type=F32) + bx_ref[:, c0:c0 + GATE_BLOCK])
    log_a = (-LRU_C * r) * _softplus(-lam_ref[:, c0:c0 + GATE_BLOCK])
    a = jnp.exp(log_a)
    v = 1.0 - a * a
    b = jnp.where(v > 0.0, v * lax.rsqrt(v), 0.0) * i * xc
    return a, b


def _sample_head_attention(h, qs_ref, cache_refs, att_ref):
    scale = HEAD_DIM ** -0.5
    ri = lax.broadcasted_iota(jnp.int32, (HEAD_DIM, HEAD_DIM), 0)
    ci = lax.broadcasted_iota(jnp.int32, (HEAD_DIM, HEAD_DIM), 1)
    eye = ri == ci

    def to_col(row):
        return jnp.sum(jnp.where(eye, jnp.broadcast_to(row, (HEAD_DIM, HEAD_DIM)), 0.0), axis=-1, keepdims=True)

    def to_row(col):
        return jnp.sum(jnp.where(eye, jnp.broadcast_to(col, (HEAD_DIM, HEAD_DIM)), 0.0), axis=0, keepdims=True)

    outs, lses = [], []
    for g, (window, dilation) in enumerate(ATTN_GROUPS):
        c_ref = cache_refs[g]
        pos = lax.broadcasted_iota(jnp.int32, (1, window), 1)
        dist = (window - pos).astype(F32)
        bias = jnp.where((pos % dilation) == 0, -float(_SLOPES[g, h]) * dist, NEG_INF)
        q = qs_ref[3 * g, h:h + 1, :] * scale
        s_new = jnp.sum(qs_ref[3 * g + 1, h:h + 1, :] * q, axis=-1, keepdims=True)
        v_col = to_col(qs_ref[3 * g + 2, h:h + 1, :])
        s = jnp.sum(c_ref[0] * to_col(q), axis=0, keepdims=True) + bias
        mx = jnp.maximum(jnp.max(s, axis=-1, keepdims=True), s_new)
        e = jnp.exp(s - mx)
        e_new = jnp.exp(s_new - mx)
        l = jnp.sum(e, axis=-1, keepdims=True) + e_new
        o = jnp.sum(c_ref[1] * e, axis=-1, keepdims=True) + e_new * v_col
        outs.append(o / l)
        lses.append(mx + jnp.log(l))
    mm = jnp.maximum(jnp.maximum(lses[0], lses[1]), lses[2])
    ws = [jnp.exp(x - mm) for x in lses]
    col = (outs[0] * ws[0] + outs[1] * ws[1] + outs[2] * ws[2]) / (ws[0] + ws[1] + ws[2])
    att_ref[h:h + 1, :] = to_row(col)


def _proj_rnn_kernel(x_ref, w_ref, gpre_ref, cw_ref, cb_ref, wa_ref, wx_ref, ba_ref, bx_ref, lam_ref, wro_ref,
                     qs_ref, c0_ref, c1_ref, c2_ref,
                     q0_ref, k0_ref, v0_ref, q1_ref, k1_ref, v1_ref, q2_ref, k2_ref, v2_ref,
                     zg_ref, mixr_ref, sga_ref, kv0_ref, kv1_ref, kv2_ref, conv_ref, h_ref, att_ref,
                     xn_s, xnf_s, xseg_s, xnseg_s, xn4_s, xn16_s, u_s, a_s, b_s, yr_s,
                     cprev_s, cin_s, carry_s,
                     *, tm, n_tiles):
    m = pl.program_id(1)
    j = pl.program_id(2)
    acc_s = u_s

    def sample_head(h):
        _sample_head_attention(h, qs_ref, (c0_ref, c1_ref, c2_ref), att_ref)
    last_tile = m == n_tiles - 1
    in_kv2 = m >= n_tiles - ATTN_GROUPS[2][0] // tm
    half = TN // 2
    d1, d2 = ATTN_GROUPS[1][1], ATTN_GROUPS[2][1]

    def project(lhs_s):
        return jnp.dot(lhs_s[...], w_ref[...], preferred_element_type=F32)

    def emit(out_ref, c0, dilation):
        p = tm // dilation
        for r in range(dilation):
            out_ref[r] = acc_s[r * p:(r + 1) * p, c0:c0 + half].astype(BF16)

    def natural_tail(out_ref, out_c0, c0, dilation):
        p = tm // dilation
        for c in range(half // LANES):
            for r in range(dilation):
                xnf_s[c, pl.ds(r, p, stride=dilation), :] = \
                    acc_s[r * p:(r + 1) * p, c0 + c * LANES:c0 + (c + 1) * LANES]
            out_ref[:, out_c0 + c * LANES:out_c0 + (c + 1) * LANES] = xnf_s[c]

    seg = tm // SEGMENTS
    n_slab = D_MODEL // LANES

    @pl.when(j == 0)
    def _u_block():
        def norm(s, carry):
            r0 = pl.multiple_of(s * seg, seg)
            xv = x_ref[pl.ds(r0, seg), :]
            ms = jnp.mean(xv * xv, axis=-1, keepdims=True)
            xn = (xv * lax.rsqrt(ms + EPS)) * gpre_ref[...]
            xn_s[pl.ds(r0, seg), :] = xn.astype(BF16)
            for c in range(n_slab):
                piece = xn[:, c * LANES:(c + 1) * LANES]
                xnf_s[c, pl.ds(r0, seg), :] = piece
                xseg_s[c, pl.ds(s, seg, stride=SEGMENTS), :] = piece
            return carry

        lax.fori_loop(0, SEGMENTS, norm, 0)

        for c in range(n_slab):
            xnseg_s[:, c * LANES:(c + 1) * LANES] = xseg_s[c].astype(BF16)

        for dil, dst in ((d1, xn4_s), (d2, xn16_s)):
            p = tm // dil
            for r in range(dil):
                for c in range(n_slab):
                    dst[r * p:(r + 1) * p, c * LANES:(c + 1) * LANES] = \
                        xnf_s[c, pl.ds(r, p, stride=dil), :].astype(BF16)

        @pl.when(m == 0)
        def _reset():
            cprev_s[...] = jnp.zeros((8, D_RNN), F32)
            carry_s[...] = jnp.zeros((1, D_RNN), F32)

        u_s[...] = project(xnseg_s)

        sub = lax.broadcasted_iota(jnp.int32, (SEGMENTS, GATE_BLOCK), 0)
        for c in range(N_GATE_BLOCKS):
            cols = slice(c * GATE_BLOCK, (c + 1) * GATE_BLOCK)

            def wrapped(t):
                v = pltpu.roll(u_s[SEGMENTS * (seg + t):SEGMENTS * (seg + t + 1), cols], 1, axis=0)
                return jnp.where(sub == 0, cprev_s[8 + t:9 + t, cols], v)

            wrap = {t: wrapped(t) for t in range(1 - CONV_WIDTH, 0)}
            for rc in range(tm // GATE_ROWS):
                r0 = rc * GATE_ROWS
                xc = cb_ref[:, cols] + u_s[r0:r0 + GATE_ROWS, cols] * cw_ref[CONV_WIDTH - 1:CONV_WIDTH, cols]
                for sh in range(1, CONV_WIDTH):
                    if r0 == 0:
                        head = [wrap[t - sh] for t in range(sh)]
                        ush = jnp.concatenate(head + [u_s[0:GATE_ROWS - SEGMENTS * sh, cols]], axis=0)
                    else:
                        ush = u_s[r0 - SEGMENTS * sh:r0 - SEGMENTS * sh + GATE_ROWS, cols]
                    xc = xc + ush * cw_ref[CONV_WIDTH - 1 - sh:CONV_WIDTH - sh, cols]
                a, b = _lru_gates(xc, c, wa_ref, wx_ref, ba_ref, bx_ref, lam_ref)
                a_s[r0:r0 + GATE_ROWS, cols] = a
                b_s[r0:r0 + GATE_ROWS, cols] = b

        def scan(t, carry):
            hloc, cum = carry
            r0 = pl.multiple_of(t * SEGMENTS, SEGMENTS)
            av = a_s[pl.ds(r0, SEGMENTS), :]
            hloc = av * hloc + b_s[pl.ds(r0, SEGMENTS), :]
            cum = av * cum
            b_s[pl.ds(r0, SEGMENTS), :] = hloc
            a_s[pl.ds(r0, SEGMENTS), :] = cum
            return hloc, cum

        h_end, a_end = lax.fori_loop(
            0, seg, scan, (jnp.zeros((SEGMENTS, D_RNN), F32), jnp.ones((SEGMENTS, D_RNN), F32)), unroll=4)
        state = carry_s[...]
        for s in range(SEGMENTS):
            cin_s[s:s + 1, :] = state
            state = h_end[s:s + 1, :] + a_end[s:s + 1, :] * state
        carry_s[...] = state
        h_ref[...] = state
        for k in range(1, CONV_WIDTH):
            last = u_s[tm - SEGMENTS * (k - 1) - 1:tm - SEGMENTS * (k - 1), :]
            conv_ref[CONV_WIDTH - 1 - k:CONV_WIDTH - k, :] = last
            cprev_s[8 - k:9 - k, :] = last

    @pl.when(j == 1)
    def _z_rnn_block():
        sample_head(0)
        z = project(xnseg_s)
        split = (seg, SEGMENTS, D_RNN)
        h = b_s[...].reshape(split) + a_s[...].reshape(split) * cin_s[...][None]
        hz = (h.reshape(tm, D_RNN) * (z * _sigmoid(z))).astype(BF16)
        yr_s[...] = jnp.dot(hz, wro_ref[...], preferred_element_type=F32)

    @pl.when(j == 2)
    def _g_rnn_block():
        sample_head(1)
        mix = _sigmoid(project(xnseg_s)) * yr_s[...]
        for c in range(n_slab):
            xseg_s[c] = mix[:, c * LANES:(c + 1) * LANES]
        for s in range(SEGMENTS):
            for c in range(n_slab):
                mixr_ref[s * seg:(s + 1) * seg, c * LANES:(c + 1) * LANES] = \
                    xseg_s[c, pl.ds(s, seg, stride=SEGMENTS), :].astype(BF16)

    @pl.when(j == 3)
    def _g_attn_block():
        sample_head(2)
        sga_ref[...] = _sigmoid(project(xn_s)).astype(BF16)

    @pl.when(j == 4)
    def _zattn_q0_block():
        sample_head(3)
        acc_s[...] = project(xn_s)
        zg_ref[...] = acc_s[:, 0:half].astype(BF16)
        emit(q0_ref, half, 1)

    @pl.when(j == 5)
    def _k0_v0_block():
        sample_head(4)
        acc_s[...] = project(xn_s)
        emit(k0_ref, 0, 1)
        emit(v0_ref, half, 1)

        @pl.when(last_tile)
        def _():
            kv0_ref[...] = acc_s[tm - ATTN_GROUPS[0][0]:tm, :]

    @pl.when(j == 6)
    def _q1_k1_block():
        sample_head(5)
        acc_s[...] = project(xn4_s)
        emit(q1_ref, 0, d1)
        emit(k1_ref, half, d1)

        @pl.when(last_tile)
        def _():
            natural_tail(kv1_ref, 0, half, d1)

    @pl.when(j == 7)
    def _v1_q2_block():
        sample_head(6)
        acc_s[:, 0:half] =jnp.dot(xn4_s[...], w_ref[:, 0:half], preferred_element_type=F32)
        acc_s[:, half:TN] = jnp.dot(xn16_s[...], w_ref[:, half:TN], preferred_element_type=F32)
        emit(v1_ref, 0, d1)
        emit(q2_ref, half, d2)

        @pl.when(last_tile)
        def _():
            natural_tail(kv1_ref, half, 0, d1)

    @pl.when(j == 8)
    def _k2_v2_block():
        sample_head(7)
        acc_s[...] = project(xn16_s)
        emit(k2_ref, 0, d2)
        emit(v2_ref, half, d2)

        @pl.when(in_kv2)
        def _():
            natural_tail(kv2_ref, 0, 0, d2)
            natural_tail(kv2_ref, half, half, d2)


def _proj_rnn_prompt(x, w_in, gpre, cw, cb, wa, wx, ba, bx, lam, wro, qkv_s, caches, layer):
    B, S, D = x.shape
    tm = TM_PROMPT
    n_tiles = S // tm
    m0 = n_tiles - ATTN_GROUPS[2][0] // tm
    assert S % tm == 0 and tm == ATTN_GROUPS[1][0] and ATTN_GROUPS[2][0] % tm == 0 and S >= ATTN_GROUPS[2][0]
    N = qkv_s.shape[0]
    assert N == B * n_tiles and N_COL_BLOCKS == H_G + 1
    cache_views = [jnp.transpose(c, (0, 1, 3, 4, 5, 2)) for c in caches]
    sample_seq = lambda b, m, j: (b * n_tiles + m, 0, 0, 0)
    cache_head = lambda b, m, j: (layer, b * n_tiles + m, 0, jnp.clip(j - 1, 0, H_G - 1), 0, 0)

    const2 = lambda b, m, j: (0, 0)
    const3 = lambda b, m, j: (0, 0, 0)
    tile3 = lambda b, m, j: (b, m, 0)
    tile4 = lambda b, m, j: (b, 0, m, 0)
    per_seq = lambda b, m, j: (b, 0, 0)

    qkv_shapes, qkv_specs = [], []
    for _, dil in ATTN_GROUPS:
        for _ in range(3):
            qkv_shapes.append(jax.ShapeDtypeStruct((B, dil, S // dil, D_ATTN), BF16))
            qkv_specs.append(pl.BlockSpec((None, dil, tm // dil, D_ATTN), tile4))
    out_shape = tuple(qkv_shapes) + (
        jax.ShapeDtypeStruct((B, S, D_ATTN), BF16),
        jax.ShapeDtypeStruct((B, S, D), BF16),
        jax.ShapeDtypeStruct((B, S, D), BF16),
        jax.ShapeDtypeStruct((B, ATTN_GROUPS[0][0], 2 * D_ATTN), F32),
        jax.ShapeDtypeStruct((B, ATTN_GROUPS[1][0], 2 * D_ATTN), F32),
        jax.ShapeDtypeStruct((B, ATTN_GROUPS[2][0], 2 * D_ATTN), F32),
        jax.ShapeDtypeStruct((B, CONV_WIDTH - 1, D_RNN), F32),
        jax.ShapeDtypeStruct((B, 1, D_RNN), F32),
        jax.ShapeDtypeStruct((N, H_G, HEAD_DIM), F32),
    )
    out_specs = tuple(qkv_specs) + (
        pl.BlockSpec((None, tm, D_ATTN), tile3),
        pl.BlockSpec((None, tm, D), tile3),
        pl.BlockSpec((None, tm, D), tile3),
        pl.BlockSpec((None, ATTN_GROUPS[0][0], 2 * D_ATTN), per_seq),
        pl.BlockSpec((None, ATTN_GROUPS[1][0], 2 * D_ATTN), per_seq),
        pl.BlockSpec((None, tm, 2 * D_ATTN), lambda b, m, j: (b, jnp.maximum(m - m0, 0), 0)),
        pl.BlockSpec((None, CONV_WIDTH - 1, D_RNN), per_seq),
        pl.BlockSpec((None, 1, D_RNN), per_seq),
        pl.BlockSpec((None, H_G, HEAD_DIM), lambda b, m, j: (b * n_tiles + m, 0, 0)),
    )
    in_specs = [
        pl.BlockSpec((None, tm, D), tile3),
        pl.BlockSpec((D, TN), lambda b, m, j: (0, j)),
        pl.BlockSpec((1, D), const2),
        pl.BlockSpec((CONV_WIDTH, D_RNN), const2),
        pl.BlockSpec((1, D_RNN), const2),
        pl.BlockSpec((N_GATE_BLOCKS, GATE_BLOCK, GATE_BLOCK), const3),
        pl.BlockSpec((N_GATE_BLOCKS, GATE_BLOCK, GATE_BLOCK), const3),
        pl.BlockSpec((1, D_RNN), const2),
        pl.BlockSpec((1, D_RNN), const2),
        pl.BlockSpec((1, D_RNN), const2),
        pl.BlockSpec((D_RNN, D), const2),
        pl.BlockSpec((None, 3 * N_GROUPS, H_G, HEAD_DIM), sample_seq),
    ]
    in_specs += [pl.BlockSpec((None, None, 2, None, HEAD_DIM, window), cache_head) for window, _ in ATTN_GROUPS]
    scratch = [
        pltpu.VMEM((tm, D), BF16),
        pltpu.VMEM((D // LANES, tm, LANES), F32),
        pltpu.VMEM((D // LANES, tm, LANES), F32),
        pltpu.VMEM((tm, D), BF16),
        pltpu.VMEM((tm, D), BF16),
        pltpu.VMEM((tm, D), BF16),
        pltpu.VMEM((tm, D_RNN), F32),
        pltpu.VMEM((tm, D_RNN), F32),
        pltpu.VMEM((tm, D_RNN), F32),
        pltpu.VMEM((tm, D), F32),
        pltpu.VMEM((8, D_RNN), F32),
        pltpu.VMEM((SEGMENTS, D_RNN), F32),
        pltpu.VMEM((1, D_RNN), F32),
    ]
    return pl.pallas_call(
        functools.partial(_proj_rnn_kernel, tm=tm, n_tiles=n_tiles),
        grid=(B, n_tiles, N_COL_BLOCKS),
        in_specs=in_specs,
        out_specs=out_specs,
        out_shape=out_shape,
        scratch_shapes=scratch,
        compiler_params=pltpu.CompilerParams(
            dimension_semantics=("arbitrary", "arbitrary", "arbitrary"), vmem_limit_bytes=VMEM_LIMIT),
        name="proj_rnn_prompt",
    )(x, w_in, gpre, cw, cb, wa, wx, ba, bx, lam, wro, qkv_s, *cache_views)


def _sample_heads(local_heads, first_head, slope_of, qs_ref, cache_refs, att_ref):
    scale = HEAD_DIM ** -0.5
    ri = lax.broadcasted_iota(jnp.int32, (HEAD_DIM, HEAD_DIM), 0)
    ci = lax.broadcasted_iota(jnp.int32, (HEAD_DIM, HEAD_DIM), 1)
    eye = ri == ci

    def to_col(row):
        return jnp.sum(jnp.where(eye, jnp.broadcast_to(row, (HEAD_DIM, HEAD_DIM)), 0.0), axis=-1, keepdims=True)

    def to_row(col):
        return jnp.sum(jnp.where(eye, jnp.broadcast_to(col, (HEAD_DIM, HEAD_DIM)), 0.0), axis=0, keepdims=True)

    for i in local_heads:
        h = first_head + i
        outs, lses = [], []
        for g, (window, dilation) in enumerate(ATTN_GROUPS):
            c_ref = cache_refs[g]
            pos = lax.broadcasted_iota(jnp.int32, (1, window), 1)
            dist = (window - pos).astype(F32)
            bias = jnp.where((pos % dilation) == 0, -slope_of(g, i) * dist, NEG_INF)
            q = qs_ref[3 * g, pl.ds(h, 1), :] * scale
            s_new = jnp.sum(qs_ref[3 * g + 1, pl.ds(h, 1), :] * q, axis=-1, keepdims=True)
            v_col = to_col(qs_ref[3 * g + 2, pl.ds(h, 1), :])
            s = jnp.sum(c_ref[0, i] * to_col(q), axis=0, keepdims=True) + bias
            mx = jnp.maximum(jnp.max(s, axis=-1, keepdims=True), s_new)
            e = jnp.exp(s - mx)
            e_new = jnp.exp(s_new - mx)
            l = jnp.sum(e, axis=-1, keepdims=True) + e_new
            o = jnp.sum(c_ref[1, i] * e, axis=-1, keepdims=True) + e_new * v_col
            outs.append(o / l)
            lses.append(mx + jnp.log(l))
        mm = jnp.maximum(jnp.maximum(lses[0], lses[1]), lses[2])
        ws = [jnp.exp(x - mm) for x in lses]
        col = (outs[0] * ws[0] + outs[1] * ws[1] + outs[2] * ws[2]) / (ws[0] + ws[1] + ws[2])
        att_ref[pl.ds(h, 1), :] = to_row(col)


def _proj_rnn2_kernel(x_ref, w_ref, gpre_ref, cw_ref, cb_ref, wa_ref, wx_ref, ba_ref, bx_ref, lam_ref, wro_ref,
                      qs_ref, c0_ref, c1_ref, c2_ref,
                      q0_ref, k0_ref, v0_ref, q1_ref, k1_ref, v1_ref, q2_ref, k2_ref, v2_ref,
                      zg_ref, mixr_ref, sga_ref, kv0_ref, kv1_ref, kv2_ref, conv_ref, h_ref, att_ref,
                      xn_s, xnf_s, xseg_s, xnseg_s, xn4_s, xn16_s, u_s, a_s, b_s, cprev_s, carry_s,
                      *, tm, n_tiles):
    m = pl.program_id(1)
    j = pl.program_id(2)
    tile = pl.program_id(0) * n_tiles + m
    half = TN // 2
    d1, d2 = ATTN_GROUPS[1][1], ATTN_GROUPS[2][1]
    seg = tm // SEGMENTS
    n_slab = D_MODEL // LANES
    heads_per_step = H_G // (TILES_PER_SAMPLE * 2)

    def project(lhs_s, block, lo=0, hi=TN):
        return jnp.dot(lhs_s[...], w_ref[:, block * TN + lo:block * TN + hi], preferred_element_type=F32)

    def emit(out_ref, val, c0, dilation):
        p = tm // dilation
        for r in range(dilation):
            out_ref[r] = val[r * p:(r + 1) * p, c0:c0 + half].astype(BF16)

    def natural_tail(out_ref, out_c0, val, c0, dilation):
        p = tm // dilation
        for c in range(half // LANES):
            for r in range(dilation):
                xnf_s[c, pl.ds(r, p, stride=dilation), :] = val[r * p:(r + 1) * p, c0 + c * LANES:c0 + (c + 1) * LANES]
            out_ref[:, out_c0 + c * LANES:out_c0 + (c + 1) * LANES] = xnf_s[c]

    def sample_heads(step):
        part = tile % TILES_PER_SAMPLE
        first = part * (heads_per_step * 2) + step * heads_per_step

        def slope_of(g, i):
            table = [float(_SLOPES[g, t * heads_per_step * 2 + step * heads_per_step + i])
                     for t in range(TILES_PER_SAMPLE)]
            out = jnp.float32(table[0])
            for t in range(1, TILES_PER_SAMPLE):
                out = jnp.where(part == t, jnp.float32(table[t]), out)
            return out

        _sample_heads(range(heads_per_step), first, slope_of, qs_ref, (c0_ref, c1_ref, c2_ref), att_ref)

    @pl.when(j == 0)
    def _recurrent_blocks():
        @pl.when(m == 0)
        def _reset():
            cprev_s[...] = jnp.zeros((8, D_RNN), F32)
            carry_s[...] = jnp.zeros((1, D_RNN), F32)

        for s in range(SEGMENTS):
            r0 = s * seg
            xv = x_ref[r0:r0 + seg, :]
            ms = jnp.mean(xv * xv, axis=-1, keepdims=True)
            xn = (xv * lax.rsqrt(ms + EPS)) * gpre_ref[...]
            xn_s[r0:r0 + seg, :] = xn.astype(BF16)
            for c in range(n_slab):
                piece = xn[:, c * LANES:(c + 1) * LANES]
                xnf_s[c, r0:r0 + seg, :] = piece
                xseg_s[c, pl.ds(s, seg, stride=SEGMENTS), :] = piece

        for c in range(n_slab):
            xnseg_s[:, c * LANES:(c + 1) * LANES] = xseg_s[c].astype(BF16)

        u_s[...] = project(xnseg_s, 0)

        sub = lax.broadcasted_iota(jnp.int32, (SEGMENTS, GATE_BLOCK), 0)
        gate_rows = min(GATE_ROWS, tm)
        for c in range(N_GATE_BLOCKS):
            cols = slice(c * GATE_BLOCK, (c + 1) * GATE_BLOCK)

            def wrapped(t):
                v = pltpu.roll(u_s[SEGMENTS * (seg + t):SEGMENTS * (seg + t + 1), cols], 1, axis=0)
                return jnp.where(sub == 0, cprev_s[8 + t:9 + t, cols], v)

            wrap = {t: wrapped(t) for t in range(1 - CONV_WIDTH, 0)}
            for rc in range(tm // gate_rows):
                r0 = rc * gate_rows
                xc = cb_ref[:, cols] + u_s[r0:r0 + gate_rows, cols] * cw_ref[CONV_WIDTH - 1:CONV_WIDTH, cols]
                for sh in range(1, CONV_WIDTH):
                    if r0 == 0:
                        head = [wrap[t - sh] for t in range(sh)]
                        ush = jnp.concatenate(head + [u_s[0:gate_rows - SEGMENTS * sh, cols]], axis=0)
                    else:
                        ush = u_s[r0 - SEGMENTS * sh:r0 - SEGMENTS * sh + gate_rows, cols]
                    xc = xc + ush * cw_ref[CONV_WIDTH - 1 - sh:CONV_WIDTH - sh, cols]
                a, b = _lru_gates(xc, c, wa_ref, wx_ref, ba_ref, bx_ref, lam_ref)
                a_s[r0:r0 + gate_rows, cols] = a
                b_s[r0:r0 + gate_rows, cols] = b

        hloc = jnp.zeros((SEGMENTS, D_RNN), F32)
        cum = jnp.ones((SEGMENTS, D_RNN), F32)
        for t in range(seg):
            rows = slice(t * SEGMENTS, (t + 1) * SEGMENTS)
            av = a_s[rows, :]
            hloc = av * hloc + b_s[rows, :]
            cum = av * cum
            b_s[rows, :] = hloc
            a_s[rows, :] = cum
        state = carry_s[...]
        enter = []
        for s in range(SEGMENTS):
            enter.append(state)
            state = hloc[s:s + 1, :] + cum[s:s + 1, :] * state
        enter = jnp.concatenate(enter, axis=0)
        carry_s[...] = state
        h_ref[...] = state
        for k in range(1, CONV_WIDTH):
            last = u_s[tm - SEGMENTS * (k - 1) - 1:tm - SEGMENTS * (k - 1), :]
            conv_ref[CONV_WIDTH - 1 - k:CONV_WIDTH - k, :] = last
            cprev_s[8 - k:9 - k, :] = last

        z = project(xnseg_s, 1)
        split = (seg, SEGMENTS, D_RNN)
        h = b_s[...].reshape(split) + a_s[...].reshape(split) * enter[None]
        hz = (h.reshape(tm, D_RNN) * (z * _sigmoid(z))).astype(BF16)
        yr = jnp.dot(hz, wro_ref[...], preferred_element_type=F32)

        mix = _sigmoid(project(xnseg_s, 2)) * yr
        for c in range(n_slab):
            xseg_s[c] = mix[:, c * LANES:(c + 1) * LANES]
        for s in range(SEGMENTS):
            for c in range(n_slab):
                mixr_ref[s * seg:(s + 1) * seg, c * LANES:(c + 1) * LANES] = \
                    xseg_s[c, pl.ds(s, seg, stride=SEGMENTS), :].astype(BF16)

        sample_heads(0)

    @pl.when(j == 1)
    def _attention_blocks():
        for dil, dst in ((d1, xn4_s), (d2, xn16_s)):
            p = tm // dil
            for r in range(dil):
                for c in range(n_slab):
                    dst[r * p:(r + 1) * p, c * LANES:(c + 1) * LANES] = \
                        xnf_s[c, pl.ds(r, p, stride=dil), :].astype(BF16)

        sga_ref[...] = _sigmoid(project(xn_s, 3)).astype(BF16)

        r = project(xn_s, 4)
        zg_ref[...] = r[:, 0:half].astype(BF16)
        emit(q0_ref, r, half, 1)

        r = project(xn_s, 5)
        emit(k0_ref, r, 0, 1)
        emit(v0_ref, r, half, 1)
        kv0_ref[...] = r[tm - ATTN_GROUPS[0][0]:tm, :]

        r = project(xn4_s, 6)
        emit(q1_ref, r, 0, d1)
        emit(k1_ref, r, half, d1)
        natural_tail(kv1_ref, 0, r, half, d1)

        r = project(xn4_s, 7, 0, half)
        emit(v1_ref, r, 0, d1)
        natural_tail(kv1_ref, half, r, 0, d1)
        emit(q2_ref, project(xn16_s, 7, half, TN), 0, d2)

        r = project(xn16_s, 8)
        emit(k2_ref, r, 0, d2)
        emit(v2_ref, r, half, d2)
        natural_tail(kv2_ref, 0, r, 0, d2)
        natural_tail(kv2_ref, half, r, half, d2)
        sample_heads(1)


def _proj_rnn_prompt2(x, w_in, gpre, cw, cb, wa, wx, ba, bx, lam, wro, qkv_s, caches, layer):
    B, S, D = x.shape
    tm = TM_PROMPT2
    n_tiles = S // tm
    N = qkv_s.shape[0]
    heads_per_step = H_G // (TILES_PER_SAMPLE * 2)
    assert S % tm == 0 and tm % (SEGMENTS * 16) == 0 and tm >= ATTN_GROUPS[0][0]
    assert all(w % tm == 0 and S >= w for w, _ in ATTN_GROUPS[1:])
    assert B * n_tiles == N * TILES_PER_SAMPLE and n_tiles % TILES_PER_SAMPLE == 0
    cache_views = [jnp.transpose(c, (0, 1, 3, 4, 5, 2)) for c in caches]

    const2 = lambda b, m, j: (0, 0)
    const3 = lambda b, m, j: (0, 0, 0)
    tile3 = lambda b, m, j: (b, m, 0)
    tile4 = lambda b, m, j: (b, 0, m, 0)
    per_seq = lambda b, m, j: (b, 0, 0)
    sample_of = lambda b, m: (b * n_tiles + m) // TILES_PER_SAMPLE
    head_block = lambda b, m, j: ((b * n_tiles + m) % TILES_PER_SAMPLE) * 2 + j

    def last_rows(window):
        first_tile = n_tiles - window // tm
        return lambda b, m, j: (b, jnp.maximum(m - first_tile, 0), 0)

    resident = dict(pipeline_mode=pl.Buffered(1))
    qkv_shapes, qkv_specs = [], []
    for _, dil in ATTN_GROUPS:
        for _ in range(3):
            qkv_shapes.append(jax.ShapeDtypeStruct((B, dil, S // dil, D_ATTN), BF16))
            qkv_specs.append(pl.BlockSpec((None, dil, tm // dil, D_ATTN), tile4))
    out_shape = tuple(qkv_shapes) + (
        jax.ShapeDtypeStruct((B, S, D_ATTN), BF16),
        jax.ShapeDtypeStruct((B, S, D), BF16),
        jax.ShapeDtypeStruct((B, S, D), BF16),
        jax.ShapeDtypeStruct((B, ATTN_GROUPS[0][0], 2 * D_ATTN), F32),
        jax.ShapeDtypeStruct((B, ATTN_GROUPS[1][0], 2 * D_ATTN), F32),
        jax.ShapeDtypeStruct((B, ATTN_GROUPS[2][0], 2 * D_ATTN), F32),
        jax.ShapeDtypeStruct((B, CONV_WIDTH - 1, D_RNN), F32),
        jax.ShapeDtypeStruct((B, 1, D_RNN), F32),
        jax.ShapeDtypeStruct((N, H_G, HEAD_DIM), F32),
    )
    out_specs = tuple(qkv_specs) + (
        pl.BlockSpec((None, tm, D_ATTN), tile3),
        pl.BlockSpec((None, tm, D), tile3),
        pl.BlockSpec((None, tm, D), tile3),
        pl.BlockSpec((None, ATTN_GROUPS[0][0], 2 * D_ATTN), per_seq),
        pl.BlockSpec((None, tm, 2 * D_ATTN), last_rows(ATTN_GROUPS[1][0])),
        pl.BlockSpec((None, tm, 2 * D_ATTN), last_rows(ATTN_GROUPS[2][0])),
        pl.BlockSpec((None, CONV_WIDTH - 1, D_RNN), per_seq),
        pl.BlockSpec((None, 1, D_RNN), per_seq),
        pl.BlockSpec((None, H_G, HEAD_DIM), lambda b, m, j: (sample_of(b, m), 0, 0)),
    )
    in_specs = [
        pl.BlockSpec((None, tm, D), tile3),
        pl.BlockSpec((D, D_IN), const2, **resident),
        pl.BlockSpec((1, D), const2),
        pl.BlockSpec((CONV_WIDTH, D_RNN), const2),
        pl.BlockSpec((1, D_RNN), const2),
        pl.BlockSpec((N_GATE_BLOCKS, GATE_BLOCK, GATE_BLOCK), const3),
        pl.BlockSpec((N_GATE_BLOCKS, GATE_BLOCK, GATE_BLOCK), const3),
        pl.BlockSpec((1, D_RNN), const2),
        pl.BlockSpec((1, D_RNN), const2),
        pl.BlockSpec((1, D_RNN), const2),
        pl.BlockSpec((D_RNN, D), const2, **resident),
        pl.BlockSpec((None, 3 * N_GROUPS, H_G, HEAD_DIM), lambda b, m, j: (sample_of(b, m), 0, 0, 0)),
    ]
    in_specs += [pl.BlockSpec((None, None, 2, heads_per_step, HEAD_DIM, window),
                              lambda b, m, j: (layer, sample_of(b, m), 0, head_block(b, m, j), 0, 0))
                 for window, _ in ATTN_GROUPS]
    scratch = [
        pltpu.VMEM((tm, D), BF16),
        pltpu.VMEM((D // LANES, tm, LANES), F32),
        pltpu.VMEM((D // LANES, tm, LANES), F32),
        pltpu.VMEM((tm, D), BF16),
        pltpu.VMEM((tm, D), BF16),
        pltpu.VMEM((tm, D), BF16),
        pltpu.VMEM((tm, D_RNN), F32),
        pltpu.VMEM((tm, D_RNN), F32),
        pltpu.VMEM((tm, D_RNN), F32),
        pltpu.VMEM((8, D_RNN), F32),
        pltpu.VMEM((1, D_RNN), F32),
    ]
    return pl.pallas_call(
        functools.partial(_proj_rnn2_kernel, tm=tm, n_tiles=n_tiles),
        grid=(B, n_tiles, 2),
        in_specs=in_specs,
        out_specs=out_specs,
        out_shape=out_shape,
        scratch_shapes=scratch,
        compiler_params=pltpu.CompilerParams(
            dimension_semantics=("arbitrary", "arbitrary", "arbitrary"), vmem_limit_bytes=VMEM_LIMIT),
        name="proj_rnn_prompt",
    )(x, w_in, gpre, cw, cb, wa, wx, ba, bx, lam, wro, qkv_s, *cache_views)


def _proj_rnn_sample_kernel(x_ref, w_ref, gpre_ref, cw_ref, cb_ref, wa_ref, wx_ref, ba_ref, bx_ref, lam_ref, wro_ref,
                            sc_ref, h0_ref,
                            att_ref, mixr_ref, sga_ref, conv_ref, h_ref,
                            xn_s, h_s, yr_s):
    j = pl.program_id(0)

    def project():
        return jnp.dot(xn_s[...], w_ref[...], preferred_element_type=F32)

    @pl.when(j == 0)
    def _u_block():
        xv = x_ref[...]
        ms = jnp.mean(xv * xv, axis=-1, keepdims=True)
        xn_s[...] = ((xv * lax.rsqrt(ms + EPS)) * gpre_ref[...]).astype(BF16)
        u = project()
        taps = [sc_ref[:, k * D_RNN:(k + 1) * D_RNN] for k in range(CONV_WIDTH - 1)] + [u]
        for c in range(N_GATE_BLOCKS):
            c0 = c * GATE_BLOCK
            xc = cb_ref[:, c0:c0 + GATE_BLOCK]
            for tap in range(CONV_WIDTH):
                xc = xc + taps[tap][:, c0:c0 + GATE_BLOCK] * cw_ref[tap:tap + 1, c0:c0 + GATE_BLOCK]
            a, b = _lru_gates(xc, c, wa_ref, wx_ref, ba_ref, bx_ref, lam_ref)
            h = a * h0_ref[:, c0:c0 + GATE_BLOCK] + b
            h_s[:, c0:c0 + GATE_BLOCK] = h
            h_ref[:, c0:c0 + GATE_BLOCK] = h
        for k in range(1, CONV_WIDTH):
            conv_ref[:, (k - 1) * D_RNN:k * D_RNN] = taps[k]

    @pl.when(j == 1)
    def _z_rnn_block():
        z = project()
        hz = (h_s[...] * (z * _sigmoid(z))).astype(BF16)
        yr_s[...] = jnp.dot(hz, wro_ref[...], preferred_element_type=F32)

    @pl.when(j == 2)
    def _g_rnn_block():
        mixr_ref[...] = _sigmoid(project()) * yr_s[...]

    @pl.when(j == 3)
    def _g_attn_block():
        sga_ref[...] = _sigmoid(project())

    @pl.when(j >= 4)
    def _attn_blocks():
        att_ref[...] = project()


def _proj_rnn_sample(x, w_in, gpre, cw, cb, wa, wx, ba, bx, lam, wro, sconv, h0):
    N, D = x.shape
    const2 = lambda j: (0, 0)
    const3 = lambda j: (0, 0, 0)
    n_att = N_COL_BLOCKS - 4
    out_shape = (
        jax.ShapeDtypeStruct((N, n_att * TN), F32),
        jax.ShapeDtypeStruct((N, D), F32),
        jax.ShapeDtypeStruct((N, D), F32),
        jax.ShapeDtypeStruct((N, (CONV_WIDTH - 1) * D_RNN), F32),
        jax.ShapeDtypeStruct((N, D_RNN), F32),
    )
    out_specs = (
        pl.BlockSpec((N, TN), lambda j: (0, jnp.maximum(j - 4, 0))),
        pl.BlockSpec((N, D), const2),
        pl.BlockSpec((N, D), const2),
        pl.BlockSpec((N, (CONV_WIDTH - 1) * D_RNN), const2),
        pl.BlockSpec((N, D_RNN), const2),
    )
    in_specs = [
        pl.BlockSpec((N, D), const2),
        pl.BlockSpec((D, TN), lambda j: (0, j)),
        pl.BlockSpec((1, D), const2),
        pl.BlockSpec((CONV_WIDTH, D_RNN), const2),
        pl.BlockSpec((1, D_RNN), const2),
        pl.BlockSpec((N_GATE_BLOCKS, GATE_BLOCK, GATE_BLOCK), const3),
        pl.BlockSpec((N_GATE_BLOCKS, GATE_BLOCK, GATE_BLOCK), const3),
        pl.BlockSpec((1, D_RNN), const2),
        pl.BlockSpec((1, D_RNN), const2),
        pl.BlockSpec((1, D_RNN), const2),
        pl.BlockSpec((D_RNN, D), const2),
        pl.BlockSpec((N, (CONV_WIDTH - 1) * D_RNN), const2),
        pl.BlockSpec((N, D_RNN), const2),
    ]
    scratch = [pltpu.VMEM((N, D), BF16), pltpu.VMEM((N, D_RNN), F32), pltpu.VMEM((N, D), F32)]
    return pl.pallas_call(
        _proj_rnn_sample_kernel,
        grid=(N_COL_BLOCKS,),
        in_specs=in_specs,
        out_specs=out_specs,
        out_shape=out_shape,
        scratch_shapes=scratch,
        compiler_params=pltpu.CompilerParams(dimension_semantics=("arbitrary",), vmem_limit_bytes=VMEM_LIMIT),
        name="proj_rnn_sample",
    )(x, w_in, gpre, cw, cb, wa, wx, ba, bx, lam, wro, sconv, h0)


def _attn_kernel(q_ref, kc_ref, vc_ref, kp_ref, vp_ref, o_ref, st_ref, bias_s, *, group, chunk):
    dilation = ATTN_GROUPS[group][1]
    c = pl.program_id(2)
    nt = (((1,), (1,)), ((), ()))

    @pl.when((pl.program_id(0) == 0) & (pl.program_id(1) == 0) & (c == 0))
    def _init_bias():
        qi = lax.broadcasted_iota(jnp.int32, (Q_BLOCK, Q_BLOCK), 0)
        kj = lax.broadcasted_iota(jnp.int32, (Q_BLOCK, Q_BLOCK), 1)
        steps_prev = Q_BLOCK + qi - kj
        steps_cur = qi - kj
        dist_prev = (steps_prev * dilation).astype(F32)
        dist_cur = (steps_cur * dilation).astype(F32)
        for h in range(H_G):
            slope = float(_SLOPES[group, h])
            bias_s[h] = jnp.where(steps_cur >= 0, -slope * dist_cur, NEG_INF)
            bias_s[H_G + h] = jnp.where(steps_prev <= Q_BLOCK, -slope * dist_prev, NEG_INF)
            bias_s[2 * H_G + h] = jnp.full((Q_BLOCK, Q_BLOCK), NEG_INF, F32)

    lane = lax.broadcasted_iota(jnp.int32, (Q_BLOCK, LANES), 1)
    low = lane < HEAD_DIM
    lane2 = lax.broadcasted_iota(jnp.int32, (2 * Q_BLOCK, LANES), 1)
    low2 = lane2 < HEAD_DIM
    ones_lo = jnp.where(low2, 1.0, 0.0).astype(BF16)
    ones_hi = jnp.where(low2, 0.0, 1.0).astype(BF16)
    stat_lane = lax.broadcasted_iota(jnp.int32, (Q_BLOCK, STAT_LANES), 1)

    def keys_of(block, cur_ref, prev_ref, sl):
        if block == 0:
            return jnp.concatenate([prev_ref[:, sl], cur_ref[0:Q_BLOCK, sl]], axis=0)
        return cur_ref[(block - 1) * Q_BLOCK:(block + 1) * Q_BLOCK, sl]

    for i in range(chunk // Q_BLOCK):
        r0 = i * Q_BLOCK
        first = jnp.where(c == 0, H_G, 0) if i == 0 else 0
        stats = jnp.zeros((Q_BLOCK, STAT_LANES), F32)
        for p in range(H_G // 2):
            sl = slice(LANES * p, LANES * (p + 1))
            qp = q_ref[r0:r0 + Q_BLOCK, sl] * (HEAD_DIM ** -0.5)
            kp = keys_of(i, kc_ref, kp_ref, sl)
            vp = keys_of(i, vc_ref, vp_ref, sl)
            es, ms = [], []
            for hh in range(2):
                h = 2 * p + hh
                msk = low if hh == 0 else jnp.logical_not(low)
                qm = jnp.where(msk, qp, jnp.zeros_like(qp))
                s = lax.dot_general(qm, kp, nt, preferred_element_type=F32)
                s_p = s[:, 0:Q_BLOCK] + bias_s[H_G + h + first]
                s_c = s[:, Q_BLOCK:2 * Q_BLOCK] + bias_s[h]
                mx = jnp.maximum(jnp.max(s_p, axis=-1, keepdims=True), jnp.max(s_c, axis=-1, keepdims=True))
                es.append(jnp.exp(s_p - mx).astype(BF16))
                es.append(jnp.exp(s_c - mx).astype(BF16))
                ms.append(mx)
            vm0 = jnp.where(low2, vp, jnp.zeros_like(vp))
            vm1 = jnp.where(low2, jnp.zeros_like(vp), vp)
            w = jnp.concatenate([jnp.concatenate([vm0, ones_lo], axis=1),
                                 jnp.concatenate([vm1, ones_hi], axis=1)], axis=0)
            acc = jnp.dot(jnp.concatenate(es, axis=1), w, preferred_element_type=F32)
            l_pair = acc[:, LANES:2 * LANES]
            o_ref[r0:r0 + Q_BLOCK, sl] = (acc[:, 0:LANES] / l_pair).astype(o_ref.dtype)
            lse_pair = jnp.where(low, ms[0], ms[1]) + jnp.log(l_pair)
            keep = ((stat_lane % (STAT_LANES // 2)) // STAT_LANES_PER_HEAD) == p
            stats = jnp.where(keep, lse_pair, stats)
        st_ref[r0:r0 + Q_BLOCK, :] = stats


def _attn_group(q, k, v, group):
    B, dil, L, _ = q.shape
    chunk = min(ATTN_CHUNK, L)
    assert L % chunk == 0 and chunk % Q_BLOCK == 0
    cur = lambda b, r, c: (b, r, c, 0)
    prev = lambda b, r, c: (b, r, jnp.maximum(c * (chunk // Q_BLOCK) - 1, 0), 0)
    blk = (None, None, chunk, D_ATTN)
    pblk = (None, None, Q_BLOCK, D_ATTN)
    return pl.pallas_call(
        functools.partial(_attn_kernel, group=group, chunk=chunk),
        grid=(B, dil, L // chunk),
        in_specs=[pl.BlockSpec(blk, cur), pl.BlockSpec(blk, cur), pl.BlockSpec(blk, cur),
                  pl.BlockSpec(pblk, prev), pl.BlockSpec(pblk, prev)],
        out_specs=[pl.BlockSpec(blk, cur), pl.BlockSpec((None, None, chunk, STAT_LANES), cur)],
        out_shape=[jax.ShapeDtypeStruct((B, dil, L, D_ATTN), BF16),
                   jax.ShapeDtypeStruct((B, dil, L, STAT_LANES), F32)],
        scratch_shapes=[pltpu.VMEM((3 * H_G, Q_BLOCK, Q_BLOCK), F32)],
        compiler_params=pltpu.CompilerParams(
            dimension_semantics=("arbitrary", "arbitrary", "arbitrary"), vmem_limit_bytes=VMEM_LIMIT),
        name=f"attn_group{group}",
    )(q, k, v, k, v)


def _finish(att, rows, z_ref, sga_ref, mixr_ref, x_ref, wao_ref, wo_ref, gpost_ref, y_ref):
    z = z_ref[rows, :].astype(F32)
    ya_in = (att * (z * _sigmoid(z))).astype(BF16)
    ya = jnp.dot(ya_in, wao_ref[...], preferred_element_type=F32)
    mixed = mixr_ref[rows, :].astype(F32) + sga_ref[rows, :].astype(F32) * ya
    out = jnp.dot(mixed.astype(BF16), wo_ref[...], preferred_element_type=F32)
    ms = jnp.mean(out * out, axis=-1, keepdims=True)
    y_ref[rows, :] = x_ref[rows, :] + (out * lax.rsqrt(ms + EPS)) * gpost_ref[...]


def _out_kernel(att_ref, z_ref, sga_ref, mixr_ref, x_ref, wao_ref, wo_ref, gpost_ref, y_ref):
    rows = slice(0, x_ref.shape[0])
    _finish(att_ref[...].astype(F32), rows, z_ref, sga_ref, mixr_ref, x_ref, wao_ref, wo_ref, gpost_ref, y_ref)


def _merge_out_kernel(o0_ref, o1_ref, o2_ref, s0_ref, s1_ref, s2_ref, z_ref, sga_ref, mixr_ref, x_ref,
                      wao_ref, wo_ref, gpost_ref, y_ref, nat1_s, nat2_s, st1_s, st2_s, *, tm):
    for o_ref, s_ref, nat_s, stn_s, dil in ((o1_ref, s1_ref, nat1_s, st1_s, ATTN_GROUPS[1][1]),
                                            (o2_ref, s2_ref, nat2_s, st2_s, ATTN_GROUPS[2][1])):
        p = tm // dil
        for r in range(dil):
            stn_s[pl.ds(r, p, stride=dil), :] = s_ref[r]
            for c in range(D_ATTN // LANES):
                nat_s[c, pl.ds(r, p, stride=dil), :] = o_ref[r, :, c * LANES:(c + 1) * LANES].astype(F32)

    src = lax.broadcasted_iota(jnp.int32, (STAT_LANES, D_ATTN), 0)
    dst_head = lax.broadcasted_iota(jnp.int32, (STAT_LANES, D_ATTN), 1) // HEAD_DIM
    expand = jnp.where(src == (dst_head % 2) * (STAT_LANES // 2) + (dst_head // 2) * STAT_LANES_PER_HEAD,
                       1.0, 0.0).astype(BF16)
    for k in range(tm // OUT_ROWS):
        rows = slice(k * OUT_ROWS, (k + 1) * OUT_ROWS)
        lse = [s0_ref[rows, :], st1_s[rows, :], st2_s[rows, :]]
        mm = jnp.maximum(jnp.maximum(lse[0], lse[1]), lse[2])
        ws = [jnp.exp(x - mm) for x in lse]
        den = ws[0] + ws[1] + ws[2]
        outs = [o0_ref[rows, :].astype(F32),
                jnp.concatenate([nat1_s[c, rows, :] for c in range(D_ATTN // LANES)], axis=1),
                jnp.concatenate([nat2_s[c, rows, :] for c in range(D_ATTN // LANES)], axis=1)]
        att = jnp.zeros((OUT_ROWS, D_ATTN), F32)
        for g in range(N_GROUPS):
            wexp = jnp.dot((ws[g] / den).astype(BF16), expand, preferred_element_type=F32)
            att = att + wexp * outs[g]
        _finish(att, rows, z_ref, sga_ref, mixr_ref, x_ref, wao_ref, wo_ref, gpost_ref, y_ref)


def _merge_out_prompt(os, sts, zg, sga, mixr, x, wao, wo, gpost):
    B, S, D = x.shape
    tm = TM_OUT
    tile3 = lambda b, m: (b, m, 0)
    tile4 = lambda b, m: (b, 0, m, 0)
    const = lambda b, m: (0, 0)
    in_specs = []
    for (_, dil), width in [(g, D_ATTN) for g in ATTN_GROUPS] + [(g, STAT_LANES) for g in ATTN_GROUPS]:
        if dil == 1:
            in_specs.append(pl.BlockSpec((None, None, tm, width), tile4))
        else:
            in_specs.append(pl.BlockSpec((None, dil, tm // dil, width), tile4))
    in_specs += [
        pl.BlockSpec((None, tm, D_ATTN), tile3),
        pl.BlockSpec((None, tm, D), tile3),
        pl.BlockSpec((None, tm, D), tile3),
        pl.BlockSpec((None, tm, D), tile3),
        pl.BlockSpec((D_ATTN, D), const),
        pl.BlockSpec((D, D), const),
        pl.BlockSpec((1, D), const),
    ]
    scratch = [
        pltpu.VMEM((D_ATTN // LANES, tm, LANES), F32),
        pltpu.VMEM((D_ATTN // LANES, tm, LANES), F32),
        pltpu.VMEM((tm, STAT_LANES), F32),
        pltpu.VMEM((tm, STAT_LANES), F32),
    ]
    return pl.pallas_call(
        functools.partial(_merge_out_kernel, tm=tm),
        grid=(B, S // tm),
        in_specs=in_specs,
        out_specs=pl.BlockSpec((None, tm, D), tile3),
        out_shape=jax.ShapeDtypeStruct((B, S, D), F32),
        scratch_shapes=scratch,
        compiler_params=pltpu.CompilerParams(
            dimension_semantics=("arbitrary", "arbitrary"), vmem_limit_bytes=VMEM_LIMIT),
        name="merge_out_prompt",
    )(*os, *sts, zg, sga, mixr, x, wao, wo, gpost)


def _out_sample(att, z, sga, mixr, x, wao, wo, gpost):
    N, D = x.shape
    full = lambda shape: pl.BlockSpec(shape, lambda i: (0, 0))
    return pl.pallas_call(
        _out_kernel,
        grid=(1,),
        in_specs=[full((N, D_ATTN)), full((N, D_ATTN)), full((N, D)), full((N, D)), full((N, D)),
                  full((D_ATTN, D)), full((D, D)), full((1, D))],
        out_specs=full((N, D)),
        out_shape=jax.ShapeDtypeStruct((N, D), F32),
        compiler_params=pltpu.CompilerParams(dimension_semantics=("arbitrary",), vmem_limit_bytes=VMEM_LIMIT),
        name="out_sample",
    )(att, z, sga, mixr, x, wao, wo, gpost)


def _block_diag_chunks(w):
    per = GATE_BLOCK // RNN_BLOCK
    w = w.reshape(N_GATE_BLOCKS, per, RNN_BLOCK, RNN_BLOCK)
    eye = jnp.eye(per, dtype=w.dtype)
    dense = w[:, :, :, None, :] * eye[None, :, None, :, None]
    return dense.reshape(N_GATE_BLOCKS, GATE_BLOCK, GATE_BLOCK)


def _reorder_w_in(w_in):
    sizes = (D_RNN, D_RNN, N_GROUPS * D_ATTN, N_GROUPS * D_ATTN, N_GROUPS * D_ATTN, D_ATTN, D_MODEL, D_MODEL)
    starts = np.concatenate([[0], np.cumsum(sizes)[:-1]])
    u0, z0, q0, k0, v0, za0, gr0, ga0 = (int(s) for s in starts)
    pieces = [(u0, D_RNN), (z0, D_RNN), (gr0, D_MODEL), (ga0, D_MODEL), (za0, D_ATTN)]
    for g in range(N_GROUPS):
        pieces += [(q0 + g * D_ATTN, D_ATTN), (k0 + g * D_ATTN, D_ATTN), (v0 + g * D_ATTN, D_ATTN)]
    return jnp.concatenate([w_in[:, s:s + n] for s, n in pieces], axis=1).astype(BF16)


def _layer(layer, yp, ys, sconv, h0, caches, norm_pre, norm_post, w_in, conv_w, conv_b, lru_w_a, lru_b_a, lru_w_x,
           lru_b_x, lru_lambda, w_rnn_out, w_attn_out, w_out):
    B, S, D = yp.shape
    N = ys.shape[0]
    row = lambda v: v.reshape(1, -1)
    w_in_b = _reorder_w_in(w_in)
    wa = _block_diag_chunks(lru_w_a).astype(BF16)
    wx = _block_diag_chunks(lru_w_x).astype(BF16)
    wro = w_rnn_out.astype(BF16)
    wao = w_attn_out.astype(BF16)
    wo = w_out.astype(BF16)
    shared = (w_in_b, row(norm_pre), conv_w, row(conv_b), wa, wx, row(lru_b_a), row(lru_b_x), row(lru_lambda), wro)

    xs = ys.reshape(N, D)
    att_in, mixr_s, sga_s, conv_s, h_s = _proj_rnn_sample(
        xs, *shared, sconv.reshape(N, (CONV_WIDTH - 1) * D_RNN), h0)
    z_s = att_in[:, 0:D_ATTN]
    qkv_s = att_in[:, D_ATTN:].reshape(N, 3 * N_GROUPS, H_G, HEAD_DIM)

    outs = _proj_rnn_prompt2(yp, *shared, qkv_s, caches, layer)
    qkv, (zg, mixr, sga, kv0, kv1, kv2, conv_p, h_p, att_s) = outs[:9], outs[9:]
    os, sts = [], []
    for g in range(N_GROUPS):
        o, st = _attn_group(qkv[3 * g], qkv[3 * g + 1], qkv[3 * g + 2], g)
        os.append(o)
        sts.append(st)
    y_p = _merge_out_prompt(os, sts, zg, sga, mixr, yp, wao, wo, row(norm_post))
    kv_p = [kv.reshape(B, kv.shape[1], 2, H_G, HEAD_DIM) for kv in (kv0, kv1, kv2)]

    y_s = _out_sample(att_s.reshape(N, D_ATTN), z_s, sga_s, mixr_s, xs, wao, wo, row(norm_post))
    kv_s = [qkv_s[:, 3 * g + 1:3 * g + 3].reshape(N, 1, 2, H_G, HEAD_DIM) for g in range(N_GROUPS)]

    return (y_p, y_s.reshape(N, 1, D), conv_p, conv_s.reshape(N, CONV_WIDTH - 1, D_RNN),
            h_p.reshape(B, D_RNN), h_s, kv_p, kv_s)


def kernel(x_prompt, x_sample, state_conv, state_h, cache_kv_w128, cache_kv_w512, cache_kv_w2048, norm_pre, norm_post, w_in, conv_w, conv_b, lru_w_a, lru_b_a, lru_w_x, lru_b_x, lru_lambda, w_rnn_out, w_attn_out, w_out):
    depth = norm_pre.shape[0]
    caches = (cache_kv_w128, cache_kv_w512, cache_kv_w2048)
    yp, ys = x_prompt, x_sample
    conv_p, conv_s, h_p, h_s = [], [], [], []
    kvp = ([], [], [])
    kvs = ([], [], [])
    for l in range(depth):
        yp, ys, cp, cs, hp, hs, kv_p, kv_s = _layer(
            l, yp, ys, state_conv[l], state_h[l], caches,
            norm_pre[l], norm_post[l], w_in[l], conv_w[l], conv_b[l], lru_w_a[l], lru_b_a[l], lru_w_x[l],
            lru_b_x[l], lru_lambda[l], w_rnn_out[l], w_attn_out[l], w_out[l])
        conv_p.append(cp)
        conv_s.append(cs)
        h_p.append(hp)
        h_s.append(hs)
        for g in range(N_GROUPS):
            kvp[g].append(kv_p[g])
            kvs[g].append(kv_s[g])
    return (yp, ys, jnp.stack(conv_p), jnp.stack(conv_s), jnp.stack(h_p), jnp.stack(h_s),
            jnp.stack(kvp[0]), jnp.stack(kvs[0]), jnp.stack(kvp[1]), jnp.stack(kvs[1]),
            jnp.stack(kvp[2]), jnp.stack(kvs[2]))
```

```python
import functools

import numpy as np
import jax
import jax.numpy as jnp
from jax import lax
from jax.experimental import pallas as pl
from jax.experimental.pallas import tpu as pltpu

F32 = jnp.float32
BF16 = jnp.bfloat16

D_MODEL = 1024
D_RNN = 1024
N_RNN_BLOCKS = 16
RNN_BLOCK = D_RNN // N_RNN_BLOCKS
CONV_WIDTH = 4
LRU_C = 8.0
HEAD_DIM = 64
H_G = 8
ATTN_GROUPS = ((128, 1), (512, 4), (2048, 16))
N_GROUPS = 3
D_ATTN = H_G * HEAD_DIM
Q_BLOCK = 128
ALIBI_MAX = 8.0
EPS = 1e-6
NEG_INF = -1e30
D_IN = 2 * D_RNN + 3 * N_GROUPS * D_ATTN + D_ATTN + 2 * D_MODEL

LANES = 128
TN = 1024
N_COL_BLOCKS = D_IN // TN
COL_U = 0
COL_Z_RNN = COL_U + D_RNN
COL_Q = COL_Z_RNN + D_RNN
COL_K = COL_Q + N_GROUPS * D_ATTN
COL_V = COL_K + N_GROUPS * D_ATTN
COL_Z_ATTN = COL_V + N_GROUPS * D_ATTN
COL_G_RNN = COL_Z_ATTN + D_ATTN
COL_G_ATTN = COL_G_RNN + D_MODEL
GATE_BLOCK = 256
N_GATE_BLOCKS = D_RNN // GATE_BLOCK
STAT_LANES = LANES
STAT_LANES_PER_HEAD = STAT_LANES // H_G

TM_PROMPT = 512
TM_PROMPT2 = 256
TILES_PER_SAMPLE = 2
SEGMENTS = 8
TM_OUT = 512
OUT_ROWS = TM_OUT
ATTN_CHUNK = 512
ROW_CHUNK = 64
GATE_ROWS = 128
VMEM_LIMIT = 56 * 1024 * 1024


def _alibi_slopes():
    n = N_GROUPS * H_G
    s = np.float32(2.0) ** (np.float32(-ALIBI_MAX) * np.arange(1, n + 1, dtype=np.float32) / np.float32(n))
    return s.reshape(N_GROUPS, H_G)


_SLOPES = _alibi_slopes()


def _stat_lane(h):
    return (h % 2) * (STAT_LANES // 2) + (h // 2) * STAT_LANES_PER_HEAD


def _softplus(y):
    return jnp.maximum(y, 0.0) + jnp.log1p(jnp.exp(-jnp.abs(y)))


def _sigmoid(x):
    return 0.5 * jnp.tanh(0.5 * x) + 0.5


def _for_rows(n_rows, chunk, fn):
    if n_rows <= chunk:
        fn(0)
        return

    def body(c, carry):
        fn(pl.multiple_of(c * chunk, chunk))
        return carry

    lax.fori_loop(0, n_rows // chunk, body, 0)


def _lru_gates(xc, c, wa_ref, wx_ref, ba_ref, bx_ref, lam_ref):
    c0 = c * GATE_BLOCK
    xcb = xc.astype(BF16)
    r = _sigmoid(jnp.dot(xcb, wa_ref[c], preferred_element_type=F32) + ba_ref[:, c0:c0 + GATE_BLOCK])
    i = _sigmoid(jnp.dot(xcb, wx_ref[c], preferred_element_type=F32) + bx_ref[:, c0:c0 + GATE_BLOCK])
    log_a = (-LRU_C * r) * _softplus(-lam_ref[:, c0:c0 + GATE_BLOCK])
    a = jnp.exp(log_a)
    v = 1.0 - a * a
    b = jnp.where(v > 0.0, v * lax.rsqrt(v), 0.0) * i * xc
    return a, b


def _sample_head_attention(h, qs_ref, cache_refs, att_ref):
    scale = HEAD_DIM ** -0.5
    ri = lax.broadcasted_iota(jnp.int32, (HEAD_DIM, HEAD_DIM), 0)
    ci = lax.broadcasted_iota(jnp.int32, (HEAD_DIM, HEAD_DIM), 1)
    eye = ri == ci

    def to_col(row):
        return jnp.sum(jnp.where(eye, jnp.broadcast_to(row, (HEAD_DIM, HEAD_DIM)), 0.0), axis=-1, keepdims=True)

    def to_row(col):
        return jnp.sum(jnp.where(eye, jnp.broadcast_to(col, (HEAD_DIM, HEAD_DIM)), 0.0), axis=0, keepdims=True)

    outs, lses = [], []
    for g, (window, dilation) in enumerate(ATTN_GROUPS):
        c_ref = cache_refs[g]
        pos = lax.broadcasted_iota(jnp.int32, (1, window), 1)
        dist = (window - pos).astype(F32)
        bias = jnp.where((pos % dilation) == 0, -float(_SLOPES[g, h]) * dist, NEG_INF)
        q = qs_ref[3 * g, h:h + 1, :] * scale
        s_new = jnp.sum(qs_ref[3 * g + 1, h:h + 1, :] * q, axis=-1, keepdims=True)
        v_col = to_col(qs_ref[3 * g + 2, h:h + 1, :])
        s = jnp.sum(c_ref[0] * to_col(q), axis=0, keepdims=True) + bias
        mx = jnp.maximum(jnp.max(s, axis=-1, keepdims=True), s_new)
        e = jnp.exp(s - mx)
        e_new = jnp.exp(s_new - mx)
        l = jnp.sum(e, axis=-1, keepdims=True) + e_new
        o = jnp.sum(c_ref[1] * e, axis=-1, keepdims=True) + e_new * v_col
        outs.append(o / l)
        lses.append(mx + jnp.log(l))
    mm = jnp.maximum(jnp.maximum(lses[0], lses[1]), lses[2])
    ws = [jnp.exp(x - mm) for x in lses]
    col = (outs[0] * ws[0] + outs[1] * ws[1] + outs[2] * ws[2]) / (ws[0] + ws[1] + ws[2])
    att_ref[h:h + 1, :] = to_row(col)


def _proj_rnn_kernel(x_ref, w_ref, gpre_ref, cw_ref, cb_ref, wa_ref, wx_ref, ba_ref, bx_ref, lam_ref, wro_ref,
                     qs_ref, c0_ref, c1_ref, c2_ref,
                     q0_ref, k0_ref, v0_ref, q1_ref, k1_ref, v1_ref, q2_ref, k2_ref, v2_ref,
                     zg_ref, mixr_ref, sga_ref, kv0_ref, kv1_ref, kv2_ref, conv_ref, h_ref, att_ref,
                     xn_s, xnf_s, xseg_s, xnseg_s, xn4_s, xn16_s, u_s, a_s, b_s, yr_s,
                     cprev_s, cin_s, carry_s,
                     *, tm, n_tiles):
    m = pl.program_id(1)
    j = pl.program_id(2)
    acc_s = u_s

    def sample_head(h):
        _sample_head_attention(h, qs_ref, (c0_ref, c1_ref, c2_ref), att_ref)
    last_tile = m == n_tiles - 1
    in_kv2 = m >= n_tiles - ATTN_GROUPS[2][0] // tm
    half = TN // 2
    d1, d2 = ATTN_GROUPS[1][1], ATTN_GROUPS[2][1]

    def project(lhs_s):
        return jnp.dot(lhs_s[...], w_ref[...], preferred_element_type=F32)

    def emit(out_ref, c0, dilation):
        p = tm // dilation
        for r in range(dilation):
            out_ref[r] = acc_s[r * p:(r + 1) * p, c0:c0 + half].astype(BF16)

    def natural_tail(out_ref, out_c0, c0, dilation):
        p = tm // dilation
        for c in range(half // LANES):
            for r in range(dilation):
                xnf_s[c, pl.ds(r, p, stride=dilation), :] = \
                    acc_s[r * p:(r + 1) * p, c0 + c * LANES:c0 + (c + 1) * LANES]
            out_ref[:, out_c0 + c * LANES:out_c0 + (c + 1) * LANES] = xnf_s[c]

    seg = tm // SEGMENTS
    n_slab = D_MODEL // LANES

    @pl.when(j == 0)
    def _u_block():
        def norm(s, carry):
            r0 = pl.multiple_of(s * seg, seg)
            xv = x_ref[pl.ds(r0, seg), :]
            ms = jnp.mean(xv * xv, axis=-1, keepdims=True)
            xn = (xv * lax.rsqrt(ms + EPS)) * gpre_ref[...]
            xn_s[pl.ds(r0, seg), :] = xn.astype(BF16)
            for c in range(n_slab):
                piece = xn[:, c * LANES:(c + 1) * LANES]
                xnf_s[c, pl.ds(r0, seg), :] = piece
                xseg_s[c, pl.ds(s, seg, stride=SEGMENTS), :] = piece
            return carry

        lax.fori_loop(0, SEGMENTS, norm, 0)

        for c in range(n_slab):
            xnseg_s[:, c * LANES:(c + 1) * LANES] = xseg_s[c].astype(BF16)

        for dil, dst in ((d1, xn4_s), (d2, xn16_s)):
            p = tm // dil
            for r in range(dil):
                for c in range(n_slab):
                    dst[r * p:(r + 1) * p, c * LANES:(c + 1) * LANES] = \
                        xnf_s[c, pl.ds(r, p, stride=dil), :].astype(BF16)

        @pl.when(m == 0)
        def _reset():
            cprev_s[...] = jnp.zeros((8, D_RNN), F32)
            carry_s[...] = jnp.zeros((1, D_RNN), F32)

        u_s[...] = project(xnseg_s)

        sub = lax.broadcasted_iota(jnp.int32, (SEGMENTS, GATE_BLOCK), 0)
        for c in range(N_GATE_BLOCKS):
            cols = slice(c * GATE_BLOCK, (c + 1) * GATE_BLOCK)

            def wrapped(t):
                v = pltpu.roll(u_s[SEGMENTS * (seg + t):SEGMENTS * (seg + t + 1), cols], 1, axis=0)
                return jnp.where(sub == 0, cprev_s[8 + t:9 + t, cols], v)

            wrap = {t: wrapped(t) for t in range(1 - CONV_WIDTH, 0)}
            for rc in range(tm // GATE_ROWS):
                r0 = rc * GATE_ROWS
                xc = cb_ref[:, cols] + u_s[r0:r0 + GATE_ROWS, cols] * cw_ref[CONV_WIDTH - 1:CONV_WIDTH, cols]
                for sh in range(1, CONV_WIDTH):
                    if r0 == 0:
                        head = [wrap[t - sh] for t in range(sh)]
                        ush = jnp.concatenate(head + [u_s[0:GATE_ROWS - SEGMENTS * sh, cols]], axis=0)
                    else:
                        ush = u_s[r0 - SEGMENTS * sh:r0 - SEGMENTS * sh + GATE_ROWS, cols]
                    xc = xc + ush * cw_ref[CONV_WIDTH - 1 - sh:CONV_WIDTH - sh, cols]
                a, b = _lru_gates(xc, c, wa_ref, wx_ref, ba_ref, bx_ref, lam_ref)
                a_s[r0:r0 + GATE_ROWS, cols] = a
                b_s[r0:r0 + GATE_ROWS, cols] = b

        def scan(t, carry):
            hloc, cum = carry
            r0 = pl.multiple_of(t * SEGMENTS, SEGMENTS)
            av = a_s[pl.ds(r0, SEGMENTS), :]
            hloc = av * hloc + b_s[pl.ds(r0, SEGMENTS), :]
            cum = av * cum
            b_s[pl.ds(r0, SEGMENTS), :] = hloc
            a_s[pl.ds(r0, SEGMENTS), :] = cum
            return hloc, cum

        h_end, a_end = lax.fori_loop(
            0, seg, scan, (jnp.zeros((SEGMENTS, D_RNN), F32), jnp.ones((SEGMENTS, D_RNN), F32)), unroll=4)
        state = carry_s[...]
        for s in range(SEGMENTS):
            cin_s[s:s + 1, :] = state
            state = h_end[s:s + 1, :] + a_end[s:s + 1, :] * state
        carry_s[...] = state
        h_ref[...] = state
        for k in range(1, CONV_WIDTH):
            last = u_s[tm - SEGMENTS * (k - 1) - 1:tm - SEGMENTS * (k - 1), :]
            conv_ref[CONV_WIDTH - 1 - k:CONV_WIDTH - k, :] = last
            cprev_s[8 - k:9 - k, :] = last

    @pl.when(j == 1)
    def _z_rnn_block():
        sample_head(0)
        z = project(xnseg_s)
        split = (seg, SEGMENTS, D_RNN)
        h = b_s[...].reshape(split) + a_s[...].reshape(split) * cin_s[...][None]
        hz = (h.reshape(tm, D_RNN) * (z * _sigmoid(z))).astype(BF16)
        yr_s[...] = jnp.dot(hz, wro_ref[...], preferred_element_type=F32)

    @pl.when(j == 2)
    def _g_rnn_block():
        sample_head(1)
        mix = _sigmoid(project(xnseg_s)) * yr_s[...]
        for c in range(n_slab):
            xseg_s[c] = mix[:, c * LANES:(c + 1) * LANES]
        for s in range(SEGMENTS):
            for c in range(n_slab):
                mixr_ref[s * seg:(s + 1) * seg, c * LANES:(c + 1) * LANES] = \
                    xseg_s[c, pl.ds(s, seg, stride=SEGMENTS), :].astype(BF16)

    @pl.when(j == 3)
    def _g_attn_block():
        sample_head(2)
        sga_ref[...] = _sigmoid(project(xn_s)).astype(BF16)

    @pl.when(j == 4)
    def _zattn_q0_block():
        sample_head(3)
        acc_s[...] = project(xn_s)
        zg_ref[...] = acc_s[:, 0:half].astype(BF16)
        emit(q0_ref, half, 1)

    @pl.when(j == 5)
    def _k0_v0_block():
        sample_head(4)
        acc_s[...] = project(xn_s)
        emit(k0_ref, 0, 1)
        emit(v0_ref, half, 1)

        @pl.when(last_tile)
        def _():
            kv0_ref[...] = acc_s[tm - ATTN_GROUPS[0][0]:tm, :]

    @pl.when(j == 6)
    def _q1_k1_block():
        sample_head(5)
        acc_s[...] = project(xn4_s)
        emit(q1_ref, 0, d1)
        emit(k1_ref, half, d1)

        @pl.when(last_tile)
        def _():
            natural_tail(kv1_ref, 0, half, d1)

    @pl.when(j == 7)
    def _v1_q2_block():
        sample_head(6)
        acc_s[:, 0:half] =jnp.dot(xn4_s[...], w_ref[:, 0:half], preferred_element_type=F32)
        acc_s[:, half:TN] = jnp.dot(xn16_s[...], w_ref[:, half:TN], preferred_element_type=F32)
        emit(v1_ref, 0, d1)
        emit(q2_ref, half, d2)

        @pl.when(last_tile)
        def _():
            natural_tail(kv1_ref, half, 0, d1)

    @pl.when(j == 8)
    def _k2_v2_block():
        sample_head(7)
        acc_s[...] = project(xn16_s)
        emit(k2_ref, 0, d2)
        emit(v2_ref, half, d2)

        @pl.when(in_kv2)
        def _():
            natural_tail(kv2_ref, 0, 0, d2)
            natural_tail(kv2_ref, half, half, d2)


def _proj_rnn_prompt(x, w_in, gpre, cw, cb, wa, wx, ba, bx, lam, wro, qkv_s, caches, layer):
    B, S, D = x.shape
    tm = TM_PROMPT
    n_tiles = S // tm
    m0 = n_tiles - ATTN_GROUPS[2][0] // tm
    assert S % tm == 0 and tm == ATTN_GROUPS[1][0] and ATTN_GROUPS[2][0] % tm == 0 and S >= ATTN_GROUPS[2][0]
    N = qkv_s.shape[0]
    assert N == B * n_tiles and N_COL_BLOCKS == H_G + 1
    cache_views = [jnp.transpose(c, (0, 1, 3, 4, 5, 2)) for c in caches]
    sample_seq = lambda b, m, j: (b * n_tiles + m, 0, 0, 0)
    cache_head = lambda b, m, j: (layer, b * n_tiles + m, 0, jnp.clip(j - 1, 0, H_G - 1), 0, 0)

    const2 = lambda b, m, j: (0, 0)
    const3 = lambda b, m, j: (0, 0, 0)
    tile3 = lambda b, m, j: (b, m, 0)
    tile4 = lambda b, m, j: (b, 0, m, 0)
    per_seq = lambda b, m, j: (b, 0, 0)

    qkv_shapes, qkv_specs = [], []
    for _, dil in ATTN_GROUPS:
        for _ in range(3):
            qkv_shapes.append(jax.ShapeDtypeStruct((B, dil, S // dil, D_ATTN), BF16))
            qkv_specs.append(pl.BlockSpec((None, dil, tm // dil, D_ATTN), tile4))
    out_shape = tuple(qkv_shapes) + (
        jax.ShapeDtypeStruct((B, S, D_ATTN), BF16),
        jax.ShapeDtypeStruct((B, S, D), BF16),
        jax.ShapeDtypeStruct((B, S, D), BF16),
        jax.ShapeDtypeStruct((B, ATTN_GROUPS[0][0], 2 * D_ATTN), F32),
        jax.ShapeDtypeStruct((B, ATTN_GROUPS[1][0], 2 * D_ATTN), F32),
        jax.ShapeDtypeStruct((B, ATTN_GROUPS[2][0], 2 * D_ATTN), F32),
        jax.ShapeDtypeStruct((B, CONV_WIDTH - 1, D_RNN), F32),
        jax.ShapeDtypeStruct((B, 1, D_RNN), F32),
        jax.ShapeDtypeStruct((N, H_G, HEAD_DIM), F32),
    )
    out_specs = tuple(qkv_specs) + (
        pl.BlockSpec((None, tm, D_ATTN), tile3),
        pl.BlockSpec((None, tm, D), tile3),
        pl.BlockSpec((None, tm, D), tile3),
        pl.BlockSpec((None, ATTN_GROUPS[0][0], 2 * D_ATTN), per_seq),
        pl.BlockSpec((None, ATTN_GROUPS[1][0], 2 * D_ATTN), per_seq),
        pl.BlockSpec((None, tm, 2 * D_ATTN), lambda b, m, j: (b, jnp.maximum(m - m0, 0), 0)),
        pl.BlockSpec((None, CONV_WIDTH - 1, D_RNN), per_seq),
        pl.BlockSpec((None, 1, D_RNN), per_seq),
        pl.BlockSpec((None, H_G, HEAD_DIM), lambda b, m, j: (b * n_tiles + m, 0, 0)),
    )
    in_specs = [
        pl.BlockSpec((None, tm, D), tile3),
        pl.BlockSpec((D, TN), lambda b, m, j: (0, j)),
        pl.BlockSpec((1, D), const2),
        pl.BlockSpec((CONV_WIDTH, D_RNN), const2),
        pl.BlockSpec((1, D_RNN), const2),
        pl.BlockSpec((N_GATE_BLOCKS, GATE_BLOCK, GATE_BLOCK), const3),
        pl.BlockSpec((N_GATE_BLOCKS, GATE_BLOCK, GATE_BLOCK), const3),
        pl.BlockSpec((1, D_RNN), const2),
        pl.BlockSpec((1, D_RNN), const2),
        pl.BlockSpec((1, D_RNN), const2),
        pl.BlockSpec((D_RNN, D), const2),
        pl.BlockSpec((None, 3 * N_GROUPS, H_G, HEAD_DIM), sample_seq),
    ]
    in_specs += [pl.BlockSpec((None, None, 2, None, HEAD_DIM, window), cache_head) for window, _ in ATTN_GROUPS]
    scratch = [
        pltpu.VMEM((tm, D), BF16),
        pltpu.VMEM((D // LANES, tm, LANES), F32),
        pltpu.VMEM((D // LANES, tm, LANES), F32),
        pltpu.VMEM((tm, D), BF16),
        pltpu.VMEM((tm, D), BF16),
        pltpu.VMEM((tm, D), BF16),
        pltpu.VMEM((tm, D_RNN), F32),
        pltpu.VMEM((tm, D_RNN), F32),
        pltpu.VMEM((tm, D_RNN), F32),
        pltpu.VMEM((tm, D), F32),
        pltpu.VMEM((8, D_RNN), F32),
        pltpu.VMEM((SEGMENTS, D_RNN), F32),
        pltpu.VMEM((1, D_RNN), F32),
    ]
    return pl.pallas_call(
        functools.partial(_proj_rnn_kernel, tm=tm, n_tiles=n_tiles),
        grid=(B, n_tiles, N_COL_BLOCKS),
        in_specs=in_specs,
        out_specs=out_specs,
        out_shape=out_shape,
        scratch_shapes=scratch,
        compiler_params=pltpu.CompilerParams(
            dimension_semantics=("arbitrary", "arbitrary", "arbitrary"), vmem_limit_bytes=VMEM_LIMIT),
        name="proj_rnn_prompt",
    )(x, w_in, gpre, cw, cb, wa, wx, ba, bx, lam, wro, qkv_s, *cache_views)


def _sample_heads(local_heads, first_head, slope_of, qs_ref, cache_refs, att_ref):
    scale = HEAD_DIM ** -0.5
    ri = lax.broadcasted_iota(jnp.int32, (HEAD_DIM, HEAD_DIM), 0)
    ci = lax.broadcasted_iota(jnp.int32, (HEAD_DIM, HEAD_DIM), 1)
    eye = ri == ci

    def to_col(row):
        return jnp.sum(jnp.where(eye, jnp.broadcast_to(row, (HEAD_DIM, HEAD_DIM)), 0.0), axis=-1, keepdims=True)

    def to_row(col):
        return jnp.sum(jnp.where(eye, jnp.broadcast_to(col, (HEAD_DIM, HEAD_DIM)), 0.0), axis=0, keepdims=True)

    for i in local_heads:
        h = first_head + i
        outs, lses = [], []
        for g, (window, dilation) in enumerate(ATTN_GROUPS):
            c_ref = cache_refs[g]
            pos = lax.broadcasted_iota(jnp.int32, (1, window), 1)
            dist = (window - pos).astype(F32)
            bias = jnp.where((pos % dilation) == 0, -slope_of(g, i) * dist, NEG_INF)
            q = qs_ref[g, pl.ds(h, 1), :] * scale
            s_new = jnp.sum(qs_ref[N_GROUPS + g, pl.ds(h, 1), :] * q, axis=-1, keepdims=True)
            v_col = to_col(qs_ref[2 * N_GROUPS + g, pl.ds(h, 1), :])
            s = jnp.sum(c_ref[0, i] * to_col(q), axis=0, keepdims=True) + bias
            mx = jnp.maximum(jnp.max(s, axis=-1, keepdims=True), s_new)
            e = jnp.exp(s - mx)
            e_new = jnp.exp(s_new - mx)
            l = jnp.sum(e, axis=-1, keepdims=True) + e_new
            o = jnp.sum(c_ref[1, i] * e, axis=-1, keepdims=True) + e_new * v_col
            outs.append(o / l)
            lses.append(mx + jnp.log(l))
        mm = jnp.maximum(jnp.maximum(lses[0], lses[1]), lses[2])
        ws = [jnp.exp(x - mm) for x in lses]
        col = (outs[0] * ws[0] + outs[1] * ws[1] + outs[2] * ws[2]) / (ws[0] + ws[1] + ws[2])
        att_ref[pl.ds(h, 1), :] = to_row(col)


def _proj_rnn2_kernel(x_ref, w_ref, gpre_ref, cw_ref, cb_ref, wa_ref, wx_ref, ba_ref, bx_ref, lam_ref, wro_ref,
                      qs_ref, c0_ref, c1_ref, c2_ref,
                      q0_ref, k0_ref, v0_ref, q1_ref, k1_ref, v1_ref, q2_ref, k2_ref, v2_ref,
                      zg_ref, mixr_ref, sga_ref, kv0_ref, kv1_ref, kv2_ref, conv_ref, h_ref, att_ref,
                      xn_s, xnf_s, xseg_s, xnseg_s, xn4_s, xn16_s, u_s, a_s, b_s, cprev_s, carry_s,
                      *, tm, n_tiles):
    m = pl.program_id(1)
    j = pl.program_id(2)
    tile = pl.program_id(0) * n_tiles + m
    half = TN // 2
    d1, d2 = ATTN_GROUPS[1][1], ATTN_GROUPS[2][1]
    seg = tm // SEGMENTS
    n_slab = D_MODEL // LANES
    heads_per_step = H_G // (TILES_PER_SAMPLE * 2)

    def project(lhs_s, col, width=half):
        return jnp.dot(lhs_s[...], w_ref[:, col:col + width], preferred_element_type=F32)

    def emit(out_ref, val, c0, dilation):
        p = tm // dilation
        for r in range(dilation):
            out_ref[r] = val[r * p:(r + 1) * p, c0:c0 + half].astype(BF16)

    def natural_tail(out_ref, out_c0, val, c0, dilation):
        p = tm // dilation
        for c in range(half // LANES):
            for r in range(dilation):
                xnf_s[c, pl.ds(r, p, stride=dilation), :] = val[r * p:(r + 1) * p, c0 + c * LANES:c0 + (c + 1) * LANES]
            out_ref[:, out_c0 + c * LANES:out_c0 + (c + 1) * LANES] = xnf_s[c]

    def sample_heads(step):
        part = tile % TILES_PER_SAMPLE
        first = part * (heads_per_step * 2) + step * heads_per_step

        def slope_of(g, i):
            table = [float(_SLOPES[g, t * heads_per_step * 2 + step * heads_per_step + i])
                     for t in range(TILES_PER_SAMPLE)]
            out = jnp.float32(table[0])
            for t in range(1, TILES_PER_SAMPLE):
                out = jnp.where(part == t, jnp.float32(table[t]), out)
            return out

        _sample_heads(range(heads_per_step), first, slope_of, qs_ref, (c0_ref, c1_ref, c2_ref), att_ref)

    @pl.when(j == 0)
    def _recurrent_blocks():
        @pl.when(m == 0)
        def _reset():
            cprev_s[...] = jnp.zeros((8, D_RNN), F32)
            carry_s[...] = jnp.zeros((1, D_RNN), F32)

        for s in range(SEGMENTS):
            r0 = s * seg
            xv = x_ref[r0:r0 + seg, :]
            ms = jnp.mean(xv * xv, axis=-1, keepdims=True)
            xn = (xv * lax.rsqrt(ms + EPS)) * gpre_ref[...]
            xn_s[r0:r0 + seg, :] = xn.astype(BF16)
            for c in range(n_slab):
                piece = xn[:, c * LANES:(c + 1) * LANES]
                xnf_s[c, r0:r0 + seg, :] = piece
                xseg_s[c, pl.ds(s, seg, stride=SEGMENTS), :] = piece

        for c in range(n_slab):
            xnseg_s[:, c * LANES:(c + 1) * LANES] = xseg_s[c].astype(BF16)

        u_s[...] = project(xnseg_s, COL_U, D_RNN)

        sub = lax.broadcasted_iota(jnp.int32, (SEGMENTS, GATE_BLOCK), 0)
        gate_rows = min(GATE_ROWS, tm)
        for c in range(N_GATE_BLOCKS):
            cols = slice(c * GATE_BLOCK, (c + 1) * GATE_BLOCK)

            def wrapped(t):
                v = pltpu.roll(u_s[SEGMENTS * (seg + t):SEGMENTS * (seg + t + 1), cols], 1, axis=0)
                return jnp.where(sub == 0, cprev_s[8 + t:9 + t, cols], v)

            wrap = {t: wrapped(t) for t in range(1 - CONV_WIDTH, 0)}
            for rc in range(tm // gate_rows):
                r0 = rc * gate_rows
                xc = cb_ref[:, cols] + u_s[r0:r0 + gate_rows, cols] * cw_ref[CONV_WIDTH - 1:CONV_WIDTH, cols]
                for sh in range(1, CONV_WIDTH):
                    if r0 == 0:
                        head = [wrap[t - sh] for t in range(sh)]
                        ush = jnp.concatenate(head + [u_s[0:gate_rows - SEGMENTS * sh, cols]], axis=0)
                    else:
                        ush = u_s[r0 - SEGMENTS * sh:r0 - SEGMENTS * sh + gate_rows, cols]
                    xc = xc + ush * cw_ref[CONV_WIDTH - 1 - sh:CONV_WIDTH - sh, cols]
                a, b = _lru_gates(xc, c, wa_ref, wx_ref, ba_ref, bx_ref, lam_ref)
                a_s[r0:r0 + gate_rows, cols] = a
                b_s[r0:r0 + gate_rows, cols] = b

        hloc = jnp.zeros((SEGMENTS, D_RNN), F32)
        cum = jnp.ones((SEGMENTS, D_RNN), F32)
        for t in range(seg):
            rows = slice(t * SEGMENTS, (t + 1) * SEGMENTS)
            av = a_s[rows, :]
            hloc = av * hloc + b_s[rows, :]
            cum = av * cum
            b_s[rows, :] = hloc
            a_s[rows, :] = cum
        state = carry_s[...]
        enter = []
        for s in range(SEGMENTS):
            enter.append(state)
            state = hloc[s:s + 1, :] + cum[s:s + 1, :] * state
        enter = jnp.concatenate(enter, axis=0)
        carry_s[...] = state
        h_ref[...] = state
        for k in range(1, CONV_WIDTH):
            last = u_s[tm - SEGMENTS * (k - 1) - 1:tm - SEGMENTS * (k - 1), :]
            conv_ref[CONV_WIDTH - 1 - k:CONV_WIDTH - k, :] = last
            cprev_s[8 - k:9 - k, :] = last

        z = project(xnseg_s, COL_Z_RNN, D_RNN)
        split = (seg, SEGMENTS, D_RNN)
        h = b_s[...].reshape(split) + a_s[...].reshape(split) * enter[None]
        hz = (h.reshape(tm, D_RNN) * (z * _sigmoid(z))).astype(BF16)
        yr = jnp.dot(hz, wro_ref[...], preferred_element_type=F32)

        mix = _sigmoid(project(xnseg_s, COL_G_RNN, D_MODEL)) * yr
        for c in range(n_slab):
            xseg_s[c] = mix[:, c * LANES:(c + 1) * LANES]
        for s in range(SEGMENTS):
            for c in range(n_slab):
                mixr_ref[s * seg:(s + 1) * seg, c * LANES:(c + 1) * LANES] = \
                    xseg_s[c, pl.ds(s, seg, stride=SEGMENTS), :].astype(BF16)

        sample_heads(0)

    @pl.when(j == 1)
    def _attention_blocks():
        for dil, dst in ((d1, xn4_s), (d2, xn16_s)):
            p = tm // dil
            for r in range(dil):
                for c in range(n_slab):
                    dst[r * p:(r + 1) * p, c * LANES:(c + 1) * LANES] = \
                        xnf_s[c, pl.ds(r, p, stride=dil), :].astype(BF16)

        sga_ref[...] = _sigmoid(project(xn_s, COL_G_ATTN, D_MODEL)).astype(BF16)
        zg_ref[...] = project(xn_s, COL_Z_ATTN).astype(BF16)

        lhs = (xn_s, xn4_s, xn16_s)
        qkv_refs = ((q0_ref, k0_ref, v0_ref), (q1_ref, k1_ref, v1_ref), (q2_ref, k2_ref, v2_ref))
        for g, (_, dil) in enumerate(ATTN_GROUPS):
            q_ref, k_ref, v_ref = qkv_refs[g]
            emit(q_ref, project(lhs[g], COL_Q + g * D_ATTN), 0, dil)
            for col, out_ref, c0 in ((COL_K, k_ref, 0), (COL_V, v_ref, half)):
                r = project(lhs[g], col + g * D_ATTN)
                emit(out_ref, r, 0, dil)
                if g == 0:
                    kv0_ref[:, c0:c0 + half] = r[tm - ATTN_GROUPS[0][0]:tm, :]
                else:
                    natural_tail(kv1_ref if g == 1 else kv2_ref, c0, r, 0, dil)
        sample_heads(1)


def _proj_rnn_prompt2(x, w_in, gpre, cw, cb, wa, wx, ba, bx, lam, wro, qkv_s, caches, layer):
    B, S, D = x.shape
    tm = TM_PROMPT2
    n_tiles = S // tm
    N = qkv_s.shape[0]
    heads_per_step = H_G // (TILES_PER_SAMPLE * 2)
    assert S % tm == 0 and tm % (SEGMENTS * 16) == 0 and tm >= ATTN_GROUPS[0][0]
    assert all(w % tm == 0 and S >= w for w, _ in ATTN_GROUPS[1:])
    assert B * n_tiles == N * TILES_PER_SAMPLE and n_tiles % TILES_PER_SAMPLE == 0
    cache_views = [jnp.transpose(c, (0, 1, 3, 4, 5, 2)) for c in caches]

    const2 = lambda b, m, j: (0, 0)
    const3 = lambda b, m, j: (0, 0, 0)
    tile3 = lambda b, m, j: (b, m, 0)
    tile4 = lambda b, m, j: (b, 0, m, 0)
    per_seq = lambda b, m, j: (b, 0, 0)
    sample_of = lambda b, m: (b * n_tiles + m) // TILES_PER_SAMPLE
    head_block = lambda b, m, j: ((b * n_tiles + m) % TILES_PER_SAMPLE) * 2 + j

    def last_rows(window):
        first_tile = n_tiles - window // tm
        return lambda b, m, j: (b, jnp.maximum(m - first_tile, 0), 0)

    resident = dict(pipeline_mode=pl.Buffered(1))
    qkv_shapes, qkv_specs = [], []
    for _, dil in ATTN_GROUPS:
        for _ in range(3):
            qkv_shapes.append(jax.ShapeDtypeStruct((B, dil, S // dil, D_ATTN), BF16))
            qkv_specs.append(pl.BlockSpec((None, dil, tm // dil, D_ATTN), tile4))
    out_shape = tuple(qkv_shapes) + (
        jax.ShapeDtypeStruct((B, S, D_ATTN), BF16),
        jax.ShapeDtypeStruct((B, S, D), BF16),
        jax.ShapeDtypeStruct((B, S, D), BF16),
        jax.ShapeDtypeStruct((B, ATTN_GROUPS[0][0], 2 * D_ATTN), F32),
        jax.ShapeDtypeStruct((B, ATTN_GROUPS[1][0], 2 * D_ATTN), F32),
        jax.ShapeDtypeStruct((B, ATTN_GROUPS[2][0], 2 * D_ATTN), F32),
        jax.ShapeDtypeStruct((B, CONV_WIDTH - 1, D_RNN), F32),
        jax.ShapeDtypeStruct((B, 1, D_RNN), F32),
        jax.ShapeDtypeStruct((N, H_G, HEAD_DIM), F32),
    )
    out_specs = tuple(qkv_specs) + (
        pl.BlockSpec((None, tm, D_ATTN), tile3),
        pl.BlockSpec((None, tm, D), tile3),
        pl.BlockSpec((None, tm, D), tile3),
        pl.BlockSpec((None, ATTN_GROUPS[0][0], 2 * D_ATTN), per_seq),
        pl.BlockSpec((None, tm, 2 * D_ATTN), last_rows(ATTN_GROUPS[1][0])),
        pl.BlockSpec((None, tm, 2 * D_ATTN), last_rows(ATTN_GROUPS[2][0])),
        pl.BlockSpec((None, CONV_WIDTH - 1, D_RNN), per_seq),
        pl.BlockSpec((None, 1, D_RNN), per_seq),
        pl.BlockSpec((None, H_G, HEAD_DIM), lambda b, m, j: (sample_of(b, m), 0, 0)),
    )
    in_specs = [
        pl.BlockSpec((None, tm, D), tile3),
        pl.BlockSpec((D, D_IN), const2, **resident),
        pl.BlockSpec((1, D), const2),
        pl.BlockSpec((CONV_WIDTH, D_RNN), const2),
        pl.BlockSpec((1, D_RNN), const2),
        pl.BlockSpec((N_GATE_BLOCKS, GATE_BLOCK, GATE_BLOCK), const3),
        pl.BlockSpec((N_GATE_BLOCKS, GATE_BLOCK, GATE_BLOCK), const3),
        pl.BlockSpec((1, D_RNN), const2),
        pl.BlockSpec((1, D_RNN), const2),
        pl.BlockSpec((1, D_RNN), const2),
        pl.BlockSpec((D_RNN, D), const2, **resident),
        pl.BlockSpec((None, 3 * N_GROUPS, H_G, HEAD_DIM), lambda b, m, j: (sample_of(b, m), 0, 0, 0)),
    ]
    in_specs += [pl.BlockSpec((None, None, 2, heads_per_step, HEAD_DIM, window),
                              lambda b, m, j: (layer, sample_of(b, m), 0, head_block(b, m, j), 0, 0))
                 for window, _ in ATTN_GROUPS]
    scratch = [
        pltpu.VMEM((tm, D), BF16),
        pltpu.VMEM((D // LANES, tm, LANES), F32),
        pltpu.VMEM((D // LANES, tm, LANES), F32),
        pltpu.VMEM((tm, D), BF16),
        pltpu.VMEM((tm, D), BF16),
        pltpu.VMEM((tm, D), BF16),
        pltpu.VMEM((tm, D_RNN), F32),
        pltpu.VMEM((tm, D_RNN), F32),
        pltpu.VMEM((tm, D_RNN), F32),
        pltpu.VMEM((8, D_RNN), F32),
        pltpu.VMEM((1, D_RNN), F32),
    ]
    return pl.pallas_call(
        functools.partial(_proj_rnn2_kernel, tm=tm, n_tiles=n_tiles),
        grid=(B, n_tiles, 2),
        in_specs=in_specs,
        out_specs=out_specs,
        out_shape=out_shape,
        scratch_shapes=scratch,
        compiler_params=pltpu.CompilerParams(
            dimension_semantics=("arbitrary", "arbitrary", "arbitrary"), vmem_limit_bytes=VMEM_LIMIT),
        name="proj_rnn_prompt",
    )(x, w_in, gpre, cw, cb, wa, wx, ba, bx, lam, wro, qkv_s, *cache_views)


def _proj_rnn_sample_kernel(x_ref, w_ref, gpre_ref, cw_ref, cb_ref, wa_ref, wx_ref, ba_ref, bx_ref, lam_ref, wro_ref,
                            sc_ref, h0_ref,
                            att_ref, mixr_ref, sga_ref, conv_ref, h_ref,
                            xn_s, h_s, yr_s):
    j = pl.program_id(0)

    def project():
        return jnp.dot(xn_s[...], w_ref[...], preferred_element_type=F32)

    @pl.when(j == 0)
    def _u_block():
        xv = x_ref[...]
        ms = jnp.mean(xv * xv, axis=-1, keepdims=True)
        xn_s[...] = ((xv * lax.rsqrt(ms + EPS)) * gpre_ref[...]).astype(BF16)
        u = project()
        taps = [sc_ref[:, k * D_RNN:(k + 1) * D_RNN] for k in range(CONV_WIDTH - 1)] + [u]
        for c in range(N_GATE_BLOCKS):
            c0 = c * GATE_BLOCK
            xc = cb_ref[:, c0:c0 + GATE_BLOCK]
            for tap in range(CONV_WIDTH):
                xc = xc + taps[tap][:, c0:c0 + GATE_BLOCK] * cw_ref[tap:tap + 1, c0:c0 + GATE_BLOCK]
            a, b = _lru_gates(xc, c, wa_ref, wx_ref, ba_ref, bx_ref, lam_ref)
            h = a * h0_ref[:, c0:c0 + GATE_BLOCK] + b
            h_s[:, c0:c0 + GATE_BLOCK] = h
            h_ref[:, c0:c0 + GATE_BLOCK] = h
        for k in range(1, CONV_WIDTH):
            conv_ref[:, (k - 1) * D_RNN:k * D_RNN] = taps[k]

    @pl.when(j == 1)
    def _z_rnn_block():
        z = project()
        hz = (h_s[...] * (z * _sigmoid(z))).astype(BF16)
        yr_s[...] = jnp.dot(hz, wro_ref[...], preferred_element_type=F32)

    @pl.when(j == 2)
    def _g_rnn_block():
        mixr_ref[...] = _sigmoid(project()) * yr_s[...]

    @pl.when(j == 3)
    def _g_attn_block():
        sga_ref[...] = _sigmoid(project())

    @pl.when(j >= 4)
    def _attn_blocks():
        att_ref[...] = project()


def _proj_rnn_sample(x, w_in, gpre, cw, cb, wa, wx, ba, bx, lam, wro, sconv, h0):
    N, D = x.shape
    const2 = lambda j: (0, 0)
    const3 = lambda j: (0, 0, 0)
    n_att = N_COL_BLOCKS - 4
    first_att, g_rnn_block = COL_Q // TN, COL_G_RNN // TN
    assert COL_Q % TN == 0 and COL_G_RNN % TN == 0 and COL_G_RNN - COL_Q == n_att * TN
    w_block = lambda j: jnp.where(j < 2, j, jnp.where(j < 4, j + g_rnn_block - 2, j - 4 + first_att))
    out_shape = (
        jax.ShapeDtypeStruct((N, n_att * TN), F32),
        jax.ShapeDtypeStruct((N, D), F32),
        jax.ShapeDtypeStruct((N, D), F32),
        jax.ShapeDtypeStruct((N, (CONV_WIDTH - 1) * D_RNN), F32),
        jax.ShapeDtypeStruct((N, D_RNN), F32),
    )
    out_specs = (
        pl.BlockSpec((N, TN), lambda j: (0, jnp.maximum(j - 4, 0))),
        pl.BlockSpec((N, D), const2),
        pl.BlockSpec((N, D), const2),
        pl.BlockSpec((N, (CONV_WIDTH - 1) * D_RNN), const2),
        pl.BlockSpec((N, D_RNN), const2),
    )
    in_specs = [
        pl.BlockSpec((N, D), const2),
        pl.BlockSpec((D, TN), lambda j: (0, w_block(j))),
        pl.BlockSpec((1, D), const2),
        pl.BlockSpec((CONV_WIDTH, D_RNN), const2),
        pl.BlockSpec((1, D_RNN), const2),
        pl.BlockSpec((N_GATE_BLOCKS, GATE_BLOCK, GATE_BLOCK), const3),
        pl.BlockSpec((N_GATE_BLOCKS, GATE_BLOCK, GATE_BLOCK), const3),
        pl.BlockSpec((1, D_RNN), const2),
        pl.BlockSpec((1, D_RNN), const2),
        pl.BlockSpec((1, D_RNN), const2),
        pl.BlockSpec((D_RNN, D), const2),
        pl.BlockSpec((N, (CONV_WIDTH - 1) * D_RNN), const2),
        pl.BlockSpec((N, D_RNN), const2),
    ]
    scratch = [pltpu.VMEM((N, D), BF16), pltpu.VMEM((N, D_RNN), F32), pltpu.VMEM((N, D), F32)]
    return pl.pallas_call(
        _proj_rnn_sample_kernel,
        grid=(N_COL_BLOCKS,),
        in_specs=in_specs,
        out_specs=out_specs,
        out_shape=out_shape,
        scratch_shapes=scratch,
        compiler_params=pltpu.CompilerParams(dimension_semantics=("arbitrary",), vmem_limit_bytes=VMEM_LIMIT),
        name="proj_rnn_sample",
    )(x, w_in, gpre, cw, cb, wa, wx, ba, bx, lam, wro, sconv, h0)


def _attn_kernel(q_ref, kc_ref, vc_ref, kp_ref, vp_ref, o_ref, st_ref, bias_s, *, group, chunk, n_res):
    dilation = ATTN_GROUPS[group][1]
    c = pl.program_id(2)
    nt = (((1,), (1,)), ((), ()))

    @pl.when((pl.program_id(0) == 0) & (pl.program_id(1) == 0) & (c == 0))
    def _init_bias():
        qi = lax.broadcasted_iota(jnp.int32, (Q_BLOCK, Q_BLOCK), 0)
        kj = lax.broadcasted_iota(jnp.int32, (Q_BLOCK, Q_BLOCK), 1)
        steps_prev = Q_BLOCK + qi - kj
        steps_cur = qi - kj
        dist_prev = (steps_prev * dilation).astype(F32)
        dist_cur = (steps_cur * dilation).astype(F32)
        for h in range(H_G):
            slope = float(_SLOPES[group, h])
            bias_s[h] = jnp.where(steps_cur >= 0, -slope * dist_cur, NEG_INF)
            bias_s[H_G + h] = jnp.where(steps_prev <= Q_BLOCK, -slope * dist_prev, NEG_INF)
            bias_s[2 * H_G + h] = jnp.full((Q_BLOCK, Q_BLOCK), NEG_INF, F32)

    lane = lax.broadcasted_iota(jnp.int32, (Q_BLOCK, LANES), 1)
    low = lane < HEAD_DIM
    lane2 = lax.broadcasted_iota(jnp.int32, (2 * Q_BLOCK, LANES), 1)
    low2 = lane2 < HEAD_DIM
    ones_lo = jnp.where(low2, 1.0, 0.0).astype(BF16)
    ones_hi = jnp.where(low2, 0.0, 1.0).astype(BF16)
    stat_lane = lax.broadcasted_iota(jnp.int32, (Q_BLOCK, STAT_LANES), 1)

    def keys_of(res, block, cur_ref, prev_ref, sl):
        if block == 0:
            return jnp.concatenate([prev_ref[res, :, sl], cur_ref[res, 0:Q_BLOCK, sl]], axis=0)
        return cur_ref[res, (block - 1) * Q_BLOCK:(block + 1) * Q_BLOCK, sl]

    for res, i in [(res, i) for res in range(n_res) for i in range(chunk // Q_BLOCK)]:
        r0 = i * Q_BLOCK
        first = jnp.where(c == 0, H_G, 0) if i == 0 else 0
        stats = jnp.zeros((Q_BLOCK, STAT_LANES), F32)
        for p in range(H_G // 2):
            sl = slice(LANES * p, LANES * (p + 1))
            qp = q_ref[res, r0:r0 + Q_BLOCK, sl] * (HEAD_DIM ** -0.5)
            kp = keys_of(res, i, kc_ref, kp_ref, sl)
            vp = keys_of(res, i, vc_ref, vp_ref, sl)
            es, ms = [], []
            for hh in range(2):
                h = 2 * p + hh
                msk = low if hh == 0 else jnp.logical_not(low)
                qm = jnp.where(msk, qp, jnp.zeros_like(qp))
                s = lax.dot_general(qm, kp, nt, preferred_element_type=F32)
                s_p = s[:, 0:Q_BLOCK] + bias_s[H_G + h + first]
                s_c = s[:, Q_BLOCK:2 * Q_BLOCK] + bias_s[h]
                mx = jnp.maximum(jnp.max(s_p, axis=-1, keepdims=True), jnp.max(s_c, axis=-1, keepdims=True))
                es.append(jnp.exp(s_p - mx).astype(BF16))
                es.append(jnp.exp(s_c - mx).astype(BF16))
                ms.append(mx)
            vm0 = jnp.where(low2, vp, jnp.zeros_like(vp))
            vm1 = jnp.where(low2, jnp.zeros_like(vp), vp)
            w = jnp.concatenate([jnp.concatenate([vm0, ones_lo], axis=1),
                                 jnp.concatenate([vm1, ones_hi], axis=1)], axis=0)
            acc = jnp.dot(jnp.concatenate(es, axis=1), w, preferred_element_type=F32)
            l_pair = acc[:, LANES:2 * LANES]
            o_ref[res, r0:r0 + Q_BLOCK, sl] = (acc[:, 0:LANES] / l_pair).astype(o_ref.dtype)
            lse_pair = jnp.where(low, ms[0], ms[1]) + jnp.log(l_pair)
            keep = ((stat_lane % (STAT_LANES // 2)) // STAT_LANES_PER_HEAD) == p
            stats = jnp.where(keep, lse_pair, stats)
        st_ref[res, r0:r0 + Q_BLOCK, :] = stats


def _attn_group(q, k, v, group):
    B, dil, L, _ = q.shape
    chunk = min(ATTN_CHUNK, L)
    n_res = min(ATTN_CHUNK // chunk, dil)
    assert L % chunk == 0 and chunk % Q_BLOCK == 0 and dil % n_res == 0
    cur = lambda b, r, c: (b, r, c, 0)
    prev = lambda b, r, c: (b, r, jnp.maximum(c * (chunk // Q_BLOCK) - 1, 0), 0)
    blk = (None, n_res, chunk, D_ATTN)
    pblk = (None, n_res, Q_BLOCK, D_ATTN)
    return pl.pallas_call(
        functools.partial(_attn_kernel, group=group, chunk=chunk, n_res=n_res),
        grid=(B, dil // n_res, L // chunk),
        in_specs=[pl.BlockSpec(blk, cur), pl.BlockSpec(blk, cur), pl.BlockSpec(blk, cur),
                  pl.BlockSpec(pblk, prev), pl.BlockSpec(pblk, prev)],
        out_specs=[pl.BlockSpec(blk, cur), pl.BlockSpec((None, n_res, chunk, STAT_LANES), cur)],
        out_shape=[jax.ShapeDtypeStruct((B, dil, L, D_ATTN), BF16),
                   jax.ShapeDtypeStruct((B, dil, L, STAT_LANES), F32)],
        scratch_shapes=[pltpu.VMEM((3 * H_G, Q_BLOCK, Q_BLOCK), F32)],
        compiler_params=pltpu.CompilerParams(
            dimension_semantics=("arbitrary", "arbitrary", "arbitrary"), vmem_limit_bytes=VMEM_LIMIT),
        name=f"attn_group{group}",
    )(q, k, v, k, v)


def _finish(att, rows, z_ref, sga_ref, mixr_ref, x_ref, wao_ref, wo_ref, gpost_ref, y_ref):
    z = z_ref[rows, :].astype(F32)
    ya_in = (att * (z * _sigmoid(z))).astype(BF16)
    ya = jnp.dot(ya_in, wao_ref[...], preferred_element_type=F32)
    mixed = mixr_ref[rows, :].astype(F32) + sga_ref[rows, :].astype(F32) * ya
    out = jnp.dot(mixed.astype(BF16), wo_ref[...], preferred_element_type=F32)
    ms = jnp.mean(out * out, axis=-1, keepdims=True)
    y_ref[rows, :] = x_ref[rows, :] + (out * lax.rsqrt(ms + EPS)) * gpost_ref[...]


def _out_kernel(att_ref, z_ref, sga_ref, mixr_ref, x_ref, wao_ref, wo_ref, gpost_ref, y_ref):
    rows = slice(0, x_ref.shape[0])
    _finish(att_ref[...].astype(F32), rows, z_ref, sga_ref, mixr_ref, x_ref, wao_ref, wo_ref, gpost_ref, y_ref)


def _merge_out_kernel(o0_ref, o1_ref, o2_ref, s0_ref, s1_ref, s2_ref, z_ref, sga_ref, mixr_ref, x_ref,
                      wao_ref, wo_ref, gpost_ref, y_ref, nat1_s, nat2_s, st1_s, st2_s, *, tm):
    for o_ref, s_ref, nat_s, stn_s, dil in ((o1_ref, s1_ref, nat1_s, st1_s, ATTN_GROUPS[1][1]),
                                            (o2_ref, s2_ref, nat2_s, st2_s, ATTN_GROUPS[2][1])):
        p = tm // dil
        for r in range(dil):
            stn_s[pl.ds(r, p, stride=dil), :] = s_ref[r]
            for c in range(D_ATTN // LANES):
                nat_s[c, pl.ds(r, p, stride=dil), :] = o_ref[r, :, c * LANES:(c + 1) * LANES].astype(F32)

    src = lax.broadcasted_iota(jnp.int32, (STAT_LANES, D_ATTN), 0)
    dst_head = lax.broadcasted_iota(jnp.int32, (STAT_LANES, D_ATTN), 1) // HEAD_DIM
    expand = jnp.where(src == (dst_head % 2) * (STAT_LANES // 2) + (dst_head // 2) * STAT_LANES_PER_HEAD,
                       1.0, 0.0).astype(BF16)
    for k in range(tm // OUT_ROWS):
        rows = slice(k * OUT_ROWS, (k + 1) * OUT_ROWS)
        lse = [s0_ref[rows, :], st1_s[rows, :], st2_s[rows, :]]
        mm = jnp.maximum(jnp.maximum(lse[0], lse[1]), lse[2])
        ws = [jnp.exp(x - mm) for x in lse]
        den = ws[0] + ws[1] + ws[2]
        outs = [o0_ref[rows, :].astype(F32),
                jnp.concatenate([nat1_s[c, rows, :] for c in range(D_ATTN // LANES)], axis=1),
                jnp.concatenate([nat2_s[c, rows, :] for c in range(D_ATTN // LANES)], axis=1)]
        att = jnp.zeros((OUT_ROWS, D_ATTN), F32)
        for g in range(N_GROUPS):
            wexp = jnp.dot((ws[g] / den).astype(BF16), expand, preferred_element_type=F32)
            att = att + wexp * outs[g]
        _finish(att, rows, z_ref, sga_ref, mixr_ref, x_ref, wao_ref, wo_ref, gpost_ref, y_ref)


def _merge_out_prompt(os, sts, zg, sga, mixr, x, wao, wo, gpost):
    B, S, D = x.shape
    tm = TM_OUT
    tile3 = lambda b, m: (b, m, 0)
    tile4 = lambda b, m: (b, 0, m, 0)
    const = lambda b, m: (0, 0)
    in_specs = []
    for (_, dil), width in [(g, D_ATTN) for g in ATTN_GROUPS] + [(g, STAT_LANES) for g in ATTN_GROUPS]:
        if dil == 1:
            in_specs.append(pl.BlockSpec((None, None, tm, width), tile4))
        else:
            in_specs.append(pl.BlockSpec((None, dil, tm // dil, width), tile4))
    in_specs += [
        pl.BlockSpec((None, tm, D_ATTN), tile3),
        pl.BlockSpec((None, tm, D), tile3),
        pl.BlockSpec((None, tm, D), tile3),
        pl.BlockSpec((None, tm, D), tile3),
        pl.BlockSpec((D_ATTN, D), const),
        pl.BlockSpec((D, D), const),
        pl.BlockSpec((1, D), const),
    ]
    scratch = [
        pltpu.VMEM((D_ATTN // LANES, tm, LANES), F32),
        pltpu.VMEM((D_ATTN // LANES, tm, LANES), F32),
        pltpu.VMEM((tm, STAT_LANES), F32),
        pltpu.VMEM((tm, STAT_LANES), F32),
    ]
    return pl.pallas_call(
        functools.partial(_merge_out_kernel, tm=tm),
        grid=(B, S // tm),
        in_specs=in_specs,
        out_specs=pl.BlockSpec((None, tm, D), tile3),
        out_shape=jax.ShapeDtypeStruct((B, S, D), F32),
        scratch_shapes=scratch,
        compiler_params=pltpu.CompilerParams(
            dimension_semantics=("arbitrary", "arbitrary"), vmem_limit_bytes=VMEM_LIMIT),
        name="merge_out_prompt",
    )(*os, *sts, zg, sga, mixr, x, wao, wo, gpost)


def _out_sample(att, z, sga, mixr, x, wao, wo, gpost):
    N, D = x.shape
    full = lambda shape: pl.BlockSpec(shape, lambda i: (0, 0))
    return pl.pallas_call(
        _out_kernel,
        grid=(1,),
        in_specs=[full((N, D_ATTN)), full((N, D_ATTN)), full((N, D)), full((N, D)), full((N, D)),
                  full((D_ATTN, D)), full((D, D)), full((1, D))],
        out_specs=full((N, D)),
        out_shape=jax.ShapeDtypeStruct((N, D), F32),
        compiler_params=pltpu.CompilerParams(dimension_semantics=("arbitrary",), vmem_limit_bytes=VMEM_LIMIT),
        name="out_sample",
    )(att, z, sga, mixr, x, wao, wo, gpost)


def _block_diag_chunks(w):
    per = GATE_BLOCK // RNN_BLOCK
    w = w.reshape(N_GATE_BLOCKS, per, RNN_BLOCK, RNN_BLOCK)
    eye = jnp.eye(per, dtype=w.dtype)
    dense = w[:, :, :, None, :] * eye[None, :, None, :, None]
    return dense.reshape(N_GATE_BLOCKS, GATE_BLOCK, GATE_BLOCK)


def _reorder_w_in(w_in):
    sizes = (D_RNN, D_RNN, N_GROUPS * D_ATTN, N_GROUPS * D_ATTN, N_GROUPS * D_ATTN, D_ATTN, D_MODEL, D_MODEL)
    starts = np.concatenate([[0], np.cumsum(sizes)[:-1]])
    u0, z0, q0, k0, v0, za0, gr0, ga0 = (int(s) for s in starts)
    pieces = [(u0, D_RNN), (z0, D_RNN), (gr0, D_MODEL), (ga0, D_MODEL), (za0, D_ATTN)]
    for g in range(N_GROUPS):
        pieces += [(q0 + g * D_ATTN, D_ATTN), (k0 + g * D_ATTN, D_ATTN), (v0 + g * D_ATTN, D_ATTN)]
    return jnp.concatenate([w_in[:, s:s + n] for s, n in pieces], axis=1).astype(BF16)


def _layer(layer, yp, ys, sconv, h0, caches, norm_pre, norm_post, w_in, conv_w, conv_b, lru_w_a, lru_b_a, lru_w_x,
           lru_b_x, lru_lambda, w_rnn_out, w_attn_out, w_out):
    B, S, D = yp.shape
    N = ys.shape[0]
    row = lambda v: v.reshape(1, -1)
    w_in_b = w_in.astype(BF16)
    wa = _block_diag_chunks(lru_w_a).astype(BF16)
    wx = _block_diag_chunks(lru_w_x).astype(BF16)
    wro = w_rnn_out.astype(BF16)
    wao = w_attn_out.astype(BF16)
    wo = w_out.astype(BF16)
    shared = (w_in_b, row(norm_pre), conv_w, row(conv_b), wa, wx, row(lru_b_a), row(lru_b_x), row(lru_lambda), wro)

    xs = ys.reshape(N, D)
    att_in, mixr_s, sga_s, conv_s, h_s = _proj_rnn_sample(
        xs, *shared, sconv.reshape(N, (CONV_WIDTH - 1) * D_RNN), h0)
    n_qkv = 3 * N_GROUPS
    z_s = att_in[:, n_qkv * D_ATTN:]
    qkv_s = att_in[:, :n_qkv * D_ATTN].reshape(N, n_qkv, H_G, HEAD_DIM)

    outs = _proj_rnn_prompt2(yp, *shared, qkv_s, caches, layer)
    qkv, (zg, mixr, sga, kv0, kv1, kv2, conv_p, h_p, att_s) = outs[:9], outs[9:]
    os, sts = [], []
    for g in range(N_GROUPS):
        o, st = _attn_group(qkv[3 * g], qkv[3 * g + 1], qkv[3 * g + 2], g)
        os.append(o)
        sts.append(st)
    y_p = _merge_out_prompt(os, sts, zg, sga, mixr, yp, wao, wo, row(norm_post))
    kv_p = [kv.reshape(B, kv.shape[1], 2, H_G, HEAD_DIM) for kv in (kv0, kv1, kv2)]

    y_s = _out_sample(att_s.reshape(N, D_ATTN), z_s, sga_s, mixr_s, xs, wao, wo, row(norm_post))
    kv_s = [jnp.stack([qkv_s[:, N_GROUPS + g], qkv_s[:, 2 * N_GROUPS + g]], axis=1).reshape(
        N, 1, 2, H_G, HEAD_DIM) for g in range(N_GROUPS)]

    return (y_p, y_s.reshape(N, 1, D), conv_p, conv_s.reshape(N, CONV_WIDTH - 1, D_RNN),
            h_p.reshape(B, D_RNN), h_s, kv_p, kv_s)


def kernel(x_prompt, x_sample, state_conv, state_h, cache_kv_w128, cache_kv_w512, cache_kv_w2048, norm_pre, norm_post, w_in, conv_w, conv_b, lru_w_a, lru_b_a, lru_w_x, lru_b_x, lru_lambda, w_rnn_out, w_attn_out, w_out):
    depth = norm_pre.shape[0]
    caches = (cache_kv_w128, cache_kv_w512, cache_kv_w2048)
    yp, ys = x_prompt, x_sample
    conv_p, conv_s, h_p, h_s = [], [], [], []
    kvp = ([], [], [])
    kvs = ([], [], [])
    for l in range(depth):
        yp, ys, cp, cs, hp, hs, kv_p, kv_s = _layer(
            l, yp, ys, state_conv[l], state_h[l], caches,
            norm_pre[l], norm_post[l], w_in[l], conv_w[l], conv_b[l], lru_w_a[l], lru_b_a[l], lru_w_x[l],
            lru_b_x[l], lru_lambda[l], w_rnn_out[l], w_attn_out[l], w_out[l])
        conv_p.append(cp)
        conv_s.append(cs)
        h_p.append(hp)
        h_s.append(hs)
        for g in range(N_GROUPS):
            kvp[g].append(kv_p[g])
            kvs[g].append(kv_s[g])
    return (yp, ys, jnp.stack(conv_p), jnp.stack(conv_s), jnp.stack(h_p), jnp.stack(h_s),
            jnp.stack(kvp[0]), jnp.stack(kvs[0]), jnp.stack(kvp[1]), jnp.stack(kvs[1]),
            jnp.stack(kvp[2]), jnp.stack(kvs[2]))
```

```python
import functools

import numpy as np
import jax
import jax.numpy as jnp
from jax import lax
from jax.experimental import pallas as pl
from jax.experimental.pallas import tpu as pltpu

F32 = jnp.float32
BF16 = jnp.bfloat16

D_MODEL = 1024
D_RNN = 1024
N_RNN_BLOCKS = 16
RNN_BLOCK = D_RNN // N_RNN_BLOCKS
CONV_WIDTH = 4
LRU_C = 8.0
HEAD_DIM = 64
H_G = 8
ATTN_GROUPS = ((128, 1), (512, 4), (2048, 16))
N_GROUPS = 3
D_ATTN = H_G * HEAD_DIM
Q_BLOCK = 128
ALIBI_MAX = 8.0
EPS = 1e-6
NEG_INF = -1e30
D_IN = 2 * D_RNN + 3 * N_GROUPS * D_ATTN + D_ATTN + 2 * D_MODEL

LANES = 128
TN = 1024
N_COL_BLOCKS = D_IN // TN
COL_U = 0
COL_Z_RNN = COL_U + D_RNN
COL_Q = COL_Z_RNN + D_RNN
COL_K = COL_Q + N_GROUPS * D_ATTN
COL_V = COL_K + N_GROUPS * D_ATTN
COL_Z_ATTN = COL_V + N_GROUPS * D_ATTN
COL_G_RNN = COL_Z_ATTN + D_ATTN
COL_G_ATTN = COL_G_RNN + D_MODEL
GATE_BLOCK = 256
N_GATE_BLOCKS = D_RNN // GATE_BLOCK
STAT_LANES = LANES
STAT_LANES_PER_HEAD = STAT_LANES // H_G

TM_PROMPT = 512
TM_PROMPT2 = 256
TILES_PER_SAMPLE = 2
SEGMENTS = 8
TM_OUT = 512
OUT_ROWS = TM_OUT
ATTN_CHUNK = 512
ROW_CHUNK = 64
GATE_ROWS = 128
VMEM_LIMIT = 56 * 1024 * 1024


def _alibi_slopes():
    n = N_GROUPS * H_G
    s = np.float32(2.0) ** (np.float32(-ALIBI_MAX) * np.arange(1, n + 1, dtype=np.float32) / np.float32(n))
    return s.reshape(N_GROUPS, H_G)


_SLOPES = _alibi_slopes()


def _stat_lane(h):
    return (h % 2) * (STAT_LANES // 2) + (h // 2) * STAT_LANES_PER_HEAD


def _softplus(y):
    return jnp.maximum(y, 0.0) + jnp.log1p(jnp.exp(-jnp.abs(y)))


def _sigmoid(x):
    return 0.5 * jnp.tanh(0.5 * x) + 0.5


def _for_rows(n_rows, chunk, fn):
    if n_rows <= chunk:
        fn(0)
        return

    def body(c, carry):
        fn(pl.multiple_of(c * chunk, chunk))
        return carry

    lax.fori_loop(0, n_rows // chunk, body, 0)


def _lru_gates(xc, c, wa_ref, wx_ref, ba_ref, bx_ref, lam_ref):
    c0 = c * GATE_BLOCK
    xcb = xc.astype(BF16)
    r = _sigmoid(jnp.dot(xcb, wa_ref[c], preferred_element_type=F32) + ba_ref[:, c0:c0 + GATE_BLOCK])
    i = _sigmoid(jnp.dot(xcb, wx_ref[c], preferred_element_type=F32) + bx_ref[:, c0:c0 + GATE_BLOCK])
    log_a = (-LRU_C * r) * _softplus(-lam_ref[:, c0:c0 + GATE_BLOCK])
    a = jnp.exp(log_a)
    v = 1.0 - a * a
    b = jnp.where(v > 0.0, v * lax.rsqrt(v), 0.0) * i * xc
    return a, b


def _sample_head_attention(h, qs_ref, cache_refs, att_ref):
    scale = HEAD_DIM ** -0.5
    ri = lax.broadcasted_iota(jnp.int32, (HEAD_DIM, HEAD_DIM), 0)
    ci = lax.broadcasted_iota(jnp.int32, (HEAD_DIM, HEAD_DIM), 1)
    eye = ri == ci

    def to_col(row):
        return jnp.sum(jnp.where(eye, jnp.broadcast_to(row, (HEAD_DIM, HEAD_DIM)), 0.0), axis=-1, keepdims=True)

    def to_row(col):
        return jnp.sum(jnp.where(eye, jnp.broadcast_to(col, (HEAD_DIM, HEAD_DIM)), 0.0), axis=0, keepdims=True)

    outs, lses = [], []
    for g, (window, dilation) in enumerate(ATTN_GROUPS):
        c_ref = cache_refs[g]
        pos = lax.broadcasted_iota(jnp.int32, (1, window), 1)
        dist = (window - pos).astype(F32)
        bias = jnp.where((pos % dilation) == 0, -float(_SLOPES[g, h]) * dist, NEG_INF)
        q = qs_ref[3 * g, h:h + 1, :] * scale
        s_new = jnp.sum(qs_ref[3 * g + 1, h:h + 1, :] * q, axis=-1, keepdims=True)
        v_col = to_col(qs_ref[3 * g + 2, h:h + 1, :])
        s = jnp.sum(c_ref[0] * to_col(q), axis=0, keepdims=True) + bias
        mx = jnp.maximum(jnp.max(s, axis=-1, keepdims=True), s_new)
        e = jnp.exp(s - mx)
        e_new = jnp.exp(s_new - mx)
        l = jnp.sum(e, axis=-1, keepdims=True) + e_new
        o = jnp.sum(c_ref[1] * e, axis=-1, keepdims=True) + e_new * v_col
        outs.append(o / l)
        lses.append(mx + jnp.log(l))
    mm = jnp.maximum(jnp.maximum(lses[0], lses[1]), lses[2])
    ws = [jnp.exp(x - mm) for x in lses]
    col = (outs[0] * ws[0] + outs[1] * ws[1] + outs[2] * ws[2]) / (ws[0] + ws[1] + ws[2])
    att_ref[h:h + 1, :] = to_row(col)


def _proj_rnn_kernel(x_ref, w_ref, gpre_ref, cw_ref, cb_ref, wa_ref, wx_ref, ba_ref, bx_ref, lam_ref, wro_ref,
                     qs_ref, c0_ref, c1_ref, c2_ref,
                     q0_ref, k0_ref, v0_ref, q1_ref, k1_ref, v1_ref, q2_ref, k2_ref, v2_ref,
                     zg_ref, mixr_ref, sga_ref, kv0_ref, kv1_ref, kv2_ref, conv_ref, h_ref, att_ref,
                     xn_s, xnf_s, xseg_s, xnseg_s, xn4_s, xn16_s, u_s, a_s, b_s, yr_s,
                     cprev_s, cin_s, carry_s,
                     *, tm, n_tiles):
    m = pl.program_id(1)
    j = pl.program_id(2)
    acc_s = u_s

    def sample_head(h):
        _sample_head_attention(h, qs_ref, (c0_ref, c1_ref, c2_ref), att_ref)
    last_tile = m == n_tiles - 1
    in_kv2 = m >= n_tiles - ATTN_GROUPS[2][0] // tm
    half = TN // 2
    d1, d2 = ATTN_GROUPS[1][1], ATTN_GROUPS[2][1]

    def project(lhs_s):
        return jnp.dot(lhs_s[...], w_ref[...], preferred_element_type=F32)

    def emit(out_ref, c0, dilation):
        p = tm // dilation
        for r in range(dilation):
            out_ref[r] = acc_s[r * p:(r + 1) * p, c0:c0 + half].astype(BF16)

    def natural_tail(out_ref, out_c0, c0, dilation):
        p = tm // dilation
        for c in range(half // LANES):
            for r in range(dilation):
                xnf_s[c, pl.ds(r, p, stride=dilation), :] = \
                    acc_s[r * p:(r + 1) * p, c0 + c * LANES:c0 + (c + 1) * LANES]
            out_ref[:, out_c0 + c * LANES:out_c0 + (c + 1) * LANES] = xnf_s[c]

    seg = tm // SEGMENTS
    n_slab = D_MODEL // LANES

    @pl.when(j == 0)
    def _u_block():
        def norm(s, carry):
            r0 = pl.multiple_of(s * seg, seg)
            xv = x_ref[pl.ds(r0, seg), :]
            ms = jnp.mean(xv * xv, axis=-1, keepdims=True)
            xn = (xv * lax.rsqrt(ms + EPS)) * gpre_ref[...]
            xn_s[pl.ds(r0, seg), :] = xn.astype(BF16)
            for c in range(n_slab):
                piece = xn[:, c * LANES:(c + 1) * LANES]
                xnf_s[c, pl.ds(r0, seg), :] = piece
                xseg_s[c, pl.ds(s, seg, stride=SEGMENTS), :] = piece
            return carry

        lax.fori_loop(0, SEGMENTS, norm, 0)

        for c in range(n_slab):
            xnseg_s[:, c * LANES:(c + 1) * LANES] = xseg_s[c].astype(BF16)

        for dil, dst in ((d1, xn4_s), (d2, xn16_s)):
            p = tm // dil
            for r in range(dil):
                for c in range(n_slab):
                    dst[r * p:(r + 1) * p, c * LANES:(c + 1) * LANES] = \
                        xnf_s[c, pl.ds(r, p, stride=dil), :].astype(BF16)

        @pl.when(m == 0)
        def _reset():
            cprev_s[...] = jnp.zeros((8, D_RNN), F32)
            carry_s[...] = jnp.zeros((1, D_RNN), F32)

        u_s[...] = project(xnseg_s)

        sub = lax.broadcasted_iota(jnp.int32, (SEGMENTS, GATE_BLOCK), 0)
        for c in range(N_GATE_BLOCKS):
            cols = slice(c * GATE_BLOCK, (c + 1) * GATE_BLOCK)

            def wrapped(t):
                v = pltpu.roll(u_s[SEGMENTS * (seg + t):SEGMENTS * (seg + t + 1), cols], 1, axis=0)
                return jnp.where(sub == 0, cprev_s[8 + t:9 + t, cols], v)

            wrap = {t: wrapped(t) for t in range(1 - CONV_WIDTH, 0)}
            for rc in range(tm // GATE_ROWS):
                r0 = rc * GATE_ROWS
                xc = cb_ref[:, cols] + u_s[r0:r0 + GATE_ROWS, cols] * cw_ref[CONV_WIDTH - 1:CONV_WIDTH, cols]
                for sh in range(1, CONV_WIDTH):
                    if r0 == 0:
                        head = [wrap[t - sh] for t in range(sh)]
                        ush = jnp.concatenate(head + [u_s[0:GATE_ROWS - SEGMENTS * sh, cols]], axis=0)
                    else:
                        ush = u_s[r0 - SEGMENTS * sh:r0 - SEGMENTS * sh + GATE_ROWS, cols]
                    xc = xc + ush * cw_ref[CONV_WIDTH - 1 - sh:CONV_WIDTH - sh, cols]
                a, b = _lru_gates(xc, c, wa_ref, wx_ref, ba_ref, bx_ref, lam_ref)
                a_s[r0:r0 + GATE_ROWS, cols] = a
                b_s[r0:r0 + GATE_ROWS, cols] = b

        def scan(t, carry):
            hloc, cum = carry
            r0 = pl.multiple_of(t * SEGMENTS, SEGMENTS)
            av = a_s[pl.ds(r0, SEGMENTS), :]
            hloc = av * hloc + b_s[pl.ds(r0, SEGMENTS), :]
            cum = av * cum
            b_s[pl.ds(r0, SEGMENTS), :] = hloc
            a_s[pl.ds(r0, SEGMENTS), :] = cum
            return hloc, cum

        h_end, a_end = lax.fori_loop(
            0, seg, scan, (jnp.zeros((SEGMENTS, D_RNN), F32), jnp.ones((SEGMENTS, D_RNN), F32)), unroll=4)
        state = carry_s[...]
        for s in range(SEGMENTS):
            cin_s[s:s + 1, :] = state
            state = h_end[s:s + 1, :] + a_end[s:s + 1, :] * state
        carry_s[...] = state
        h_ref[...] = state
        for k in range(1, CONV_WIDTH):
            last = u_s[tm - SEGMENTS * (k - 1) - 1:tm - SEGMENTS * (k - 1), :]
            conv_ref[CONV_WIDTH - 1 - k:CONV_WIDTH - k, :] = last
            cprev_s[8 - k:9 - k, :] = last

    @pl.when(j == 1)
    def _z_rnn_block():
        sample_head(0)
        z = project(xnseg_s)
        split = (seg, SEGMENTS, D_RNN)
        h = b_s[...].reshape(split) + a_s[...].reshape(split) * cin_s[...][None]
        hz = (h.reshape(tm, D_RNN) * (z * _sigmoid(z))).astype(BF16)
        yr_s[...] = jnp.dot(hz, wro_ref[...], preferred_element_type=F32)

    @pl.when(j == 2)
    def _g_rnn_block():
        sample_head(1)
        mix = _sigmoid(project(xnseg_s)) * yr_s[...]
        for c in range(n_slab):
            xseg_s[c] = mix[:, c * LANES:(c + 1) * LANES]
        for s in range(SEGMENTS):
            for c in range(n_slab):
                mixr_ref[s * seg:(s + 1) * seg, c * LANES:(c + 1) * LANES] = \
                    xseg_s[c, pl.ds(s, seg, stride=SEGMENTS), :].astype(BF16)

    @pl.when(j == 3)
    def _g_attn_block():
        sample_head(2)
        sga_ref[...] = _sigmoid(project(xn_s)).astype(BF16)

    @pl.when(j == 4)
    def _zattn_q0_block():
        sample_head(3)
        acc_s[...] = project(xn_s)
        zg_ref[...] = acc_s[:, 0:half].astype(BF16)
        emit(q0_ref, half, 1)

    @pl.when(j == 5)
    def _k0_v0_block():
        sample_head(4)
        acc_s[...] = project(xn_s)
        emit(k0_ref, 0, 1)
        emit(v0_ref, half, 1)

        @pl.when(last_tile)
        def _():
            kv0_ref[...] = acc_s[tm - ATTN_GROUPS[0][0]:tm, :]

    @pl.when(j == 6)
    def _q1_k1_block():
        sample_head(5)
        acc_s[...] = project(xn4_s)
        emit(q1_ref, 0, d1)
        emit(k1_ref, half, d1)

        @pl.when(last_tile)
        def _():
            natural_tail(kv1_ref, 0, half, d1)

    @pl.when(j == 7)
    def _v1_q2_block():
        sample_head(6)
        acc_s[:, 0:half] =jnp.dot(xn4_s[...], w_ref[:, 0:half], preferred_element_type=F32)
        acc_s[:, half:TN] = jnp.dot(xn16_s[...], w_ref[:, half:TN], preferred_element_type=F32)
        emit(v1_ref, 0, d1)
        emit(q2_ref, half, d2)

        @pl.when(last_tile)
        def _():
            natural_tail(kv1_ref, half, 0, d1)

    @pl.when(j == 8)
    def _k2_v2_block():
        sample_head(7)
        acc_s[...] = project(xn16_s)
        emit(k2_ref, 0, d2)
        emit(v2_ref, half, d2)

        @pl.when(in_kv2)
        def _():
            natural_tail(kv2_ref, 0, 0, d2)
            natural_tail(kv2_ref, half, half, d2)


def _proj_rnn_prompt(x, w_in, gpre, cw, cb, wa, wx, ba, bx, lam, wro, qkv_s, caches, layer):
    B, S, D = x.shape
    tm = TM_PROMPT
    n_tiles = S // tm
    m0 = n_tiles - ATTN_GROUPS[2][0] // tm
    assert S % tm == 0 and tm == ATTN_GROUPS[1][0] and ATTN_GROUPS[2][0] % tm == 0 and S >= ATTN_GROUPS[2][0]
    N = qkv_s.shape[0]
    assert N == B * n_tiles and N_COL_BLOCKS == H_G + 1
    cache_views = [jnp.transpose(c, (0, 1, 3, 4, 5, 2)) for c in caches]
    sample_seq = lambda b, m, j: (b * n_tiles + m, 0, 0, 0)
    cache_head = lambda b, m, j: (layer, b * n_tiles + m, 0, jnp.clip(j - 1, 0, H_G - 1), 0, 0)

    const2 = lambda b, m, j: (0, 0)
    const3 = lambda b, m, j: (0, 0, 0)
    tile3 = lambda b, m, j: (b, m, 0)
    tile4 = lambda b, m, j: (b, 0, m, 0)
    per_seq = lambda b, m, j: (b, 0, 0)

    qkv_shapes, qkv_specs = [], []
    for _, dil in ATTN_GROUPS:
        for _ in range(3):
            qkv_shapes.append(jax.ShapeDtypeStruct((B, dil, S // dil, D_ATTN), BF16))
            qkv_specs.append(pl.BlockSpec((None, dil, tm // dil, D_ATTN), tile4))
    out_shape = tuple(qkv_shapes) + (
        jax.ShapeDtypeStruct((B, S, D_ATTN), BF16),
        jax.ShapeDtypeStruct((B, S, D), BF16),
        jax.ShapeDtypeStruct((B, S, D), BF16),
        jax.ShapeDtypeStruct((B, ATTN_GROUPS[0][0], 2 * D_ATTN), F32),
        jax.ShapeDtypeStruct((B, ATTN_GROUPS[1][0], 2 * D_ATTN), F32),
        jax.ShapeDtypeStruct((B, ATTN_GROUPS[2][0], 2 * D_ATTN), F32),
        jax.ShapeDtypeStruct((B, CONV_WIDTH - 1, D_RNN), F32),
        jax.ShapeDtypeStruct((B, 1, D_RNN), F32),
        jax.ShapeDtypeStruct((N, H_G, HEAD_DIM), F32),
    )
    out_specs = tuple(qkv_specs) + (
        pl.BlockSpec((None, tm, D_ATTN), tile3),
        pl.BlockSpec((None, tm, D), tile3),
        pl.BlockSpec((None, tm, D), tile3),
        pl.BlockSpec((None, ATTN_GROUPS[0][0], 2 * D_ATTN), per_seq),
        pl.BlockSpec((None, ATTN_GROUPS[1][0], 2 * D_ATTN), per_seq),
        pl.BlockSpec((None, tm, 2 * D_ATTN), lambda b, m, j: (b, jnp.maximum(m - m0, 0), 0)),
        pl.BlockSpec((None, CONV_WIDTH - 1, D_RNN), per_seq),
        pl.BlockSpec((None, 1, D_RNN), per_seq),
        pl.BlockSpec((None, H_G, HEAD_DIM), lambda b, m, j: (b * n_tiles + m, 0, 0)),
    )
    in_specs = [
        pl.BlockSpec((None, tm, D), tile3),
        pl.BlockSpec((D, TN), lambda b, m, j: (0, j)),
        pl.BlockSpec((1, D), const2),
        pl.BlockSpec((CONV_WIDTH, D_RNN), const2),
        pl.BlockSpec((1, D_RNN), const2),
        pl.BlockSpec((N_GATE_BLOCKS, GATE_BLOCK, GATE_BLOCK), const3),
        pl.BlockSpec((N_GATE_BLOCKS, GATE_BLOCK, GATE_BLOCK), const3),
        pl.BlockSpec((1, D_RNN), const2),
        pl.BlockSpec((1, D_RNN), const2),
        pl.BlockSpec((1, D_RNN), const2),
        pl.BlockSpec((D_RNN, D), const2),
        pl.BlockSpec((None, 3 * N_GROUPS, H_G, HEAD_DIM), sample_seq),
    ]
    in_specs += [pl.BlockSpec((None, None, 2, None, HEAD_DIM, window), cache_head) for window, _ in ATTN_GROUPS]
    scratch = [
        pltpu.VMEM((tm, D), BF16),
        pltpu.VMEM((D // LANES, tm, LANES), F32),
        pltpu.VMEM((D // LANES, tm, LANES), F32),
        pltpu.VMEM((tm, D), BF16),
        pltpu.VMEM((tm, D), BF16),
        pltpu.VMEM((tm, D), BF16),
        pltpu.VMEM((tm, D_RNN), F32),
        pltpu.VMEM((tm, D_RNN), F32),
        pltpu.VMEM((tm, D_RNN), F32),
        pltpu.VMEM((tm, D), F32),
        pltpu.VMEM((8, D_RNN), F32),
        pltpu.VMEM((SEGMENTS, D_RNN), F32),
        pltpu.VMEM((1, D_RNN), F32),
    ]
    return pl.pallas_call(
        functools.partial(_proj_rnn_kernel, tm=tm, n_tiles=n_tiles),
        grid=(B, n_tiles, N_COL_BLOCKS),
        in_specs=in_specs,
        out_specs=out_specs,
        out_shape=out_shape,
        scratch_shapes=scratch,
        compiler_params=pltpu.CompilerParams(
            dimension_semantics=("arbitrary", "arbitrary", "arbitrary"), vmem_limit_bytes=VMEM_LIMIT),
        name="proj_rnn_prompt",
    )(x, w_in, gpre, cw, cb, wa, wx, ba, bx, lam, wro, qkv_s, *cache_views)


def _sample_heads(local_heads, first_head, slope_of, qs_ref, cache_refs, att_ref):
    scale = HEAD_DIM ** -0.5
    ri = lax.broadcasted_iota(jnp.int32, (HEAD_DIM, HEAD_DIM), 0)
    ci = lax.broadcasted_iota(jnp.int32, (HEAD_DIM, HEAD_DIM), 1)
    eye = ri == ci

    def to_col(row):
        return jnp.sum(jnp.where(eye, jnp.broadcast_to(row, (HEAD_DIM, HEAD_DIM)), 0.0), axis=-1, keepdims=True)

    def to_row(col):
        return jnp.sum(jnp.where(eye, jnp.broadcast_to(col, (HEAD_DIM, HEAD_DIM)), 0.0), axis=0, keepdims=True)

    for i in local_heads:
        h = first_head + i
        outs, lses = [], []
        for g, (window, dilation) in enumerate(ATTN_GROUPS):
            c_ref = cache_refs[g]
            pos = lax.broadcasted_iota(jnp.int32, (1, window), 1)
            dist = (window - pos).astype(F32)
            bias = jnp.where((pos % dilation) == 0, -slope_of(g, i) * dist, NEG_INF)
            q = qs_ref[g, pl.ds(h, 1), :] * scale
            s_new = jnp.sum(qs_ref[N_GROUPS + g, pl.ds(h, 1), :] * q, axis=-1, keepdims=True)
            v_col = to_col(qs_ref[2 * N_GROUPS + g, pl.ds(h, 1), :])
            s = jnp.sum(c_ref[0, i] * to_col(q), axis=0, keepdims=True) + bias
            mx = jnp.maximum(jnp.max(s, axis=-1, keepdims=True), s_new)
            e = jnp.exp(s - mx)
            e_new = jnp.exp(s_new - mx)
            l = jnp.sum(e, axis=-1, keepdims=True) + e_new
            o = jnp.sum(c_ref[1, i] * e, axis=-1, keepdims=True) + e_new * v_col
            outs.append(o / l)
            lses.append(mx + jnp.log(l))
        mm = jnp.maximum(jnp.maximum(lses[0], lses[1]), lses[2])
        ws = [jnp.exp(x - mm) for x in lses]
        col = (outs[0] * ws[0] + outs[1] * ws[1] + outs[2] * ws[2]) / (ws[0] + ws[1] + ws[2])
        att_ref[pl.ds(h, 1), :] = to_row(col)


def _proj_rnn2_kernel(x_ref, w_ref, gpre_ref, cw_ref, cb_ref, wa_ref, wx_ref, ba_ref, bx_ref, lam_ref, wro_ref,
                      qs_ref, c0_ref, c1_ref, c2_ref,
                      q0_ref, k0_ref, v0_ref, q1_ref, k1_ref, v1_ref, q2_ref, k2_ref, v2_ref,
                      zg_ref, mixr_ref, sga_ref, kv0_ref, kv1_ref, kv2_ref, conv_ref, h_ref, att_ref,
                      xn_s, xnf_s, xseg_s, xnseg_s, xn4_s, xn16_s, u_s, a_s, b_s, zg_s, cprev_s, carry_s,
                      *, tm, n_tiles):
    m = pl.program_id(1)
    j = pl.program_id(2)
    tile = pl.program_id(0) * n_tiles + m
    half = TN // 2
    d1, d2 = ATTN_GROUPS[1][1], ATTN_GROUPS[2][1]
    seg = tm // SEGMENTS
    n_slab = D_MODEL // LANES
    heads_per_step = H_G // (TILES_PER_SAMPLE * 2)

    def project(lhs_s, col, width=half):
        return jnp.dot(lhs_s[...], w_ref[:, col:col + width], preferred_element_type=F32)

    def emit(out_ref, val, c0, dilation):
        p = tm // dilation
        for r in range(dilation):
            out_ref[r] = val[r * p:(r + 1) * p, c0:c0 + half].astype(BF16)

    def natural_tail(out_ref, out_c0, val, c0, dilation):
        p = tm // dilation
        for c in range(half // LANES):
            for r in range(dilation):
                xnf_s[c, pl.ds(r, p, stride=dilation), :] = val[r * p:(r + 1) * p, c0 + c * LANES:c0 + (c + 1) * LANES]
            out_ref[:, out_c0 + c * LANES:out_c0 + (c + 1) * LANES] = xnf_s[c]

    def sample_heads(step):
        part = tile % TILES_PER_SAMPLE
        first = part * (heads_per_step * 2) + step * heads_per_step

        def slope_of(g, i):
            table = [float(_SLOPES[g, t * heads_per_step * 2 + step * heads_per_step + i])
                     for t in range(TILES_PER_SAMPLE)]
            out = jnp.float32(table[0])
            for t in range(1, TILES_PER_SAMPLE):
                out = jnp.where(part == t, jnp.float32(table[t]), out)
            return out

        _sample_heads(range(heads_per_step), first, slope_of, qs_ref, (c0_ref, c1_ref, c2_ref), att_ref)

    @pl.when(j == 0)
    def _recurrent_blocks():
        @pl.when(m == 0)
        def _reset():
            cprev_s[...] = jnp.zeros((8, D_RNN), F32)
            carry_s[...] = jnp.zeros((1, D_RNN), F32)

        for s in range(SEGMENTS):
            r0 = s * seg
            xv = x_ref[r0:r0 + seg, :]
            ms = jnp.mean(xv * xv, axis=-1, keepdims=True)
            xn = (xv * lax.rsqrt(ms + EPS)) * gpre_ref[...]
            xn_s[r0:r0 + seg, :] = xn.astype(BF16)
            for c in range(n_slab):
                piece = xn[:, c * LANES:(c + 1) * LANES]
                xnf_s[c, r0:r0 + seg, :] = piece
                xseg_s[c, pl.ds(s, seg, stride=SEGMENTS), :] = piece

        for c in range(n_slab):
            xnseg_s[:, c * LANES:(c + 1) * LANES] = xseg_s[c].astype(BF16)

        u_s[...] = project(xnseg_s, COL_U, D_RNN)

        sub = lax.broadcasted_iota(jnp.int32, (SEGMENTS, GATE_BLOCK), 0)
        gate_rows = min(GATE_ROWS, tm)
        n_pieces = N_GATE_BLOCKS * (tm // gate_rows)
        piece_w = (D_RNN + D_MODEL) // n_pieces
        assert D_RNN % piece_w == 0

        def zg_col(piece):
            col = piece * piece_w
            return COL_Z_RNN + col if col < D_RNN else COL_G_RNN + col - D_RNN

        for c in range(N_GATE_BLOCKS):
            cols = slice(c * GATE_BLOCK, (c + 1) * GATE_BLOCK)

            def wrapped(t):
                v = pltpu.roll(u_s[SEGMENTS * (seg + t):SEGMENTS * (seg + t + 1), cols], 1, axis=0)
                return jnp.where(sub == 0, cprev_s[8 + t:9 + t, cols], v)

            wrap = {t: wrapped(t) for t in range(1 - CONV_WIDTH, 0)}
            for rc in range(tm // gate_rows):
                r0 = rc * gate_rows
                xc = cb_ref[:, cols] + u_s[r0:r0 + gate_rows, cols] * cw_ref[CONV_WIDTH - 1:CONV_WIDTH, cols]
                for sh in range(1, CONV_WIDTH):
                    if r0 == 0:
                        head = [wrap[t - sh] for t in range(sh)]
                        ush = jnp.concatenate(head + [u_s[0:gate_rows - SEGMENTS * sh, cols]], axis=0)
                    else:
                        ush = u_s[r0 - SEGMENTS * sh:r0 - SEGMENTS * sh + gate_rows, cols]
                    xc = xc + ush * cw_ref[CONV_WIDTH - 1 - sh:CONV_WIDTH - sh, cols]
                a, b = _lru_gates(xc, c, wa_ref, wx_ref, ba_ref, bx_ref, lam_ref)
                a_s[r0:r0 + gate_rows, cols] = a
                b_s[r0:r0 + gate_rows, cols] = b
                piece = c * (tm // gate_rows) + rc
                pcols = slice(piece * piece_w, (piece + 1) * piece_w)
                zg_s[:, pcols] = project(xnseg_s, zg_col(piece), piece_w)

        hloc = jnp.zeros((SEGMENTS, D_RNN), F32)
        cum = jnp.ones((SEGMENTS, D_RNN), F32)
        for t in range(seg):
            rows = slice(t * SEGMENTS, (t + 1) * SEGMENTS)
            av = a_s[rows, :]
            hloc = av * hloc + b_s[rows, :]
            cum = av * cum
            b_s[rows, :] = hloc
            a_s[rows, :] = cum
        state = carry_s[...]
        enter = []
        for s in range(SEGMENTS):
            enter.append(state)
            state = hloc[s:s + 1, :] + cum[s:s + 1, :] * state
        enter = jnp.concatenate(enter, axis=0)
        carry_s[...] = state
        h_ref[...] = state
        for k in range(1, CONV_WIDTH):
            last = u_s[tm - SEGMENTS * (k - 1) - 1:tm - SEGMENTS * (k - 1), :]
            conv_ref[CONV_WIDTH - 1 - k:CONV_WIDTH - k, :] = last
            cprev_s[8 - k:9 - k, :] = last

        z = zg_s[:, 0:D_RNN]
        split = (seg, SEGMENTS, D_RNN)
        h = b_s[...].reshape(split) + a_s[...].reshape(split) * enter[None]
        hz = (h.reshape(tm, D_RNN) * (z * _sigmoid(z))).astype(BF16)
        yr = jnp.dot(hz, wro_ref[...], preferred_element_type=F32)

        mix = _sigmoid(zg_s[:, D_RNN:D_RNN + D_MODEL]) * yr
        for c in range(n_slab):
            xseg_s[c] = mix[:, c * LANES:(c + 1) * LANES]
        for s in range(SEGMENTS):
            for c in range(n_slab):
                mixr_ref[s * seg:(s + 1) * seg, c * LANES:(c + 1) * LANES] = \
                    xseg_s[c, pl.ds(s, seg, stride=SEGMENTS), :].astype(BF16)

        sample_heads(0)

    @pl.when(j == 1)
    def _attention_blocks():
        for dil, dst in ((d1, xn4_s), (d2, xn16_s)):
            p = tm // dil
            for r in range(dil):
                for c in range(n_slab):
                    dst[r * p:(r + 1) * p, c * LANES:(c + 1) * LANES] = \
                        xnf_s[c, pl.ds(r, p, stride=dil), :].astype(BF16)

        sga_ref[...] = _sigmoid(project(xn_s, COL_G_ATTN, D_MODEL)).astype(BF16)
        zg_ref[...] = project(xn_s, COL_Z_ATTN).astype(BF16)

        lhs = (xn_s, xn4_s, xn16_s)
        qkv_refs = ((q0_ref, k0_ref, v0_ref), (q1_ref, k1_ref, v1_ref), (q2_ref, k2_ref, v2_ref))
        for g, (_, dil) in enumerate(ATTN_GROUPS):
            q_ref, k_ref, v_ref = qkv_refs[g]
            emit(q_ref, project(lhs[g], COL_Q + g * D_ATTN), 0, dil)
            for col, out_ref, c0 in ((COL_K, k_ref, 0), (COL_V, v_ref, half)):
                r = project(lhs[g], col + g * D_ATTN)
                emit(out_ref, r, 0, dil)
                if g == 0:
                    kv0_ref[:, c0:c0 + half] = r[tm - ATTN_GROUPS[0][0]:tm, :]
                else:
                    natural_tail(kv1_ref if g == 1 else kv2_ref, c0, r, 0, dil)
        sample_heads(1)


def _proj_rnn_prompt2(x, w_in, gpre, cw, cb, wa, wx, ba, bx, lam, wro, qkv_s, caches, layer):
    B, S, D = x.shape
    tm = TM_PROMPT2
    n_tiles = S // tm
    N = qkv_s.shape[0]
    heads_per_step = H_G // (TILES_PER_SAMPLE * 2)
    assert S % tm == 0 and tm % (SEGMENTS * 16) == 0 and tm >= ATTN_GROUPS[0][0]
    assert all(w % tm == 0 and S >= w for w, _ in ATTN_GROUPS[1:])
    assert B * n_tiles == N * TILES_PER_SAMPLE and n_tiles % TILES_PER_SAMPLE == 0
    cache_views = [jnp.transpose(c, (0, 1, 3, 4, 5, 2)) for c in caches]

    const2 = lambda b, m, j: (0, 0)
    const3 = lambda b, m, j: (0, 0, 0)
    tile3 = lambda b, m, j: (b, m, 0)
    tile4 = lambda b, m, j: (b, 0, m, 0)
    per_seq = lambda b, m, j: (b, 0, 0)
    sample_of = lambda b, m: (b * n_tiles + m) // TILES_PER_SAMPLE
    head_block = lambda b, m, j: ((b * n_tiles + m) % TILES_PER_SAMPLE) * 2 + j

    def last_rows(window):
        first_tile = n_tiles - window // tm
        return lambda b, m, j: (b, jnp.maximum(m - first_tile, 0), 0)

    resident = dict(pipeline_mode=pl.Buffered(1))
    qkv_shapes, qkv_specs = [], []
    for _, dil in ATTN_GROUPS:
        for _ in range(3):
            qkv_shapes.append(jax.ShapeDtypeStruct((B, dil, S // dil, D_ATTN), BF16))
            qkv_specs.append(pl.BlockSpec((None, dil, tm // dil, D_ATTN), tile4))
    out_shape = tuple(qkv_shapes) + (
        jax.ShapeDtypeStruct((B, S, D_ATTN), BF16),
        jax.ShapeDtypeStruct((B, S, D), BF16),
        jax.ShapeDtypeStruct((B, S, D), BF16),
        jax.ShapeDtypeStruct((B, ATTN_GROUPS[0][0], 2 * D_ATTN), F32),
        jax.ShapeDtypeStruct((B, ATTN_GROUPS[1][0], 2 * D_ATTN), F32),
        jax.ShapeDtypeStruct((B, ATTN_GROUPS[2][0], 2 * D_ATTN), F32),
        jax.ShapeDtypeStruct((B, CONV_WIDTH - 1, D_RNN), F32),
        jax.ShapeDtypeStruct((B, 1, D_RNN), F32),
        jax.ShapeDtypeStruct((N, H_G, HEAD_DIM), F32),
    )
    out_specs = tuple(qkv_specs) + (
        pl.BlockSpec((None, tm, D_ATTN), tile3),
        pl.BlockSpec((None, tm, D), tile3),
        pl.BlockSpec((None, tm, D), tile3),
        pl.BlockSpec((None, ATTN_GROUPS[0][0], 2 * D_ATTN), per_seq),
        pl.BlockSpec((None, tm, 2 * D_ATTN), last_rows(ATTN_GROUPS[1][0])),
        pl.BlockSpec((None, tm, 2 * D_ATTN), last_rows(ATTN_GROUPS[2][0])),
        pl.BlockSpec((None, CONV_WIDTH - 1, D_RNN), per_seq),
        pl.BlockSpec((None, 1, D_RNN), per_seq),
        pl.BlockSpec((None, H_G, HEAD_DIM), lambda b, m, j: (sample_of(b, m), 0, 0)),
    )
    in_specs = [
        pl.BlockSpec((None, tm, D), tile3),
        pl.BlockSpec((D, D_IN), const2, **resident),
        pl.BlockSpec((1, D), const2),
        pl.BlockSpec((CONV_WIDTH, D_RNN), const2),
        pl.BlockSpec((1, D_RNN), const2),
        pl.BlockSpec((N_GATE_BLOCKS, GATE_BLOCK, GATE_BLOCK), const3),
        pl.BlockSpec((N_GATE_BLOCKS, GATE_BLOCK, GATE_BLOCK), const3),
        pl.BlockSpec((1, D_RNN), const2),
        pl.BlockSpec((1, D_RNN), const2),
        pl.BlockSpec((1, D_RNN), const2),
        pl.BlockSpec((D_RNN, D), const2, **resident),
        pl.BlockSpec((None, 3 * N_GROUPS, H_G, HEAD_DIM), lambda b, m, j: (sample_of(b, m), 0, 0, 0)),
    ]
    in_specs += [pl.BlockSpec((None, None, 2, heads_per_step, HEAD_DIM, window),
                              lambda b, m, j: (layer, sample_of(b, m), 0, head_block(b, m, j), 0, 0))
                 for window, _ in ATTN_GROUPS]
    scratch = [
        pltpu.VMEM((tm, D), BF16),
        pltpu.VMEM((D // LANES, tm, LANES), F32),
        pltpu.VMEM((D // LANES, tm, LANES), F32),
        pltpu.VMEM((tm, D), BF16),
        pltpu.VMEM((tm, D), BF16),
        pltpu.VMEM((tm, D), BF16),
        pltpu.VMEM((tm, D_RNN), F32),
        pltpu.VMEM((tm, D_RNN), F32),
        pltpu.VMEM((tm, D_RNN), F32),
        pltpu.VMEM((tm, D_RNN + D), F32),
        pltpu.VMEM((8, D_RNN), F32),
        pltpu.VMEM((1, D_RNN), F32),
    ]
    return pl.pallas_call(
        functools.partial(_proj_rnn2_kernel, tm=tm, n_tiles=n_tiles),
        grid=(B, n_tiles, 2),
        in_specs=in_specs,
        out_specs=out_specs,
        out_shape=out_shape,
        scratch_shapes=scratch,
        compiler_params=pltpu.CompilerParams(
            dimension_semantics=("arbitrary", "arbitrary", "arbitrary"), vmem_limit_bytes=VMEM_LIMIT),
        name="proj_rnn_prompt",
    )(x, w_in, gpre, cw, cb, wa, wx, ba, bx, lam, wro, qkv_s, *cache_views)


def _proj_rnn_sample_kernel(x_ref, w_ref, gpre_ref, cw_ref, cb_ref, wa_ref, wx_ref, ba_ref, bx_ref, lam_ref, wro_ref,
                            sc_ref, h0_ref,
                            att_ref, mixr_ref, sga_ref, conv_ref, h_ref,
                            xn_s, h_s, yr_s):
    j = pl.program_id(0)

    def project():
        return jnp.dot(xn_s[...], w_ref[...], preferred_element_type=F32)

    @pl.when(j == 0)
    def _u_block():
        xv = x_ref[...]
        ms = jnp.mean(xv * xv, axis=-1, keepdims=True)
        xn_s[...] = ((xv * lax.rsqrt(ms + EPS)) * gpre_ref[...]).astype(BF16)
        u = project()
        taps = [sc_ref[:, k * D_RNN:(k + 1) * D_RNN] for k in range(CONV_WIDTH - 1)] + [u]
        for c in range(N_GATE_BLOCKS):
            c0 = c * GATE_BLOCK
            xc = cb_ref[:, c0:c0 + GATE_BLOCK]
            for tap in range(CONV_WIDTH):
                xc = xc + taps[tap][:, c0:c0 + GATE_BLOCK] * cw_ref[tap:tap + 1, c0:c0 + GATE_BLOCK]
            a, b = _lru_gates(xc, c, wa_ref, wx_ref, ba_ref, bx_ref, lam_ref)
            h = a * h0_ref[:, c0:c0 + GATE_BLOCK] + b
            h_s[:, c0:c0 + GATE_BLOCK] = h
            h_ref[:, c0:c0 + GATE_BLOCK] = h
        for k in range(1, CONV_WIDTH):
            conv_ref[:, (k - 1) * D_RNN:k * D_RNN] = taps[k]

    @pl.when(j == 1)
    def _z_rnn_block():
        z = project()
        hz = (h_s[...] * (z * _sigmoid(z))).astype(BF16)
        yr_s[...] = jnp.dot(hz, wro_ref[...], preferred_element_type=F32)

    @pl.when(j == 2)
    def _g_rnn_block():
        mixr_ref[...] = _sigmoid(project()) * yr_s[...]

    @pl.when(j == 3)
    def _g_attn_block():
        sga_ref[...] = _sigmoid(project())

    @pl.when(j >= 4)
    def _attn_blocks():
        att_ref[...] = project()


def _proj_rnn_sample(x, w_in, gpre, cw, cb, wa, wx, ba, bx, lam, wro, sconv, h0):
    N, D = x.shape
    const2 = lambda j: (0, 0)
    const3 = lambda j: (0, 0, 0)
    n_att = N_COL_BLOCKS - 4
    first_att, g_rnn_block = COL_Q // TN, COL_G_RNN // TN
    assert COL_Q % TN == 0 and COL_G_RNN % TN == 0 and COL_G_RNN - COL_Q == n_att * TN
    w_block = lambda j: jnp.where(j < 2, j, jnp.where(j < 4, j + g_rnn_block - 2, j - 4 + first_att))
    out_shape = (
        jax.ShapeDtypeStruct((N, n_att * TN), F32),
        jax.ShapeDtypeStruct((N, D), F32),
        jax.ShapeDtypeStruct((N, D), F32),
        jax.ShapeDtypeStruct((N, (CONV_WIDTH - 1) * D_RNN), F32),
        jax.ShapeDtypeStruct((N, D_RNN), F32),
    )
    out_specs = (
        pl.BlockSpec((N, TN), lambda j: (0, jnp.maximum(j - 4, 0))),
        pl.BlockSpec((N, D), const2),
        pl.BlockSpec((N, D), const2),
        pl.BlockSpec((N, (CONV_WIDTH - 1) * D_RNN), const2),
        pl.BlockSpec((N, D_RNN), const2),
    )
    in_specs = [
        pl.BlockSpec((N, D), const2),
        pl.BlockSpec((D, TN), lambda j: (0, w_block(j))),
        pl.BlockSpec((1, D), const2),
        pl.BlockSpec((CONV_WIDTH, D_RNN), const2),
        pl.BlockSpec((1, D_RNN), const2),
        pl.BlockSpec((N_GATE_BLOCKS, GATE_BLOCK, GATE_BLOCK), const3),
        pl.BlockSpec((N_GATE_BLOCKS, GATE_BLOCK, GATE_BLOCK), const3),
        pl.BlockSpec((1, D_RNN), const2),
        pl.BlockSpec((1, D_RNN), const2),
        pl.BlockSpec((1, D_RNN), const2),
        pl.BlockSpec((D_RNN, D), const2),
        pl.BlockSpec((N, (CONV_WIDTH - 1) * D_RNN), const2),
        pl.BlockSpec((N, D_RNN), const2),
    ]
    scratch = [pltpu.VMEM((N, D), BF16), pltpu.VMEM((N, D_RNN), F32), pltpu.VMEM((N, D), F32)]
    return pl.pallas_call(
        _proj_rnn_sample_kernel,
        grid=(N_COL_BLOCKS,),
        in_specs=in_specs,
        out_specs=out_specs,
        out_shape=out_shape,
        scratch_shapes=scratch,
        compiler_params=pltpu.CompilerParams(dimension_semantics=("arbitrary",), vmem_limit_bytes=VMEM_LIMIT),
        name="proj_rnn_sample",
    )(x, w_in, gpre, cw, cb, wa, wx, ba, bx, lam, wro, sconv, h0)


def _attn_kernel(q_ref, kc_ref, vc_ref, kp_ref, vp_ref, o_ref, st_ref, bias_s, *, group, chunk, n_res):
    dilation = ATTN_GROUPS[group][1]
    c = pl.program_id(2)
    nt = (((1,), (1,)), ((), ()))

    @pl.when((pl.program_id(0) == 0) & (pl.program_id(1) == 0) & (c == 0))
    def _init_bias():
        qi = lax.broadcasted_iota(jnp.int32, (Q_BLOCK, Q_BLOCK), 0)
        kj = lax.broadcasted_iota(jnp.int32, (Q_BLOCK, Q_BLOCK), 1)
        steps_prev = Q_BLOCK + qi - kj
        steps_cur = qi - kj
        dist_prev = (steps_prev * dilation).astype(F32)
        dist_cur = (steps_cur * dilation).astype(F32)
        for h in range(H_G):
            slope = float(_SLOPES[group, h])
            bias_s[h] = jnp.where(steps_cur >= 0, -slope * dist_cur, NEG_INF)
            bias_s[H_G + h] = jnp.where(steps_prev <= Q_BLOCK, -slope * dist_prev, NEG_INF)
            bias_s[2 * H_G + h] = jnp.full((Q_BLOCK, Q_BLOCK), NEG_INF, F32)

    lane = lax.broadcasted_iota(jnp.int32, (Q_BLOCK, LANES), 1)
    low = lane < HEAD_DIM
    lane2 = lax.broadcasted_iota(jnp.int32, (2 * Q_BLOCK, LANES), 1)
    low2 = lane2 < HEAD_DIM
    ones_lo = jnp.where(low2, 1.0, 0.0).astype(BF16)
    ones_hi = jnp.where(low2, 0.0, 1.0).astype(BF16)
    stat_lane = lax.broadcasted_iota(jnp.int32, (Q_BLOCK, STAT_LANES), 1)

    def keys_of(res, block, cur_ref, prev_ref, sl):
        if block == 0:
            return jnp.concatenate([prev_ref[res, :, sl], cur_ref[res, 0:Q_BLOCK, sl]], axis=0)
        return cur_ref[res, (block - 1) * Q_BLOCK:(block + 1) * Q_BLOCK, sl]

    for res, i in [(res, i) for res in range(n_res) for i in range(chunk // Q_BLOCK)]:
        r0 = i * Q_BLOCK
        first = jnp.where(c == 0, H_G, 0) if i == 0 else 0
        stats = jnp.zeros((Q_BLOCK, STAT_LANES), F32)
        for p in range(H_G // 2):
            sl = slice(LANES * p, LANES * (p + 1))
            qp = q_ref[res, r0:r0 + Q_BLOCK, sl] * (HEAD_DIM ** -0.5)
            kp = keys_of(res, i, kc_ref, kp_ref, sl)
            vp = keys_of(res, i, vc_ref, vp_ref, sl)
            es, ms = [], []
            for hh in range(2):
                h = 2 * p + hh
                msk = low if hh == 0 else jnp.logical_not(low)
                qm = jnp.where(msk, qp, jnp.zeros_like(qp))
                s = lax.dot_general(qm, kp, nt, preferred_element_type=F32)
                s_p = s[:, 0:Q_BLOCK] + bias_s[H_G + h + first]
                s_c = s[:, Q_BLOCK:2 * Q_BLOCK] + bias_s[h]
                mx = jnp.maximum(jnp.max(s_p, axis=-1, keepdims=True), jnp.max(s_c, axis=-1, keepdims=True))
                es.append(jnp.exp(s_p - mx).astype(BF16))
                es.append(jnp.exp(s_c - mx).astype(BF16))
                ms.append(mx)
            vm0 = jnp.where(low2, vp, jnp.zeros_like(vp))
            vm1 = jnp.where(low2, jnp.zeros_like(vp), vp)
            w = jnp.concatenate([jnp.concatenate([vm0, ones_lo], axis=1),
                                 jnp.concatenate([vm1, ones_hi], axis=1)], axis=0)
            acc = jnp.dot(jnp.concatenate(es, axis=1), w, preferred_element_type=F32)
            l_pair = acc[:, LANES:2 * LANES]
            o_ref[res, r0:r0 + Q_BLOCK, sl] = (acc[:, 0:LANES] / l_pair).astype(o_ref.dtype)
            lse_pair = jnp.where(low, ms[0], ms[1]) + jnp.log(l_pair)
            keep = ((stat_lane % (STAT_LANES // 2)) // STAT_LANES_PER_HEAD) == p
            stats = jnp.where(keep, lse_pair, stats)
        st_ref[res, r0:r0 + Q_BLOCK, :] = stats


def _attn_group(q, k, v, group):
    B, dil, L, _ = q.shape
    chunk = min(ATTN_CHUNK, L)
    n_res = min(ATTN_CHUNK // chunk, dil)
    assert L % chunk == 0 and chunk % Q_BLOCK == 0 and dil % n_res == 0
    cur = lambda b, r, c: (b, r, c, 0)
    prev = lambda b, r, c: (b, r, jnp.maximum(c * (chunk // Q_BLOCK) - 1, 0), 0)
    blk = (None, n_res, chunk, D_ATTN)
    pblk = (None, n_res, Q_BLOCK, D_ATTN)
    return pl.pallas_call(
        functools.partial(_attn_kernel, group=group, chunk=chunk, n_res=n_res),
        grid=(B, dil // n_res, L // chunk),
        in_specs=[pl.BlockSpec(blk, cur), pl.BlockSpec(blk, cur), pl.BlockSpec(blk, cur),
                  pl.BlockSpec(pblk, prev), pl.BlockSpec(pblk, prev)],
        out_specs=[pl.BlockSpec(blk, cur), pl.BlockSpec((None, n_res, chunk, STAT_LANES), cur)],
        out_shape=[jax.ShapeDtypeStruct((B, dil, L, D_ATTN), BF16),
                   jax.ShapeDtypeStruct((B, dil, L, STAT_LANES), F32)],
        scratch_shapes=[pltpu.VMEM((3 * H_G, Q_BLOCK, Q_BLOCK), F32)],
        compiler_params=pltpu.CompilerParams(
            dimension_semantics=("arbitrary", "arbitrary", "arbitrary"), vmem_limit_bytes=VMEM_LIMIT),
        name=f"attn_group{group}",
    )(q, k, v, k, v)


def _finish(att, rows, z_ref, sga_ref, mixr_ref, x_ref, wao_ref, wo_ref, gpost_ref, y_ref):
    z = z_ref[rows, :].astype(F32)
    ya_in = (att * (z * _sigmoid(z))).astype(BF16)
    ya = jnp.dot(ya_in, wao_ref[...], preferred_element_type=F32)
    mixed = mixr_ref[rows, :].astype(F32) + sga_ref[rows, :].astype(F32) * ya
    out = jnp.dot(mixed.astype(BF16), wo_ref[...], preferred_element_type=F32)
    ms = jnp.mean(out * out, axis=-1, keepdims=True)
    y_ref[rows, :] = x_ref[rows, :] + (out * lax.rsqrt(ms + EPS)) * gpost_ref[...]


def _out_kernel(att_ref, z_ref, sga_ref, mixr_ref, x_ref, wao_ref, wo_ref, gpost_ref, y_ref):
    rows = slice(0, x_ref.shape[0])
    _finish(att_ref[...].astype(F32), rows, z_ref, sga_ref, mixr_ref, x_ref, wao_ref, wo_ref, gpost_ref, y_ref)


def _merge_out_kernel(o0_ref, o1_ref, o2_ref, s0_ref, s1_ref, s2_ref, z_ref, sga_ref, mixr_ref, x_ref,
                      wao_ref, wo_ref, gpost_ref, y_ref, nat1_s, nat2_s, st1_s, st2_s, *, tm):
    for o_ref, s_ref, nat_s, stn_s, dil in ((o1_ref, s1_ref, nat1_s, st1_s, ATTN_GROUPS[1][1]),
                                            (o2_ref, s2_ref, nat2_s, st2_s, ATTN_GROUPS[2][1])):
        p = tm // dil
        for r in range(dil):
            stn_s[pl.ds(r, p, stride=dil), :] = s_ref[r]
            for c in range(D_ATTN // LANES):
                nat_s[c, pl.ds(r, p, stride=dil), :] = o_ref[r, :, c * LANES:(c + 1) * LANES].astype(F32)

    src = lax.broadcasted_iota(jnp.int32, (STAT_LANES, D_ATTN), 0)
    dst_head = lax.broadcasted_iota(jnp.int32, (STAT_LANES, D_ATTN), 1) // HEAD_DIM
    expand = jnp.where(src == (dst_head % 2) * (STAT_LANES // 2) + (dst_head // 2) * STAT_LANES_PER_HEAD,
                       1.0, 0.0).astype(BF16)
    for k in range(tm // OUT_ROWS):
        rows = slice(k * OUT_ROWS, (k + 1) * OUT_ROWS)
        lse = [s0_ref[rows, :], st1_s[rows, :], st2_s[rows, :]]
        mm = jnp.maximum(jnp.maximum(lse[0], lse[1]), lse[2])
        ws = [jnp.exp(x - mm) for x in lse]
        den = ws[0] + ws[1] + ws[2]
        outs = [o0_ref[rows, :].astype(F32),
                jnp.concatenate([nat1_s[c, rows, :] for c in range(D_ATTN // LANES)], axis=1),
                jnp.concatenate([nat2_s[c, rows, :] for c in range(D_ATTN // LANES)], axis=1)]
        att = jnp.zeros((OUT_ROWS, D_ATTN), F32)
        for g in range(N_GROUPS):
            wexp = jnp.dot((ws[g] / den).astype(BF16), expand, preferred_element_type=F32)
            att = att + wexp * outs[g]
        _finish(att, rows, z_ref, sga_ref, mixr_ref, x_ref, wao_ref, wo_ref, gpost_ref, y_ref)


def _merge_out_prompt(os, sts, zg, sga, mixr, x, wao, wo, gpost):
    B, S, D = x.shape
    tm = TM_OUT
    tile3 = lambda b, m: (b, m, 0)
    tile4 = lambda b, m: (b, 0, m, 0)
    const = lambda b, m: (0, 0)
    in_specs = []
    for (_, dil), width in [(g, D_ATTN) for g in ATTN_GROUPS] + [(g, STAT_LANES) for g in ATTN_GROUPS]:
        if dil == 1:
            in_specs.append(pl.BlockSpec((None, None, tm, width), tile4))
        else:
            in_specs.append(pl.BlockSpec((None, dil, tm // dil, width), tile4))
    in_specs += [
        pl.BlockSpec((None, tm, D_ATTN), tile3),
        pl.BlockSpec((None, tm, D), tile3),
        pl.BlockSpec((None, tm, D), tile3),
        pl.BlockSpec((None, tm, D), tile3),
        pl.BlockSpec((D_ATTN, D), const),
        pl.BlockSpec((D, D), const),
        pl.BlockSpec((1, D), const),
    ]
    scratch = [
        pltpu.VMEM((D_ATTN // LANES, tm, LANES), F32),
        pltpu.VMEM((D_ATTN // LANES, tm, LANES), F32),
        pltpu.VMEM((tm, STAT_LANES), F32),
        pltpu.VMEM((tm, STAT_LANES), F32),
    ]
    return pl.pallas_call(
        functools.partial(_merge_out_kernel, tm=tm),
        grid=(B, S // tm),
        in_specs=in_specs,
        out_specs=pl.BlockSpec((None, tm, D), tile3),
        out_shape=jax.ShapeDtypeStruct((B, S, D), F32),
        scratch_shapes=scratch,
        compiler_params=pltpu.CompilerParams(
            dimension_semantics=("arbitrary", "arbitrary"), vmem_limit_bytes=VMEM_LIMIT),
        name="merge_out_prompt",
    )(*os, *sts, zg, sga, mixr, x, wao, wo, gpost)


def _out_sample(att, z, sga, mixr, x, wao, wo, gpost):
    N, D = x.shape
    full = lambda shape: pl.BlockSpec(shape, lambda i: (0, 0))
    return pl.pallas_call(
        _out_kernel,
        grid=(1,),
        in_specs=[full((N, D_ATTN)), full((N, D_ATTN)), full((N, D)), full((N, D)), full((N, D)),
                  full((D_ATTN, D)), full((D, D)), full((1, D))],
        out_specs=full((N, D)),
        out_shape=jax.ShapeDtypeStruct((N, D), F32),
        compiler_params=pltpu.CompilerParams(dimension_semantics=("arbitrary",), vmem_limit_bytes=VMEM_LIMIT),
        name="out_sample",
    )(att, z, sga, mixr, x, wao, wo, gpost)


def _block_diag_chunks(w):
    per = GATE_BLOCK // RNN_BLOCK
    w = w.reshape(N_GATE_BLOCKS, per, RNN_BLOCK, RNN_BLOCK)
    eye = jnp.eye(per, dtype=w.dtype)
    dense = w[:, :, :, None, :] * eye[None, :, None, :, None]
    return dense.reshape(N_GATE_BLOCKS, GATE_BLOCK, GATE_BLOCK)


def _reorder_w_in(w_in):
    sizes = (D_RNN, D_RNN, N_GROUPS * D_ATTN, N_GROUPS * D_ATTN, N_GROUPS * D_ATTN, D_ATTN, D_MODEL, D_MODEL)
    starts = np.concatenate([[0], np.cumsum(sizes)[:-1]])
    u0, z0, q0, k0, v0, za0, gr0, ga0 = (int(s) for s in starts)
    pieces = [(u0, D_RNN), (z0, D_RNN), (gr0, D_MODEL), (ga0, D_MODEL), (za0, D_ATTN)]
    for g in range(N_GROUPS):
        pieces += [(q0 + g * D_ATTN, D_ATTN), (k0 + g * D_ATTN, D_ATTN), (v0 + g * D_ATTN, D_ATTN)]
    return jnp.concatenate([w_in[:, s:s + n] for s, n in pieces], axis=1).astype(BF16)


def _layer(layer, yp, ys, sconv, h0, caches, norm_pre, norm_post, w_in, conv_w, conv_b, lru_w_a, lru_b_a, lru_w_x,
           lru_b_x, lru_lambda, w_rnn_out, w_attn_out, w_out):
    B, S, D = yp.shape
    N = ys.shape[0]
    row = lambda v: v.reshape(1, -1)
    w_in_b = w_in.astype(BF16)
    wa = _block_diag_chunks(lru_w_a).astype(BF16)
    wx = _block_diag_chunks(lru_w_x).astype(BF16)
    wro = w_rnn_out.astype(BF16)
    wao = w_attn_out.astype(BF16)
    wo = w_out.astype(BF16)
    shared = (w_in_b, row(norm_pre), conv_w, row(conv_b), wa, wx, row(lru_b_a), row(lru_b_x), row(lru_lambda), wro)

    xs = ys.reshape(N, D)
    att_in, mixr_s, sga_s, conv_s, h_s = _proj_rnn_sample(
        xs, *shared, sconv.reshape(N, (CONV_WIDTH - 1) * D_RNN), h0)
    n_qkv = 3 * N_GROUPS
    z_s = att_in[:, n_qkv * D_ATTN:]
    qkv_s = att_in[:, :n_qkv * D_ATTN].reshape(N, n_qkv, H_G, HEAD_DIM)

    outs = _proj_rnn_prompt2(yp, *shared, qkv_s, caches, layer)
    qkv, (zg, mixr, sga, kv0, kv1, kv2, conv_p, h_p, att_s) = outs[:9], outs[9:]
    os, sts = [], []
    for g in range(N_GROUPS):
        o, st = _attn_group(qkv[3 * g], qkv[3 * g + 1], qkv[3 * g + 2], g)
        os.append(o)
        sts.append(st)
    y_p = _merge_out_prompt(os, sts, zg, sga, mixr, yp, wao, wo, row(norm_post))
    kv_p = [kv.reshape(B, kv.shape[1], 2, H_G, HEAD_DIM) for kv in (kv0, kv1, kv2)]

    y_s = _out_sample(att_s.reshape(N, D_ATTN), z_s, sga_s, mixr_s, xs, wao, wo, row(norm_post))
    kv_s = [jnp.stack([qkv_s[:, N_GROUPS + g], qkv_s[:, 2 * N_GROUPS + g]], axis=1).reshape(
        N, 1, 2, H_G, HEAD_DIM) for g in range(N_GROUPS)]

    return (y_p, y_s.reshape(N, 1, D), conv_p, conv_s.reshape(N, CONV_WIDTH - 1, D_RNN),
            h_p.reshape(B, D_RNN), h_s, kv_p, kv_s)


def kernel(x_prompt, x_sample, state_conv, state_h, cache_kv_w128, cache_kv_w512, cache_kv_w2048, norm_pre, norm_post, w_in, conv_w, conv_b, lru_w_a, lru_b_a, lru_w_x, lru_b_x, lru_lambda, w_rnn_out, w_attn_out, w_out):
    depth = norm_pre.shape[0]
    caches = (cache_kv_w128, cache_kv_w512, cache_kv_w2048)
    yp, ys = x_prompt, x_sample
    conv_p, conv_s, h_p, h_s = [], [], [], []
    kvp = ([], [], [])
    kvs = ([], [], [])
    for l in range(depth):
        yp, ys, cp, cs, hp, hs, kv_p, kv_s = _layer(
            l, yp, ys, state_conv[l], state_h[l], caches,
            norm_pre[l], norm_post[l], w_in[l], conv_w[l], conv_b[l], lru_w_a[l], lru_b_a[l], lru_w_x[l],
            lru_b_x[l], lru_lambda[l], w_rnn_out[l], w_attn_out[l], w_out[l])
        conv_p.append(cp)
        conv_s.append(cs)
        h_p.append(hp)
        h_s.append(hs)
        for g in range(N_GROUPS):
            kvp[g].append(kv_p[g])
            kvs[g].append(kv_s[g])
    return (yp, ys, jnp.stack(conv_p), jnp.stack(conv_s), jnp.stack(h_p), jnp.stack(h_s),
            jnp.stack(kvp[0]), jnp.stack(kvs[0]), jnp.stack(kvp[1]), jnp.stack(kvs[1]),
            jnp.stack(kvp[2]), jnp.stack(kvs[2]))
```

```python
import functools

import numpy as np
import jax
import jax.numpy as jnp
from jax import lax
from jax.experimental import pallas as pl
from jax.experimental.pallas import tpu as pltpu

F32 = jnp.float32
BF16 = jnp.bfloat16

D_MODEL = 1024
D_RNN = 1024
N_RNN_BLOCKS = 16
RNN_BLOCK = D_RNN // N_RNN_BLOCKS
CONV_WIDTH = 4
LRU_C = 8.0
HEAD_DIM = 64
H_G = 8
ATTN_GROUPS = ((128, 1), (512, 4), (2048, 16))
N_GROUPS = 3
D_ATTN = H_G * HEAD_DIM
Q_BLOCK = 128
ALIBI_MAX = 8.0
EPS = 1e-6
NEG_INF = -1e30
D_IN = 2 * D_RNN + 3 * N_GROUPS * D_ATTN + D_ATTN + 2 * D_MODEL

LANES = 128
TN = 1024
N_COL_BLOCKS = D_IN // TN
COL_U = 0
COL_Z_RNN = COL_U + D_RNN
COL_Q = COL_Z_RNN + D_RNN
COL_K = COL_Q + N_GROUPS * D_ATTN
COL_V = COL_K + N_GROUPS * D_ATTN
COL_Z_ATTN = COL_V + N_GROUPS * D_ATTN
COL_G_RNN = COL_Z_ATTN + D_ATTN
COL_G_ATTN = COL_G_RNN + D_MODEL
GATE_BLOCK = 256
N_GATE_BLOCKS = D_RNN // GATE_BLOCK
STAT_LANES = LANES
STAT_LANES_PER_HEAD = STAT_LANES // H_G

TM_PROMPT = 512
TM_PROMPT2 = 256
TILES_PER_SAMPLE = 2
SEGMENTS = 8
TM_OUT = 1024
OUT_ROWS = TM_OUT
ATTN_CHUNK = 1024
ROW_CHUNK = 64
GATE_ROWS = 128
VMEM_LIMIT = 56 * 1024 * 1024


def _alibi_slopes():
    n = N_GROUPS * H_G
    s = np.float32(2.0) ** (np.float32(-ALIBI_MAX) * np.arange(1, n + 1, dtype=np.float32) / np.float32(n))
    return s.reshape(N_GROUPS, H_G)


_SLOPES = _alibi_slopes()


def _stat_lane(h):
    return (h % 2) * (STAT_LANES // 2) + (h // 2) * STAT_LANES_PER_HEAD


def _softplus(y):
    return jnp.maximum(y, 0.0) + jnp.log1p(jnp.exp(-jnp.abs(y)))


def _sigmoid(x):
    return 0.5 * jnp.tanh(0.5 * x) + 0.5


def _for_rows(n_rows, chunk, fn):
    if n_rows <= chunk:
        fn(0)
        return

    def body(c, carry):
        fn(pl.multiple_of(c * chunk, chunk))
        return carry

    lax.fori_loop(0, n_rows // chunk, body, 0)


def _lru_gates(xc, c, wa_ref, wx_ref, ba_ref, bx_ref, lam_ref):
    c0 = c * GATE_BLOCK
    xcb = xc.astype(BF16)
    r = _sigmoid(jnp.dot(xcb, wa_ref[c], preferred_element_type=F32) + ba_ref[:, c0:c0 + GATE_BLOCK])
    i = _sigmoid(jnp.dot(xcb, wx_ref[c], preferred_element_type=F32) + bx_ref[:, c0:c0 + GATE_BLOCK])
    log_a = (-LRU_C * r) * _softplus(-lam_ref[:, c0:c0 + GATE_BLOCK])
    a = jnp.exp(log_a)
    v = 1.0 - a * a
    b = jnp.where(v > 0.0, v * lax.rsqrt(v), 0.0) * i * xc
    return a, b


def _sample_head_attention(h, qs_ref, cache_refs, att_ref):
    scale = HEAD_DIM ** -0.5
    ri = lax.broadcasted_iota(jnp.int32, (HEAD_DIM, HEAD_DIM), 0)
    ci = lax.broadcasted_iota(jnp.int32, (HEAD_DIM, HEAD_DIM), 1)
    eye = ri == ci

    def to_col(row):
        return jnp.sum(jnp.where(eye, jnp.broadcast_to(row, (HEAD_DIM, HEAD_DIM)), 0.0), axis=-1, keepdims=True)

    def to_row(col):
        return jnp.sum(jnp.where(eye, jnp.broadcast_to(col, (HEAD_DIM, HEAD_DIM)), 0.0), axis=0, keepdims=True)

    outs, lses = [], []
    for g, (window, dilation) in enumerate(ATTN_GROUPS):
        c_ref = cache_refs[g]
        pos = lax.broadcasted_iota(jnp.int32, (1, window), 1)
        dist = (window - pos).astype(F32)
        bias = jnp.where((pos % dilation) == 0, -float(_SLOPES[g, h]) * dist, NEG_INF)
        q = qs_ref[3 * g, h:h + 1, :] * scale
        s_new = jnp.sum(qs_ref[3 * g + 1, h:h + 1, :] * q, axis=-1, keepdims=True)
        v_col = to_col(qs_ref[3 * g + 2, h:h + 1, :])
        s = jnp.sum(c_ref[0] * to_col(q), axis=0, keepdims=True) + bias
        mx = jnp.maximum(jnp.max(s, axis=-1, keepdims=True), s_new)
        e = jnp.exp(s - mx)
        e_new = jnp.exp(s_new - mx)
        l = jnp.sum(e, axis=-1, keepdims=True) + e_new
        o = jnp.sum(c_ref[1] * e, axis=-1, keepdims=True) + e_new * v_col
        outs.append(o / l)
        lses.append(mx + jnp.log(l))
    mm = jnp.maximum(jnp.maximum(lses[0], lses[1]), lses[2])
    ws = [jnp.exp(x - mm) for x in lses]
    col = (outs[0] * ws[0] + outs[1] * ws[1] + outs[2] * ws[2]) / (ws[0] + ws[1] + ws[2])
    att_ref[h:h + 1, :] = to_row(col)


def _proj_rnn_kernel(x_ref, w_ref, gpre_ref, cw_ref, cb_ref, wa_ref, wx_ref, ba_ref, bx_ref, lam_ref, wro_ref,
                     qs_ref, c0_ref, c1_ref, c2_ref,
                     q0_ref, k0_ref, v0_ref, q1_ref, k1_ref, v1_ref, q2_ref, k2_ref, v2_ref,
                     zg_ref, mixr_ref, sga_ref, kv0_ref, kv1_ref, kv2_ref, conv_ref, h_ref, att_ref,
                     xn_s, xnf_s, xseg_s, xnseg_s, xn4_s, xn16_s, u_s, a_s, b_s, yr_s,
                     cprev_s, cin_s, carry_s,
                     *, tm, n_tiles):
    m = pl.program_id(1)
    j = pl.program_id(2)
    acc_s = u_s

    def sample_head(h):
        _sample_head_attention(h, qs_ref, (c0_ref, c1_ref, c2_ref), att_ref)
    last_tile = m == n_tiles - 1
    in_kv2 = m >= n_tiles - ATTN_GROUPS[2][0] // tm
    half = TN // 2
    d1, d2 = ATTN_GROUPS[1][1], ATTN_GROUPS[2][1]

    def project(lhs_s):
        return jnp.dot(lhs_s[...], w_ref[...], preferred_element_type=F32)

    def emit(out_ref, c0, dilation):
        p = tm // dilation
        for r in range(dilation):
            out_ref[r] = acc_s[r * p:(r + 1) * p, c0:c0 + half].astype(BF16)

    def natural_tail(out_ref, out_c0, c0, dilation):
        p = tm // dilation
        for c in range(half // LANES):
            for r in range(dilation):
                xnf_s[c, pl.ds(r, p, stride=dilation), :] = \
                    acc_s[r * p:(r + 1) * p, c0 + c * LANES:c0 + (c + 1) * LANES]
            out_ref[:, out_c0 + c * LANES:out_c0 + (c + 1) * LANES] = xnf_s[c]

    seg = tm // SEGMENTS
    n_slab = D_MODEL // LANES

    @pl.when(j == 0)
    def _u_block():
        def norm(s, carry):
            r0 = pl.multiple_of(s * seg, seg)
            xv = x_ref[pl.ds(r0, seg), :]
            ms = jnp.mean(xv * xv, axis=-1, keepdims=True)
            xn = (xv * lax.rsqrt(ms + EPS)) * gpre_ref[...]
            xn_s[pl.ds(r0, seg), :] = xn.astype(BF16)
            for c in range(n_slab):
                piece = xn[:, c * LANES:(c + 1) * LANES]
                xnf_s[c, pl.ds(r0, seg), :] = piece
                xseg_s[c, pl.ds(s, seg, stride=SEGMENTS), :] = piece
            return carry

        lax.fori_loop(0, SEGMENTS, norm, 0)

        for c in range(n_slab):
            xnseg_s[:, c * LANES:(c + 1) * LANES] = xseg_s[c].astype(BF16)

        for dil, dst in ((d1, xn4_s), (d2, xn16_s)):
            p = tm // dil
            for r in range(dil):
                for c in range(n_slab):
                    dst[r * p:(r + 1) * p, c * LANES:(c + 1) * LANES] = \
                        xnf_s[c, pl.ds(r, p, stride=dil), :].astype(BF16)

        @pl.when(m == 0)
        def _reset():
            cprev_s[...] = jnp.zeros((8, D_RNN), F32)
            carry_s[...] = jnp.zeros((1, D_RNN), F32)

        u_s[...] = project(xnseg_s)

        sub = lax.broadcasted_iota(jnp.int32, (SEGMENTS, GATE_BLOCK), 0)
        for c in range(N_GATE_BLOCKS):
            cols = slice(c * GATE_BLOCK, (c + 1) * GATE_BLOCK)

            def wrapped(t):
                v = pltpu.roll(u_s[SEGMENTS * (seg + t):SEGMENTS * (seg + t + 1), cols], 1, axis=0)
                return jnp.where(sub == 0, cprev_s[8 + t:9 + t, cols], v)

            wrap = {t: wrapped(t) for t in range(1 - CONV_WIDTH, 0)}
            for rc in range(tm // GATE_ROWS):
                r0 = rc * GATE_ROWS
                xc = cb_ref[:, cols] + u_s[r0:r0 + GATE_ROWS, cols] * cw_ref[CONV_WIDTH - 1:CONV_WIDTH, cols]
                for sh in range(1, CONV_WIDTH):
                    if r0 == 0:
                        head = [wrap[t - sh] for t in range(sh)]
                        ush = jnp.concatenate(head + [u_s[0:GATE_ROWS - SEGMENTS * sh, cols]], axis=0)
                    else:
                        ush = u_s[r0 - SEGMENTS * sh:r0 - SEGMENTS * sh + GATE_ROWS, cols]
                    xc = xc + ush * cw_ref[CONV_WIDTH - 1 - sh:CONV_WIDTH - sh, cols]
                a, b = _lru_gates(xc, c, wa_ref, wx_ref, ba_ref, bx_ref, lam_ref)
                a_s[r0:r0 + GATE_ROWS, cols] = a
                b_s[r0:r0 + GATE_ROWS, cols] = b

        def scan(t, carry):
            hloc, cum = carry
            r0 = pl.multiple_of(t * SEGMENTS, SEGMENTS)
            av = a_s[pl.ds(r0, SEGMENTS), :]
            hloc = av * hloc + b_s[pl.ds(r0, SEGMENTS), :]
            cum = av * cum
            b_s[pl.ds(r0, SEGMENTS), :] = hloc
            a_s[pl.ds(r0, SEGMENTS), :] = cum
            return hloc, cum

        h_end, a_end = lax.fori_loop(
            0, seg, scan, (jnp.zeros((SEGMENTS, D_RNN), F32), jnp.ones((SEGMENTS, D_RNN), F32)), unroll=4)
        state = carry_s[...]
        for s in range(SEGMENTS):
            cin_s[s:s + 1, :] = state
            state = h_end[s:s + 1, :] + a_end[s:s + 1, :] * state
        carry_s[...] = state
        h_ref[...] = state
        for k in range(1, CONV_WIDTH):
            last = u_s[tm - SEGMENTS * (k - 1) - 1:tm - SEGMENTS * (k - 1), :]
            conv_ref[CONV_WIDTH - 1 - k:CONV_WIDTH - k, :] = last
            cprev_s[8 - k:9 - k, :] = last

    @pl.when(j == 1)
    def _z_rnn_block():
        sample_head(0)
        z = project(xnseg_s)
        split = (seg, SEGMENTS, D_RNN)
        h = b_s[...].reshape(split) + a_s[...].reshape(split) * cin_s[...][None]
        hz = (h.reshape(tm, D_RNN) * (z * _sigmoid(z))).astype(BF16)
        yr_s[...] = jnp.dot(hz, wro_ref[...], preferred_element_type=F32)

    @pl.when(j == 2)
    def _g_rnn_block():
        sample_head(1)
        mix = _sigmoid(project(xnseg_s)) * yr_s[...]
        for c in range(n_slab):
            xseg_s[c] = mix[:, c * LANES:(c + 1) * LANES]
        for s in range(SEGMENTS):
            for c in range(n_slab):
                mixr_ref[s * seg:(s + 1) * seg, c * LANES:(c + 1) * LANES] = \
                    xseg_s[c, pl.ds(s, seg, stride=SEGMENTS), :].astype(BF16)

    @pl.when(j == 3)
    def _g_attn_block():
        sample_head(2)
        sga_ref[...] = _sigmoid(project(xn_s)).astype(BF16)

    @pl.when(j == 4)
    def _zattn_q0_block():
        sample_head(3)
        acc_s[...] = project(xn_s)
        zg_ref[...] = acc_s[:, 0:half].astype(BF16)
        emit(q0_ref, half, 1)

    @pl.when(j == 5)
    def _k0_v0_block():
        sample_head(4)
        acc_s[...] = project(xn_s)
        emit(k0_ref, 0, 1)
        emit(v0_ref, half, 1)

        @pl.when(last_tile)
        def _():
            kv0_ref[...] = acc_s[tm - ATTN_GROUPS[0][0]:tm, :]

    @pl.when(j == 6)
    def _q1_k1_block():
        sample_head(5)
        acc_s[...] = project(xn4_s)
        emit(q1_ref, 0, d1)
        emit(k1_ref, half, d1)

        @pl.when(last_tile)
        def _():
            natural_tail(kv1_ref, 0, half, d1)

    @pl.when(j == 7)
    def _v1_q2_block():
        sample_head(6)
        acc_s[:, 0:half] =jnp.dot(xn4_s[...], w_ref[:, 0:half], preferred_element_type=F32)
        acc_s[:, half:TN] = jnp.dot(xn16_s[...], w_ref[:, half:TN], preferred_element_type=F32)
        emit(v1_ref, 0, d1)
        emit(q2_ref, half, d2)

        @pl.when(last_tile)
        def _():
            natural_tail(kv1_ref, half, 0, d1)

    @pl.when(j == 8)
    def _k2_v2_block():
        sample_head(7)
        acc_s[...] = project(xn16_s)
        emit(k2_ref, 0, d2)
        emit(v2_ref, half, d2)

        @pl.when(in_kv2)
        def _():
            natural_tail(kv2_ref, 0, 0, d2)
            natural_tail(kv2_ref, half, half, d2)


def _proj_rnn_prompt(x, w_in, gpre, cw, cb, wa, wx, ba, bx, lam, wro, qkv_s, caches, layer):
    B, S, D = x.shape
    tm = TM_PROMPT
    n_tiles = S // tm
    m0 = n_tiles - ATTN_GROUPS[2][0] // tm
    assert S % tm == 0 and tm == ATTN_GROUPS[1][0] and ATTN_GROUPS[2][0] % tm == 0 and S >= ATTN_GROUPS[2][0]
    N = qkv_s.shape[0]
    assert N == B * n_tiles and N_COL_BLOCKS == H_G + 1
    cache_views = [jnp.transpose(c, (0, 1, 3, 4, 5, 2)) for c in caches]
    sample_seq = lambda b, m, j: (b * n_tiles + m, 0, 0, 0)
    cache_head = lambda b, m, j: (layer, b * n_tiles + m, 0, jnp.clip(j - 1, 0, H_G - 1), 0, 0)

    const2 = lambda b, m, j: (0, 0)
    const3 = lambda b, m, j: (0, 0, 0)
    tile3 = lambda b, m, j: (b, m, 0)
    tile4 = lambda b, m, j: (b, 0, m, 0)
    per_seq = lambda b, m, j: (b, 0, 0)

    qkv_shapes, qkv_specs = [], []
    for _, dil in ATTN_GROUPS:
        for _ in range(3):
            qkv_shapes.append(jax.ShapeDtypeStruct((B, dil, S // dil, D_ATTN), BF16))
            qkv_specs.append(pl.BlockSpec((None, dil, tm // dil, D_ATTN), tile4))
    out_shape = tuple(qkv_shapes) + (
        jax.ShapeDtypeStruct((B, S, D_ATTN), BF16),
        jax.ShapeDtypeStruct((B, S, D), BF16),
        jax.ShapeDtypeStruct((B, S, D), BF16),
        jax.ShapeDtypeStruct((B, ATTN_GROUPS[0][0], 2 * D_ATTN), F32),
        jax.ShapeDtypeStruct((B, ATTN_GROUPS[1][0], 2 * D_ATTN), F32),
        jax.ShapeDtypeStruct((B, ATTN_GROUPS[2][0], 2 * D_ATTN), F32),
        jax.ShapeDtypeStruct((B, CONV_WIDTH - 1, D_RNN), F32),
        jax.ShapeDtypeStruct((B, 1, D_RNN), F32),
        jax.ShapeDtypeStruct((N, H_G, HEAD_DIM), F32),
    )
    out_specs = tuple(qkv_specs) + (
        pl.BlockSpec((None, tm, D_ATTN), tile3),
        pl.BlockSpec((None, tm, D), tile3),
        pl.BlockSpec((None, tm, D), tile3),
        pl.BlockSpec((None, ATTN_GROUPS[0][0], 2 * D_ATTN), per_seq),
        pl.BlockSpec((None, ATTN_GROUPS[1][0], 2 * D_ATTN), per_seq),
        pl.BlockSpec((None, tm, 2 * D_ATTN), lambda b, m, j: (b, jnp.maximum(m - m0, 0), 0)),
        pl.BlockSpec((None, CONV_WIDTH - 1, D_RNN), per_seq),
        pl.BlockSpec((None, 1, D_RNN), per_seq),
        pl.BlockSpec((None, H_G, HEAD_DIM), lambda b, m, j: (b * n_tiles + m, 0, 0)),
    )
    in_specs = [
        pl.BlockSpec((None, tm, D), tile3),
        pl.BlockSpec((D, TN), lambda b, m, j: (0, j)),
        pl.BlockSpec((1, D), const2),
        pl.BlockSpec((CONV_WIDTH, D_RNN), const2),
        pl.BlockSpec((1, D_RNN), const2),
        pl.BlockSpec((N_GATE_BLOCKS, GATE_BLOCK, GATE_BLOCK), const3),
        pl.BlockSpec((N_GATE_BLOCKS, GATE_BLOCK, GATE_BLOCK), const3),
        pl.BlockSpec((1, D_RNN), const2),
        pl.BlockSpec((1, D_RNN), const2),
        pl.BlockSpec((1, D_RNN), const2),
        pl.BlockSpec((D_RNN, D), const2),
        pl.BlockSpec((None, 3 * N_GROUPS, H_G, HEAD_DIM), sample_seq),
    ]
    in_specs += [pl.BlockSpec((None, None, 2, None, HEAD_DIM, window), cache_head) for window, _ in ATTN_GROUPS]
    scratch = [
        pltpu.VMEM((tm, D), BF16),
        pltpu.VMEM((D // LANES, tm, LANES), F32),
        pltpu.VMEM((D // LANES, tm, LANES), F32),
        pltpu.VMEM((tm, D), BF16),
        pltpu.VMEM((tm, D), BF16),
        pltpu.VMEM((tm, D), BF16),
        pltpu.VMEM((tm, D_RNN), F32),
        pltpu.VMEM((tm, D_RNN), F32),
        pltpu.VMEM((tm, D_RNN), F32),
        pltpu.VMEM((tm, D), F32),
        pltpu.VMEM((8, D_RNN), F32),
        pltpu.VMEM((SEGMENTS, D_RNN), F32),
        pltpu.VMEM((1, D_RNN), F32),
    ]
    return pl.pallas_call(
        functools.partial(_proj_rnn_kernel, tm=tm, n_tiles=n_tiles),
        grid=(B, n_tiles, N_COL_BLOCKS),
        in_specs=in_specs,
        out_specs=out_specs,
        out_shape=out_shape,
        scratch_shapes=scratch,
        compiler_params=pltpu.CompilerParams(
            dimension_semantics=("arbitrary", "arbitrary", "arbitrary"), vmem_limit_bytes=VMEM_LIMIT),
        name="proj_rnn_prompt",
    )(x, w_in, gpre, cw, cb, wa, wx, ba, bx, lam, wro, qkv_s, *cache_views)


def _sample_heads(local_heads, first_head, slope_of, qs_ref, cache_refs, att_ref):
    scale = HEAD_DIM ** -0.5
    ri = lax.broadcasted_iota(jnp.int32, (HEAD_DIM, HEAD_DIM), 0)
    ci = lax.broadcasted_iota(jnp.int32, (HEAD_DIM, HEAD_DIM), 1)
    eye = ri == ci

    def to_col(row):
        return jnp.sum(jnp.where(eye, jnp.broadcast_to(row, (HEAD_DIM, HEAD_DIM)), 0.0), axis=-1, keepdims=True)

    def to_row(col):
        return jnp.sum(jnp.where(eye, jnp.broadcast_to(col, (HEAD_DIM, HEAD_DIM)), 0.0), axis=0, keepdims=True)

    for i in local_heads:
        h = first_head + i
        outs, lses = [], []
        for g, (window, dilation) in enumerate(ATTN_GROUPS):
            c_ref = cache_refs[g]
            pos = lax.broadcasted_iota(jnp.int32, (1, window), 1)
            dist = (window - pos).astype(F32)
            bias = jnp.where((pos % dilation) == 0, -slope_of(g, i) * dist, NEG_INF)
            q = qs_ref[g, pl.ds(h, 1), :] * scale
            s_new = jnp.sum(qs_ref[N_GROUPS + g, pl.ds(h, 1), :] * q, axis=-1, keepdims=True)
            v_col = to_col(qs_ref[2 * N_GROUPS + g, pl.ds(h, 1), :])
            s = jnp.sum(c_ref[0, i] * to_col(q), axis=0, keepdims=True) + bias
            mx = jnp.maximum(jnp.max(s, axis=-1, keepdims=True), s_new)
            e = jnp.exp(s - mx)
            e_new = jnp.exp(s_new - mx)
            l = jnp.sum(e, axis=-1, keepdims=True) + e_new
            o = jnp.sum(c_ref[1, i] * e, axis=-1, keepdims=True) + e_new * v_col
            outs.append(o / l)
            lses.append(mx + jnp.log(l))
        mm = jnp.maximum(jnp.maximum(lses[0], lses[1]), lses[2])
        ws = [jnp.exp(x - mm) for x in lses]
        col = (outs[0] * ws[0] + outs[1] * ws[1] + outs[2] * ws[2]) / (ws[0] + ws[1] + ws[2])
        att_ref[pl.ds(h, 1), :] = to_row(col)


def _proj_rnn2_kernel(x_ref, w_ref, gpre_ref, cw_ref, cb_ref, wa_ref, wx_ref, ba_ref, bx_ref, lam_ref, wro_ref,
                      qs_ref, c0_ref, c1_ref, c2_ref,
                      q0_ref, k0_ref, v0_ref, q1_ref, k1_ref, v1_ref, q2_ref, k2_ref, v2_ref,
                      zg_ref, mixr_ref, sga_ref, kv0_ref, kv1_ref, kv2_ref, conv_ref, h_ref, att_ref,
                      xn_s, xnf_s, xseg_s, xnseg_s, xn4_s, xn16_s, u_s, a_s, b_s, zg_s, cprev_s, carry_s,
                      *, tm, n_tiles):
    m = pl.program_id(1)
    j = pl.program_id(2)
    tile = pl.program_id(0) * n_tiles + m
    half = TN // 2
    d1, d2 = ATTN_GROUPS[1][1], ATTN_GROUPS[2][1]
    seg = tm // SEGMENTS
    n_slab = D_MODEL // LANES
    heads_per_step = H_G // (TILES_PER_SAMPLE * 2)

    def project(lhs_s, col, width=half):
        return jnp.dot(lhs_s[...], w_ref[:, col:col + width], preferred_element_type=F32)

    def emit(out_ref, val, c0, dilation):
        p = tm // dilation
        for r in range(dilation):
            out_ref[r] = val[r * p:(r + 1) * p, c0:c0 + half].astype(BF16)

    def natural_tail(out_ref, out_c0, val, c0, dilation):
        p = tm // dilation
        for c in range(half // LANES):
            for r in range(dilation):
                xnf_s[c, pl.ds(r, p, stride=dilation), :] = val[r * p:(r + 1) * p, c0 + c * LANES:c0 + (c + 1) * LANES]
            out_ref[:, out_c0 + c * LANES:out_c0 + (c + 1) * LANES] = xnf_s[c]

    def sample_heads(step):
        part = tile % TILES_PER_SAMPLE
        first = part * (heads_per_step * 2) + step * heads_per_step

        def slope_of(g, i):
            table = [float(_SLOPES[g, t * heads_per_step * 2 + step * heads_per_step + i])
                     for t in range(TILES_PER_SAMPLE)]
            out = jnp.float32(table[0])
            for t in range(1, TILES_PER_SAMPLE):
                out = jnp.where(part == t, jnp.float32(table[t]), out)
            return out

        _sample_heads(range(heads_per_step), first, slope_of, qs_ref, (c0_ref, c1_ref, c2_ref), att_ref)

    @pl.when(j == 0)
    def _recurrent_blocks():
        @pl.when(m == 0)
        def _reset():
            cprev_s[...] = jnp.zeros((8, D_RNN), F32)
            carry_s[...] = jnp.zeros((1, D_RNN), F32)

        for s in range(SEGMENTS):
            r0 = s * seg
            xv = x_ref[r0:r0 + seg, :]
            ms = jnp.mean(xv * xv, axis=-1, keepdims=True)
            xn = (xv * lax.rsqrt(ms + EPS)) * gpre_ref[...]
            xn_s[r0:r0 + seg, :] = xn.astype(BF16)
            for c in range(n_slab):
                piece = xn[:, c * LANES:(c + 1) * LANES]
                xnf_s[c, r0:r0 + seg, :] = piece
                xseg_s[c, pl.ds(s, seg, stride=SEGMENTS), :] = piece

        for c in range(n_slab):
            xnseg_s[:, c * LANES:(c + 1) * LANES] = xseg_s[c].astype(BF16)

        u_s[...] = project(xnseg_s, COL_U, D_RNN)
        zg_s[:, 0:D_RNN] = project(xnseg_s, COL_Z_RNN, D_RNN)
        zg_s[:, D_RNN:D_RNN + D_MODEL] = project(xnseg_s, COL_G_RNN, D_MODEL)

        sub = lax.broadcasted_iota(jnp.int32, (SEGMENTS, GATE_BLOCK), 0)
        gate_rows = min(GATE_ROWS, tm)
        for c in range(N_GATE_BLOCKS):
            cols = slice(c * GATE_BLOCK, (c + 1) * GATE_BLOCK)

            def wrapped(t):
                v = pltpu.roll(u_s[SEGMENTS * (seg + t):SEGMENTS * (seg + t + 1), cols], 1, axis=0)
                return jnp.where(sub == 0, cprev_s[8 + t:9 + t, cols], v)

            wrap = {t: wrapped(t) for t in range(1 - CONV_WIDTH, 0)}
            for rc in range(tm // gate_rows):
                r0 = rc * gate_rows
                xc = cb_ref[:, cols] + u_s[r0:r0 + gate_rows, cols] * cw_ref[CONV_WIDTH - 1:CONV_WIDTH, cols]
                for sh in range(1, CONV_WIDTH):
                    if r0 == 0:
                        head = [wrap[t - sh] for t in range(sh)]
                        ush = jnp.concatenate(head + [u_s[0:gate_rows - SEGMENTS * sh, cols]], axis=0)
                    else:
                        ush = u_s[r0 - SEGMENTS * sh:r0 - SEGMENTS * sh + gate_rows, cols]
                    xc = xc + ush * cw_ref[CONV_WIDTH - 1 - sh:CONV_WIDTH - sh, cols]
                a, b = _lru_gates(xc, c, wa_ref, wx_ref, ba_ref, bx_ref, lam_ref)
                a_s[r0:r0 + gate_rows, cols] = a
                b_s[r0:r0 + gate_rows, cols] = b

        hloc = jnp.zeros((SEGMENTS, D_RNN), F32)
        cum = jnp.ones((SEGMENTS, D_RNN), F32)
        for t in range(seg):
            rows = slice(t * SEGMENTS, (t + 1) * SEGMENTS)
            av = a_s[rows, :]
            hloc = av * hloc + b_s[rows, :]
            cum = av * cum
            b_s[rows, :] = hloc
            a_s[rows, :] = cum
        state = carry_s[...]
        enter = []
        for s in range(SEGMENTS):
            enter.append(state)
            state = hloc[s:s + 1, :] + cum[s:s + 1, :] * state
        enter = jnp.concatenate(enter, axis=0)
        carry_s[...] = state
        h_ref[...] = state
        for k in range(1, CONV_WIDTH):
            last = u_s[tm - SEGMENTS * (k - 1) - 1:tm - SEGMENTS * (k - 1), :]
            conv_ref[CONV_WIDTH - 1 - k:CONV_WIDTH - k, :] = last
            cprev_s[8 - k:9 - k, :] = last

        z = zg_s[:, 0:D_RNN]
        split = (seg, SEGMENTS, D_RNN)
        h = b_s[...].reshape(split) + a_s[...].reshape(split) * enter[None]
        hz = (h.reshape(tm, D_RNN) * (z * _sigmoid(z))).astype(BF16)
        yr = jnp.dot(hz, wro_ref[...], preferred_element_type=F32)

        mix = _sigmoid(zg_s[:, D_RNN:D_RNN + D_MODEL]) * yr
        for c in range(n_slab):
            xseg_s[c] = mix[:, c * LANES:(c + 1) * LANES]
        for s in range(SEGMENTS):
            for c in range(n_slab):
                mixr_ref[s * seg:(s + 1) * seg, c * LANES:(c + 1) * LANES] = \
                    xseg_s[c, pl.ds(s, seg, stride=SEGMENTS), :].astype(BF16)

        sample_heads(0)

    @pl.when(j == 1)
    def _attention_blocks():
        for dil, dst in ((d1, xn4_s), (d2, xn16_s)):
            p = tm // dil
            for r in range(dil):
                for c in range(n_slab):
                    dst[r * p:(r + 1) * p, c * LANES:(c + 1) * LANES] = \
                        xnf_s[c, pl.ds(r, p, stride=dil), :].astype(BF16)

        sga_ref[...] = _sigmoid(project(xn_s, COL_G_ATTN, D_MODEL)).astype(BF16)
        zg_ref[...] = project(xn_s, COL_Z_ATTN).astype(BF16)

        lhs = (xn_s, xn4_s, xn16_s)
        qkv_refs = ((q0_ref, k0_ref, v0_ref), (q1_ref, k1_ref, v1_ref), (q2_ref, k2_ref, v2_ref))
        for g, (_, dil) in enumerate(ATTN_GROUPS):
            q_ref, k_ref, v_ref = qkv_refs[g]
            emit(q_ref, project(lhs[g], COL_Q + g * D_ATTN), 0, dil)
            for col, out_ref, c0 in ((COL_K, k_ref, 0), (COL_V, v_ref, half)):
                r = project(lhs[g], col + g * D_ATTN)
                emit(out_ref, r, 0, dil)
                if g == 0:
                    kv0_ref[:, c0:c0 + half] = r[tm - ATTN_GROUPS[0][0]:tm, :]
                else:
                    natural_tail(kv1_ref if g == 1 else kv2_ref, c0, r, 0, dil)
        sample_heads(1)


def _proj_rnn_prompt2(x, w_in, gpre, cw, cb, wa, wx, ba, bx, lam, wro, qkv_s, caches, layer):
    B, S, D = x.shape
    tm = TM_PROMPT2
    n_tiles = S // tm
    N = qkv_s.shape[0]
    heads_per_step = H_G // (TILES_PER_SAMPLE * 2)
    assert S % tm == 0 and tm % (SEGMENTS * 16) == 0 and tm >= ATTN_GROUPS[0][0]
    assert all(w % tm == 0 and S >= w for w, _ in ATTN_GROUPS[1:])
    assert B * n_tiles == N * TILES_PER_SAMPLE and n_tiles % TILES_PER_SAMPLE == 0
    cache_views = [jnp.transpose(c, (0, 1, 3, 4, 5, 2)) for c in caches]

    const2 = lambda b, m, j: (0, 0)
    const3 = lambda b, m, j: (0, 0, 0)
    tile3 = lambda b, m, j: (b, m, 0)
    tile4 = lambda b, m, j: (b, 0, m, 0)
    per_seq = lambda b, m, j: (b, 0, 0)
    sample_of = lambda b, m: (b * n_tiles + m) // TILES_PER_SAMPLE
    head_block = lambda b, m, j: ((b * n_tiles + m) % TILES_PER_SAMPLE) * 2 + j

    def last_rows(window):
        first_tile = n_tiles - window // tm
        return lambda b, m, j: (b, jnp.maximum(m - first_tile, 0), 0)

    resident = dict(pipeline_mode=pl.Buffered(1))
    qkv_shapes, qkv_specs = [], []
    for _, dil in ATTN_GROUPS:
        for _ in range(3):
            qkv_shapes.append(jax.ShapeDtypeStruct((B, dil, S // dil, D_ATTN), BF16))
            qkv_specs.append(pl.BlockSpec((None, dil, tm // dil, D_ATTN), tile4))
    out_shape = tuple(qkv_shapes) + (
        jax.ShapeDtypeStruct((B, S, D_ATTN), BF16),
        jax.ShapeDtypeStruct((B, S, D), BF16),
        jax.ShapeDtypeStruct((B, S, D), BF16),
        jax.ShapeDtypeStruct((B, ATTN_GROUPS[0][0], 2 * D_ATTN), F32),
        jax.ShapeDtypeStruct((B, ATTN_GROUPS[1][0], 2 * D_ATTN), F32),
        jax.ShapeDtypeStruct((B, ATTN_GROUPS[2][0], 2 * D_ATTN), F32),
        jax.ShapeDtypeStruct((B, CONV_WIDTH - 1, D_RNN), F32),
        jax.ShapeDtypeStruct((B, 1, D_RNN), F32),
        jax.ShapeDtypeStruct((N, H_G, HEAD_DIM), F32),
    )
    out_specs = tuple(qkv_specs) + (
        pl.BlockSpec((None, tm, D_ATTN), tile3),
        pl.BlockSpec((None, tm, D), tile3),
        pl.BlockSpec((None, tm, D), tile3),
        pl.BlockSpec((None, ATTN_GROUPS[0][0], 2 * D_ATTN), per_seq),
        pl.BlockSpec((None, tm, 2 * D_ATTN), last_rows(ATTN_GROUPS[1][0])),
        pl.BlockSpec((None, tm, 2 * D_ATTN), last_rows(ATTN_GROUPS[2][0])),
        pl.BlockSpec((None, CONV_WIDTH - 1, D_RNN), per_seq),
        pl.BlockSpec((None, 1, D_RNN), per_seq),
        pl.BlockSpec((None, H_G, HEAD_DIM), lambda b, m, j: (sample_of(b, m), 0, 0)),
    )
    in_specs = [
        pl.BlockSpec((None, tm, D), tile3),
        pl.BlockSpec((D, D_IN), const2, **resident),
        pl.BlockSpec((1, D), const2),
        pl.BlockSpec((CONV_WIDTH, D_RNN), const2),
        pl.BlockSpec((1, D_RNN), const2),
        pl.BlockSpec((N_GATE_BLOCKS, GATE_BLOCK, GATE_BLOCK), const3),
        pl.BlockSpec((N_GATE_BLOCKS, GATE_BLOCK, GATE_BLOCK), const3),
        pl.BlockSpec((1, D_RNN), const2),
        pl.BlockSpec((1, D_RNN), const2),
        pl.BlockSpec((1, D_RNN), const2),
        pl.BlockSpec((D_RNN, D), const2, **resident),
        pl.BlockSpec((None, 3 * N_GROUPS, H_G, HEAD_DIM), lambda b, m, j: (sample_of(b, m), 0, 0, 0)),
    ]
    in_specs += [pl.BlockSpec((None, None, 2, heads_per_step, HEAD_DIM, window),
                              lambda b, m, j: (layer, sample_of(b, m), 0, head_block(b, m, j), 0, 0))
                 for window, _ in ATTN_GROUPS]
    scratch = [
        pltpu.VMEM((tm, D), BF16),
        pltpu.VMEM((D // LANES, tm, LANES), F32),
        pltpu.VMEM((D // LANES, tm, LANES), F32),
        pltpu.VMEM((tm, D), BF16),
        pltpu.VMEM((tm, D), BF16),
        pltpu.VMEM((tm, D), BF16),
        pltpu.VMEM((tm, D_RNN), F32),
        pltpu.VMEM((tm, D_RNN), F32),
        pltpu.VMEM((tm, D_RNN), F32),
        pltpu.VMEM((tm, D_RNN + D), F32),
        pltpu.VMEM((8, D_RNN), F32),
        pltpu.VMEM((1, D_RNN), F32),
    ]
    return pl.pallas_call(
        functools.partial(_proj_rnn2_kernel, tm=tm, n_tiles=n_tiles),
        grid=(B, n_tiles, 2),
        in_specs=in_specs,
        out_specs=out_specs,
        out_shape=out_shape,
        scratch_shapes=scratch,
        compiler_params=pltpu.CompilerParams(
            dimension_semantics=("arbitrary", "arbitrary", "arbitrary"), vmem_limit_bytes=VMEM_LIMIT),
        name="proj_rnn_prompt",
    )(x, w_in, gpre, cw, cb, wa, wx, ba, bx, lam, wro, qkv_s, *cache_views)


def _proj_rnn_sample_kernel(x_ref, w_ref, gpre_ref, cw_ref, cb_ref, wa_ref, wx_ref, ba_ref, bx_ref, lam_ref, wro_ref,
                            sc_ref, h0_ref,
                            att_ref, mixr_ref, sga_ref, conv_ref, h_ref,
                            xn_s, h_s, yr_s):
    j = pl.program_id(0)

    def project():
        return jnp.dot(xn_s[...], w_ref[...], preferred_element_type=F32)

    @pl.when(j == 0)
    def _u_block():
        xv = x_ref[...]
        ms = jnp.mean(xv * xv, axis=-1, keepdims=True)
        xn_s[...] = ((xv * lax.rsqrt(ms + EPS)) * gpre_ref[...]).astype(BF16)
        u = project()
        taps = [sc_ref[:, k * D_RNN:(k + 1) * D_RNN] for k in range(CONV_WIDTH - 1)] + [u]
        for c in range(N_GATE_BLOCKS):
            c0 = c * GATE_BLOCK
            xc = cb_ref[:, c0:c0 + GATE_BLOCK]
            for tap in range(CONV_WIDTH):
                xc = xc + taps[tap][:, c0:c0 + GATE_BLOCK] * cw_ref[tap:tap + 1, c0:c0 + GATE_BLOCK]
            a, b = _lru_gates(xc, c, wa_ref, wx_ref, ba_ref, bx_ref, lam_ref)
            h = a * h0_ref[:, c0:c0 + GATE_BLOCK] + b
            h_s[:, c0:c0 + GATE_BLOCK] = h
            h_ref[:, c0:c0 + GATE_BLOCK] = h
        for k in range(1, CONV_WIDTH):
            conv_ref[:, (k - 1) * D_RNN:k * D_RNN] = taps[k]

    @pl.when(j == 1)
    def _z_rnn_block():
        z = project()
        hz = (h_s[...] * (z * _sigmoid(z))).astype(BF16)
        yr_s[...] = jnp.dot(hz, wro_ref[...], preferred_element_type=F32)

    @pl.when(j == 2)
    def _g_rnn_block():
        mixr_ref[...] = _sigmoid(project()) * yr_s[...]

    @pl.when(j == 3)
    def _g_attn_block():
        sga_ref[...] = _sigmoid(project())

    @pl.when(j >= 4)
    def _attn_blocks():
        att_ref[...] = project()


def _proj_rnn_sample(x, w_in, gpre, cw, cb, wa, wx, ba, bx, lam, wro, sconv, h0):
    N, D = x.shape
    const2 = lambda j: (0, 0)
    const3 = lambda j: (0, 0, 0)
    n_att = N_COL_BLOCKS - 4
    first_att, g_rnn_block = COL_Q // TN, COL_G_RNN // TN
    assert COL_Q % TN == 0 and COL_G_RNN % TN == 0 and COL_G_RNN - COL_Q == n_att * TN
    w_block = lambda j: jnp.where(j < 2, j, jnp.where(j < 4, j + g_rnn_block - 2, j - 4 + first_att))
    out_shape = (
        jax.ShapeDtypeStruct((N, n_att * TN), F32),
        jax.ShapeDtypeStruct((N, D), F32),
        jax.ShapeDtypeStruct((N, D), F32),
        jax.ShapeDtypeStruct((N, (CONV_WIDTH - 1) * D_RNN), F32),
        jax.ShapeDtypeStruct((N, D_RNN), F32),
    )
    out_specs = (
        pl.BlockSpec((N, TN), lambda j: (0, jnp.maximum(j - 4, 0))),
        pl.BlockSpec((N, D), const2),
        pl.BlockSpec((N, D), const2),
        pl.BlockSpec((N, (CONV_WIDTH - 1) * D_RNN), const2),
        pl.BlockSpec((N, D_RNN), const2),
    )
    in_specs = [
        pl.BlockSpec((N, D), const2),
        pl.BlockSpec((D, TN), lambda j: (0, w_block(j))),
        pl.BlockSpec((1, D), const2),
        pl.BlockSpec((CONV_WIDTH, D_RNN), const2),
        pl.BlockSpec((1, D_RNN), const2),
        pl.BlockSpec((N_GATE_BLOCKS, GATE_BLOCK, GATE_BLOCK), const3),
        pl.BlockSpec((N_GATE_BLOCKS, GATE_BLOCK, GATE_BLOCK), const3),
        pl.BlockSpec((1, D_RNN), const2),
        pl.BlockSpec((1, D_RNN), const2),
        pl.BlockSpec((1, D_RNN), const2),
        pl.BlockSpec((D_RNN, D), const2),
        pl.BlockSpec((N, (CONV_WIDTH - 1) * D_RNN), const2),
        pl.BlockSpec((N, D_RNN), const2),
    ]
    scratch = [pltpu.VMEM((N, D), BF16), pltpu.VMEM((N, D_RNN), F32), pltpu.VMEM((N, D), F32)]
    return pl.pallas_call(
        _proj_rnn_sample_kernel,
        grid=(N_COL_BLOCKS,),
        in_specs=in_specs,
        out_specs=out_specs,
        out_shape=out_shape,
        scratch_shapes=scratch,
        compiler_params=pltpu.CompilerParams(dimension_semantics=("arbitrary",), vmem_limit_bytes=VMEM_LIMIT),
        name="proj_rnn_sample",
    )(x, w_in, gpre, cw, cb, wa, wx, ba, bx, lam, wro, sconv, h0)


def _attn_kernel(q_ref, kc_ref, vc_ref, kp_ref, vp_ref, o_ref, st_ref, bias_s, *, group, chunk, n_res):
    dilation = ATTN_GROUPS[group][1]
    c = pl.program_id(2)
    nt = (((1,), (1,)), ((), ()))

    @pl.when((pl.program_id(0) == 0) & (pl.program_id(1) == 0) & (c == 0))
    def _init_bias():
        qi = lax.broadcasted_iota(jnp.int32, (Q_BLOCK, Q_BLOCK), 0)
        kj = lax.broadcasted_iota(jnp.int32, (Q_BLOCK, Q_BLOCK), 1)
        steps_prev = Q_BLOCK + qi - kj
        steps_cur = qi - kj
        dist_prev = (steps_prev * dilation).astype(F32)
        dist_cur = (steps_cur * dilation).astype(F32)
        for h in range(H_G):
            slope = float(_SLOPES[group, h])
            bias_s[h] = jnp.where(steps_cur >= 0, -slope * dist_cur, NEG_INF)
            bias_s[H_G + h] = jnp.where(steps_prev <= Q_BLOCK, -slope * dist_prev, NEG_INF)
            bias_s[2 * H_G + h] = jnp.full((Q_BLOCK, Q_BLOCK), NEG_INF, F32)

    lane = lax.broadcasted_iota(jnp.int32, (Q_BLOCK, LANES), 1)
    low = lane < HEAD_DIM
    lane2 = lax.broadcasted_iota(jnp.int32, (2 * Q_BLOCK, LANES), 1)
    low2 = lane2 < HEAD_DIM
    ones_lo = jnp.where(low2, 1.0, 0.0).astype(BF16)
    ones_hi = jnp.where(low2, 0.0, 1.0).astype(BF16)
    stat_lane = lax.broadcasted_iota(jnp.int32, (Q_BLOCK, STAT_LANES), 1)

    def keys_of(res, block, cur_ref, prev_ref, sl):
        if block == 0:
            return jnp.concatenate([prev_ref[res, :, sl], cur_ref[res, 0:Q_BLOCK, sl]], axis=0)
        return cur_ref[res, (block - 1) * Q_BLOCK:(block + 1) * Q_BLOCK, sl]

    for res, i in [(res, i) for res in range(n_res) for i in range(chunk // Q_BLOCK)]:
        r0 = i * Q_BLOCK
        first = jnp.where(c == 0, H_G, 0) if i == 0 else 0
        stats = jnp.zeros((Q_BLOCK, STAT_LANES), F32)
        for p in range(H_G // 2):
            sl = slice(LANES * p, LANES * (p + 1))
            qp = q_ref[res, r0:r0 + Q_BLOCK, sl] * (HEAD_DIM ** -0.5)
            kp = keys_of(res, i, kc_ref, kp_ref, sl)
            vp = keys_of(res, i, vc_ref, vp_ref, sl)
            es, ms = [], []
            for hh in range(2):
                h = 2 * p + hh
                msk = low if hh == 0 else jnp.logical_not(low)
                qm = jnp.where(msk, qp, jnp.zeros_like(qp))
                s = lax.dot_general(qm, kp, nt, preferred_element_type=F32)
                s_p = s[:, 0:Q_BLOCK] + bias_s[H_G + h + first]
                s_c = s[:, Q_BLOCK:2 * Q_BLOCK] + bias_s[h]
                mx = jnp.maximum(jnp.max(s_p, axis=-1, keepdims=True), jnp.max(s_c, axis=-1, keepdims=True))
                es.append(jnp.exp(s_p - mx).astype(BF16))
                es.append(jnp.exp(s_c - mx).astype(BF16))
                ms.append(mx)
            vm0 = jnp.where(low2, vp, jnp.zeros_like(vp))
            vm1 = jnp.where(low2, jnp.zeros_like(vp), vp)
            w = jnp.concatenate([jnp.concatenate([vm0, ones_lo], axis=1),
                                 jnp.concatenate([vm1, ones_hi], axis=1)], axis=0)
            acc = jnp.dot(jnp.concatenate(es, axis=1), w, preferred_element_type=F32)
            l_pair = acc[:, LANES:2 * LANES]
            o_ref[res, r0:r0 + Q_BLOCK, sl] = (acc[:, 0:LANES] / l_pair).astype(o_ref.dtype)
            lse_pair = jnp.where(low, ms[0], ms[1]) + jnp.log(l_pair)
            keep = ((stat_lane % (STAT_LANES // 2)) // STAT_LANES_PER_HEAD) == p
            stats = jnp.where(keep, lse_pair, stats)
        st_ref[res, r0:r0 + Q_BLOCK, :] = stats


def _attn_group(q, k, v, group):
    B, dil, L, _ = q.shape
    chunk = min(ATTN_CHUNK, L)
    n_res = min(ATTN_CHUNK // chunk, dil)
    assert L % chunk == 0 and chunk % Q_BLOCK == 0 and dil % n_res == 0
    cur = lambda b, r, c: (b, r, c, 0)
    prev = lambda b, r, c: (b, r, jnp.maximum(c * (chunk // Q_BLOCK) - 1, 0), 0)
    blk = (None, n_res, chunk, D_ATTN)
    pblk = (None, n_res, Q_BLOCK, D_ATTN)
    return pl.pallas_call(
        functools.partial(_attn_kernel, group=group, chunk=chunk, n_res=n_res),
        grid=(B, dil // n_res, L // chunk),
        in_specs=[pl.BlockSpec(blk, cur), pl.BlockSpec(blk, cur), pl.BlockSpec(blk, cur),
                  pl.BlockSpec(pblk, prev), pl.BlockSpec(pblk, prev)],
        out_specs=[pl.BlockSpec(blk, cur), pl.BlockSpec((None, n_res, chunk, STAT_LANES), cur)],
        out_shape=[jax.ShapeDtypeStruct((B, dil, L, D_ATTN), BF16),
                   jax.ShapeDtypeStruct((B, dil, L, STAT_LANES), F32)],
        scratch_shapes=[pltpu.VMEM((3 * H_G, Q_BLOCK, Q_BLOCK), F32)],
        compiler_params=pltpu.CompilerParams(
            dimension_semantics=("arbitrary", "arbitrary", "arbitrary"), vmem_limit_bytes=VMEM_LIMIT),
        name=f"attn_group{group}",
    )(q, k, v, k, v)


def _finish(att, rows, z_ref, sga_ref, mixr_ref, x_ref, wao_ref, wo_ref, gpost_ref, y_ref):
    z = z_ref[rows, :].astype(F32)
    ya_in = (att * (z * _sigmoid(z))).astype(BF16)
    ya = jnp.dot(ya_in, wao_ref[...], preferred_element_type=F32)
    mixed = mixr_ref[rows, :].astype(F32) + sga_ref[rows, :].astype(F32) * ya
    out = jnp.dot(mixed.astype(BF16), wo_ref[...], preferred_element_type=F32)
    ms = jnp.mean(out * out, axis=-1, keepdims=True)
    y_ref[rows, :] = x_ref[rows, :] + (out * lax.rsqrt(ms + EPS)) * gpost_ref[...]


def _out_kernel(att_ref, z_ref, sga_ref, mixr_ref, x_ref, wao_ref, wo_ref, gpost_ref, y_ref):
    rows = slice(0, x_ref.shape[0])
    _finish(att_ref[...].astype(F32), rows, z_ref, sga_ref, mixr_ref, x_ref, wao_ref, wo_ref, gpost_ref, y_ref)


def _merge_out_kernel(o0_ref, o1_ref, o2_ref, s0_ref, s1_ref, s2_ref, z_ref, sga_ref, mixr_ref, x_ref,
                      wao_ref, wo_ref, gpost_ref, y_ref, nat1_s, nat2_s, st1_s, st2_s, *, tm):
    for o_ref, s_ref, nat_s, stn_s, dil in ((o1_ref, s1_ref, nat1_s, st1_s, ATTN_GROUPS[1][1]),
                                            (o2_ref, s2_ref, nat2_s, st2_s, ATTN_GROUPS[2][1])):
        p = tm // dil
        for r in range(dil):
            stn_s[pl.ds(r, p, stride=dil), :] = s_ref[r]
            for c in range(D_ATTN // LANES):
                nat_s[c, pl.ds(r, p, stride=dil), :] = o_ref[r, :, c * LANES:(c + 1) * LANES].astype(F32)

    src = lax.broadcasted_iota(jnp.int32, (STAT_LANES, D_ATTN), 0)
    dst_head = lax.broadcasted_iota(jnp.int32, (STAT_LANES, D_ATTN), 1) // HEAD_DIM
    expand = jnp.where(src == (dst_head % 2) * (STAT_LANES // 2) + (dst_head // 2) * STAT_LANES_PER_HEAD,
                       1.0, 0.0).astype(BF16)
    for k in range(tm // OUT_ROWS):
        rows = slice(k * OUT_ROWS, (k + 1) * OUT_ROWS)
        lse = [s0_ref[rows, :], st1_s[rows, :], st2_s[rows, :]]
        mm = jnp.maximum(jnp.maximum(lse[0], lse[1]), lse[2])
        ws = [jnp.exp(x - mm) for x in lse]
        den = ws[0] + ws[1] + ws[2]
        outs = [o0_ref[rows, :].astype(F32),
                jnp.concatenate([nat1_s[c, rows, :] for c in range(D_ATTN // LANES)], axis=1),
                jnp.concatenate([nat2_s[c, rows, :] for c in range(D_ATTN // LANES)], axis=1)]
        att = jnp.zeros((OUT_ROWS, D_ATTN), F32)
        for g in range(N_GROUPS):
            wexp = jnp.dot((ws[g] / den).astype(BF16), expand, preferred_element_type=F32)
            att = att + wexp * outs[g]
        _finish(att, rows, z_ref, sga_ref, mixr_ref, x_ref, wao_ref, wo_ref, gpost_ref, y_ref)


def _merge_out_prompt(os, sts, zg, sga, mixr, x, wao, wo, gpost):
    B, S, D = x.shape
    tm = TM_OUT
    tile3 = lambda b, m: (b, m, 0)
    tile4 = lambda b, m: (b, 0, m, 0)
    const = lambda b, m: (0, 0)
    in_specs = []
    for (_, dil), width in [(g, D_ATTN) for g in ATTN_GROUPS] + [(g, STAT_LANES) for g in ATTN_GROUPS]:
        if dil == 1:
            in_specs.append(pl.BlockSpec((None, None, tm, width), tile4))
        else:
            in_specs.append(pl.BlockSpec((None, dil, tm // dil, width), tile4))
    in_specs += [
        pl.BlockSpec((None, tm, D_ATTN), tile3),
        pl.BlockSpec((None, tm, D), tile3),
        pl.BlockSpec((None, tm, D), tile3),
        pl.BlockSpec((None, tm, D), tile3),
        pl.BlockSpec((D_ATTN, D), const),
        pl.BlockSpec((D, D), const),
        pl.BlockSpec((1, D), const),
    ]
    scratch = [
        pltpu.VMEM((D_ATTN // LANES, tm, LANES), F32),
        pltpu.VMEM((D_ATTN // LANES, tm, LANES), F32),
        pltpu.VMEM((tm, STAT_LANES), F32),
        pltpu.VMEM((tm, STAT_LANES), F32),
    ]
    return pl.pallas_call(
        functools.partial(_merge_out_kernel, tm=tm),
        grid=(B, S // tm),
        in_specs=in_specs,
        out_specs=pl.BlockSpec((None, tm, D), tile3),
        out_shape=jax.ShapeDtypeStruct((B, S, D), F32),
        scratch_shapes=scratch,
        compiler_params=pltpu.CompilerParams(
            dimension_semantics=("arbitrary", "arbitrary"), vmem_limit_bytes=VMEM_LIMIT),
        name="merge_out_prompt",
    )(*os, *sts, zg, sga, mixr, x, wao, wo, gpost)


def _out_sample(att, z, sga, mixr, x, wao, wo, gpost):
    N, D = x.shape
    full = lambda shape: pl.BlockSpec(shape, lambda i: (0, 0))
    return pl.pallas_call(
        _out_kernel,
        grid=(1,),
        in_specs=[full((N, D_ATTN)), full((N, D_ATTN)), full((N, D)), full((N, D)), full((N, D)),
                  full((D_ATTN, D)), full((D, D)), full((1, D))],
        out_specs=full((N, D)),
        out_shape=jax.ShapeDtypeStruct((N, D), F32),
        compiler_params=pltpu.CompilerParams(dimension_semantics=("arbitrary",), vmem_limit_bytes=VMEM_LIMIT),
        name="out_sample",
    )(att, z, sga, mixr, x, wao, wo, gpost)


def _block_diag_chunks(w):
    per = GATE_BLOCK // RNN_BLOCK
    w = w.reshape(N_GATE_BLOCKS, per, RNN_BLOCK, RNN_BLOCK)
    eye = jnp.eye(per, dtype=w.dtype)
    dense = w[:, :, :, None, :] * eye[None, :, None, :, None]
    return dense.reshape(N_GATE_BLOCKS, GATE_BLOCK, GATE_BLOCK)


def _reorder_w_in(w_in):
    sizes = (D_RNN, D_RNN, N_GROUPS * D_ATTN, N_GROUPS * D_ATTN, N_GROUPS * D_ATTN, D_ATTN, D_MODEL, D_MODEL)
    starts = np.concatenate([[0], np.cumsum(sizes)[:-1]])
    u0, z0, q0, k0, v0, za0, gr0, ga0 = (int(s) for s in starts)
    pieces = [(u0, D_RNN), (z0, D_RNN), (gr0, D_MODEL), (ga0, D_MODEL), (za0, D_ATTN)]
    for g in range(N_GROUPS):
        pieces += [(q0 + g * D_ATTN, D_ATTN), (k0 + g * D_ATTN, D_ATTN), (v0 + g * D_ATTN, D_ATTN)]
    return jnp.concatenate([w_in[:, s:s + n] for s, n in pieces], axis=1).astype(BF16)


def _layer(layer, yp, ys, sconv, h0, caches, norm_pre, norm_post, w_in, conv_w, conv_b, lru_w_a, lru_b_a, lru_w_x,
           lru_b_x, lru_lambda, w_rnn_out, w_attn_out, w_out):
    B, S, D = yp.shape
    N = ys.shape[0]
    row = lambda v: v.reshape(1, -1)
    w_in_b = w_in.astype(BF16)
    wa = _block_diag_chunks(lru_w_a).astype(BF16)
    wx = _block_diag_chunks(lru_w_x).astype(BF16)
    wro = w_rnn_out.astype(BF16)
    wao = w_attn_out.astype(BF16)
    wo = w_out.astype(BF16)
    shared = (w_in_b, row(norm_pre), conv_w, row(conv_b), wa, wx, row(lru_b_a), row(lru_b_x), row(lru_lambda), wro)

    xs = ys.reshape(N, D)
    att_in, mixr_s, sga_s, conv_s, h_s = _proj_rnn_sample(
        xs, *shared, sconv.reshape(N, (CONV_WIDTH - 1) * D_RNN), h0)
    n_qkv = 3 * N_GROUPS
    z_s = att_in[:, n_qkv * D_ATTN:]
    qkv_s = att_in[:, :n_qkv * D_ATTN].reshape(N, n_qkv, H_G, HEAD_DIM)

    outs = _proj_rnn_prompt2(yp, *shared, qkv_s, caches, layer)
    qkv, (zg, mixr, sga, kv0, kv1, kv2, conv_p, h_p, att_s) = outs[:9], outs[9:]
    os, sts = [], []
    for g in range(N_GROUPS):
        o, st = _attn_group(qkv[3 * g], qkv[3 * g + 1], qkv[3 * g + 2], g)
        os.append(o)
        sts.append(st)
    y_p = _merge_out_prompt(os, sts, zg, sga, mixr, yp, wao, wo, row(norm_post))
    kv_p = [kv.reshape(B, kv.shape[1], 2, H_G, HEAD_DIM) for kv in (kv0, kv1, kv2)]

    y_s = _out_sample(att_s.reshape(N, D_ATTN), z_s, sga_s, mixr_s, xs, wao, wo, row(norm_post))
    kv_s = [jnp.stack([qkv_s[:, N_GROUPS + g], qkv_s[:, 2 * N_GROUPS + g]], axis=1).reshape(
        N, 1, 2, H_G, HEAD_DIM) for g in range(N_GROUPS)]

    return (y_p, y_s.reshape(N, 1, D), conv_p, conv_s.reshape(N, CONV_WIDTH - 1, D_RNN),
            h_p.reshape(B, D_RNN), h_s, kv_p, kv_s)


def kernel(x_prompt, x_sample, state_conv, state_h, cache_kv_w128, cache_kv_w512, cache_kv_w2048, norm_pre, norm_post, w_in, conv_w, conv_b, lru_w_a, lru_b_a, lru_w_x, lru_b_x, lru_lambda, w_rnn_out, w_attn_out, w_out):
    depth = norm_pre.shape[0]
    caches = (cache_kv_w128, cache_kv_w512, cache_kv_w2048)
    yp, ys = x_prompt, x_sample
    conv_p, conv_s, h_p, h_s = [], [], [], []
    kvp = ([], [], [])
    kvs = ([], [], [])
    for l in range(depth):
        yp, ys, cp, cs, hp, hs, kv_p, kv_s = _layer(
            l, yp, ys, state_conv[l], state_h[l], caches,
            norm_pre[l], norm_post[l], w_in[l], conv_w[l], conv_b[l], lru_w_a[l], lru_b_a[l], lru_w_x[l],
            lru_b_x[l], lru_lambda[l], w_rnn_out[l], w_attn_out[l], w_out[l])
        conv_p.append(cp)
        conv_s.append(cs)
        h_p.append(hp)
        h_s.append(hs)
        for g in range(N_GROUPS):
            kvp[g].append(kv_p[g])
            kvs[g].append(kv_s[g])
    return (yp, ys, jnp.stack(conv_p), jnp.stack(conv_s), jnp.stack(h_p), jnp.stack(h_s),
            jnp.stack(kvp[0]), jnp.stack(kvs[0]), jnp.stack(kvp[1]), jnp.stack(kvs[1]),
            jnp.stack(kvp[2]), jnp.stack(kvs[2]))
```

```python
import functools

import numpy as np
import jax
import jax.numpy as jnp
from jax import lax
from jax.experimental import pallas as pl
from jax.experimental.pallas import tpu as pltpu

F32 = jnp.float32
BF16 = jnp.bfloat16

D_MODEL = 1024
D_RNN = 1024
N_RNN_BLOCKS = 16
RNN_BLOCK = D_RNN // N_RNN_BLOCKS
CONV_WIDTH = 4
LRU_C = 8.0
HEAD_DIM = 64
H_G = 8
ATTN_GROUPS = ((128, 1), (512, 4), (2048, 16))
N_GROUPS = 3
D_ATTN = H_G * HEAD_DIM
Q_BLOCK = 128
ALIBI_MAX = 8.0
EPS = 1e-6
NEG_INF = -1e30
D_IN = 2 * D_RNN + 3 * N_GROUPS * D_ATTN + D_ATTN + 2 * D_MODEL

LANES = 128
TN = 1024
N_COL_BLOCKS = D_IN // TN
COL_U = 0
COL_Z_RNN = COL_U + D_RNN
COL_Q = COL_Z_RNN + D_RNN
COL_K = COL_Q + N_GROUPS * D_ATTN
COL_V = COL_K + N_GROUPS * D_ATTN
COL_Z_ATTN = COL_V + N_GROUPS * D_ATTN
COL_G_RNN = COL_Z_ATTN + D_ATTN
COL_G_ATTN = COL_G_RNN + D_MODEL
GATE_BLOCK = 256
N_GATE_BLOCKS = D_RNN // GATE_BLOCK
STAT_LANES = LANES
STAT_LANES_PER_HEAD = STAT_LANES // H_G

TM_PROMPT = 512
TM_PROMPT2 = 256
TILES_PER_SAMPLE = 2
SEGMENTS = 8
TM_OUT = 1024
OUT_ROWS = TM_OUT
ATTN_CHUNK = 1024
ROW_CHUNK = 64
GATE_ROWS = 128
VMEM_LIMIT = 56 * 1024 * 1024


def _alibi_slopes():
    n = N_GROUPS * H_G
    s = np.float32(2.0) ** (np.float32(-ALIBI_MAX) * np.arange(1, n + 1, dtype=np.float32) / np.float32(n))
    return s.reshape(N_GROUPS, H_G)


_SLOPES = _alibi_slopes()


def _stat_lane(h):
    return (h % 2) * (STAT_LANES // 2) + (h // 2) * STAT_LANES_PER_HEAD


def _softplus(y):
    return jnp.maximum(y, 0.0) + jnp.log1p(jnp.exp(-jnp.abs(y)))


def _sigmoid(x):
    return 0.5 * jnp.tanh(0.5 * x) + 0.5


def _for_rows(n_rows, chunk, fn):
    if n_rows <= chunk:
        fn(0)
        return

    def body(c, carry):
        fn(pl.multiple_of(c * chunk, chunk))
        return carry

    lax.fori_loop(0, n_rows // chunk, body, 0)


def _lru_gates(xc, c, wa_ref, wx_ref, ba_ref, bx_ref, lam_ref):
    c0 = c * GATE_BLOCK
    xcb = xc.astype(BF16)
    r = _sigmoid(jnp.dot(xcb, wa_ref[c], preferred_element_type=F32) + ba_ref[:, c0:c0 + GATE_BLOCK])
    i = _sigmoid(jnp.dot(xcb, wx_ref[c], preferred_element_type=F32) + bx_ref[:, c0:c0 + GATE_BLOCK])
    log_a = (-LRU_C * r) * _softplus(-lam_ref[:, c0:c0 + GATE_BLOCK])
    a = jnp.exp(log_a)
    v = 1.0 - a * a
    b = jnp.where(v > 0.0, v * lax.rsqrt(v), 0.0) * i * xc
    return a, b


def _sample_head_attention(h, qs_ref, cache_refs, att_ref):
    scale = HEAD_DIM ** -0.5
    ri = lax.broadcasted_iota(jnp.int32, (HEAD_DIM, HEAD_DIM), 0)
    ci = lax.broadcasted_iota(jnp.int32, (HEAD_DIM, HEAD_DIM), 1)
    eye = ri == ci

    def to_col(row):
        return jnp.sum(jnp.where(eye, jnp.broadcast_to(row, (HEAD_DIM, HEAD_DIM)), 0.0), axis=-1, keepdims=True)

    def to_row(col):
        return jnp.sum(jnp.where(eye, jnp.broadcast_to(col, (HEAD_DIM, HEAD_DIM)), 0.0), axis=0, keepdims=True)

    outs, lses = [], []
    for g, (window, dilation) in enumerate(ATTN_GROUPS):
        c_ref = cache_refs[g]
        pos = lax.broadcasted_iota(jnp.int32, (1, window), 1)
        dist = (window - pos).astype(F32)
        bias = jnp.where((pos % dilation) == 0, -float(_SLOPES[g, h]) * dist, NEG_INF)
        q = qs_ref[3 * g, h:h + 1, :] * scale
        s_new = jnp.sum(qs_ref[3 * g + 1, h:h + 1, :] * q, axis=-1, keepdims=True)
        v_col = to_col(qs_ref[3 * g + 2, h:h + 1, :])
        s = jnp.sum(c_ref[0] * to_col(q), axis=0, keepdims=True) + bias
        mx = jnp.maximum(jnp.max(s, axis=-1, keepdims=True), s_new)
        e = jnp.exp(s - mx)
        e_new = jnp.exp(s_new - mx)
        l = jnp.sum(e, axis=-1, keepdims=True) + e_new
        o = jnp.sum(c_ref[1] * e, axis=-1, keepdims=True) + e_new * v_col
        outs.append(o / l)
        lses.append(mx + jnp.log(l))
    mm = jnp.maximum(jnp.maximum(lses[0], lses[1]), lses[2])
    ws = [jnp.exp(x - mm) for x in lses]
    col = (outs[0] * ws[0] + outs[1] * ws[1] + outs[2] * ws[2]) / (ws[0] + ws[1] + ws[2])
    att_ref[h:h + 1, :] = to_row(col)


def _proj_rnn_kernel(x_ref, w_ref, gpre_ref, cw_ref, cb_ref, wa_ref, wx_ref, ba_ref, bx_ref, lam_ref, wro_ref,
                     qs_ref, c0_ref, c1_ref, c2_ref,
                     q0_ref, k0_ref, v0_ref, q1_ref, k1_ref, v1_ref, q2_ref, k2_ref, v2_ref,
                     zg_ref, mixr_ref, sga_ref, kv0_ref, kv1_ref, kv2_ref, conv_ref, h_ref, att_ref,
                     xn_s, xnf_s, xseg_s, xnseg_s, xn4_s, xn16_s, u_s, a_s, b_s, yr_s,
                     cprev_s, cin_s, carry_s,
                     *, tm, n_tiles):
    m = pl.program_id(1)
    j = pl.program_id(2)
    acc_s = u_s

    def sample_head(h):
        _sample_head_attention(h, qs_ref, (c0_ref, c1_ref, c2_ref), att_ref)
    last_tile = m == n_tiles - 1
    in_kv2 = m >= n_tiles - ATTN_GROUPS[2][0] // tm
    half = TN // 2
    d1, d2 = ATTN_GROUPS[1][1], ATTN_GROUPS[2][1]

    def project(lhs_s):
        return jnp.dot(lhs_s[...], w_ref[...], preferred_element_type=F32)

    def emit(out_ref, c0, dilation):
        p = tm // dilation
        for r in range(dilation):
            out_ref[r] = acc_s[r * p:(r + 1) * p, c0:c0 + half].astype(BF16)

    def natural_tail(out_ref, out_c0, c0, dilation):
        p = tm // dilation
        for c in range(half // LANES):
            for r in range(dilation):
                xnf_s[c, pl.ds(r, p, stride=dilation), :] = \
                    acc_s[r * p:(r + 1) * p, c0 + c * LANES:c0 + (c + 1) * LANES]
            out_ref[:, out_c0 + c * LANES:out_c0 + (c + 1) * LANES] = xnf_s[c]

    seg = tm // SEGMENTS
    n_slab = D_MODEL // LANES

    @pl.when(j == 0)
    def _u_block():
        def norm(s, carry):
            r0 = pl.multiple_of(s * seg, seg)
            xv = x_ref[pl.ds(r0, seg), :]
            ms = jnp.mean(xv * xv, axis=-1, keepdims=True)
            xn = (xv * lax.rsqrt(ms + EPS)) * gpre_ref[...]
            xn_s[pl.ds(r0, seg), :] = xn.astype(BF16)
            for c in range(n_slab):
                piece = xn[:, c * LANES:(c + 1) * LANES]
                xnf_s[c, pl.ds(r0, seg), :] = piece
                xseg_s[c, pl.ds(s, seg, stride=SEGMENTS), :] = piece
            return carry

        lax.fori_loop(0, SEGMENTS, norm, 0)

        for c in range(n_slab):
            xnseg_s[:, c * LANES:(c + 1) * LANES] = xseg_s[c].astype(BF16)

        for dil, dst in ((d1, xn4_s), (d2, xn16_s)):
            p = tm // dil
            for r in range(dil):
                for c in range(n_slab):
                    dst[r * p:(r + 1) * p, c * LANES:(c + 1) * LANES] = \
                        xnf_s[c, pl.ds(r, p, stride=dil), :].astype(BF16)

        @pl.when(m == 0)
        def _reset():
            cprev_s[...] = jnp.zeros((8, D_RNN), F32)
            carry_s[...] = jnp.zeros((1, D_RNN), F32)

        u_s[...] = project(xnseg_s)

        sub = lax.broadcasted_iota(jnp.int32, (SEGMENTS, GATE_BLOCK), 0)
        for c in range(N_GATE_BLOCKS):
            cols = slice(c * GATE_BLOCK, (c + 1) * GATE_BLOCK)

            def wrapped(t):
                v = pltpu.roll(u_s[SEGMENTS * (seg + t):SEGMENTS * (seg + t + 1), cols], 1, axis=0)
                return jnp.where(sub == 0, cprev_s[8 + t:9 + t, cols], v)

            wrap = {t: wrapped(t) for t in range(1 - CONV_WIDTH, 0)}
            for rc in range(tm // GATE_ROWS):
                r0 = rc * GATE_ROWS
                xc = cb_ref[:, cols] + u_s[r0:r0 + GATE_ROWS, cols] * cw_ref[CONV_WIDTH - 1:CONV_WIDTH, cols]
                for sh in range(1, CONV_WIDTH):
                    if r0 == 0:
                        head = [wrap[t - sh] for t in range(sh)]
                        ush = jnp.concatenate(head + [u_s[0:GATE_ROWS - SEGMENTS * sh, cols]], axis=0)
                    else:
                        ush = u_s[r0 - SEGMENTS * sh:r0 - SEGMENTS * sh + GATE_ROWS, cols]
                    xc = xc + ush * cw_ref[CONV_WIDTH - 1 - sh:CONV_WIDTH - sh, cols]
                a, b = _lru_gates(xc, c, wa_ref, wx_ref, ba_ref, bx_ref, lam_ref)
                a_s[r0:r0 + GATE_ROWS, cols] = a
                b_s[r0:r0 + GATE_ROWS, cols] = b

        def scan(t, carry):
            hloc, cum = carry
            r0 = pl.multiple_of(t * SEGMENTS, SEGMENTS)
            av = a_s[pl.ds(r0, SEGMENTS), :]
            hloc = av * hloc + b_s[pl.ds(r0, SEGMENTS), :]
            cum = av * cum
            b_s[pl.ds(r0, SEGMENTS), :] = hloc
            a_s[pl.ds(r0, SEGMENTS), :] = cum
            return hloc, cum

        h_end, a_end = lax.fori_loop(
            0, seg, scan, (jnp.zeros((SEGMENTS, D_RNN), F32), jnp.ones((SEGMENTS, D_RNN), F32)), unroll=4)
        state = carry_s[...]
        for s in range(SEGMENTS):
            cin_s[s:s + 1, :] = state
            state = h_end[s:s + 1, :] + a_end[s:s + 1, :] * state
        carry_s[...] = state
        h_ref[...] = state
        for k in range(1, CONV_WIDTH):
            last = u_s[tm - SEGMENTS * (k - 1) - 1:tm - SEGMENTS * (k - 1), :]
            conv_ref[CONV_WIDTH - 1 - k:CONV_WIDTH - k, :] = last
            cprev_s[8 - k:9 - k, :] = last

    @pl.when(j == 1)
    def _z_rnn_block():
        sample_head(0)
        z = project(xnseg_s)
        split = (seg, SEGMENTS, D_RNN)
        h = b_s[...].reshape(split) + a_s[...].reshape(split) * cin_s[...][None]
        hz = (h.reshape(tm, D_RNN) * (z * _sigmoid(z))).astype(BF16)
        yr_s[...] = jnp.dot(hz, wro_ref[...], preferred_element_type=F32)

    @pl.when(j == 2)
    def _g_rnn_block():
        sample_head(1)
        mix = _sigmoid(project(xnseg_s)) * yr_s[...]
        for c in range(n_slab):
            xseg_s[c] = mix[:, c * LANES:(c + 1) * LANES]
        for s in range(SEGMENTS):
            for c in range(n_slab):
                mixr_ref[s * seg:(s + 1) * seg, c * LANES:(c + 1) * LANES] = \
                    xseg_s[c, pl.ds(s, seg, stride=SEGMENTS), :].astype(BF16)

    @pl.when(j == 3)
    def _g_attn_block():
        sample_head(2)
        sga_ref[...] = _sigmoid(project(xn_s)).astype(BF16)

    @pl.when(j == 4)
    def _zattn_q0_block():
        sample_head(3)
        acc_s[...] = project(xn_s)
        zg_ref[...] = acc_s[:, 0:half].astype(BF16)
        emit(q0_ref, half, 1)

    @pl.when(j == 5)
    def _k0_v0_block():
        sample_head(4)
        acc_s[...] = project(xn_s)
        emit(k0_ref, 0, 1)
        emit(v0_ref, half, 1)

        @pl.when(last_tile)
        def _():
            kv0_ref[...] = acc_s[tm - ATTN_GROUPS[0][0]:tm, :]

    @pl.when(j == 6)
    def _q1_k1_block():
        sample_head(5)
        acc_s[...] = project(xn4_s)
        emit(q1_ref, 0, d1)
        emit(k1_ref, half, d1)

        @pl.when(last_tile)
        def _():
            natural_tail(kv1_ref, 0, half, d1)

    @pl.when(j == 7)
    def _v1_q2_block():
        sample_head(6)
        acc_s[:, 0:half] =jnp.dot(xn4_s[...], w_ref[:, 0:half], preferred_element_type=F32)
        acc_s[:, half:TN] = jnp.dot(xn16_s[...], w_ref[:, half:TN], preferred_element_type=F32)
        emit(v1_ref, 0, d1)
        emit(q2_ref, half, d2)

        @pl.when(last_tile)
        def _():
            natural_tail(kv1_ref, half, 0, d1)

    @pl.when(j == 8)
    def _k2_v2_block():
        sample_head(7)
        acc_s[...] = project(xn16_s)
        emit(k2_ref, 0, d2)
        emit(v2_ref, half, d2)

        @pl.when(in_kv2)
        def _():
            natural_tail(kv2_ref, 0, 0, d2)
            natural_tail(kv2_ref, half, half, d2)


def _proj_rnn_prompt(x, w_in, gpre, cw, cb, wa, wx, ba, bx, lam, wro, qkv_s, caches, layer):
    B, S, D = x.shape
    tm = TM_PROMPT
    n_tiles = S // tm
    m0 = n_tiles - ATTN_GROUPS[2][0] // tm
    assert S % tm == 0 and tm == ATTN_GROUPS[1][0] and ATTN_GROUPS[2][0] % tm == 0 and S >= ATTN_GROUPS[2][0]
    N = qkv_s.shape[0]
    assert N == B * n_tiles and N_COL_BLOCKS == H_G + 1
    cache_views = [jnp.transpose(c, (0, 1, 3, 4, 5, 2)) for c in caches]
    sample_seq = lambda b, m, j: (b * n_tiles + m, 0, 0, 0)
    cache_head = lambda b, m, j: (layer, b * n_tiles + m, 0, jnp.clip(j - 1, 0, H_G - 1), 0, 0)

    const2 = lambda b, m, j: (0, 0)
    const3 = lambda b, m, j: (0, 0, 0)
    tile3 = lambda b, m, j: (b, m, 0)
    tile4 = lambda b, m, j: (b, 0, m, 0)
    per_seq = lambda b, m, j: (b, 0, 0)

    qkv_shapes, qkv_specs = [], []
    for _, dil in ATTN_GROUPS:
        for _ in range(3):
            qkv_shapes.append(jax.ShapeDtypeStruct((B, dil, S // dil, D_ATTN), BF16))
            qkv_specs.append(pl.BlockSpec((None, dil, tm // dil, D_ATTN), tile4))
    out_shape = tuple(qkv_shapes) + (
        jax.ShapeDtypeStruct((B, S, D_ATTN), BF16),
        jax.ShapeDtypeStruct((B, S, D), BF16),
        jax.ShapeDtypeStruct((B, S, D), BF16),
        jax.ShapeDtypeStruct((B, ATTN_GROUPS[0][0], 2 * D_ATTN), F32),
        jax.ShapeDtypeStruct((B, ATTN_GROUPS[1][0], 2 * D_ATTN), F32),
        jax.ShapeDtypeStruct((B, ATTN_GROUPS[2][0], 2 * D_ATTN), F32),
        jax.ShapeDtypeStruct((B, CONV_WIDTH - 1, D_RNN), F32),
        jax.ShapeDtypeStruct((B, 1, D_RNN), F32),
        jax.ShapeDtypeStruct((N, H_G, HEAD_DIM), F32),
    )
    out_specs = tuple(qkv_specs) + (
        pl.BlockSpec((None, tm, D_ATTN), tile3),
        pl.BlockSpec((None, tm, D), tile3),
        pl.BlockSpec((None, tm, D), tile3),
        pl.BlockSpec((None, ATTN_GROUPS[0][0], 2 * D_ATTN), per_seq),
        pl.BlockSpec((None, ATTN_GROUPS[1][0], 2 * D_ATTN), per_seq),
        pl.BlockSpec((None, tm, 2 * D_ATTN), lambda b, m, j: (b, jnp.maximum(m - m0, 0), 0)),
        pl.BlockSpec((None, CONV_WIDTH - 1, D_RNN), per_seq),
        pl.BlockSpec((None, 1, D_RNN), per_seq),
        pl.BlockSpec((None, H_G, HEAD_DIM), lambda b, m, j: (b * n_tiles + m, 0, 0)),
    )
    in_specs = [
        pl.BlockSpec((None, tm, D), tile3),
        pl.BlockSpec((D, TN), lambda b, m, j: (0, j)),
        pl.BlockSpec((1, D), const2),
        pl.BlockSpec((CONV_WIDTH, D_RNN), const2),
        pl.BlockSpec((1, D_RNN), const2),
        pl.BlockSpec((N_GATE_BLOCKS, GATE_BLOCK, GATE_BLOCK), const3),
        pl.BlockSpec((N_GATE_BLOCKS, GATE_BLOCK, GATE_BLOCK), const3),
        pl.BlockSpec((1, D_RNN), const2),
        pl.BlockSpec((1, D_RNN), const2),
        pl.BlockSpec((1, D_RNN), const2),
        pl.BlockSpec((D_RNN, D), const2),
        pl.BlockSpec((None, 3 * N_GROUPS, H_G, HEAD_DIM), sample_seq),
    ]
    in_specs += [pl.BlockSpec((None, None, 2, None, HEAD_DIM, window), cache_head) for window, _ in ATTN_GROUPS]
    scratch = [
        pltpu.VMEM((tm, D), BF16),
        pltpu.VMEM((D // LANES, tm, LANES), F32),
        pltpu.VMEM((D // LANES, tm, LANES), F32),
        pltpu.VMEM((tm, D), BF16),
        pltpu.VMEM((tm, D), BF16),
        pltpu.VMEM((tm, D), BF16),
        pltpu.VMEM((tm, D_RNN), F32),
        pltpu.VMEM((tm, D_RNN), F32),
        pltpu.VMEM((tm, D_RNN), F32),
        pltpu.VMEM((tm, D), F32),
        pltpu.VMEM((8, D_RNN), F32),
        pltpu.VMEM((SEGMENTS, D_RNN), F32),
        pltpu.VMEM((1, D_RNN), F32),
    ]
    return pl.pallas_call(
        functools.partial(_proj_rnn_kernel, tm=tm, n_tiles=n_tiles),
        grid=(B, n_tiles, N_COL_BLOCKS),
        in_specs=in_specs,
        out_specs=out_specs,
        out_shape=out_shape,
        scratch_shapes=scratch,
        compiler_params=pltpu.CompilerParams(
            dimension_semantics=("arbitrary", "arbitrary", "arbitrary"), vmem_limit_bytes=VMEM_LIMIT),
        name="proj_rnn_prompt",
    )(x, w_in, gpre, cw, cb, wa, wx, ba, bx, lam, wro, qkv_s, *cache_views)


def _sample_heads(local_heads, first_head, slope_of, qs_ref, cache_refs, att_ref):
    scale = HEAD_DIM ** -0.5
    ri = lax.broadcasted_iota(jnp.int32, (HEAD_DIM, HEAD_DIM), 0)
    ci = lax.broadcasted_iota(jnp.int32, (HEAD_DIM, HEAD_DIM), 1)
    eye = ri == ci

    def to_col(row):
        return jnp.sum(jnp.where(eye, jnp.broadcast_to(row, (HEAD_DIM, HEAD_DIM)), 0.0), axis=-1, keepdims=True)

    def to_row(col):
        return jnp.sum(jnp.where(eye, jnp.broadcast_to(col, (HEAD_DIM, HEAD_DIM)), 0.0), axis=0, keepdims=True)

    for i in local_heads:
        h = first_head + i
        outs, lses = [], []
        for g, (window, dilation) in enumerate(ATTN_GROUPS):
            c_ref = cache_refs[g]
            pos = lax.broadcasted_iota(jnp.int32, (1, window), 1)
            dist = (window - pos).astype(F32)
            bias = jnp.where((pos % dilation) == 0, -slope_of(g, i) * dist, NEG_INF)
            q = qs_ref[g, pl.ds(h, 1), :] * scale
            s_new = jnp.sum(qs_ref[N_GROUPS + g, pl.ds(h, 1), :] * q, axis=-1, keepdims=True)
            v_col = to_col(qs_ref[2 * N_GROUPS + g, pl.ds(h, 1), :])
            s = jnp.sum(c_ref[0, i] * to_col(q), axis=0, keepdims=True) + bias
            mx = jnp.maximum(jnp.max(s, axis=-1, keepdims=True), s_new)
            e = jnp.exp(s - mx)
            e_new = jnp.exp(s_new - mx)
            l = jnp.sum(e, axis=-1, keepdims=True) + e_new
            o = jnp.sum(c_ref[1, i] * e, axis=-1, keepdims=True) + e_new * v_col
            outs.append(o / l)
            lses.append(mx + jnp.log(l))
        mm = jnp.maximum(jnp.maximum(lses[0], lses[1]), lses[2])
        ws = [jnp.exp(x - mm) for x in lses]
        col = (outs[0] * ws[0] + outs[1] * ws[1] + outs[2] * ws[2]) / (ws[0] + ws[1] + ws[2])
        att_ref[pl.ds(h, 1), :] = to_row(col)


def _proj_rnn2_kernel(x_ref, w_ref, gpre_ref, cw_ref, cb_ref, wa_ref, wx_ref, ba_ref, bx_ref, lam_ref, wro_ref,
                      qs_ref, c0_ref, c1_ref, c2_ref,
                      q0_ref, k0_ref, v0_ref, q1_ref, k1_ref, v1_ref, q2_ref, k2_ref, v2_ref,
                      zg_ref, mixr_ref, sga_ref, kv0_ref, kv1_ref, kv2_ref, conv_ref, h_ref, att_ref,
                      xn_s, xnf_s, xseg_s, xnseg_s, xn4_s, xn16_s, u_s, a_s, b_s, cprev_s, carry_s,
                      *, tm, n_tiles):
    m = pl.program_id(1)
    j = pl.program_id(2)
    tile = pl.program_id(0) * n_tiles + m
    half = TN // 2
    d1, d2 = ATTN_GROUPS[1][1], ATTN_GROUPS[2][1]
    seg = tm // SEGMENTS
    n_slab = D_MODEL // LANES
    heads_per_step = H_G // (TILES_PER_SAMPLE * 2)

    def project(lhs_s, col, width=half):
        return jnp.dot(lhs_s[...], w_ref[:, col:col + width], preferred_element_type=F32)

    def emit(out_ref, val, c0, dilation):
        p = tm // dilation
        for r in range(dilation):
            out_ref[r] = val[r * p:(r + 1) * p, c0:c0 + half].astype(BF16)

    def natural_tail(out_ref, out_c0, val, c0, dilation):
        p = tm // dilation
        for c in range(half // LANES):
            for r in range(dilation):
                xnf_s[c, pl.ds(r, p, stride=dilation), :] = val[r * p:(r + 1) * p, c0 + c * LANES:c0 + (c + 1) * LANES]
            out_ref[:, out_c0 + c * LANES:out_c0 + (c + 1) * LANES] = xnf_s[c]

    def sample_heads(step):
        part = tile % TILES_PER_SAMPLE
        first = part * (heads_per_step * 2) + step * heads_per_step

        def slope_of(g, i):
            table = [float(_SLOPES[g, t * heads_per_step * 2 + step * heads_per_step + i])
                     for t in range(TILES_PER_SAMPLE)]
            out = jnp.float32(table[0])
            for t in range(1, TILES_PER_SAMPLE):
                out = jnp.where(part == t, jnp.float32(table[t]), out)
            return out

        _sample_heads(range(heads_per_step), first, slope_of, qs_ref, (c0_ref, c1_ref, c2_ref), att_ref)

    @pl.when(j == 0)
    def _recurrent_blocks():
        @pl.when(m == 0)
        def _reset():
            cprev_s[...] = jnp.zeros((8, D_RNN), F32)
            carry_s[...] = jnp.zeros((1, D_RNN), F32)

        for s in range(SEGMENTS):
            r0 = s * seg
            xv = x_ref[r0:r0 + seg, :]
            ms = jnp.mean(xv * xv, axis=-1, keepdims=True)
            xn = (xv * lax.rsqrt(ms + EPS)) * gpre_ref[...]
            xn_s[r0:r0 + seg, :] = xn.astype(BF16)
            for c in range(n_slab):
                piece = xn[:, c * LANES:(c + 1) * LANES]
                xnf_s[c, r0:r0 + seg, :] = piece
                xseg_s[c, pl.ds(s, seg, stride=SEGMENTS), :] = piece

        for c in range(n_slab):
            xnseg_s[:, c * LANES:(c + 1) * LANES] = xseg_s[c].astype(BF16)

        u_s[...] = project(xnseg_s, COL_U, D_RNN)

        sub = lax.broadcasted_iota(jnp.int32, (SEGMENTS, GATE_BLOCK), 0)
        gate_rows = min(GATE_ROWS, tm)
        for c in range(N_GATE_BLOCKS):
            cols = slice(c * GATE_BLOCK, (c + 1) * GATE_BLOCK)

            def wrapped(t):
                v = pltpu.roll(u_s[SEGMENTS * (seg + t):SEGMENTS * (seg + t + 1), cols], 1, axis=0)
                return jnp.where(sub == 0, cprev_s[8 + t:9 + t, cols], v)

            wrap = {t: wrapped(t) for t in range(1 - CONV_WIDTH, 0)}
            for rc in range(tm // gate_rows):
                r0 = rc * gate_rows
                xc = cb_ref[:, cols] + u_s[r0:r0 + gate_rows, cols] * cw_ref[CONV_WIDTH - 1:CONV_WIDTH, cols]
                for sh in range(1, CONV_WIDTH):
                    if r0 == 0:
                        head = [wrap[t - sh] for t in range(sh)]
                        ush = jnp.concatenate(head + [u_s[0:gate_rows - SEGMENTS * sh, cols]], axis=0)
                    else:
                        ush = u_s[r0 - SEGMENTS * sh:r0 - SEGMENTS * sh + gate_rows, cols]
                    xc = xc + ush * cw_ref[CONV_WIDTH - 1 - sh:CONV_WIDTH - sh, cols]
                a, b = _lru_gates(xc, c, wa_ref, wx_ref, ba_ref, bx_ref, lam_ref)
                a_s[r0:r0 + gate_rows, cols] = a
                b_s[r0:r0 + gate_rows, cols] = b

        hloc = jnp.zeros((SEGMENTS, D_RNN), F32)
        cum = jnp.ones((SEGMENTS, D_RNN), F32)
        for t in range(seg):
            rows = slice(t * SEGMENTS, (t + 1) * SEGMENTS)
            av = a_s[rows, :]
            hloc = av * hloc + b_s[rows, :]
            cum = av * cum
            b_s[rows, :] = hloc
            a_s[rows, :] = cum
        state = carry_s[...]
        enter = []
        for s in range(SEGMENTS):
            enter.append(state)
            state = hloc[s:s + 1, :] + cum[s:s + 1, :] * state
        enter = jnp.concatenate(enter, axis=0)
        carry_s[...] = state
        h_ref[...] = state
        for k in range(1, CONV_WIDTH):
            last = u_s[tm - SEGMENTS * (k - 1) - 1:tm - SEGMENTS * (k - 1), :]
            conv_ref[CONV_WIDTH - 1 - k:CONV_WIDTH - k, :] = last
            cprev_s[8 - k:9 - k, :] = last

        z = project(xnseg_s, COL_Z_RNN, D_RNN)
        split = (seg, SEGMENTS, D_RNN)
        h = b_s[...].reshape(split) + a_s[...].reshape(split) * enter[None]
        hz = (h.reshape(tm, D_RNN) * (z * _sigmoid(z))).astype(BF16)
        yr = jnp.dot(hz, wro_ref[...], preferred_element_type=F32)

        mix = _sigmoid(project(xnseg_s, COL_G_RNN, D_MODEL)) * yr
        for c in range(n_slab):
            xseg_s[c] = mix[:, c * LANES:(c + 1) * LANES]
        for s in range(SEGMENTS):
            for c in range(n_slab):
                mixr_ref[s * seg:(s + 1) * seg, c * LANES:(c + 1) * LANES] = \
                    xseg_s[c, pl.ds(s, seg, stride=SEGMENTS), :].astype(BF16)

        sample_heads(0)

    @pl.when(j == 1)
    def _attention_blocks():
        for dil, dst in ((d1, xn4_s), (d2, xn16_s)):
            p = tm // dil
            for r in range(dil):
                for c in range(n_slab):
                    dst[r * p:(r + 1) * p, c * LANES:(c + 1) * LANES] = \
                        xnf_s[c, pl.ds(r, p, stride=dil), :].astype(BF16)

        sga_ref[...] = _sigmoid(project(xn_s, COL_G_ATTN, D_MODEL)).astype(BF16)
        zg_ref[...] = project(xn_s, COL_Z_ATTN).astype(BF16)

        lhs = (xn_s, xn4_s, xn16_s)
        qkv_refs = ((q0_ref, k0_ref, v0_ref), (q1_ref, k1_ref, v1_ref), (q2_ref, k2_ref, v2_ref))
        for g, (_, dil) in enumerate(ATTN_GROUPS):
            q_ref, k_ref, v_ref = qkv_refs[g]
            emit(q_ref, project(lhs[g], COL_Q + g * D_ATTN), 0, dil)
            for col, out_ref, c0 in ((COL_K, k_ref, 0), (COL_V, v_ref, half)):
                r = project(lhs[g], col + g * D_ATTN)
                emit(out_ref, r, 0, dil)
                if g == 0:
                    kv0_ref[:, c0:c0 + half] = r[tm - ATTN_GROUPS[0][0]:tm, :]
                else:
                    natural_tail(kv1_ref if g == 1 else kv2_ref, c0, r, 0, dil)
        sample_heads(1)


def _proj_rnn_prompt2(x, w_in, gpre, cw, cb, wa, wx, ba, bx, lam, wro, qkv_s, caches, layer):
    B, S, D = x.shape
    tm = TM_PROMPT2
    n_tiles = S // tm
    N = qkv_s.shape[0]
    heads_per_step = H_G // (TILES_PER_SAMPLE * 2)
    assert S % tm == 0 and tm % (SEGMENTS * 16) == 0 and tm >= ATTN_GROUPS[0][0]
    assert all(w % tm == 0 and S >= w for w, _ in ATTN_GROUPS[1:])
    assert B * n_tiles == N * TILES_PER_SAMPLE and n_tiles % TILES_PER_SAMPLE == 0
    cache_views = [jnp.transpose(c, (0, 1, 3, 4, 5, 2)) for c in caches]

    const2 = lambda b, m, j: (0, 0)
    const3 = lambda b, m, j: (0, 0, 0)
    tile3 = lambda b, m, j: (b, m, 0)
    tile4 = lambda b, m, j: (b, 0, m, 0)
    per_seq = lambda b, m, j: (b, 0, 0)
    sample_of = lambda b, m: (b * n_tiles + m) // TILES_PER_SAMPLE
    head_block = lambda b, m, j: ((b * n_tiles + m) % TILES_PER_SAMPLE) * 2 + j

    def last_rows(window):
        first_tile = n_tiles - window // tm
        return lambda b, m, j: (b, jnp.maximum(m - first_tile, 0), 0)

    resident = dict(pipeline_mode=pl.Buffered(1))
    qkv_shapes, qkv_specs = [], []
    for _, dil in ATTN_GROUPS:
        for _ in range(3):
            qkv_shapes.append(jax.ShapeDtypeStruct((B, dil, S // dil, D_ATTN), BF16))
            qkv_specs.append(pl.BlockSpec((None, dil, tm // dil, D_ATTN), tile4))
    out_shape = tuple(qkv_shapes) + (
        jax.ShapeDtypeStruct((B, S, D_ATTN), BF16),
        jax.ShapeDtypeStruct((B, S, D), BF16),
        jax.ShapeDtypeStruct((B, S, D), BF16),
        jax.ShapeDtypeStruct((B, ATTN_GROUPS[0][0], 2 * D_ATTN), F32),
        jax.ShapeDtypeStruct((B, ATTN_GROUPS[1][0], 2 * D_ATTN), F32),
        jax.ShapeDtypeStruct((B, ATTN_GROUPS[2][0], 2 * D_ATTN), F32),
        jax.ShapeDtypeStruct((B, CONV_WIDTH - 1, D_RNN), F32),
        jax.ShapeDtypeStruct((B, 1, D_RNN), F32),
        jax.ShapeDtypeStruct((N, H_G, HEAD_DIM), F32),
    )
    out_specs = tuple(qkv_specs) + (
        pl.BlockSpec((None, tm, D_ATTN), tile3),
        pl.BlockSpec((None, tm, D), tile3),
        pl.BlockSpec((None, tm, D), tile3),
        pl.BlockSpec((None, ATTN_GROUPS[0][0], 2 * D_ATTN), per_seq),
        pl.BlockSpec((None, tm, 2 * D_ATTN), last_rows(ATTN_GROUPS[1][0])),
        pl.BlockSpec((None, tm, 2 * D_ATTN), last_rows(ATTN_GROUPS[2][0])),
        pl.BlockSpec((None, CONV_WIDTH - 1, D_RNN), per_seq),
        pl.BlockSpec((None, 1, D_RNN), per_seq),
        pl.BlockSpec((None, H_G, HEAD_DIM), lambda b, m, j: (sample_of(b, m), 0, 0)),
    )
    in_specs = [
        pl.BlockSpec((None, tm, D), tile3),
        pl.BlockSpec((D, D_IN), const2, **resident),
        pl.BlockSpec((1, D), const2),
        pl.BlockSpec((CONV_WIDTH, D_RNN), const2),
        pl.BlockSpec((1, D_RNN), const2),
        pl.BlockSpec((N_GATE_BLOCKS, GATE_BLOCK, GATE_BLOCK), const3),
        pl.BlockSpec((N_GATE_BLOCKS, GATE_BLOCK, GATE_BLOCK), const3),
        pl.BlockSpec((1, D_RNN), const2),
        pl.BlockSpec((1, D_RNN), const2),
        pl.BlockSpec((1, D_RNN), const2),
        pl.BlockSpec((D_RNN, D), const2, **resident),
        pl.BlockSpec((None, 3 * N_GROUPS, H_G, HEAD_DIM), lambda b, m, j: (sample_of(b, m), 0, 0, 0)),
    ]
    in_specs += [pl.BlockSpec((None, None, 2, heads_per_step, HEAD_DIM, window),
                              lambda b, m, j: (layer, sample_of(b, m), 0, head_block(b, m, j), 0, 0))
                 for window, _ in ATTN_GROUPS]
    scratch = [
        pltpu.VMEM((tm, D), BF16),
        pltpu.VMEM((D // LANES, tm, LANES), F32),
        pltpu.VMEM((D // LANES, tm, LANES), F32),
        pltpu.VMEM((tm, D), BF16),
        pltpu.VMEM((tm, D), BF16),
        pltpu.VMEM((tm, D), BF16),
        pltpu.VMEM((tm, D_RNN), F32),
        pltpu.VMEM((tm, D_RNN), F32),
        pltpu.VMEM((tm, D_RNN), F32),
        pltpu.VMEM((8, D_RNN), F32),
        pltpu.VMEM((1, D_RNN), F32),
    ]
    return pl.pallas_call(
        functools.partial(_proj_rnn2_kernel, tm=tm, n_tiles=n_tiles),
        grid=(B, n_tiles, 2),
        in_specs=in_specs,
        out_specs=out_specs,
        out_shape=out_shape,
        scratch_shapes=scratch,
        compiler_params=pltpu.CompilerParams(
            dimension_semantics=("arbitrary", "arbitrary", "arbitrary"), vmem_limit_bytes=VMEM_LIMIT),
        name="proj_rnn_prompt",
    )(x, w_in, gpre, cw, cb, wa, wx, ba, bx, lam, wro, qkv_s, *cache_views)


def _proj_rnn_sample_kernel(x_ref, w_ref, gpre_ref, cw_ref, cb_ref, wa_ref, wx_ref, ba_ref, bx_ref, lam_ref, wro_ref,
                            sc_ref, h0_ref,
                            att_ref, mixr_ref, sga_ref, conv_ref, h_ref,
                            xn_s, h_s, yr_s):
    j = pl.program_id(0)

    def project():
        return jnp.dot(xn_s[...], w_ref[...], preferred_element_type=F32)

    @pl.when(j == 0)
    def _u_block():
        xv = x_ref[...]
        ms = jnp.mean(xv * xv, axis=-1, keepdims=True)
        xn_s[...] = ((xv * lax.rsqrt(ms + EPS)) * gpre_ref[...]).astype(BF16)
        u = project()
        taps = [sc_ref[:, k * D_RNN:(k + 1) * D_RNN] for k in range(CONV_WIDTH - 1)] + [u]
        for c in range(N_GATE_BLOCKS):
            c0 = c * GATE_BLOCK
            xc = cb_ref[:, c0:c0 + GATE_BLOCK]
            for tap in range(CONV_WIDTH):
                xc = xc + taps[tap][:, c0:c0 + GATE_BLOCK] * cw_ref[tap:tap + 1, c0:c0 + GATE_BLOCK]
            a, b = _lru_gates(xc, c, wa_ref, wx_ref, ba_ref, bx_ref, lam_ref)
            h = a * h0_ref[:, c0:c0 + GATE_BLOCK] + b
            h_s[:, c0:c0 + GATE_BLOCK] = h
            h_ref[:, c0:c0 + GATE_BLOCK] = h
        for k in range(1, CONV_WIDTH):
            conv_ref[:, (k - 1) * D_RNN:k * D_RNN] = taps[k]

    @pl.when(j == 1)
    def _z_rnn_block():
        z = project()
        hz = (h_s[...] * (z * _sigmoid(z))).astype(BF16)
        yr_s[...] = jnp.dot(hz, wro_ref[...], preferred_element_type=F32)

    @pl.when(j == 2)
    def _g_rnn_block():
        mixr_ref[...] = _sigmoid(project()) * yr_s[...]

    @pl.when(j == 3)
    def _g_attn_block():
        sga_ref[...] = _sigmoid(project())

    @pl.when(j >= 4)
    def _attn_blocks():
        att_ref[...] = project()


def _proj_rnn_sample(x, w_in, gpre, cw, cb, wa, wx, ba, bx, lam, wro, sconv, h0):
    N, D = x.shape
    const2 = lambda j: (0, 0)
    const3 = lambda j: (0, 0, 0)
    n_att = N_COL_BLOCKS - 4
    first_att, g_rnn_block = COL_Q // TN, COL_G_RNN // TN
    assert COL_Q % TN == 0 and COL_G_RNN % TN == 0 and COL_G_RNN - COL_Q == n_att * TN
    w_block = lambda j: jnp.where(j < 2, j, jnp.where(j < 4, j + g_rnn_block - 2, j - 4 + first_att))
    out_shape = (
        jax.ShapeDtypeStruct((N, n_att * TN), F32),
        jax.ShapeDtypeStruct((N, D), F32),
        jax.ShapeDtypeStruct((N, D), F32),
        jax.ShapeDtypeStruct((N, (CONV_WIDTH - 1) * D_RNN), F32),
        jax.ShapeDtypeStruct((N, D_RNN), F32),
    )
    out_specs = (
        pl.BlockSpec((N, TN), lambda j: (0, jnp.maximum(j - 4, 0))),
        pl.BlockSpec((N, D), const2),
        pl.BlockSpec((N, D), const2),
        pl.BlockSpec((N, (CONV_WIDTH - 1) * D_RNN), const2),
        pl.BlockSpec((N, D_RNN), const2),
    )
    in_specs = [
        pl.BlockSpec((N, D), const2),
        pl.BlockSpec((D, TN), lambda j: (0, w_block(j))),
        pl.BlockSpec((1, D), const2),
        pl.BlockSpec((CONV_WIDTH, D_RNN), const2),
        pl.BlockSpec((1, D_RNN), const2),
        pl.BlockSpec((N_GATE_BLOCKS, GATE_BLOCK, GATE_BLOCK), const3),
        pl.BlockSpec((N_GATE_BLOCKS, GATE_BLOCK, GATE_BLOCK), const3),
        pl.BlockSpec((1, D_RNN), const2),
        pl.BlockSpec((1, D_RNN), const2),
        pl.BlockSpec((1, D_RNN), const2),
        pl.BlockSpec((D_RNN, D), const2),
        pl.BlockSpec((N, (CONV_WIDTH - 1) * D_RNN), const2),
        pl.BlockSpec((N, D_RNN), const2),
    ]
    scratch = [pltpu.VMEM((N, D), BF16), pltpu.VMEM((N, D_RNN), F32), pltpu.VMEM((N, D), F32)]
    return pl.pallas_call(
        _proj_rnn_sample_kernel,
        grid=(N_COL_BLOCKS,),
        in_specs=in_specs,
        out_specs=out_specs,
        out_shape=out_shape,
        scratch_shapes=scratch,
        compiler_params=pltpu.CompilerParams(dimension_semantics=("arbitrary",), vmem_limit_bytes=VMEM_LIMIT),
        name="proj_rnn_sample",
    )(x, w_in, gpre, cw, cb, wa, wx, ba, bx, lam, wro, sconv, h0)


def _attn_kernel(q_ref, kc_ref, vc_ref, kp_ref, vp_ref, o_ref, st_ref, bias_s, *, group, chunk, n_res):
    dilation = ATTN_GROUPS[group][1]
    c = pl.program_id(2)
    nt = (((1,), (1,)), ((), ()))

    @pl.when((pl.program_id(0) == 0) & (pl.program_id(1) == 0) & (c == 0))
    def _init_bias():
        qi = lax.broadcasted_iota(jnp.int32, (Q_BLOCK, Q_BLOCK), 0)
        kj = lax.broadcasted_iota(jnp.int32, (Q_BLOCK, Q_BLOCK), 1)
        steps_prev = Q_BLOCK + qi - kj
        steps_cur = qi - kj
        dist_prev = (steps_prev * dilation).astype(F32)
        dist_cur = (steps_cur * dilation).astype(F32)
        for h in range(H_G):
            slope = float(_SLOPES[group, h])
            bias_s[h] = jnp.where(steps_cur >= 0, -slope * dist_cur, NEG_INF)
            bias_s[H_G + h] = jnp.where(steps_prev <= Q_BLOCK, -slope * dist_prev, NEG_INF)
            bias_s[2 * H_G + h] = jnp.full((Q_BLOCK, Q_BLOCK), NEG_INF, F32)

    lane = lax.broadcasted_iota(jnp.int32, (Q_BLOCK, LANES), 1)
    low = lane < HEAD_DIM
    lane2 = lax.broadcasted_iota(jnp.int32, (2 * Q_BLOCK, LANES), 1)
    low2 = lane2 < HEAD_DIM
    ones_lo = jnp.where(low2, 1.0, 0.0).astype(BF16)
    ones_hi = jnp.where(low2, 0.0, 1.0).astype(BF16)
    stat_lane = lax.broadcasted_iota(jnp.int32, (Q_BLOCK, STAT_LANES), 1)

    def keys_of(res, block, cur_ref, prev_ref, sl):
        if block == 0:
            return jnp.concatenate([prev_ref[res, :, sl], cur_ref[res, 0:Q_BLOCK, sl]], axis=0)
        return cur_ref[res, (block - 1) * Q_BLOCK:(block + 1) * Q_BLOCK, sl]

    for res, i in [(res, i) for res in range(n_res) for i in range(chunk // Q_BLOCK)]:
        r0 = i * Q_BLOCK
        first = jnp.where(c == 0, H_G, 0) if i == 0 else 0
        stats = jnp.zeros((Q_BLOCK, STAT_LANES), F32)
        for p in range(H_G // 2):
            sl = slice(LANES * p, LANES * (p + 1))
            qp = q_ref[res, r0:r0 + Q_BLOCK, sl] * (HEAD_DIM ** -0.5)
            kp = keys_of(res, i, kc_ref, kp_ref, sl)
            vp = keys_of(res, i, vc_ref, vp_ref, sl)
            es, ms = [], []
            for hh in range(2):
                h = 2 * p + hh
                msk = low if hh == 0 else jnp.logical_not(low)
                qm = jnp.where(msk, qp, jnp.zeros_like(qp))
                s = lax.dot_general(qm, kp, nt, preferred_element_type=F32)
                s_p = s[:, 0:Q_BLOCK] + bias_s[H_G + h + first]
                s_c = s[:, Q_BLOCK:2 * Q_BLOCK] + bias_s[h]
                mx = jnp.maximum(jnp.max(s_p, axis=-1, keepdims=True), jnp.max(s_c, axis=-1, keepdims=True))
                es.append(jnp.exp(s_p - mx).astype(BF16))
                es.append(jnp.exp(s_c - mx).astype(BF16))
                ms.append(mx)
            vm0 = jnp.where(low2, vp, jnp.zeros_like(vp))
            vm1 = jnp.where(low2, jnp.zeros_like(vp), vp)
            w = jnp.concatenate([jnp.concatenate([vm0, ones_lo], axis=1),
                                 jnp.concatenate([vm1, ones_hi], axis=1)], axis=0)
            acc = jnp.dot(jnp.concatenate(es, axis=1), w, preferred_element_type=F32)
            l_pair = acc[:, LANES:2 * LANES]
            o_ref[res, r0:r0 + Q_BLOCK, sl] = (acc[:, 0:LANES] / l_pair).astype(o_ref.dtype)
            lse_pair = jnp.where(low, ms[0], ms[1]) + jnp.log(l_pair)
            keep = ((stat_lane % (STAT_LANES // 2)) // STAT_LANES_PER_HEAD) == p
            stats = jnp.where(keep, lse_pair, stats)
        st_ref[res, r0:r0 + Q_BLOCK, :] = stats


def _attn_group(q, k, v, group):
    B, dil, L, _ = q.shape
    chunk = min(ATTN_CHUNK, L)
    n_res = min(ATTN_CHUNK // chunk, dil)
    assert L % chunk == 0 and chunk % Q_BLOCK == 0 and dil % n_res == 0
    cur = lambda b, r, c: (b, r, c, 0)
    prev = lambda b, r, c: (b, r, jnp.maximum(c * (chunk // Q_BLOCK) - 1, 0), 0)
    blk = (None, n_res, chunk, D_ATTN)
    pblk = (None, n_res, Q_BLOCK, D_ATTN)
    return pl.pallas_call(
        functools.partial(_attn_kernel, group=group, chunk=chunk, n_res=n_res),
        grid=(B, dil // n_res, L // chunk),
        in_specs=[pl.BlockSpec(blk, cur), pl.BlockSpec(blk, cur), pl.BlockSpec(blk, cur),
                  pl.BlockSpec(pblk, prev), pl.BlockSpec(pblk, prev)],
        out_specs=[pl.BlockSpec(blk, cur), pl.BlockSpec((None, n_res, chunk, STAT_LANES), cur)],
        out_shape=[jax.ShapeDtypeStruct((B, dil, L, D_ATTN), BF16),
                   jax.ShapeDtypeStruct((B, dil, L, STAT_LANES), F32)],
        scratch_shapes=[pltpu.VMEM((3 * H_G, Q_BLOCK, Q_BLOCK), F32)],
        compiler_params=pltpu.CompilerParams(
            dimension_semantics=("arbitrary", "arbitrary", "arbitrary"), vmem_limit_bytes=VMEM_LIMIT),
        name=f"attn_group{group}",
    )(q, k, v, k, v)


def _finish(att, rows, z_ref, sga_ref, mixr_ref, x_ref, wao_ref, wo_ref, gpost_ref, y_ref):
    z = z_ref[rows, :].astype(F32)
    ya_in = (att * (z * _sigmoid(z))).astype(BF16)
    ya = jnp.dot(ya_in, wao_ref[...], preferred_element_type=F32)
    mixed = mixr_ref[rows, :].astype(F32) + sga_ref[rows, :].astype(F32) * ya
    out = jnp.dot(mixed.astype(BF16), wo_ref[...], preferred_element_type=F32)
    ms = jnp.mean(out * out, axis=-1, keepdims=True)
    y_ref[rows, :] = x_ref[rows, :] + (out * lax.rsqrt(ms + EPS)) * gpost_ref[...]


def _out_kernel(att_ref, z_ref, sga_ref, mixr_ref, x_ref, wao_ref, wo_ref, gpost_ref, y_ref):
    rows = slice(0, x_ref.shape[0])
    _finish(att_ref[...].astype(F32), rows, z_ref, sga_ref, mixr_ref, x_ref, wao_ref, wo_ref, gpost_ref, y_ref)


def _merge_out_kernel(o0_ref, o1_ref, o2_ref, s0_ref, s1_ref, s2_ref, z_ref, sga_ref, mixr_ref, x_ref,
                      wao_ref, wo_ref, gpost_ref, y_ref, nat1_s, nat2_s, st1_s, st2_s, *, tm):
    for o_ref, s_ref, nat_s, stn_s, dil in ((o1_ref, s1_ref, nat1_s, st1_s, ATTN_GROUPS[1][1]),
                                            (o2_ref, s2_ref, nat2_s, st2_s, ATTN_GROUPS[2][1])):
        p = tm // dil
        for r in range(dil):
            stn_s[pl.ds(r, p, stride=dil), :] = s_ref[r]
            for c in range(D_ATTN // LANES):
                nat_s[c, pl.ds(r, p, stride=dil), :] = o_ref[r, :, c * LANES:(c + 1) * LANES].astype(F32)

    src = lax.broadcasted_iota(jnp.int32, (STAT_LANES, D_ATTN), 0)
    dst_head = lax.broadcasted_iota(jnp.int32, (STAT_LANES, D_ATTN), 1) // HEAD_DIM
    expand = jnp.where(src == (dst_head % 2) * (STAT_LANES // 2) + (dst_head // 2) * STAT_LANES_PER_HEAD,
                       1.0, 0.0).astype(BF16)
    for k in range(tm // OUT_ROWS):
        rows = slice(k * OUT_ROWS, (k + 1) * OUT_ROWS)
        lse = [s0_ref[rows, :], st1_s[rows, :], st2_s[rows, :]]
        mm = jnp.maximum(jnp.maximum(lse[0], lse[1]), lse[2])
        ws = [jnp.exp(x - mm) for x in lse]
        den = ws[0] + ws[1] + ws[2]
        outs = [o0_ref[rows, :].astype(F32),
                jnp.concatenate([nat1_s[c, rows, :] for c in range(D_ATTN // LANES)], axis=1),
                jnp.concatenate([nat2_s[c, rows, :] for c in range(D_ATTN // LANES)], axis=1)]
        att = jnp.zeros((OUT_ROWS, D_ATTN), F32)
        for g in range(N_GROUPS):
            wexp = jnp.dot((ws[g] / den).astype(BF16), expand, preferred_element_type=F32)
            att = att + wexp * outs[g]
        _finish(att, rows, z_ref, sga_ref, mixr_ref, x_ref, wao_ref, wo_ref, gpost_ref, y_ref)


def _merge_out_prompt(os, sts, zg, sga, mixr, x, wao, wo, gpost):
    B, S, D = x.shape
    tm = TM_OUT
    tile3 = lambda b, m: (b, m, 0)
    tile4 = lambda b, m: (b, 0, m, 0)
    const = lambda b, m: (0, 0)
    in_specs = []
    for (_, dil), width in [(g, D_ATTN) for g in ATTN_GROUPS] + [(g, STAT_LANES) for g in ATTN_GROUPS]:
        if dil == 1:
            in_specs.append(pl.BlockSpec((None, None, tm, width), tile4))
        else:
            in_specs.append(pl.BlockSpec((None, dil, tm // dil, width), tile4))
    in_specs += [
        pl.BlockSpec((None, tm, D_ATTN), tile3),
        pl.BlockSpec((None, tm, D), tile3),
        pl.BlockSpec((None, tm, D), tile3),
        pl.BlockSpec((None, tm, D), tile3),
        pl.BlockSpec((D_ATTN, D), const),
        pl.BlockSpec((D, D), const),
        pl.BlockSpec((1, D), const),
    ]
    scratch = [
        pltpu.VMEM((D_ATTN // LANES, tm, LANES), F32),
        pltpu.VMEM((D_ATTN // LANES, tm, LANES), F32),
        pltpu.VMEM((tm, STAT_LANES), F32),
        pltpu.VMEM((tm, STAT_LANES), F32),
    ]
    return pl.pallas_call(
        functools.partial(_merge_out_kernel, tm=tm),
        grid=(B, S // tm),
        in_specs=in_specs,
        out_specs=pl.BlockSpec((None, tm, D), tile3),
        out_shape=jax.ShapeDtypeStruct((B, S, D), F32),
        scratch_shapes=scratch,
        compiler_params=pltpu.CompilerParams(
            dimension_semantics=("arbitrary", "arbitrary"), vmem_limit_bytes=VMEM_LIMIT),
        name="merge_out_prompt",
    )(*os, *sts, zg, sga, mixr, x, wao, wo, gpost)


def _out_sample(att, z, sga, mixr, x, wao, wo, gpost):
    N, D = x.shape
    full = lambda shape: pl.BlockSpec(shape, lambda i: (0, 0))
    return pl.pallas_call(
        _out_kernel,
        grid=(1,),
        in_specs=[full((N, D_ATTN)), full((N, D_ATTN)), full((N, D)), full((N, D)), full((N, D)),
                  full((D_ATTN, D)), full((D, D)), full((1, D))],
        out_specs=full((N, D)),
        out_shape=jax.ShapeDtypeStruct((N, D), F32),
        compiler_params=pltpu.CompilerParams(dimension_semantics=("arbitrary",), vmem_limit_bytes=VMEM_LIMIT),
        name="out_sample",
    )(att, z, sga, mixr, x, wao, wo, gpost)


def _block_diag_chunks(w):
    per = GATE_BLOCK // RNN_BLOCK
    w = w.reshape(N_GATE_BLOCKS, per, RNN_BLOCK, RNN_BLOCK)
    eye = jnp.eye(per, dtype=w.dtype)
    dense = w[:, :, :, None, :] * eye[None, :, None, :, None]
    return dense.reshape(N_GATE_BLOCKS, GATE_BLOCK, GATE_BLOCK)


def _reorder_w_in(w_in):
    sizes = (D_RNN, D_RNN, N_GROUPS * D_ATTN, N_GROUPS * D_ATTN, N_GROUPS * D_ATTN, D_ATTN, D_MODEL, D_MODEL)
    starts = np.concatenate([[0], np.cumsum(sizes)[:-1]])
    u0, z0, q0, k0, v0, za0, gr0, ga0 = (int(s) for s in starts)
    pieces = [(u0, D_RNN), (z0, D_RNN), (gr0, D_MODEL), (ga0, D_MODEL), (za0, D_ATTN)]
    for g in range(N_GROUPS):
        pieces += [(q0 + g * D_ATTN, D_ATTN), (k0 + g * D_ATTN, D_ATTN), (v0 + g * D_ATTN, D_ATTN)]
    return jnp.concatenate([w_in[:, s:s + n] for s, n in pieces], axis=1).astype(BF16)


def _layer(layer, yp, ys, sconv, h0, caches, norm_pre, norm_post, w_in, conv_w, conv_b, lru_w_a, lru_b_a, lru_w_x,
           lru_b_x, lru_lambda, w_rnn_out, w_attn_out, w_out):
    B, S, D = yp.shape
    N = ys.shape[0]
    row = lambda v: v.reshape(1, -1)
    w_in_b = w_in.astype(BF16)
    wa = _block_diag_chunks(lru_w_a).astype(BF16)
    wx = _block_diag_chunks(lru_w_x).astype(BF16)
    wro = w_rnn_out.astype(BF16)
    wao = w_attn_out.astype(BF16)
    wo = w_out.astype(BF16)
    shared = (w_in_b, row(norm_pre), conv_w, row(conv_b), wa, wx, row(lru_b_a), row(lru_b_x), row(lru_lambda), wro)

    xs = ys.reshape(N, D)
    att_in, mixr_s, sga_s, conv_s, h_s = _proj_rnn_sample(
        xs, *shared, sconv.reshape(N, (CONV_WIDTH - 1) * D_RNN), h0)
    n_qkv = 3 * N_GROUPS
    z_s = att_in[:, n_qkv * D_ATTN:]
    qkv_s = att_in[:, :n_qkv * D_ATTN].reshape(N, n_qkv, H_G, HEAD_DIM)

    outs = _proj_rnn_prompt2(yp, *shared, qkv_s, caches, layer)
    qkv, (zg, mixr, sga, kv0, kv1, kv2, conv_p, h_p, att_s) = outs[:9], outs[9:]
    os, sts = [], []
    for g in range(N_GROUPS):
        o, st = _attn_group(qkv[3 * g], qkv[3 * g + 1], qkv[3 * g + 2], g)
        os.append(o)
        sts.append(st)
    y_p = _merge_out_prompt(os, sts, zg, sga, mixr, yp, wao, wo, row(norm_post))
    kv_p = [kv.reshape(B, kv.shape[1], 2, H_G, HEAD_DIM) for kv in (kv0, kv1, kv2)]

    y_s = _out_sample(att_s.reshape(N, D_ATTN), z_s, sga_s, mixr_s, xs, wao, wo, row(norm_post))
    kv_s = [jnp.stack([qkv_s[:, N_GROUPS + g], qkv_s[:, 2 * N_GROUPS + g]], axis=1).reshape(
        N, 1, 2, H_G, HEAD_DIM) for g in range(N_GROUPS)]

    return (y_p, y_s.reshape(N, 1, D), conv_p, conv_s.reshape(N, CONV_WIDTH - 1, D_RNN),
            h_p.reshape(B, D_RNN), h_s, kv_p, kv_s)


def kernel(x_prompt, x_sample, state_conv, state_h, cache_kv_w128, cache_kv_w512, cache_kv_w2048, norm_pre, norm_post, w_in, conv_w, conv_b, lru_w_a, lru_b_a, lru_w_x, lru_b_x, lru_lambda, w_rnn_out, w_attn_out, w_out):
    depth = norm_pre.shape[0]
    caches = (cache_kv_w128, cache_kv_w512, cache_kv_w2048)
    yp, ys = x_prompt, x_sample
    conv_p, conv_s, h_p, h_s = [], [], [], []
    kvp = ([], [], [])
    kvs = ([], [], [])
    for l in range(depth):
        yp, ys, cp, cs, hp, hs, kv_p, kv_s = _layer(
            l, yp, ys, state_conv[l], state_h[l], caches,
            norm_pre[l], norm_post[l], w_in[l], conv_w[l], conv_b[l], lru_w_a[l], lru_b_a[l], lru_w_x[l],
            lru_b_x[l], lru_lambda[l], w_rnn_out[l], w_attn_out[l], w_out[l])
        conv_p.append(cp)
        conv_s.append(cs)
        h_p.append(hp)
        h_s.append(hs)
        for g in range(N_GROUPS):
            kvp[g].append(kv_p[g])
            kvs[g].append(kv_s[g])
    return (yp, ys, jnp.stack(conv_p), jnp.stack(conv_s), jnp.stack(h_p), jnp.stack(h_s),
            jnp.stack(kvp[0]), jnp.stack(kvs[0]), jnp.stack(kvp[1]), jnp.stack(kvs[1]),
            jnp.stack(kvp[2]), jnp.stack(kvs[2]))
```

```python
import functools

import numpy as np
import jax
import jax.numpy as jnp
from jax import lax
from jax.experimental import pallas as pl
from jax.experimental.pallas import tpu as pltpu

F32 = jnp.float32
BF16 = jnp.bfloat16

D_MODEL = 1024
D_RNN = 1024
N_RNN_BLOCKS = 16
RNN_BLOCK = D_RNN // N_RNN_BLOCKS
CONV_WIDTH = 4
LRU_C = 8.0
HEAD_DIM = 64
H_G = 8
ATTN_GROUPS = ((128, 1), (512, 4), (2048, 16))
N_GROUPS = 3
D_ATTN = H_G * HEAD_DIM
Q_BLOCK = 128
ALIBI_MAX = 8.0
EPS = 1e-6
NEG_INF = -1e30
D_IN = 2 * D_RNN + 3 * N_GROUPS * D_ATTN + D_ATTN + 2 * D_MODEL

LANES = 128
TN = 1024
N_COL_BLOCKS = D_IN // TN
COL_U = 0
COL_Z_RNN = COL_U + D_RNN
COL_Q = COL_Z_RNN + D_RNN
COL_K = COL_Q + N_GROUPS * D_ATTN
COL_V = COL_K + N_GROUPS * D_ATTN
COL_Z_ATTN = COL_V + N_GROUPS * D_ATTN
COL_G_RNN = COL_Z_ATTN + D_ATTN
COL_G_ATTN = COL_G_RNN + D_MODEL
GATE_BLOCK = 256
N_GATE_BLOCKS = D_RNN // GATE_BLOCK
STAT_LANES = LANES
STAT_LANES_PER_HEAD = STAT_LANES // H_G

TM_PROMPT = 256
TILES_PER_SAMPLE = 2
SEGMENTS = 8
TM_OUT = 1024
ATTN_CHUNK = 2048
GATE_ROWS = 128
VMEM_LIMIT = 56 * 1024 * 1024


def _alibi_slopes():
    n = N_GROUPS * H_G
    s = np.float32(2.0) ** (np.float32(-ALIBI_MAX) * np.arange(1, n + 1, dtype=np.float32) / np.float32(n))
    return s.reshape(N_GROUPS, H_G)


_SLOPES = _alibi_slopes()


def _softplus(y):
    return jnp.maximum(y, 0.0) + jnp.log1p(jnp.exp(-jnp.abs(y)))


def _sigmoid(x):
    return 0.5 * jnp.tanh(0.5 * x) + 0.5


def _lru_gates(xc, c, wa_ref, wx_ref, ba_ref, bx_ref, lam_ref):
    c0 = c * GATE_BLOCK
    xcb = xc.astype(BF16)
    r = _sigmoid(jnp.dot(xcb, wa_ref[c], preferred_element_type=F32) + ba_ref[:, c0:c0 + GATE_BLOCK])
    i = _sigmoid(jnp.dot(xcb, wx_ref[c], preferred_element_type=F32) + bx_ref[:, c0:c0 + GATE_BLOCK])
    log_a = (-LRU_C * r) * _softplus(-lam_ref[:, c0:c0 + GATE_BLOCK])
    a = jnp.exp(log_a)
    v = 1.0 - a * a
    b = jnp.where(v > 0.0, v * lax.rsqrt(v), 0.0) * i * xc
    return a, b


def _sample_heads(local_heads, first_head, slope_of, qs_ref, cache_refs, att_ref):
    scale = HEAD_DIM ** -0.5
    ri = lax.broadcasted_iota(jnp.int32, (HEAD_DIM, HEAD_DIM), 0)
    ci = lax.broadcasted_iota(jnp.int32, (HEAD_DIM, HEAD_DIM), 1)
    eye = ri == ci

    def to_col(row):
        return jnp.sum(jnp.where(eye, jnp.broadcast_to(row, (HEAD_DIM, HEAD_DIM)), 0.0), axis=-1, keepdims=True)

    def to_row(col):
        return jnp.sum(jnp.where(eye, jnp.broadcast_to(col, (HEAD_DIM, HEAD_DIM)), 0.0), axis=0, keepdims=True)

    for i in local_heads:
        h = first_head + i
        outs, lses = [], []
        for g, (window, dilation) in enumerate(ATTN_GROUPS):
            c_ref = cache_refs[g]
            pos = lax.broadcasted_iota(jnp.int32, (1, window), 1)
            dist = (window - pos).astype(F32)
            bias = jnp.where((pos % dilation) == 0, -slope_of(g, i) * dist, NEG_INF)
            q = qs_ref[g, pl.ds(h, 1), :] * scale
            s_new = jnp.sum(qs_ref[N_GROUPS + g, pl.ds(h, 1), :] * q, axis=-1, keepdims=True)
            v_col = to_col(qs_ref[2 * N_GROUPS + g, pl.ds(h, 1), :])
            s = jnp.sum(c_ref[0, i] * to_col(q), axis=0, keepdims=True) + bias
            mx = jnp.maximum(jnp.max(s, axis=-1, keepdims=True), s_new)
            e = jnp.exp(s - mx)
            e_new = jnp.exp(s_new - mx)
            l = jnp.sum(e, axis=-1, keepdims=True) + e_new
            o = jnp.sum(c_ref[1, i] * e, axis=-1, keepdims=True) + e_new * v_col
            outs.append(o / l)
            lses.append(mx + jnp.log(l))
        mm = jnp.maximum(jnp.maximum(lses[0], lses[1]), lses[2])
        ws = [jnp.exp(x - mm) for x in lses]
        col = (outs[0] * ws[0] + outs[1] * ws[1] + outs[2] * ws[2]) / (ws[0] + ws[1] + ws[2])
        att_ref[pl.ds(h, 1), :] = to_row(col)


def _proj_rnn_kernel(x_ref, w_ref, gpre_ref, cw_ref, cb_ref, wa_ref, wx_ref, ba_ref, bx_ref, lam_ref, wro_ref,
                     qs_ref, c0_ref, c1_ref, c2_ref,
                     q0_ref, k0_ref, v0_ref, q1_ref, k1_ref, v1_ref, q2_ref, k2_ref, v2_ref,
                     zg_ref, mixr_ref, sga_ref, kv0_ref, kv1_ref, kv2_ref, conv_ref, h_ref, att_ref,
                     xn_s, xnf_s, xseg_s, xnseg_s, xn4_s, xn16_s, u_s, a_s, b_s, cprev_s, carry_s,
                     *, tm, n_tiles):
    m = pl.program_id(1)
    j = pl.program_id(2)
    tile = pl.program_id(0) * n_tiles + m
    half = D_ATTN
    d1, d2 = ATTN_GROUPS[1][1], ATTN_GROUPS[2][1]
    seg = tm // SEGMENTS
    n_slab = D_MODEL // LANES
    heads_per_step = H_G // (TILES_PER_SAMPLE * 2)

    def project(lhs_s, col, width=half):
        return jnp.dot(lhs_s[...], w_ref[:, col:col + width], preferred_element_type=F32)

    def emit(out_ref, val, dilation):
        p = tm // dilation
        for r in range(dilation):
            out_ref[r] = val[r * p:(r + 1) * p, :].astype(BF16)

    def natural_tail(out_ref, out_c0, val, dilation):
        p = tm // dilation
        for c in range(half // LANES):
            for r in range(dilation):
                xnf_s[c, pl.ds(r, p, stride=dilation), :] = val[r * p:(r + 1) * p, c * LANES:(c + 1) * LANES]
            out_ref[:, out_c0 + c * LANES:out_c0 + (c + 1) * LANES] = xnf_s[c]

    def sample_heads(step):
        part = tile % TILES_PER_SAMPLE
        first = part * (heads_per_step * 2) + step * heads_per_step

        def slope_of(g, i):
            table = [float(_SLOPES[g, t * heads_per_step * 2 + step * heads_per_step + i])
                     for t in range(TILES_PER_SAMPLE)]
            out = jnp.float32(table[0])
            for t in range(1, TILES_PER_SAMPLE):
                out = jnp.where(part == t, jnp.float32(table[t]), out)
            return out

        _sample_heads(range(heads_per_step), first, slope_of, qs_ref, (c0_ref, c1_ref, c2_ref), att_ref)

    @pl.when(j == 0)
    def _recurrent_blocks():
        @pl.when(m == 0)
        def _reset():
            cprev_s[...] = jnp.zeros((8, D_RNN), F32)
            carry_s[...] = jnp.zeros((1, D_RNN), F32)

        for s in range(SEGMENTS):
            r0 = s * seg
            xv = x_ref[r0:r0 + seg, :]
            ms = jnp.mean(xv * xv, axis=-1, keepdims=True)
            xn = (xv * lax.rsqrt(ms + EPS)) * gpre_ref[...]
            xn_s[r0:r0 + seg, :] = xn.astype(BF16)
            for c in range(n_slab):
                piece = xn[:, c * LANES:(c + 1) * LANES]
                xnf_s[c, r0:r0 + seg, :] = piece
                xseg_s[c, pl.ds(s, seg, stride=SEGMENTS), :] = piece

        for c in range(n_slab):
            xnseg_s[:, c * LANES:(c + 1) * LANES] = xseg_s[c].astype(BF16)

        u_s[...] = project(xnseg_s, COL_U, D_RNN)

        sub = lax.broadcasted_iota(jnp.int32, (SEGMENTS, GATE_BLOCK), 0)
        gate_rows = min(GATE_ROWS, tm)
        for c in range(N_GATE_BLOCKS):
            cols = slice(c * GATE_BLOCK, (c + 1) * GATE_BLOCK)

            def wrapped(t):
                v = pltpu.roll(u_s[SEGMENTS * (seg + t):SEGMENTS * (seg + t + 1), cols], 1, axis=0)
                return jnp.where(sub == 0, cprev_s[8 + t:9 + t, cols], v)

            wrap = {t: wrapped(t) for t in range(1 - CONV_WIDTH, 0)}
            for rc in range(tm // gate_rows):
                r0 = rc * gate_rows
                xc = cb_ref[:, cols] + u_s[r0:r0 + gate_rows, cols] * cw_ref[CONV_WIDTH - 1:CONV_WIDTH, cols]
                for sh in range(1, CONV_WIDTH):
                    if r0 == 0:
                        head = [wrap[t - sh] for t in range(sh)]
                        ush = jnp.concatenate(head + [u_s[0:gate_rows - SEGMENTS * sh, cols]], axis=0)
                    else:
                        ush = u_s[r0 - SEGMENTS * sh:r0 - SEGMENTS * sh + gate_rows, cols]
                    xc = xc + ush * cw_ref[CONV_WIDTH - 1 - sh:CONV_WIDTH - sh, cols]
                a, b = _lru_gates(xc, c, wa_ref, wx_ref, ba_ref, bx_ref, lam_ref)
                a_s[r0:r0 + gate_rows, cols] = a
                b_s[r0:r0 + gate_rows, cols] = b

        hloc = jnp.zeros((SEGMENTS, D_RNN), F32)
        cum = jnp.ones((SEGMENTS, D_RNN), F32)
        for t in range(seg):
            rows = slice(t * SEGMENTS, (t + 1) * SEGMENTS)
            av = a_s[rows, :]
            hloc = av * hloc + b_s[rows, :]
            cum = av * cum
            b_s[rows, :] = hloc
            a_s[rows, :] = cum
        state = carry_s[...]
        enter = []
        for s in range(SEGMENTS):
            enter.append(state)
            state = hloc[s:s + 1, :] + cum[s:s + 1, :] * state
        enter = jnp.concatenate(enter, axis=0)
        carry_s[...] = state
        h_ref[...] = state
        for k in range(1, CONV_WIDTH):
            last = u_s[tm - SEGMENTS * (k - 1) - 1:tm - SEGMENTS * (k - 1), :]
            conv_ref[CONV_WIDTH - 1 - k:CONV_WIDTH - k, :] = last
            cprev_s[8 - k:9 - k, :] = last

        z = project(xnseg_s, COL_Z_RNN, D_RNN)
        split = (seg, SEGMENTS, D_RNN)
        h = b_s[...].reshape(split) + a_s[...].reshape(split) * enter[None]
        hz = (h.reshape(tm, D_RNN) * (z * _sigmoid(z))).astype(BF16)
        yr = jnp.dot(hz, wro_ref[...], preferred_element_type=F32)

        mix = _sigmoid(project(xnseg_s, COL_G_RNN, D_MODEL)) * yr
        for c in range(n_slab):
            xseg_s[c] = mix[:, c * LANES:(c + 1) * LANES]
        for s in range(SEGMENTS):
            for c in range(n_slab):
                mixr_ref[s * seg:(s + 1) * seg, c * LANES:(c + 1) * LANES] = \
                    xseg_s[c, pl.ds(s, seg, stride=SEGMENTS), :].astype(BF16)

        sample_heads(0)

    @pl.when(j == 1)
    def _attention_blocks():
        for dil, dst in ((d1, xn4_s), (d2, xn16_s)):
            p = tm // dil
            for r in range(dil):
                for c in range(n_slab):
                    dst[r * p:(r + 1) * p, c * LANES:(c + 1) * LANES] = \
                        xnf_s[c, pl.ds(r, p, stride=dil), :].astype(BF16)

        sga_ref[...] = _sigmoid(project(xn_s, COL_G_ATTN, D_MODEL)).astype(BF16)
        zg_ref[...] = project(xn_s, COL_Z_ATTN).astype(BF16)

        lhs = (xn_s, xn4_s, xn16_s)
        qkv_refs = ((q0_ref, k0_ref, v0_ref), (q1_ref, k1_ref, v1_ref), (q2_ref, k2_ref, v2_ref))
        for g, (_, dil) in enumerate(ATTN_GROUPS):
            q_ref, k_ref, v_ref = qkv_refs[g]
            emit(q_ref, project(lhs[g], COL_Q + g * D_ATTN), dil)
            for col, out_ref, c0 in ((COL_K, k_ref, 0), (COL_V, v_ref, half)):
                r = project(lhs[g], col + g * D_ATTN)
                emit(out_ref, r, dil)
                if g == 0:
                    kv0_ref[:, c0:c0 + half] = r[tm - ATTN_GROUPS[0][0]:tm, :]
                else:
                    natural_tail(kv1_ref if g == 1 else kv2_ref, c0, r, dil)
        sample_heads(1)


def _proj_rnn_prompt(x, w_in, gpre, cw, cb, wa, wx, ba, bx, lam, wro, qkv_s, caches, layer):
    B, S, D = x.shape
    tm = TM_PROMPT
    n_tiles = S // tm
    N = qkv_s.shape[0]
    heads_per_step = H_G // (TILES_PER_SAMPLE * 2)
    assert S % tm == 0 and tm % (SEGMENTS * 16) == 0 and tm >= ATTN_GROUPS[0][0]
    assert all(w % tm == 0 and S >= w for w, _ in ATTN_GROUPS[1:])
    assert B * n_tiles == N * TILES_PER_SAMPLE and n_tiles % TILES_PER_SAMPLE == 0
    cache_views = [jnp.transpose(c, (0, 1, 3, 4, 5, 2)) for c in caches]

    const2 = lambda b, m, j: (0, 0)
    const3 = lambda b, m, j: (0, 0, 0)
    tile3 = lambda b, m, j: (b, m, 0)
    tile4 = lambda b, m, j: (b, 0, m, 0)
    per_seq = lambda b, m, j: (b, 0, 0)
    sample_of = lambda b, m: (b * n_tiles + m) // TILES_PER_SAMPLE
    head_block = lambda b, m, j: ((b * n_tiles + m) % TILES_PER_SAMPLE) * 2 + j

    def last_rows(window):
        first_tile = n_tiles - window // tm
        return lambda b, m, j: (b, jnp.maximum(m - first_tile, 0), 0)

    resident = dict(pipeline_mode=pl.Buffered(1))
    qkv_shapes, qkv_specs = [], []
    for _, dil in ATTN_GROUPS:
        for _ in range(3):
            qkv_shapes.append(jax.ShapeDtypeStruct((B, dil, S // dil, D_ATTN), BF16))
            qkv_specs.append(pl.BlockSpec((None, dil, tm // dil, D_ATTN), tile4))
    out_shape = tuple(qkv_shapes) + (
        jax.ShapeDtypeStruct((B, S, D_ATTN), BF16),
        jax.ShapeDtypeStruct((B, S, D), BF16),
        jax.ShapeDtypeStruct((B, S, D), BF16),
        jax.ShapeDtypeStruct((B, ATTN_GROUPS[0][0], 2 * D_ATTN), F32),
        jax.ShapeDtypeStruct((B, ATTN_GROUPS[1][0], 2 * D_ATTN), F32),
        jax.ShapeDtypeStruct((B, ATTN_GROUPS[2][0], 2 * D_ATTN), F32),
        jax.ShapeDtypeStruct((B, CONV_WIDTH - 1, D_RNN), F32),
        jax.ShapeDtypeStruct((B, 1, D_RNN), F32),
        jax.ShapeDtypeStruct((N, H_G, HEAD_DIM), F32),
    )
    out_specs = tuple(qkv_specs) + (
        pl.BlockSpec((None, tm, D_ATTN), tile3),
        pl.BlockSpec((None, tm, D), tile3),
        pl.BlockSpec((None, tm, D), tile3),
        pl.BlockSpec((None, ATTN_GROUPS[0][0], 2 * D_ATTN), per_seq),
        pl.BlockSpec((None, tm, 2 * D_ATTN), last_rows(ATTN_GROUPS[1][0])),
        pl.BlockSpec((None, tm, 2 * D_ATTN), last_rows(ATTN_GROUPS[2][0])),
        pl.BlockSpec((None, CONV_WIDTH - 1, D_RNN), per_seq),
        pl.BlockSpec((None, 1, D_RNN), per_seq),
        pl.BlockSpec((None, H_G, HEAD_DIM), lambda b, m, j: (sample_of(b, m), 0, 0)),
    )
    in_specs = [
        pl.BlockSpec((None, tm, D), tile3),
        pl.BlockSpec((D, D_IN), const2, **resident),
        pl.BlockSpec((1, D), const2),
        pl.BlockSpec((CONV_WIDTH, D_RNN), const2),
        pl.BlockSpec((1, D_RNN), const2),
        pl.BlockSpec((N_GATE_BLOCKS, GATE_BLOCK, GATE_BLOCK), const3),
        pl.BlockSpec((N_GATE_BLOCKS, GATE_BLOCK, GATE_BLOCK), const3),
        pl.BlockSpec((1, D_RNN), const2),
        pl.BlockSpec((1, D_RNN), const2),
        pl.BlockSpec((1, D_RNN), const2),
        pl.BlockSpec((D_RNN, D), const2, **resident),
        pl.BlockSpec((None, 3 * N_GROUPS, H_G, HEAD_DIM), lambda b, m, j: (sample_of(b, m), 0, 0, 0)),
    ]
    in_specs += [pl.BlockSpec((None, None, 2, heads_per_step, HEAD_DIM, window),
                              lambda b, m, j: (layer, sample_of(b, m), 0, head_block(b, m, j), 0, 0))
                 for window, _ in ATTN_GROUPS]
    scratch = [
        pltpu.VMEM((tm, D), BF16),
        pltpu.VMEM((D // LANES, tm, LANES), F32),
        pltpu.VMEM((D // LANES, tm, LANES), F32),
        pltpu.VMEM((tm, D), BF16),
        pltpu.VMEM((tm, D), BF16),
        pltpu.VMEM((tm, D), BF16),
        pltpu.VMEM((tm, D_RNN), F32),
        pltpu.VMEM((tm, D_RNN), F32),
        pltpu.VMEM((tm, D_RNN), F32),
        pltpu.VMEM((8, D_RNN), F32),
        pltpu.VMEM((1, D_RNN), F32),
    ]
    return pl.pallas_call(
        functools.partial(_proj_rnn_kernel, tm=tm, n_tiles=n_tiles),
        grid=(B, n_tiles, 2),
        in_specs=in_specs,
        out_specs=out_specs,
        out_shape=out_shape,
        scratch_shapes=scratch,
        compiler_params=pltpu.CompilerParams(
            dimension_semantics=("arbitrary", "arbitrary", "arbitrary"), vmem_limit_bytes=VMEM_LIMIT),
        name="proj_rnn_prompt",
    )(x, w_in, gpre, cw, cb, wa, wx, ba, bx, lam, wro, qkv_s, *cache_views)


def _proj_rnn_sample_kernel(x_ref, w_ref, gpre_ref, cw_ref, cb_ref, wa_ref, wx_ref, ba_ref, bx_ref, lam_ref, wro_ref,
                            sc_ref, h0_ref,
                            att_ref, mixr_ref, sga_ref, conv_ref, h_ref, wb_ref,
                            xn_s, h_s, yr_s):
    j = pl.program_id(0)

    def project():
        wb_ref[...] = w_ref[...].astype(BF16)
        return jnp.dot(xn_s[...], wb_ref[...], preferred_element_type=F32)

    @pl.when(j == 0)
    def _u_block():
        xv = x_ref[...]
        ms = jnp.mean(xv * xv, axis=-1, keepdims=True)
        xn_s[...] = ((xv * lax.rsqrt(ms + EPS)) * gpre_ref[...]).astype(BF16)
        u = project()
        taps = [sc_ref[:, k * D_RNN:(k + 1) * D_RNN] for k in range(CONV_WIDTH - 1)] + [u]
        for c in range(N_GATE_BLOCKS):
            c0 = c * GATE_BLOCK
            xc = cb_ref[:, c0:c0 + GATE_BLOCK]
            for tap in range(CONV_WIDTH):
                xc = xc + taps[tap][:, c0:c0 + GATE_BLOCK] * cw_ref[tap:tap + 1, c0:c0 + GATE_BLOCK]
            a, b = _lru_gates(xc, c, wa_ref, wx_ref, ba_ref, bx_ref, lam_ref)
            h = a * h0_ref[:, c0:c0 + GATE_BLOCK] + b
            h_s[:, c0:c0 + GATE_BLOCK] = h
            h_ref[:, c0:c0 + GATE_BLOCK] = h
        for k in range(1, CONV_WIDTH):
            conv_ref[:, (k - 1) * D_RNN:k * D_RNN] = taps[k]

    @pl.when(j == 1)
    def _z_rnn_block():
        z = project()
        hz = (h_s[...] * (z * _sigmoid(z))).astype(BF16)
        yr_s[...] = jnp.dot(hz, wro_ref[...], preferred_element_type=F32)

    @pl.when(j == 2)
    def _g_rnn_block():
        mixr_ref[...] = _sigmoid(project()) * yr_s[...]

    @pl.when(j == 3)
    def _g_attn_block():
        sga_ref[...] = _sigmoid(project())

    @pl.when(j >= 4)
    def _attn_blocks():
        att_ref[...] = project()


def _proj_rnn_sample(x, w_in, gpre, cw, cb, wa, wx, ba, bx, lam, wro, sconv, h0):
    N, D = x.shape
    const2 = lambda j: (0, 0)
    const3 = lambda j: (0, 0, 0)
    n_att = N_COL_BLOCKS - 4
    first_att, g_rnn_block = COL_Q // TN, COL_G_RNN // TN
    assert COL_Q % TN == 0 and COL_G_RNN % TN == 0 and COL_G_RNN - COL_Q == n_att * TN
    w_block = lambda j: jnp.where(j < 2, j, jnp.where(j < 4, j + g_rnn_block - 2, j - 4 + first_att))
    out_shape = (
        jax.ShapeDtypeStruct((N, n_att * TN), F32),
        jax.ShapeDtypeStruct((N, D), F32),
        jax.ShapeDtypeStruct((N, D), F32),
        jax.ShapeDtypeStruct((N, (CONV_WIDTH - 1) * D_RNN), F32),
        jax.ShapeDtypeStruct((N, D_RNN), F32),
        jax.ShapeDtypeStruct((D, D_IN), BF16),
    )
    out_specs = (
        pl.BlockSpec((N, TN), lambda j: (0, jnp.maximum(j - 4, 0))),
        pl.BlockSpec((N, D), const2),
        pl.BlockSpec((N, D), const2),
        pl.BlockSpec((N, (CONV_WIDTH - 1) * D_RNN), const2),
        pl.BlockSpec((N, D_RNN), const2),
        pl.BlockSpec((D, TN), lambda j: (0, w_block(j))),
    )
    in_specs = [
        pl.BlockSpec((N, D), const2),
        pl.BlockSpec((D, TN), lambda j: (0, w_block(j))),
        pl.BlockSpec((1, D), const2),
        pl.BlockSpec((CONV_WIDTH, D_RNN), const2),
        pl.BlockSpec((1, D_RNN), const2),
        pl.BlockSpec((N_GATE_BLOCKS, GATE_BLOCK, GATE_BLOCK), const3),
        pl.BlockSpec((N_GATE_BLOCKS, GATE_BLOCK, GATE_BLOCK), const3),
        pl.BlockSpec((1, D_RNN), const2),
        pl.BlockSpec((1, D_RNN), const2),
        pl.BlockSpec((1, D_RNN), const2),
        pl.BlockSpec((D_RNN, D), const2),
        pl.BlockSpec((N, (CONV_WIDTH - 1) * D_RNN), const2),
        pl.BlockSpec((N, D_RNN), const2),
    ]
    scratch = [pltpu.VMEM((N, D), BF16), pltpu.VMEM((N, D_RNN), F32), pltpu.VMEM((N, D), F32)]
    return pl.pallas_call(
        _proj_rnn_sample_kernel,
        grid=(N_COL_BLOCKS,),
        in_specs=in_specs,
        out_specs=out_specs,
        out_shape=out_shape,
        scratch_shapes=scratch,
        compiler_params=pltpu.CompilerParams(dimension_semantics=("arbitrary",), vmem_limit_bytes=VMEM_LIMIT),
        name="proj_rnn_sample",
    )(x, w_in, gpre, cw, cb, wa, wx, ba, bx, lam, wro, sconv, h0)


def _attn_kernel(q_ref, kc_ref, vc_ref, kp_ref, vp_ref, o_ref, st_ref, bias_s, *, group, chunk, n_res):
    dilation = ATTN_GROUPS[group][1]
    c = pl.program_id(2)
    nt = (((1,), (1,)), ((), ()))

    @pl.when((pl.program_id(0) == 0) & (pl.program_id(1) == 0) & (c == 0))
    def _init_bias():
        qi = lax.broadcasted_iota(jnp.int32, (Q_BLOCK, Q_BLOCK), 0)
        kj = lax.broadcasted_iota(jnp.int32, (Q_BLOCK, Q_BLOCK), 1)
        steps_prev = Q_BLOCK + qi - kj
        steps_cur = qi - kj
        dist_prev = (steps_prev * dilation).astype(F32)
        dist_cur = (steps_cur * dilation).astype(F32)
        for h in range(H_G):
            slope = float(_SLOPES[group, h])
            bias_s[h] = jnp.where(steps_cur >= 0, -slope * dist_cur, NEG_INF)
            bias_s[H_G + h] = jnp.where(steps_prev <= Q_BLOCK, -slope * dist_prev, NEG_INF)
            bias_s[2 * H_G + h] = jnp.full((Q_BLOCK, Q_BLOCK), NEG_INF, F32)

    lane = lax.broadcasted_iota(jnp.int32, (Q_BLOCK, LANES), 1)
    low = lane < HEAD_DIM
    lane2 = lax.broadcasted_iota(jnp.int32, (2 * Q_BLOCK, LANES), 1)
    low2 = lane2 < HEAD_DIM
    ones_lo = jnp.where(low2, 1.0, 0.0).astype(BF16)
    ones_hi = jnp.where(low2, 0.0, 1.0).astype(BF16)
    stat_lane = lax.broadcasted_iota(jnp.int32, (Q_BLOCK, STAT_LANES), 1)

    def keys_of(res, block, cur_ref, prev_ref, sl):
        if block == 0:
            return jnp.concatenate([prev_ref[res, :, sl], cur_ref[res, 0:Q_BLOCK, sl]], axis=0)
        return cur_ref[res, (block - 1) * Q_BLOCK:(block + 1) * Q_BLOCK, sl]

    for res, i in [(res, i) for res in range(n_res) for i in range(chunk // Q_BLOCK)]:
        r0 = i * Q_BLOCK
        first = jnp.where(c == 0, H_G, 0) if i == 0 else 0
        stats = jnp.zeros((Q_BLOCK, STAT_LANES), F32)
        for p in range(H_G // 2):
            sl = slice(LANES * p, LANES * (p + 1))
            qp = q_ref[res, r0:r0 + Q_BLOCK, sl] * (HEAD_DIM ** -0.5)
            kp = keys_of(res, i, kc_ref, kp_ref, sl)
            vp = keys_of(res, i, vc_ref, vp_ref, sl)
            es, ms = [], []
            for hh in range(2):
                h = 2 * p + hh
                msk = low if hh == 0 else jnp.logical_not(low)
                qm = jnp.where(msk, qp, jnp.zeros_like(qp))
                s = lax.dot_general(qm, kp, nt, preferred_element_type=F32)
                s_p = s[:, 0:Q_BLOCK] + bias_s[H_G + h + first]
                s_c = s[:, Q_BLOCK:2 * Q_BLOCK] + bias_s[h]
                mx = jnp.maximum(jnp.max(s_p, axis=-1, keepdims=True), jnp.max(s_c, axis=-1, keepdims=True))
                es.append(jnp.exp(s_p - mx).astype(BF16))
                es.append(jnp.exp(s_c - mx).astype(BF16))
                ms.append(mx)
            vm0 = jnp.where(low2, vp, jnp.zeros_like(vp))
            vm1 = jnp.where(low2, jnp.zeros_like(vp), vp)
            w = jnp.concatenate([jnp.concatenate([vm0, ones_lo], axis=1),
                                 jnp.concatenate([vm1, ones_hi], axis=1)], axis=0)
            acc = jnp.dot(jnp.concatenate(es, axis=1), w, preferred_element_type=F32)
            l_pair = acc[:, LANES:2 * LANES]
            o_ref[res, r0:r0 + Q_BLOCK, sl] = (acc[:, 0:LANES] / l_pair).astype(o_ref.dtype)
            lse_pair = jnp.where(low, ms[0], ms[1]) + jnp.log(l_pair)
            keep = ((stat_lane % (STAT_LANES // 2)) // STAT_LANES_PER_HEAD) == p
            stats = jnp.where(keep, lse_pair, stats)
        st_ref[res, r0:r0 + Q_BLOCK, :] = stats


def _attn_group(q, k, v, group):
    B, dil, L, _ = q.shape
    chunk = min(ATTN_CHUNK, L)
    n_res = min(ATTN_CHUNK // chunk, dil)
    assert L % chunk == 0 and chunk % Q_BLOCK == 0 and dil % n_res == 0
    cur = lambda b, r, c: (b, r, c, 0)
    prev = lambda b, r, c: (b, r, jnp.maximum(c * (chunk // Q_BLOCK) - 1, 0), 0)
    blk = (None, n_res, chunk, D_ATTN)
    pblk = (None, n_res, Q_BLOCK, D_ATTN)
    return pl.pallas_call(
        functools.partial(_attn_kernel, group=group, chunk=chunk, n_res=n_res),
        grid=(B, dil // n_res, L // chunk),
        in_specs=[pl.BlockSpec(blk, cur), pl.BlockSpec(blk, cur), pl.BlockSpec(blk, cur),
                  pl.BlockSpec(pblk, prev), pl.BlockSpec(pblk, prev)],
        out_specs=[pl.BlockSpec(blk, cur), pl.BlockSpec((None, n_res, chunk, STAT_LANES), cur)],
        out_shape=[jax.ShapeDtypeStruct((B, dil, L, D_ATTN), BF16),
                   jax.ShapeDtypeStruct((B, dil, L, STAT_LANES), F32)],
        scratch_shapes=[pltpu.VMEM((3 * H_G, Q_BLOCK, Q_BLOCK), F32)],
        compiler_params=pltpu.CompilerParams(
            dimension_semantics=("arbitrary", "arbitrary", "arbitrary"), vmem_limit_bytes=VMEM_LIMIT),
        name=f"attn_group{group}",
    )(q, k, v, k, v)


def _finish(att, z_ref, sga_ref, mixr_ref, x_ref, wao_ref, wo_ref, gpost_ref, y_ref):
    z = z_ref[...].astype(F32)
    ya_in = (att * (z * _sigmoid(z))).astype(BF16)
    ya = jnp.dot(ya_in, wao_ref[...], preferred_element_type=F32)
    mixed = mixr_ref[...].astype(F32) + sga_ref[...].astype(F32) * ya
    out = jnp.dot(mixed.astype(BF16), wo_ref[...], preferred_element_type=F32)
    ms = jnp.mean(out * out, axis=-1, keepdims=True)
    y_ref[...] = x_ref[...] + (out * lax.rsqrt(ms + EPS)) * gpost_ref[...]


def _out_kernel(att_ref, z_ref, sga_ref, mixr_ref, x_ref, wao_ref, wo_ref, gpost_ref, y_ref):
    _finish(att_ref[...].astype(F32), z_ref, sga_ref, mixr_ref, x_ref, wao_ref, wo_ref, gpost_ref, y_ref)


def _merge_out_kernel(o0_ref, o1_ref, o2_ref, s0_ref, s1_ref, s2_ref, z_ref, sga_ref, mixr_ref, x_ref,
                      wao_ref, wo_ref, gpost_ref, y_ref, nat1_s, nat2_s, st1_s, st2_s, *, tm):
    for o_ref, s_ref, nat_s, stn_s, dil in ((o1_ref, s1_ref, nat1_s, st1_s, ATTN_GROUPS[1][1]),
                                            (o2_ref, s2_ref, nat2_s, st2_s, ATTN_GROUPS[2][1])):
        p = tm // dil
        for r in range(dil):
            stn_s[pl.ds(r, p, stride=dil), :] = s_ref[r]
            for c in range(D_ATTN // LANES):
                nat_s[c, pl.ds(r, p, stride=dil), :] = o_ref[r, :, c * LANES:(c + 1) * LANES].astype(F32)

    lse = [s0_ref[...], st1_s[...], st2_s[...]]
    mm = jnp.maximum(jnp.maximum(lse[0], lse[1]), lse[2])
    ws = [jnp.exp(x - mm) for x in lse]
    den = ws[0] + ws[1] + ws[2]
    src = lax.broadcasted_iota(jnp.int32, (STAT_LANES, D_ATTN), 0)
    dst_head = lax.broadcasted_iota(jnp.int32, (STAT_LANES, D_ATTN), 1) // HEAD_DIM
    expand = jnp.where(src == (dst_head % 2) * (STAT_LANES // 2) + (dst_head // 2) * STAT_LANES_PER_HEAD,
                       1.0, 0.0).astype(BF16)
    outs = [o0_ref[...].astype(F32),
            jnp.concatenate([nat1_s[c] for c in range(D_ATTN // LANES)], axis=1),
            jnp.concatenate([nat2_s[c] for c in range(D_ATTN // LANES)], axis=1)]
    att = jnp.zeros((tm, D_ATTN), F32)
    for g in range(N_GROUPS):
        wexp = jnp.dot((ws[g] / den).astype(BF16), expand, preferred_element_type=F32)
        att = att + wexp * outs[g]
    _finish(att, z_ref, sga_ref, mixr_ref, x_ref, wao_ref, wo_ref, gpost_ref, y_ref)


def _merge_out_prompt(os, sts, zg, sga, mixr, x, wao, wo, gpost):
    B, S, D = x.shape
    tm = TM_OUT
    assert S % tm == 0
    tile3 = lambda b, m: (b, m, 0)
    tile4 = lambda b, m: (b, 0, m, 0)
    const = lambda b, m: (0, 0)
    in_specs = []
    for (_, dil), width in [(g, D_ATTN) for g in ATTN_GROUPS] + [(g, STAT_LANES) for g in ATTN_GROUPS]:
        if dil == 1:
            in_specs.append(pl.BlockSpec((None, None, tm, width), tile4))
        else:
            in_specs.append(pl.BlockSpec((None, dil, tm // dil, width), tile4))
    in_specs += [
        pl.BlockSpec((None, tm, D_ATTN), tile3),
        pl.BlockSpec((None, tm, D), tile3),
        pl.BlockSpec((None, tm, D), tile3),
        pl.BlockSpec((None, tm, D), tile3),
        pl.BlockSpec((D_ATTN, D), const),
        pl.BlockSpec((D, D), const),
        pl.BlockSpec((1, D), const),
    ]
    scratch = [
        pltpu.VMEM((D_ATTN // LANES, tm, LANES), F32),
        pltpu.VMEM((D_ATTN // LANES, tm, LANES), F32),
        pltpu.VMEM((tm, STAT_LANES), F32),
        pltpu.VMEM((tm, STAT_LANES), F32),
    ]
    return pl.pallas_call(
        functools.partial(_merge_out_kernel, tm=tm),
        grid=(B, S // tm),
        in_specs=in_specs,
        out_specs=pl.BlockSpec((None, tm, D), tile3),
        out_shape=jax.ShapeDtypeStruct((B, S, D), F32),
        scratch_shapes=scratch,
        compiler_params=pltpu.CompilerParams(
            dimension_semantics=("arbitrary", "arbitrary"), vmem_limit_bytes=VMEM_LIMIT),
        name="merge_out_prompt",
    )(*os, *sts, zg, sga, mixr, x, wao, wo, gpost)


def _out_sample(att, z, sga, mixr, x, wao, wo, gpost):
    N, D = x.shape
    full = lambda shape: pl.BlockSpec(shape, lambda i: (0, 0))
    return pl.pallas_call(
        _out_kernel,
        grid=(1,),
        in_specs=[full((N, D_ATTN)), full((N, D_ATTN)), full((N, D)), full((N, D)), full((N, D)),
                  full((D_ATTN, D)), full((D, D)), full((1, D))],
        out_specs=full((N, D)),
        out_shape=jax.ShapeDtypeStruct((N, D), F32),
        compiler_params=pltpu.CompilerParams(dimension_semantics=("arbitrary",), vmem_limit_bytes=VMEM_LIMIT),
        name="out_sample",
    )(att, z, sga, mixr, x, wao, wo, gpost)


def _block_diag_chunks(w):
    per = GATE_BLOCK // RNN_BLOCK
    w = w.reshape(N_GATE_BLOCKS, per, RNN_BLOCK, RNN_BLOCK)
    eye = jnp.eye(per, dtype=w.dtype)
    dense = w[:, :, :, None, :] * eye[None, :, None, :, None]
    return dense.reshape(N_GATE_BLOCKS, GATE_BLOCK, GATE_BLOCK)


def _layer(layer, yp, ys, sconv, h0, caches, norm_pre, norm_post, w_in, conv_w, conv_b, lru_w_a, lru_b_a, lru_w_x,
           lru_b_x, lru_lambda, w_rnn_out, w_attn_out, w_out):
    B, S, D = yp.shape
    N = ys.shape[0]
    row = lambda v: v.reshape(1, -1)
    wa = _block_diag_chunks(lru_w_a).astype(BF16)
    wx = _block_diag_chunks(lru_w_x).astype(BF16)
    wro = w_rnn_out.astype(BF16)
    wao = w_attn_out.astype(BF16)
    wo = w_out.astype(BF16)
    params = (row(norm_pre), conv_w, row(conv_b), wa, wx, row(lru_b_a), row(lru_b_x), row(lru_lambda), wro)

    xs = ys.reshape(N, D)
    att_in, mixr_s, sga_s, conv_s, h_s, w_in_b = _proj_rnn_sample(
        xs, w_in, *params, sconv.reshape(N, (CONV_WIDTH - 1) * D_RNN), h0)
    shared = (w_in_b,) + params
    n_qkv = 3 * N_GROUPS
    z_s = att_in[:, n_qkv * D_ATTN:]
    qkv_s = att_in[:, :n_qkv * D_ATTN].reshape(N, n_qkv, H_G, HEAD_DIM)

    outs = _proj_rnn_prompt(yp, *shared, qkv_s, caches, layer)
    qkv, (zg, mixr, sga, kv0, kv1, kv2, conv_p, h_p, att_s) = outs[:9], outs[9:]
    os, sts = [], []
    for g in range(N_GROUPS):
        o, st = _attn_group(qkv[3 * g], qkv[3 * g + 1], qkv[3 * g + 2], g)
        os.append(o)
        sts.append(st)
    y_p = _merge_out_prompt(os, sts, zg, sga, mixr, yp, wao, wo, row(norm_post))
    kv_p = [kv.reshape(B, kv.shape[1], 2, H_G, HEAD_DIM) for kv in (kv0, kv1, kv2)]

    y_s = _out_sample(att_s.reshape(N, D_ATTN), z_s, sga_s, mixr_s, xs, wao, wo, row(norm_post))
    kv_s = [jnp.stack([qkv_s[:, N_GROUPS + g], qkv_s[:, 2 * N_GROUPS + g]], axis=1).reshape(
        N, 1, 2, H_G, HEAD_DIM) for g in range(N_GROUPS)]

    return (y_p, y_s.reshape(N, 1, D), conv_p, conv_s.reshape(N, CONV_WIDTH - 1, D_RNN),
            h_p.reshape(B, D_RNN), h_s, kv_p, kv_s)


def kernel(x_prompt, x_sample, state_conv, state_h, cache_kv_w128, cache_kv_w512, cache_kv_w2048, norm_pre, norm_post, w_in, conv_w, conv_b, lru_w_a, lru_b_a, lru_w_x, lru_b_x, lru_lambda, w_rnn_out, w_attn_out, w_out):
    depth = norm_pre.shape[0]
    caches = (cache_kv_w128, cache_kv_w512, cache_kv_w2048)
    yp, ys = x_prompt, x_sample
    conv_p, conv_s, h_p, h_s = [], [], [], []
    kvp = ([], [], [])
    kvs = ([], [], [])
    for l in range(depth):
        yp, ys, cp, cs, hp, hs, kv_p, kv_s = _layer(
            l, yp, ys, state_conv[l], state_h[l], caches,
            norm_pre[l], norm_post[l], w_in[l], conv_w[l], conv_b[l], lru_w_a[l], lru_b_a[l], lru_w_x[l],
            lru_b_x[l], lru_lambda[l], w_rnn_out[l], w_attn_out[l], w_out[l])
        conv_p.append(cp)
        conv_s.append(cs)
        h_p.append(hp)
        h_s.append(hs)
        for g in range(N_GROUPS):
            kvp[g].append(kv_p[g])
            kvs[g].append(kv_s[g])
    return (yp, ys, jnp.stack(conv_p), jnp.stack(conv_s), jnp.stack(h_p), jnp.stack(h_s),
            jnp.stack(kvp[0]), jnp.stack(kvs[0]), jnp.stack(kvp[1]), jnp.stack(kvs[1]),
            jnp.stack(kvp[2]), jnp.stack(kvs[2]))
```

```python
import functools

import numpy as np
import jax
import jax.numpy as jnp
from jax import lax
from jax.experimental import pallas as pl
from jax.experimental.pallas import tpu as pltpu

F32 = jnp.float32
BF16 = jnp.bfloat16

D_MODEL = 1024
D_RNN = 1024
N_RNN_BLOCKS = 16
RNN_BLOCK = D_RNN // N_RNN_BLOCKS
CONV_WIDTH = 4
LRU_C = 8.0
HEAD_DIM = 64
H_G = 8
ATTN_GROUPS = ((128, 1), (512, 4), (2048, 16))
N_GROUPS = 3
D_ATTN = H_G * HEAD_DIM
Q_BLOCK = 128
ALIBI_MAX = 8.0
EPS = 1e-6
NEG_INF = -1e30
D_IN = 2 * D_RNN + 3 * N_GROUPS * D_ATTN + D_ATTN + 2 * D_MODEL

LANES = 128
TN = 1024
N_COL_BLOCKS = D_IN // TN
COL_U = 0
COL_Z_RNN = COL_U + D_RNN
COL_Q = COL_Z_RNN + D_RNN
COL_K = COL_Q + N_GROUPS * D_ATTN
COL_V = COL_K + N_GROUPS * D_ATTN
COL_Z_ATTN = COL_V + N_GROUPS * D_ATTN
COL_G_RNN = COL_Z_ATTN + D_ATTN
COL_G_ATTN = COL_G_RNN + D_MODEL
GATE_BLOCK = 256
N_GATE_BLOCKS = D_RNN // GATE_BLOCK
STAT_LANES = LANES
STAT_LANES_PER_HEAD = STAT_LANES // H_G

TM_PROMPT = 256
TILES_PER_SAMPLE = 2
SEGMENTS = 8
TM_OUT = 1024
ATTN_CHUNK = 2048
GATE_ROWS = 128
VMEM_LIMIT = 56 * 1024 * 1024


def _alibi_slopes():
    n = N_GROUPS * H_G
    s = np.float32(2.0) ** (np.float32(-ALIBI_MAX) * np.arange(1, n + 1, dtype=np.float32) / np.float32(n))
    return s.reshape(N_GROUPS, H_G)


_SLOPES = _alibi_slopes()


def _softplus(y):
    return jnp.maximum(y, 0.0) + jnp.log1p(jnp.exp(-jnp.abs(y)))


def _sigmoid(x):
    return 0.5 * jnp.tanh(0.5 * x) + 0.5


def _lru_gates(xc, c, wa_ref, wx_ref, ba_ref, bx_ref, lam_ref):
    c0 = c * GATE_BLOCK
    xcb = xc.astype(BF16)
    r = _sigmoid(jnp.dot(xcb, wa_ref[c], preferred_element_type=F32) + ba_ref[:, c0:c0 + GATE_BLOCK])
    i = _sigmoid(jnp.dot(xcb, wx_ref[c], preferred_element_type=F32) + bx_ref[:, c0:c0 + GATE_BLOCK])
    log_a = (-LRU_C * r) * _softplus(-lam_ref[:, c0:c0 + GATE_BLOCK])
    a = jnp.exp(log_a)
    v = 1.0 - a * a
    b = jnp.where(v > 0.0, v * lax.rsqrt(v), 0.0) * i * xc
    return a, b


def _sample_heads(local_heads, first_head, slope_of, qs_ref, cache_refs, att_ref):
    scale = HEAD_DIM ** -0.5
    ri = lax.broadcasted_iota(jnp.int32, (HEAD_DIM, HEAD_DIM), 0)
    ci = lax.broadcasted_iota(jnp.int32, (HEAD_DIM, HEAD_DIM), 1)
    eye = ri == ci

    def to_col(row):
        return jnp.sum(jnp.where(eye, jnp.broadcast_to(row, (HEAD_DIM, HEAD_DIM)), 0.0), axis=-1, keepdims=True)

    def to_row(col):
        return jnp.sum(jnp.where(eye, jnp.broadcast_to(col, (HEAD_DIM, HEAD_DIM)), 0.0), axis=0, keepdims=True)

    for i in local_heads:
        h = first_head + i
        outs, lses = [], []
        for g, (window, dilation) in enumerate(ATTN_GROUPS):
            c_ref = cache_refs[g]
            pos = lax.broadcasted_iota(jnp.int32, (1, window), 1)
            dist = (window - pos).astype(F32)
            bias = jnp.where((pos % dilation) == 0, -slope_of(g, i) * dist, NEG_INF)
            q = qs_ref[g, pl.ds(h, 1), :] * scale
            s_new = jnp.sum(qs_ref[N_GROUPS + g, pl.ds(h, 1), :] * q, axis=-1, keepdims=True)
            v_col = to_col(qs_ref[2 * N_GROUPS + g, pl.ds(h, 1), :])
            s = jnp.sum(c_ref[0, i] * to_col(q), axis=0, keepdims=True) + bias
            mx = jnp.maximum(jnp.max(s, axis=-1, keepdims=True), s_new)
            e = jnp.exp(s - mx)
            e_new = jnp.exp(s_new - mx)
            l = jnp.sum(e, axis=-1, keepdims=True) + e_new
            o = jnp.sum(c_ref[1, i] * e, axis=-1, keepdims=True) + e_new * v_col
            outs.append(o / l)
            lses.append(mx + jnp.log(l))
        mm = jnp.maximum(jnp.maximum(lses[0], lses[1]), lses[2])
        ws = [jnp.exp(x - mm) for x in lses]
        col = (outs[0] * ws[0] + outs[1] * ws[1] + outs[2] * ws[2]) / (ws[0] + ws[1] + ws[2])
        att_ref[pl.ds(h, 1), :] = to_row(col)


def _proj_rnn_kernel(x_ref, w_ref, gpre_ref, cw_ref, cb_ref, wa_ref, wx_ref, ba_ref, bx_ref, lam_ref, wro_ref,
                     qs_ref, c0_ref, c1_ref, c2_ref,
                     q0_ref, k0_ref, v0_ref, q1_ref, k1_ref, v1_ref, q2_ref, k2_ref, v2_ref,
                     zg_ref, mixr_ref, sga_ref, kv0_ref, kv1_ref, kv2_ref, conv_ref, h_ref, att_ref,
                     xn_s, xnf_s, xseg_s, xnseg_s, xn4_s, xn16_s, u_s, a_s, b_s, cprev_s, carry_s,
                     *, tm, n_tiles):
    m = pl.program_id(1)
    j = pl.program_id(2)
    tile = pl.program_id(0) * n_tiles + m
    half = D_ATTN
    d1, d2 = ATTN_GROUPS[1][1], ATTN_GROUPS[2][1]
    seg = tm // SEGMENTS
    n_slab = D_MODEL // LANES
    heads_per_step = H_G // (TILES_PER_SAMPLE * 2)

    def project(lhs_s, col, width=half):
        return jnp.dot(lhs_s[...], w_ref[:, col:col + width], preferred_element_type=F32)

    def emit(out_ref, val, dilation):
        p = tm // dilation
        for r in range(dilation):
            out_ref[r] = val[r * p:(r + 1) * p, :].astype(BF16)

    def natural_tail(out_ref, out_c0, val, dilation):
        p = tm // dilation
        for c in range(half // LANES):
            for r in range(dilation):
                xnf_s[c, pl.ds(r, p, stride=dilation), :] = val[r * p:(r + 1) * p, c * LANES:(c + 1) * LANES]
            out_ref[:, out_c0 + c * LANES:out_c0 + (c + 1) * LANES] = xnf_s[c]

    def sample_heads(step):
        part = tile % TILES_PER_SAMPLE
        first = part * (heads_per_step * 2) + step * heads_per_step

        def slope_of(g, i):
            table = [float(_SLOPES[g, t * heads_per_step * 2 + step * heads_per_step + i])
                     for t in range(TILES_PER_SAMPLE)]
            out = jnp.float32(table[0])
            for t in range(1, TILES_PER_SAMPLE):
                out = jnp.where(part == t, jnp.float32(table[t]), out)
            return out

        _sample_heads(range(heads_per_step), first, slope_of, qs_ref, (c0_ref, c1_ref, c2_ref), att_ref)

    @pl.when(j == 0)
    def _recurrent_blocks():
        @pl.when(m == 0)
        def _reset():
            cprev_s[...] = jnp.zeros((8, D_RNN), F32)
            carry_s[...] = jnp.zeros((1, D_RNN), F32)

        for s in range(SEGMENTS):
            r0 = s * seg
            xv = x_ref[r0:r0 + seg, :]
            ms = jnp.mean(xv * xv, axis=-1, keepdims=True)
            xn = (xv * lax.rsqrt(ms + EPS)) * gpre_ref[...]
            xn_s[r0:r0 + seg, :] = xn.astype(BF16)
            for c in range(n_slab):
                piece = xn[:, c * LANES:(c + 1) * LANES]
                xnf_s[c, r0:r0 + seg, :] = piece
                xseg_s[c, pl.ds(s, seg, stride=SEGMENTS), :] = piece

        for c in range(n_slab):
            xnseg_s[:, c * LANES:(c + 1) * LANES] = xseg_s[c].astype(BF16)

        u_s[...] = project(xnseg_s, COL_U, D_RNN)

        sub = lax.broadcasted_iota(jnp.int32, (SEGMENTS, GATE_BLOCK), 0)
        gate_rows = min(GATE_ROWS, tm)
        for c in range(N_GATE_BLOCKS):
            cols = slice(c * GATE_BLOCK, (c + 1) * GATE_BLOCK)

            def wrapped(t):
                v = pltpu.roll(u_s[SEGMENTS * (seg + t):SEGMENTS * (seg + t + 1), cols], 1, axis=0)
                return jnp.where(sub == 0, cprev_s[8 + t:9 + t, cols], v)

            wrap = {t: wrapped(t) for t in range(1 - CONV_WIDTH, 0)}
            for rc in range(tm // gate_rows):
                r0 = rc * gate_rows
                xc = cb_ref[:, cols] + u_s[r0:r0 + gate_rows, cols] * cw_ref[CONV_WIDTH - 1:CONV_WIDTH, cols]
                for sh in range(1, CONV_WIDTH):
                    if r0 == 0:
                        head = [wrap[t - sh] for t in range(sh)]
                        ush = jnp.concatenate(head + [u_s[0:gate_rows - SEGMENTS * sh, cols]], axis=0)
                    else:
                        ush = u_s[r0 - SEGMENTS * sh:r0 - SEGMENTS * sh + gate_rows, cols]
                    xc = xc + ush * cw_ref[CONV_WIDTH - 1 - sh:CONV_WIDTH - sh, cols]
                a, b = _lru_gates(xc, c, wa_ref, wx_ref, ba_ref, bx_ref, lam_ref)
                a_s[r0:r0 + gate_rows, cols] = a
                b_s[r0:r0 + gate_rows, cols] = b

        hloc = jnp.zeros((SEGMENTS, D_RNN), F32)
        cum = jnp.ones((SEGMENTS, D_RNN), F32)
        for t in range(seg):
            rows = slice(t * SEGMENTS, (t + 1) * SEGMENTS)
            av = a_s[rows, :]
            hloc = av * hloc + b_s[rows, :]
            cum = av * cum
            b_s[rows, :] = hloc
            a_s[rows, :] = cum
        state = carry_s[...]
        enter = []
        for s in range(SEGMENTS):
            enter.append(state)
            state = hloc[s:s + 1, :] + cum[s:s + 1, :] * state
        enter = jnp.concatenate(enter, axis=0)
        carry_s[...] = state
        h_ref[...] = state
        for k in range(1, CONV_WIDTH):
            last = u_s[tm - SEGMENTS * (k - 1) - 1:tm - SEGMENTS * (k - 1), :]
            conv_ref[CONV_WIDTH - 1 - k:CONV_WIDTH - k, :] = last
            cprev_s[8 - k:9 - k, :] = last

        z = project(xnseg_s, COL_Z_RNN, D_RNN)
        split = (seg, SEGMENTS, D_RNN)
        h = b_s[...].reshape(split) + a_s[...].reshape(split) * enter[None]
        hz = (h.reshape(tm, D_RNN) * (z * _sigmoid(z))).astype(BF16)
        yr = jnp.dot(hz, wro_ref[...], preferred_element_type=F32)

        mix = _sigmoid(project(xnseg_s, COL_G_RNN, D_MODEL)) * yr
        for c in range(n_slab):
            xseg_s[c] = mix[:, c * LANES:(c + 1) * LANES]
        for s in range(SEGMENTS):
            for c in range(n_slab):
                mixr_ref[s * seg:(s + 1) * seg, c * LANES:(c + 1) * LANES] = \
                    xseg_s[c, pl.ds(s, seg, stride=SEGMENTS), :].astype(BF16)

        sample_heads(0)

    @pl.when(j == 1)
    def _attention_blocks():
        for dil, dst in ((d1, xn4_s), (d2, xn16_s)):
            p = tm // dil
            for r in range(dil):
                for c in range(n_slab):
                    dst[r * p:(r + 1) * p, c * LANES:(c + 1) * LANES] = \
                        xnf_s[c, pl.ds(r, p, stride=dil), :].astype(BF16)

        sga_ref[...] = _sigmoid(project(xn_s, COL_G_ATTN, D_MODEL)).astype(BF16)
        zg_ref[...] = project(xn_s, COL_Z_ATTN).astype(BF16)

        lhs = (xn_s, xn4_s, xn16_s)
        qkv_refs = ((q0_ref, k0_ref, v0_ref), (q1_ref, k1_ref, v1_ref), (q2_ref, k2_ref, v2_ref))
        for g, (_, dil) in enumerate(ATTN_GROUPS):
            q_ref, k_ref, v_ref = qkv_refs[g]
            emit(q_ref, project(lhs[g], COL_Q + g * D_ATTN), dil)
            for col, out_ref, c0 in ((COL_K, k_ref, 0), (COL_V, v_ref, half)):
                r = project(lhs[g], col + g * D_ATTN)
                emit(out_ref, r, dil)
                if g == 0:
                    kv0_ref[:, c0:c0 + half] = r[tm - ATTN_GROUPS[0][0]:tm, :]
                else:
                    natural_tail(kv1_ref if g == 1 else kv2_ref, c0, r, dil)
        sample_heads(1)


def _proj_rnn_prompt(x, w_in, gpre, cw, cb, wa, wx, ba, bx, lam, wro, qkv_s, caches, layer):
    B, S, D = x.shape
    tm = TM_PROMPT
    n_tiles = S // tm
    N = qkv_s.shape[0]
    heads_per_step = H_G // (TILES_PER_SAMPLE * 2)
    assert S % tm == 0 and tm % (SEGMENTS * 16) == 0 and tm >= ATTN_GROUPS[0][0]
    assert all(w % tm == 0 and S >= w for w, _ in ATTN_GROUPS[1:])
    assert B * n_tiles == N * TILES_PER_SAMPLE and n_tiles % TILES_PER_SAMPLE == 0
    cache_views = [jnp.transpose(c, (0, 1, 3, 4, 5, 2)) for c in caches]

    const2 = lambda b, m, j: (0, 0)
    const3 = lambda b, m, j: (0, 0, 0)
    tile3 = lambda b, m, j: (b, m, 0)
    tile4 = lambda b, m, j: (b, 0, m, 0)
    per_seq = lambda b, m, j: (b, 0, 0)
    sample_of = lambda b, m: (b * n_tiles + m) // TILES_PER_SAMPLE
    head_block = lambda b, m, j: ((b * n_tiles + m) % TILES_PER_SAMPLE) * 2 + j

    def last_rows(window):
        first_tile = n_tiles - window // tm
        return lambda b, m, j: (b, jnp.maximum(m - first_tile, 0), 0)

    resident = dict(pipeline_mode=pl.Buffered(1))
    qkv_shapes, qkv_specs = [], []
    for _, dil in ATTN_GROUPS:
        for _ in range(3):
            qkv_shapes.append(jax.ShapeDtypeStruct((B, dil, S // dil, D_ATTN), BF16))
            qkv_specs.append(pl.BlockSpec((None, dil, tm // dil, D_ATTN), tile4))
    out_shape = tuple(qkv_shapes) + (
        jax.ShapeDtypeStruct((B, S, D_ATTN), BF16),
        jax.ShapeDtypeStruct((B, S, D), BF16),
        jax.ShapeDtypeStruct((B, S, D), BF16),
        jax.ShapeDtypeStruct((B, ATTN_GROUPS[0][0], 2 * D_ATTN), F32),
        jax.ShapeDtypeStruct((B, ATTN_GROUPS[1][0], 2 * D_ATTN), F32),
        jax.ShapeDtypeStruct((B, ATTN_GROUPS[2][0], 2 * D_ATTN), F32),
        jax.ShapeDtypeStruct((B, CONV_WIDTH - 1, D_RNN), F32),
        jax.ShapeDtypeStruct((B, 1, D_RNN), F32),
        jax.ShapeDtypeStruct((N, H_G, HEAD_DIM), F32),
    )
    out_specs = tuple(qkv_specs) + (
        pl.BlockSpec((None, tm, D_ATTN), tile3),
        pl.BlockSpec((None, tm, D), tile3),
        pl.BlockSpec((None, tm, D), tile3),
        pl.BlockSpec((None, ATTN_GROUPS[0][0], 2 * D_ATTN), per_seq),
        pl.BlockSpec((None, tm, 2 * D_ATTN), last_rows(ATTN_GROUPS[1][0])),
        pl.BlockSpec((None, tm, 2 * D_ATTN), last_rows(ATTN_GROUPS[2][0])),
        pl.BlockSpec((None, CONV_WIDTH - 1, D_RNN), per_seq),
        pl.BlockSpec((None, 1, D_RNN), per_seq),
        pl.BlockSpec((None, H_G, HEAD_DIM), lambda b, m, j: (sample_of(b, m), 0, 0)),
    )
    in_specs = [
        pl.BlockSpec((None, tm, D), tile3),
        pl.BlockSpec((D, D_IN), const2, **resident),
        pl.BlockSpec((1, D), const2),
        pl.BlockSpec((CONV_WIDTH, D_RNN), const2),
        pl.BlockSpec((1, D_RNN), const2),
        pl.BlockSpec((N_GATE_BLOCKS, GATE_BLOCK, GATE_BLOCK), const3),
        pl.BlockSpec((N_GATE_BLOCKS, GATE_BLOCK, GATE_BLOCK), const3),
        pl.BlockSpec((1, D_RNN), const2),
        pl.BlockSpec((1, D_RNN), const2),
        pl.BlockSpec((1, D_RNN), const2),
        pl.BlockSpec((D_RNN, D), const2, **resident),
        pl.BlockSpec((None, 3 * N_GROUPS, H_G, HEAD_DIM), lambda b, m, j: (sample_of(b, m), 0, 0, 0)),
    ]
    in_specs += [pl.BlockSpec((None, None, 2, heads_per_step, HEAD_DIM, window),
                              lambda b, m, j: (layer, sample_of(b, m), 0, head_block(b, m, j), 0, 0))
                 for window, _ in ATTN_GROUPS]
    scratch = [
        pltpu.VMEM((tm, D), BF16),
        pltpu.VMEM((D // LANES, tm, LANES), F32),
        pltpu.VMEM((D // LANES, tm, LANES), F32),
        pltpu.VMEM((tm, D), BF16),
        pltpu.VMEM((tm, D), BF16),
        pltpu.VMEM((tm, D), BF16),
        pltpu.VMEM((tm, D_RNN), F32),
        pltpu.VMEM((tm, D_RNN), F32),
        pltpu.VMEM((tm, D_RNN), F32),
        pltpu.VMEM((8, D_RNN), F32),
        pltpu.VMEM((1, D_RNN), F32),
    ]
    return pl.pallas_call(
        functools.partial(_proj_rnn_kernel, tm=tm, n_tiles=n_tiles),
        grid=(B, n_tiles, 2),
        in_specs=in_specs,
        out_specs=out_specs,
        out_shape=out_shape,
        scratch_shapes=scratch,
        compiler_params=pltpu.CompilerParams(
            dimension_semantics=("arbitrary", "arbitrary", "arbitrary"), vmem_limit_bytes=VMEM_LIMIT),
        name="proj_rnn_prompt",
    )(x, w_in, gpre, cw, cb, wa, wx, ba, bx, lam, wro, qkv_s, *cache_views)


def _proj_rnn_sample_kernel(x_ref, w_ref, gpre_ref, cw_ref, cb_ref, wa_ref, wx_ref, ba_ref, bx_ref, lam_ref, wro_ref,
                            sc_ref, h0_ref,
                            att_ref, mixr_ref, sga_ref, conv_ref, h_ref, wb_ref, wrob_ref,
                            xn_s, h_s, yr_s):
    j = pl.program_id(0)

    def project():
        wb_ref[...] = w_ref[...].astype(BF16)
        return jnp.dot(xn_s[...], wb_ref[...], preferred_element_type=F32)

    @pl.when(j == 0)
    def _u_block():
        xv = x_ref[...]
        ms = jnp.mean(xv * xv, axis=-1, keepdims=True)
        xn_s[...] = ((xv * lax.rsqrt(ms + EPS)) * gpre_ref[...]).astype(BF16)
        u = project()
        taps = [sc_ref[:, k * D_RNN:(k + 1) * D_RNN] for k in range(CONV_WIDTH - 1)] + [u]
        for c in range(N_GATE_BLOCKS):
            c0 = c * GATE_BLOCK
            xc = cb_ref[:, c0:c0 + GATE_BLOCK]
            for tap in range(CONV_WIDTH):
                xc = xc + taps[tap][:, c0:c0 + GATE_BLOCK] * cw_ref[tap:tap + 1, c0:c0 + GATE_BLOCK]
            a, b = _lru_gates(xc, c, wa_ref, wx_ref, ba_ref, bx_ref, lam_ref)
            h = a * h0_ref[:, c0:c0 + GATE_BLOCK] + b
            h_s[:, c0:c0 + GATE_BLOCK] = h
            h_ref[:, c0:c0 + GATE_BLOCK] = h
        for k in range(1, CONV_WIDTH):
            conv_ref[:, (k - 1) * D_RNN:k * D_RNN] = taps[k]

    @pl.when(j == 1)
    def _z_rnn_block():
        z = project()
        hz = (h_s[...] * (z * _sigmoid(z))).astype(BF16)
        wrob_ref[...] = wro_ref[...].astype(BF16)
        yr_s[...] = jnp.dot(hz, wrob_ref[...], preferred_element_type=F32)

    @pl.when(j == 2)
    def _g_rnn_block():
        mixr_ref[...] = _sigmoid(project()) * yr_s[...]

    @pl.when(j == 3)
    def _g_attn_block():
        sga_ref[...] = _sigmoid(project())

    @pl.when(j >= 4)
    def _attn_blocks():
        att_ref[...] = project()


def _proj_rnn_sample(x, w_in, gpre, cw, cb, wa, wx, ba, bx, lam, wro, sconv, h0):
    N, D = x.shape
    const2 = lambda j: (0, 0)
    const3 = lambda j: (0, 0, 0)
    n_att = N_COL_BLOCKS - 4
    first_att, g_rnn_block = COL_Q // TN, COL_G_RNN // TN
    assert COL_Q % TN == 0 and COL_G_RNN % TN == 0 and COL_G_RNN - COL_Q == n_att * TN
    w_block = lambda j: jnp.where(j < 2, j, jnp.where(j < 4, j + g_rnn_block - 2, j - 4 + first_att))
    out_shape = (
        jax.ShapeDtypeStruct((N, n_att * TN), F32),
        jax.ShapeDtypeStruct((N, D), F32),
        jax.ShapeDtypeStruct((N, D), F32),
        jax.ShapeDtypeStruct((N, (CONV_WIDTH - 1) * D_RNN), F32),
        jax.ShapeDtypeStruct((N, D_RNN), F32),
        jax.ShapeDtypeStruct((D, D_IN), BF16),
        jax.ShapeDtypeStruct((D_RNN, D), BF16),
    )
    out_specs = (
        pl.BlockSpec((N, TN), lambda j: (0, jnp.maximum(j - 4, 0))),
        pl.BlockSpec((N, D), const2),
        pl.BlockSpec((N, D), const2),
        pl.BlockSpec((N, (CONV_WIDTH - 1) * D_RNN), const2),
        pl.BlockSpec((N, D_RNN), const2),
        pl.BlockSpec((D, TN), lambda j: (0, w_block(j))),
        pl.BlockSpec((D_RNN, D), const2),
    )
    in_specs = [
        pl.BlockSpec((N, D), const2),
        pl.BlockSpec((D, TN), lambda j: (0, w_block(j))),
        pl.BlockSpec((1, D), const2),
        pl.BlockSpec((CONV_WIDTH, D_RNN), const2),
        pl.BlockSpec((1, D_RNN), const2),
        pl.BlockSpec((N_GATE_BLOCKS, GATE_BLOCK, GATE_BLOCK), const3),
        pl.BlockSpec((N_GATE_BLOCKS, GATE_BLOCK, GATE_BLOCK), const3),
        pl.BlockSpec((1, D_RNN), const2),
        pl.BlockSpec((1, D_RNN), const2),
        pl.BlockSpec((1, D_RNN), const2),
        pl.BlockSpec((D_RNN, D), const2),
        pl.BlockSpec((N, (CONV_WIDTH - 1) * D_RNN), const2),
        pl.BlockSpec((N, D_RNN), const2),
    ]
    scratch = [pltpu.VMEM((N, D), BF16), pltpu.VMEM((N, D_RNN), F32), pltpu.VMEM((N, D), F32)]
    return pl.pallas_call(
        _proj_rnn_sample_kernel,
        grid=(N_COL_BLOCKS,),
        in_specs=in_specs,
        out_specs=out_specs,
        out_shape=out_shape,
        scratch_shapes=scratch,
        compiler_params=pltpu.CompilerParams(dimension_semantics=("arbitrary",), vmem_limit_bytes=VMEM_LIMIT),
        name="proj_rnn_sample",
    )(x, w_in, gpre, cw, cb, wa, wx, ba, bx, lam, wro, sconv, h0)


def _attn_kernel(q_ref, kc_ref, vc_ref, kp_ref, vp_ref, o_ref, st_ref, bias_s, *, group, chunk, n_res):
    dilation = ATTN_GROUPS[group][1]
    c = pl.program_id(2)
    nt = (((1,), (1,)), ((), ()))

    @pl.when((pl.program_id(0) == 0) & (pl.program_id(1) == 0) & (c == 0))
    def _init_bias():
        qi = lax.broadcasted_iota(jnp.int32, (Q_BLOCK, Q_BLOCK), 0)
        kj = lax.broadcasted_iota(jnp.int32, (Q_BLOCK, Q_BLOCK), 1)
        steps_prev = Q_BLOCK + qi - kj
        steps_cur = qi - kj
        dist_prev = (steps_prev * dilation).astype(F32)
        dist_cur = (steps_cur * dilation).astype(F32)
        for h in range(H_G):
            slope = float(_SLOPES[group, h])
            bias_s[h] = jnp.where(steps_cur >= 0, -slope * dist_cur, NEG_INF)
            bias_s[H_G + h] = jnp.where(steps_prev <= Q_BLOCK, -slope * dist_prev, NEG_INF)
            bias_s[2 * H_G + h] = jnp.full((Q_BLOCK, Q_BLOCK), NEG_INF, F32)

    lane = lax.broadcasted_iota(jnp.int32, (Q_BLOCK, LANES), 1)
    low = lane < HEAD_DIM
    lane2 = lax.broadcasted_iota(jnp.int32, (2 * Q_BLOCK, LANES), 1)
    low2 = lane2 < HEAD_DIM
    ones_lo = jnp.where(low2, 1.0, 0.0).astype(BF16)
    ones_hi = jnp.where(low2, 0.0, 1.0).astype(BF16)
    stat_lane = lax.broadcasted_iota(jnp.int32, (Q_BLOCK, STAT_LANES), 1)

    def keys_of(res, block, cur_ref, prev_ref, sl):
        if block == 0:
            return jnp.concatenate([prev_ref[res, :, sl], cur_ref[res, 0:Q_BLOCK, sl]], axis=0)
        return cur_ref[res, (block - 1) * Q_BLOCK:(block + 1) * Q_BLOCK, sl]

    for res, i in [(res, i) for res in range(n_res) for i in range(chunk // Q_BLOCK)]:
        r0 = i * Q_BLOCK
        first = jnp.where(c == 0, H_G, 0) if i == 0 else 0
        stats = jnp.zeros((Q_BLOCK, STAT_LANES), F32)
        for p in range(H_G // 2):
            sl = slice(LANES * p, LANES * (p + 1))
            qp = q_ref[res, r0:r0 + Q_BLOCK, sl] * (HEAD_DIM ** -0.5)
            kp = keys_of(res, i, kc_ref, kp_ref, sl)
            vp = keys_of(res, i, vc_ref, vp_ref, sl)
            es, ms = [], []
            for hh in range(2):
                h = 2 * p + hh
                msk = low if hh == 0 else jnp.logical_not(low)
                qm = jnp.where(msk, qp, jnp.zeros_like(qp))
                s = lax.dot_general(qm, kp, nt, preferred_element_type=F32)
                s_p = s[:, 0:Q_BLOCK] + bias_s[H_G + h + first]
                s_c = s[:, Q_BLOCK:2 * Q_BLOCK] + bias_s[h]
                mx = jnp.maximum(jnp.max(s_p, axis=-1, keepdims=True), jnp.max(s_c, axis=-1, keepdims=True))
                es.append(jnp.exp(s_p - mx).astype(BF16))
                es.append(jnp.exp(s_c - mx).astype(BF16))
                ms.append(mx)
            vm0 = jnp.where(low2, vp, jnp.zeros_like(vp))
            vm1 = jnp.where(low2, jnp.zeros_like(vp), vp)
            w = jnp.concatenate([jnp.concatenate([vm0, ones_lo], axis=1),
                                 jnp.concatenate([vm1, ones_hi], axis=1)], axis=0)
            acc = jnp.dot(jnp.concatenate(es, axis=1), w, preferred_element_type=F32)
            l_pair = acc[:, LANES:2 * LANES]
            o_ref[res, r0:r0 + Q_BLOCK, sl] = (acc[:, 0:LANES] / l_pair).astype(o_ref.dtype)
            lse_pair = jnp.where(low, ms[0], ms[1]) + jnp.log(l_pair)
            keep = ((stat_lane % (STAT_LANES // 2)) // STAT_LANES_PER_HEAD) == p
            stats = jnp.where(keep, lse_pair, stats)
        st_ref[res, r0:r0 + Q_BLOCK, :] = stats


def _attn_group(q, k, v, group):
    B, dil, L, _ = q.shape
    chunk = min(ATTN_CHUNK, L)
    n_res = min(ATTN_CHUNK // chunk, dil)
    assert L % chunk == 0 and chunk % Q_BLOCK == 0 and dil % n_res == 0
    cur = lambda b, r, c: (b, r, c, 0)
    prev = lambda b, r, c: (b, r, jnp.maximum(c * (chunk // Q_BLOCK) - 1, 0), 0)
    blk = (None, n_res, chunk, D_ATTN)
    pblk = (None, n_res, Q_BLOCK, D_ATTN)
    return pl.pallas_call(
        functools.partial(_attn_kernel, group=group, chunk=chunk, n_res=n_res),
        grid=(B, dil // n_res, L // chunk),
        in_specs=[pl.BlockSpec(blk, cur), pl.BlockSpec(blk, cur), pl.BlockSpec(blk, cur),
                  pl.BlockSpec(pblk, prev), pl.BlockSpec(pblk, prev)],
        out_specs=[pl.BlockSpec(blk, cur), pl.BlockSpec((None, n_res, chunk, STAT_LANES), cur)],
        out_shape=[jax.ShapeDtypeStruct((B, dil, L, D_ATTN), BF16),
                   jax.ShapeDtypeStruct((B, dil, L, STAT_LANES), F32)],
        scratch_shapes=[pltpu.VMEM((3 * H_G, Q_BLOCK, Q_BLOCK), F32)],
        compiler_params=pltpu.CompilerParams(
            dimension_semantics=("arbitrary", "arbitrary", "arbitrary"), vmem_limit_bytes=VMEM_LIMIT),
        name=f"attn_group{group}",
    )(q, k, v, k, v)


def _finish(att, z_ref, sga_ref, mixr_ref, x_ref, wao_ref, wo_ref, gpost_ref, y_ref):
    z = z_ref[...].astype(F32)
    ya_in = (att * (z * _sigmoid(z))).astype(BF16)
    ya = jnp.dot(ya_in, wao_ref[...], preferred_element_type=F32)
    mixed = mixr_ref[...].astype(F32) + sga_ref[...].astype(F32) * ya
    out = jnp.dot(mixed.astype(BF16), wo_ref[...], preferred_element_type=F32)
    ms = jnp.mean(out * out, axis=-1, keepdims=True)
    y_ref[...] = x_ref[...] + (out * lax.rsqrt(ms + EPS)) * gpost_ref[...]


def _out_sample_kernel(att_ref, z_ref, sga_ref, mixr_ref, x_ref, wao_ref, wo_ref, gpost_ref,
                       y_ref, waob_ref, wob_ref):
    waob_ref[...] = wao_ref[...].astype(BF16)
    wob_ref[...] = wo_ref[...].astype(BF16)
    _finish(att_ref[...], z_ref, sga_ref, mixr_ref, x_ref, waob_ref, wob_ref, gpost_ref, y_ref)


def _merge_out_kernel(o0_ref, o1_ref, o2_ref, s0_ref, s1_ref, s2_ref, z_ref, sga_ref, mixr_ref, x_ref,
                      wao_ref, wo_ref, gpost_ref, y_ref, nat1_s, nat2_s, st1_s, st2_s, *, tm):
    for o_ref, s_ref, nat_s, stn_s, dil in ((o1_ref, s1_ref, nat1_s, st1_s, ATTN_GROUPS[1][1]),
                                            (o2_ref, s2_ref, nat2_s, st2_s, ATTN_GROUPS[2][1])):
        p = tm // dil
        for r in range(dil):
            stn_s[pl.ds(r, p, stride=dil), :] = s_ref[r]
            for c in range(D_ATTN // LANES):
                nat_s[c, pl.ds(r, p, stride=dil), :] = o_ref[r, :, c * LANES:(c + 1) * LANES].astype(F32)

    lse = [s0_ref[...], st1_s[...], st2_s[...]]
    mm = jnp.maximum(jnp.maximum(lse[0], lse[1]), lse[2])
    ws = [jnp.exp(x - mm) for x in lse]
    den = ws[0] + ws[1] + ws[2]
    src = lax.broadcasted_iota(jnp.int32, (STAT_LANES, D_ATTN), 0)
    dst_head = lax.broadcasted_iota(jnp.int32, (STAT_LANES, D_ATTN), 1) // HEAD_DIM
    expand = jnp.where(src == (dst_head % 2) * (STAT_LANES // 2) + (dst_head // 2) * STAT_LANES_PER_HEAD,
                       1.0, 0.0).astype(BF16)
    outs = [o0_ref[...].astype(F32),
            jnp.concatenate([nat1_s[c] for c in range(D_ATTN // LANES)], axis=1),
            jnp.concatenate([nat2_s[c] for c in range(D_ATTN // LANES)], axis=1)]
    att = jnp.zeros((tm, D_ATTN), F32)
    for g in range(N_GROUPS):
        wexp = jnp.dot((ws[g] / den).astype(BF16), expand, preferred_element_type=F32)
        att = att + wexp * outs[g]
    _finish(att, z_ref, sga_ref, mixr_ref, x_ref, wao_ref, wo_ref, gpost_ref, y_ref)


def _merge_out_prompt(os, sts, zg, sga, mixr, x, wao, wo, gpost):
    B, S, D = x.shape
    tm = TM_OUT
    assert S % tm == 0
    tile3 = lambda b, m: (b, m, 0)
    tile4 = lambda b, m: (b, 0, m, 0)
    const = lambda b, m: (0, 0)
    in_specs = []
    for (_, dil), width in [(g, D_ATTN) for g in ATTN_GROUPS] + [(g, STAT_LANES) for g in ATTN_GROUPS]:
        if dil == 1:
            in_specs.append(pl.BlockSpec((None, None, tm, width), tile4))
        else:
            in_specs.append(pl.BlockSpec((None, dil, tm // dil, width), tile4))
    in_specs += [
        pl.BlockSpec((None, tm, D_ATTN), tile3),
        pl.BlockSpec((None, tm, D), tile3),
        pl.BlockSpec((None, tm, D), tile3),
        pl.BlockSpec((None, tm, D), tile3),
        pl.BlockSpec((D_ATTN, D), const),
        pl.BlockSpec((D, D), const),
        pl.BlockSpec((1, D), const),
    ]
    scratch = [
        pltpu.VMEM((D_ATTN // LANES, tm, LANES), F32),
        pltpu.VMEM((D_ATTN // LANES, tm, LANES), F32),
        pltpu.VMEM((tm, STAT_LANES), F32),
        pltpu.VMEM((tm, STAT_LANES), F32),
    ]
    return pl.pallas_call(
        functools.partial(_merge_out_kernel, tm=tm),
        grid=(B, S // tm),
        in_specs=in_specs,
        out_specs=pl.BlockSpec((None, tm, D), tile3),
        out_shape=jax.ShapeDtypeStruct((B, S, D), F32),
        scratch_shapes=scratch,
        compiler_params=pltpu.CompilerParams(
            dimension_semantics=("arbitrary", "arbitrary"), vmem_limit_bytes=VMEM_LIMIT),
        name="merge_out_prompt",
    )(*os, *sts, zg, sga, mixr, x, wao, wo, gpost)


def _out_sample(att, z, sga, mixr, x, wao, wo, gpost):
    N, D = x.shape
    full = lambda shape: pl.BlockSpec(shape, lambda i: (0, 0))
    return pl.pallas_call(
        _out_sample_kernel,
        grid=(1,),
        in_specs=[full((N, D_ATTN)), full((N, D_ATTN)), full((N, D)), full((N, D)), full((N, D)),
                  full((D_ATTN, D)), full((D, D)), full((1, D))],
        out_specs=(full((N, D)), full((D_ATTN, D)), full((D, D))),
        out_shape=(jax.ShapeDtypeStruct((N, D), F32), jax.ShapeDtypeStruct((D_ATTN, D), BF16),
                   jax.ShapeDtypeStruct((D, D), BF16)),
        compiler_params=pltpu.CompilerParams(dimension_semantics=("arbitrary",), vmem_limit_bytes=VMEM_LIMIT),
        name="out_sample",
    )(att, z, sga, mixr, x, wao, wo, gpost)


def _block_diag_chunks(w):
    per = GATE_BLOCK // RNN_BLOCK
    w = w.reshape(N_GATE_BLOCKS, per, RNN_BLOCK, RNN_BLOCK)
    eye = jnp.eye(per, dtype=w.dtype)
    dense = w[:, :, :, None, :] * eye[None, :, None, :, None]
    return dense.reshape(N_GATE_BLOCKS, GATE_BLOCK, GATE_BLOCK)


def _layer(layer, yp, ys, sconv, h0, caches, norm_pre, norm_post, w_in, conv_w, conv_b, lru_w_a, lru_b_a, lru_w_x,
           lru_b_x, lru_lambda, w_rnn_out, w_attn_out, w_out):
    B, S, D = yp.shape
    N = ys.shape[0]
    row = lambda v: v.reshape(1, -1)
    wa = _block_diag_chunks(lru_w_a).astype(BF16)
    wx = _block_diag_chunks(lru_w_x).astype(BF16)
    params = (row(norm_pre), conv_w, row(conv_b), wa, wx, row(lru_b_a), row(lru_b_x), row(lru_lambda))

    xs = ys.reshape(N, D)
    att_in, mixr_s, sga_s, conv_s, h_s, w_in_b, wro = _proj_rnn_sample(
        xs, w_in, *params, w_rnn_out, sconv.reshape(N, (CONV_WIDTH - 1) * D_RNN), h0)
    shared = (w_in_b,) + params + (wro,)
    n_qkv = 3 * N_GROUPS
    z_s = att_in[:, n_qkv * D_ATTN:]
    qkv_s = att_in[:, :n_qkv * D_ATTN].reshape(N, n_qkv, H_G, HEAD_DIM)

    outs = _proj_rnn_prompt(yp, *shared, qkv_s, caches, layer)
    qkv, (zg, mixr, sga, kv0, kv1, kv2, conv_p, h_p, att_s) = outs[:9], outs[9:]
    os, sts = [], []
    for g in range(N_GROUPS):
        o, st = _attn_group(qkv[3 * g], qkv[3 * g + 1], qkv[3 * g + 2], g)
        os.append(o)
        sts.append(st)
    y_s, wao, wo = _out_sample(att_s.reshape(N, D_ATTN), z_s, sga_s, mixr_s, xs, w_attn_out, w_out, row(norm_post))
    y_p = _merge_out_prompt(os, sts, zg, sga, mixr, yp, wao, wo, row(norm_post))
    kv_p = [kv.reshape(B, kv.shape[1], 2, H_G, HEAD_DIM) for kv in (kv0, kv1, kv2)]
    kv_s = [jnp.stack([qkv_s[:, N_GROUPS + g], qkv_s[:, 2 * N_GROUPS + g]], axis=1).reshape(
        N, 1, 2, H_G, HEAD_DIM) for g in range(N_GROUPS)]

    return (y_p, y_s.reshape(N, 1, D), conv_p, conv_s.reshape(N, CONV_WIDTH - 1, D_RNN),
            h_p.reshape(B, D_RNN), h_s, kv_p, kv_s)


def kernel(x_prompt, x_sample, state_conv, state_h, cache_kv_w128, cache_kv_w512, cache_kv_w2048, norm_pre, norm_post, w_in, conv_w, conv_b, lru_w_a, lru_b_a, lru_w_x, lru_b_x, lru_lambda, w_rnn_out, w_attn_out, w_out):
    depth = norm_pre.shape[0]
    caches = (cache_kv_w128, cache_kv_w512, cache_kv_w2048)
    yp, ys = x_prompt, x_sample
    conv_p, conv_s, h_p, h_s = [], [], [], []
    kvp = ([], [], [])
    kvs = ([], [], [])
    for l in range(depth):
        yp, ys, cp, cs, hp, hs, kv_p, kv_s = _layer(
            l, yp, ys, state_conv[l], state_h[l], caches,
            norm_pre[l], norm_post[l], w_in[l], conv_w[l], conv_b[l], lru_w_a[l], lru_b_a[l], lru_w_x[l],
            lru_b_x[l], lru_lambda[l], w_rnn_out[l], w_attn_out[l], w_out[l])
        conv_p.append(cp)
        conv_s.append(cs)
        h_p.append(hp)
        h_s.append(hs)
        for g in range(N_GROUPS):
            kvp[g].append(kv_p[g])
            kvs[g].append(kv_s[g])
    return (yp, ys, jnp.stack(conv_p), jnp.stack(conv_s), jnp.stack(h_p), jnp.stack(h_s),
            jnp.stack(kvp[0]), jnp.stack(kvs[0]), jnp.stack(kvp[1]), jnp.stack(kvs[1]),
            jnp.stack(kvp[2]), jnp.stack(kvs[2]))
```

```python
import functools

import numpy as np
import jax
import jax.numpy as jnp
from jax import lax
from jax.experimental import pallas as pl
from jax.experimental.pallas import tpu as pltpu

F32 = jnp.float32
BF16 = jnp.bfloat16

D_MODEL = 1024
D_RNN = 1024
N_RNN_BLOCKS = 16
RNN_BLOCK = D_RNN // N_RNN_BLOCKS
CONV_WIDTH = 4
LRU_C = 8.0
HEAD_DIM = 64
H_G = 8
ATTN_GROUPS = ((128, 1), (512, 4), (2048, 16))
N_GROUPS = 3
D_ATTN = H_G * HEAD_DIM
Q_BLOCK = 128
ALIBI_MAX = 8.0
EPS = 1e-6
NEG_INF = -1e30
D_IN = 2 * D_RNN + 3 * N_GROUPS * D_ATTN + D_ATTN + 2 * D_MODEL

LANES = 128
TN = 1024
N_COL_BLOCKS = D_IN // TN
COL_U = 0
COL_Z_RNN = COL_U + D_RNN
COL_Q = COL_Z_RNN + D_RNN
COL_K = COL_Q + N_GROUPS * D_ATTN
COL_V = COL_K + N_GROUPS * D_ATTN
COL_Z_ATTN = COL_V + N_GROUPS * D_ATTN
COL_G_RNN = COL_Z_ATTN + D_ATTN
COL_G_ATTN = COL_G_RNN + D_MODEL
GATE_BLOCK = 256
N_GATE_BLOCKS = D_RNN // GATE_BLOCK
STAT_LANES = LANES
STAT_LANES_PER_HEAD = STAT_LANES // H_G

TM_PROMPT = 256
TILES_PER_SAMPLE = 2
SEGMENTS = 8
TM_OUT = 1024
ATTN_CHUNK = 2048
GATE_ROWS = 128
VMEM_LIMIT = 56 * 1024 * 1024


def _alibi_slopes():
    n = N_GROUPS * H_G
    s = np.float32(2.0) ** (np.float32(-ALIBI_MAX) * np.arange(1, n + 1, dtype=np.float32) / np.float32(n))
    return s.reshape(N_GROUPS, H_G)


_SLOPES = _alibi_slopes()


def _softplus(y):
    return jnp.maximum(y, 0.0) + jnp.log1p(jnp.exp(-jnp.abs(y)))


def _sigmoid(x):
    return 0.5 * jnp.tanh(0.5 * x) + 0.5


def _lru_gates(xc, c, wa_ref, wx_ref, ba_ref, bx_ref, lam_ref):
    c0 = c * GATE_BLOCK
    xcb = xc.astype(BF16)
    r = _sigmoid(jnp.dot(xcb, wa_ref[c], preferred_element_type=F32) + ba_ref[:, c0:c0 + GATE_BLOCK])
    i = _sigmoid(jnp.dot(xcb, wx_ref[c], preferred_element_type=F32) + bx_ref[:, c0:c0 + GATE_BLOCK])
    log_a = (-LRU_C * r) * _softplus(-lam_ref[:, c0:c0 + GATE_BLOCK])
    a = jnp.exp(log_a)
    v = 1.0 - a * a
    b = jnp.where(v > 0.0, v * lax.rsqrt(v), 0.0) * i * xc
    return a, b


def _sample_heads(local_heads, first_head, slope_of, qs_ref, cache_refs, att_ref):
    scale = HEAD_DIM ** -0.5
    ri = lax.broadcasted_iota(jnp.int32, (HEAD_DIM, HEAD_DIM), 0)
    ci = lax.broadcasted_iota(jnp.int32, (HEAD_DIM, HEAD_DIM), 1)
    eye = ri == ci

    def to_col(row):
        return jnp.sum(jnp.where(eye, jnp.broadcast_to(row, (HEAD_DIM, HEAD_DIM)), 0.0), axis=-1, keepdims=True)

    def to_row(col):
        return jnp.sum(jnp.where(eye, jnp.broadcast_to(col, (HEAD_DIM, HEAD_DIM)), 0.0), axis=0, keepdims=True)

    for i in local_heads:
        h = first_head + i
        outs, lses = [], []
        for g, (window, dilation) in enumerate(ATTN_GROUPS):
            c_ref = cache_refs[g]
            pos = lax.broadcasted_iota(jnp.int32, (1, window), 1)
            dist = (window - pos).astype(F32)
            bias = jnp.where((pos % dilation) == 0, -slope_of(g, i) * dist, NEG_INF)
            q = qs_ref[g, pl.ds(h, 1), :] * scale
            s_new = jnp.sum(qs_ref[N_GROUPS + g, pl.ds(h, 1), :] * q, axis=-1, keepdims=True)
            v_col = to_col(qs_ref[2 * N_GROUPS + g, pl.ds(h, 1), :])
            s = jnp.sum(c_ref[0, i] * to_col(q), axis=0, keepdims=True) + bias
            mx = jnp.maximum(jnp.max(s, axis=-1, keepdims=True), s_new)
            e = jnp.exp(s - mx)
            e_new = jnp.exp(s_new - mx)
            l = jnp.sum(e, axis=-1, keepdims=True) + e_new
            o = jnp.sum(c_ref[1, i] * e, axis=-1, keepdims=True) + e_new * v_col
            outs.append(o / l)
            lses.append(mx + jnp.log(l))
        mm = jnp.maximum(jnp.maximum(lses[0], lses[1]), lses[2])
        ws = [jnp.exp(x - mm) for x in lses]
        col = (outs[0] * ws[0] + outs[1] * ws[1] + outs[2] * ws[2]) / (ws[0] + ws[1] + ws[2])
        att_ref[pl.ds(h, 1), :] = to_row(col)


def _proj_rnn_kernel(x_ref, w_ref, gpre_ref, cw_ref, cb_ref, wa_ref, wx_ref, ba_ref, bx_ref, lam_ref, wro_ref,
                     qs_ref, c0_ref, c1_ref, c2_ref,
                     q0_ref, k0_ref, v0_ref, q1_ref, k1_ref, v1_ref, q2_ref, k2_ref, v2_ref,
                     zg_ref, mixr_ref, sga_ref, kv0_ref, kv1_ref, kv2_ref, conv_ref, h_ref, att_ref,
                     xn_s, xnf_s, xseg_s, xnseg_s, xn4_s, xn16_s, u_s, a_s, b_s, cprev_s, carry_s,
                     *, tm, n_tiles):
    m = pl.program_id(1)
    j = pl.program_id(2)
    tile = pl.program_id(0) * n_tiles + m
    half = D_ATTN
    d1, d2 = ATTN_GROUPS[1][1], ATTN_GROUPS[2][1]
    seg = tm // SEGMENTS
    n_slab = D_MODEL // LANES
    heads_per_step = H_G // TILES_PER_SAMPLE

    def project(lhs_s, col, width=half):
        return jnp.dot(lhs_s[...], w_ref[:, col:col + width], preferred_element_type=F32)

    def emit(out_ref, val, dilation):
        p = tm // dilation
        for r in range(dilation):
            out_ref[r] = val[r * p:(r + 1) * p, :].astype(BF16)

    def natural_tail(out_ref, out_c0, val, dilation):
        p = tm // dilation
        for c in range(half // LANES):
            for r in range(dilation):
                xnf_s[c, pl.ds(r, p, stride=dilation), :] = val[r * p:(r + 1) * p, c * LANES:(c + 1) * LANES]
            out_ref[:, out_c0 + c * LANES:out_c0 + (c + 1) * LANES] = xnf_s[c]

    def sample_heads(step):
        part = tile % TILES_PER_SAMPLE
        first = part * heads_per_step + step * heads_per_step

        def slope_of(g, i):
            table = [float(_SLOPES[g, t * heads_per_step + step * heads_per_step + i])
                     for t in range(TILES_PER_SAMPLE)]
            out = jnp.float32(table[0])
            for t in range(1, TILES_PER_SAMPLE):
                out = jnp.where(part == t, jnp.float32(table[t]), out)
            return out

        _sample_heads(range(heads_per_step), first, slope_of, qs_ref, (c0_ref, c1_ref, c2_ref), att_ref)

    def _recurrent_blocks():
        @pl.when(m == 0)
        def _reset():
            cprev_s[...] = jnp.zeros((8, D_RNN), F32)
            carry_s[...] = jnp.zeros((1, D_RNN), F32)

        for s in range(SEGMENTS):
            r0 = s * seg
            xv = x_ref[r0:r0 + seg, :]
            ms = jnp.mean(xv * xv, axis=-1, keepdims=True)
            xn = (xv * lax.rsqrt(ms + EPS)) * gpre_ref[...]
            xn_s[r0:r0 + seg, :] = xn.astype(BF16)
            for c in range(n_slab):
                piece = xn[:, c * LANES:(c + 1) * LANES]
                xnf_s[c, r0:r0 + seg, :] = piece
                xseg_s[c, pl.ds(s, seg, stride=SEGMENTS), :] = piece

        for c in range(n_slab):
            xnseg_s[:, c * LANES:(c + 1) * LANES] = xseg_s[c].astype(BF16)

        u_s[...] = project(xnseg_s, COL_U, D_RNN)

        sub = lax.broadcasted_iota(jnp.int32, (SEGMENTS, GATE_BLOCK), 0)
        gate_rows = min(GATE_ROWS, tm)
        for c in range(N_GATE_BLOCKS):
            cols = slice(c * GATE_BLOCK, (c + 1) * GATE_BLOCK)

            def wrapped(t):
                v = pltpu.roll(u_s[SEGMENTS * (seg + t):SEGMENTS * (seg + t + 1), cols], 1, axis=0)
                return jnp.where(sub == 0, cprev_s[8 + t:9 + t, cols], v)

            wrap = {t: wrapped(t) for t in range(1 - CONV_WIDTH, 0)}
            for rc in range(tm // gate_rows):
                r0 = rc * gate_rows
                xc = cb_ref[:, cols] + u_s[r0:r0 + gate_rows, cols] * cw_ref[CONV_WIDTH - 1:CONV_WIDTH, cols]
                for sh in range(1, CONV_WIDTH):
                    if r0 == 0:
                        head = [wrap[t - sh] for t in range(sh)]
                        ush = jnp.concatenate(head + [u_s[0:gate_rows - SEGMENTS * sh, cols]], axis=0)
                    else:
                        ush = u_s[r0 - SEGMENTS * sh:r0 - SEGMENTS * sh + gate_rows, cols]
                    xc = xc + ush * cw_ref[CONV_WIDTH - 1 - sh:CONV_WIDTH - sh, cols]
                a, b = _lru_gates(xc, c, wa_ref, wx_ref, ba_ref, bx_ref, lam_ref)
                a_s[r0:r0 + gate_rows, cols] = a
                b_s[r0:r0 + gate_rows, cols] = b

        hloc = jnp.zeros((SEGMENTS, D_RNN), F32)
        cum = jnp.ones((SEGMENTS, D_RNN), F32)
        for t in range(seg):
            rows = slice(t * SEGMENTS, (t + 1) * SEGMENTS)
            av = a_s[rows, :]
            hloc = av * hloc + b_s[rows, :]
            cum = av * cum
            b_s[rows, :] = hloc
            a_s[rows, :] = cum
        state = carry_s[...]
        enter = []
        for s in range(SEGMENTS):
            enter.append(state)
            state = hloc[s:s + 1, :] + cum[s:s + 1, :] * state
        enter = jnp.concatenate(enter, axis=0)
        carry_s[...] = state
        h_ref[...] = state
        for k in range(1, CONV_WIDTH):
            last = u_s[tm - SEGMENTS * (k - 1) - 1:tm - SEGMENTS * (k - 1), :]
            conv_ref[CONV_WIDTH - 1 - k:CONV_WIDTH - k, :] = last
            cprev_s[8 - k:9 - k, :] = last

        z = project(xnseg_s, COL_Z_RNN, D_RNN)
        split = (seg, SEGMENTS, D_RNN)
        h = b_s[...].reshape(split) + a_s[...].reshape(split) * enter[None]
        hz = (h.reshape(tm, D_RNN) * (z * _sigmoid(z))).astype(BF16)
        yr = jnp.dot(hz, wro_ref[...], preferred_element_type=F32)

        mix = _sigmoid(project(xnseg_s, COL_G_RNN, D_MODEL)) * yr
        for c in range(n_slab):
            xseg_s[c] = mix[:, c * LANES:(c + 1) * LANES]
        for s in range(SEGMENTS):
            for c in range(n_slab):
                mixr_ref[s * seg:(s + 1) * seg, c * LANES:(c + 1) * LANES] = \
                    xseg_s[c, pl.ds(s, seg, stride=SEGMENTS), :].astype(BF16)

    def _attention_blocks():
        for dil, dst in ((d1, xn4_s), (d2, xn16_s)):
            p = tm // dil
            for r in range(dil):
                for c in range(n_slab):
                    dst[r * p:(r + 1) * p, c * LANES:(c + 1) * LANES] = \
                        xnf_s[c, pl.ds(r, p, stride=dil), :].astype(BF16)

        sga_ref[...] = _sigmoid(project(xn_s, COL_G_ATTN, D_MODEL)).astype(BF16)
        zg_ref[...] = project(xn_s, COL_Z_ATTN).astype(BF16)

        lhs = (xn_s, xn4_s, xn16_s)
        qkv_refs = ((q0_ref, k0_ref, v0_ref), (q1_ref, k1_ref, v1_ref), (q2_ref, k2_ref, v2_ref))
        for g, (_, dil) in enumerate(ATTN_GROUPS):
            q_ref, k_ref, v_ref = qkv_refs[g]
            emit(q_ref, project(lhs[g], COL_Q + g * D_ATTN), dil)
            for col, out_ref, c0 in ((COL_K, k_ref, 0), (COL_V, v_ref, half)):
                r = project(lhs[g], col + g * D_ATTN)
                emit(out_ref, r, dil)
                if g == 0:
                    kv0_ref[:, c0:c0 + half] = r[tm - ATTN_GROUPS[0][0]:tm, :]
                else:
                    natural_tail(kv1_ref if g == 1 else kv2_ref, c0, r, dil)
        sample_heads(0)

    _recurrent_blocks()
    _attention_blocks()


def _proj_rnn_prompt(x, w_in, gpre, cw, cb, wa, wx, ba, bx, lam, wro, qkv_s, caches, layer):
    B, S, D = x.shape
    tm = TM_PROMPT
    n_tiles = S // tm
    N = qkv_s.shape[0]
    heads_per_step = H_G // TILES_PER_SAMPLE
    assert S % tm == 0 and tm % (SEGMENTS * 16) == 0 and tm >= ATTN_GROUPS[0][0]
    assert all(w % tm == 0 and S >= w for w, _ in ATTN_GROUPS[1:])
    assert B * n_tiles == N * TILES_PER_SAMPLE and n_tiles % TILES_PER_SAMPLE == 0
    cache_views = [jnp.transpose(c, (0, 1, 3, 4, 5, 2)) for c in caches]

    const2 = lambda b, m, j: (0, 0)
    const3 = lambda b, m, j: (0, 0, 0)
    tile3 = lambda b, m, j: (b, m, 0)
    tile4 = lambda b, m, j: (b, 0, m, 0)
    per_seq = lambda b, m, j: (b, 0, 0)
    sample_of = lambda b, m: (b * n_tiles + m) // TILES_PER_SAMPLE
    head_block = lambda b, m, j: (b * n_tiles + m) % TILES_PER_SAMPLE

    def last_rows(window):
        first_tile = n_tiles - window // tm
        return lambda b, m, j: (b, jnp.maximum(m - first_tile, 0), 0)

    resident = dict(pipeline_mode=pl.Buffered(1))
    qkv_shapes, qkv_specs = [], []
    for _, dil in ATTN_GROUPS:
        for _ in range(3):
            qkv_shapes.append(jax.ShapeDtypeStruct((B, dil, S // dil, D_ATTN), BF16))
            qkv_specs.append(pl.BlockSpec((None, dil, tm // dil, D_ATTN), tile4))
    out_shape = tuple(qkv_shapes) + (
        jax.ShapeDtypeStruct((B, S, D_ATTN), BF16),
        jax.ShapeDtypeStruct((B, S, D), BF16),
        jax.ShapeDtypeStruct((B, S, D), BF16),
        jax.ShapeDtypeStruct((B, ATTN_GROUPS[0][0], 2 * D_ATTN), F32),
        jax.ShapeDtypeStruct((B, ATTN_GROUPS[1][0], 2 * D_ATTN), F32),
        jax.ShapeDtypeStruct((B, ATTN_GROUPS[2][0], 2 * D_ATTN), F32),
        jax.ShapeDtypeStruct((B, CONV_WIDTH - 1, D_RNN), F32),
        jax.ShapeDtypeStruct((B, 1, D_RNN), F32),
        jax.ShapeDtypeStruct((N, H_G, HEAD_DIM), F32),
    )
    out_specs = tuple(qkv_specs) + (
        pl.BlockSpec((None, tm, D_ATTN), tile3),
        pl.BlockSpec((None, tm, D), tile3),
        pl.BlockSpec((None, tm, D), tile3),
        pl.BlockSpec((None, ATTN_GROUPS[0][0], 2 * D_ATTN), per_seq),
        pl.BlockSpec((None, tm, 2 * D_ATTN), last_rows(ATTN_GROUPS[1][0])),
        pl.BlockSpec((None, tm, 2 * D_ATTN), last_rows(ATTN_GROUPS[2][0])),
        pl.BlockSpec((None, CONV_WIDTH - 1, D_RNN), per_seq),
        pl.BlockSpec((None, 1, D_RNN), per_seq),
        pl.BlockSpec((None, H_G, HEAD_DIM), lambda b, m, j: (sample_of(b, m), 0, 0)),
    )
    in_specs = [
        pl.BlockSpec((None, tm, D), tile3),
        pl.BlockSpec((D, D_IN), const2, **resident),
        pl.BlockSpec((1, D), const2),
        pl.BlockSpec((CONV_WIDTH, D_RNN), const2),
        pl.BlockSpec((1, D_RNN), const2),
        pl.BlockSpec((N_GATE_BLOCKS, GATE_BLOCK, GATE_BLOCK), const3),
        pl.BlockSpec((N_GATE_BLOCKS, GATE_BLOCK, GATE_BLOCK), const3),
        pl.BlockSpec((1, D_RNN), const2),
        pl.BlockSpec((1, D_RNN), const2),
        pl.BlockSpec((1, D_RNN), const2),
        pl.BlockSpec((D_RNN, D), const2, **resident),
        pl.BlockSpec((None, 3 * N_GROUPS, H_G, HEAD_DIM), lambda b, m, j: (sample_of(b, m), 0, 0, 0)),
    ]
    in_specs += [pl.BlockSpec((None, None, 2, heads_per_step, HEAD_DIM, window),
                              lambda b, m, j: (layer, sample_of(b, m), 0, head_block(b, m, j), 0, 0))
                 for window, _ in ATTN_GROUPS]
    scratch = [
        pltpu.VMEM((tm, D), BF16),
        pltpu.VMEM((D // LANES, tm, LANES), F32),
        pltpu.VMEM((D // LANES, tm, LANES), F32),
        pltpu.VMEM((tm, D), BF16),
        pltpu.VMEM((tm, D), BF16),
        pltpu.VMEM((tm, D), BF16),
        pltpu.VMEM((tm, D_RNN), F32),
        pltpu.VMEM((tm, D_RNN), F32),
        pltpu.VMEM((tm, D_RNN), F32),
        pltpu.VMEM((8, D_RNN), F32),
        pltpu.VMEM((1, D_RNN), F32),
    ]
    return pl.pallas_call(
        functools.partial(_proj_rnn_kernel, tm=tm, n_tiles=n_tiles),
        grid=(B, n_tiles, 1),
        in_specs=in_specs,
        out_specs=out_specs,
        out_shape=out_shape,
        scratch_shapes=scratch,
        compiler_params=pltpu.CompilerParams(
            dimension_semantics=("arbitrary", "arbitrary", "arbitrary"), vmem_limit_bytes=VMEM_LIMIT),
        name="proj_rnn_prompt",
    )(x, w_in, gpre, cw, cb, wa, wx, ba, bx, lam, wro, qkv_s, *cache_views)


def _proj_rnn_sample_kernel(x_ref, w_ref, gpre_ref, cw_ref, cb_ref, wa_ref, wx_ref, ba_ref, bx_ref, lam_ref, wro_ref,
                            sc_ref, h0_ref,
                            att_ref, mixr_ref, sga_ref, conv_ref, h_ref, wb_ref, wrob_ref,
                            xn_s, h_s, yr_s):
    j = pl.program_id(0)

    def project():
        wb_ref[...] = w_ref[...].astype(BF16)
        return jnp.dot(xn_s[...], wb_ref[...], preferred_element_type=F32)

    @pl.when(j == 0)
    def _u_block():
        xv = x_ref[...]
        ms = jnp.mean(xv * xv, axis=-1, keepdims=True)
        xn_s[...] = ((xv * lax.rsqrt(ms + EPS)) * gpre_ref[...]).astype(BF16)
        u = project()
        taps = [sc_ref[:, k * D_RNN:(k + 1) * D_RNN] for k in range(CONV_WIDTH - 1)] + [u]
        for c in range(N_GATE_BLOCKS):
            c0 = c * GATE_BLOCK
            xc = cb_ref[:, c0:c0 + GATE_BLOCK]
            for tap in range(CONV_WIDTH):
                xc = xc + taps[tap][:, c0:c0 + GATE_BLOCK] * cw_ref[tap:tap + 1, c0:c0 + GATE_BLOCK]
            a, b = _lru_gates(xc, c, wa_ref, wx_ref, ba_ref, bx_ref, lam_ref)
            h = a * h0_ref[:, c0:c0 + GATE_BLOCK] + b
            h_s[:, c0:c0 + GATE_BLOCK] = h
            h_ref[:, c0:c0 + GATE_BLOCK] = h
        for k in range(1, CONV_WIDTH):
            conv_ref[:, (k - 1) * D_RNN:k * D_RNN] = taps[k]

    @pl.when(j == 1)
    def _z_rnn_block():
        z = project()
        hz = (h_s[...] * (z * _sigmoid(z))).astype(BF16)
        wrob_ref[...] = wro_ref[...].astype(BF16)
        yr_s[...] = jnp.dot(hz, wrob_ref[...], preferred_element_type=F32)

    @pl.when(j == 2)
    def _g_rnn_block():
        mixr_ref[...] = _sigmoid(project()) * yr_s[...]

    @pl.when(j == 3)
    def _g_attn_block():
        sga_ref[...] = _sigmoid(project())

    @pl.when(j >= 4)
    def _attn_blocks():
        att_ref[...] = project()


def _proj_rnn_sample(x, w_in, gpre, cw, cb, wa, wx, ba, bx, lam, wro, sconv, h0):
    N, D = x.shape
    const2 = lambda j: (0, 0)
    const3 = lambda j: (0, 0, 0)
    n_att = N_COL_BLOCKS - 4
    first_att, g_rnn_block = COL_Q // TN, COL_G_RNN // TN
    assert COL_Q % TN == 0 and COL_G_RNN % TN == 0 and COL_G_RNN - COL_Q == n_att * TN
    w_block = lambda j: jnp.where(j < 2, j, jnp.where(j < 4, j + g_rnn_block - 2, j - 4 + first_att))
    out_shape = (
        jax.ShapeDtypeStruct((N, n_att * TN), F32),
        jax.ShapeDtypeStruct((N, D), F32),
        jax.ShapeDtypeStruct((N, D), F32),
        jax.ShapeDtypeStruct((N, (CONV_WIDTH - 1) * D_RNN), F32),
        jax.ShapeDtypeStruct((N, D_RNN), F32),
        jax.ShapeDtypeStruct((D, D_IN), BF16),
        jax.ShapeDtypeStruct((D_RNN, D), BF16),
    )
    out_specs = (
        pl.BlockSpec((N, TN), lambda j: (0, jnp.maximum(j - 4, 0))),
        pl.BlockSpec((N, D), const2),
        pl.BlockSpec((N, D), const2),
        pl.BlockSpec((N, (CONV_WIDTH - 1) * D_RNN), const2),
        pl.BlockSpec((N, D_RNN), const2),
        pl.BlockSpec((D, TN), lambda j: (0, w_block(j))),
        pl.BlockSpec((D_RNN, D), const2),
    )
    in_specs = [
        pl.BlockSpec((N, D), const2),
        pl.BlockSpec((D, TN), lambda j: (0, w_block(j))),
        pl.BlockSpec((1, D), const2),
        pl.BlockSpec((CONV_WIDTH, D_RNN), const2),
        pl.BlockSpec((1, D_RNN), const2),
        pl.BlockSpec((N_GATE_BLOCKS, GATE_BLOCK, GATE_BLOCK), const3),
        pl.BlockSpec((N_GATE_BLOCKS, GATE_BLOCK, GATE_BLOCK), const3),
        pl.BlockSpec((1, D_RNN), const2),
        pl.BlockSpec((1, D_RNN), const2),
        pl.BlockSpec((1, D_RNN), const2),
        pl.BlockSpec((D_RNN, D), const2),
        pl.BlockSpec((N, (CONV_WIDTH - 1) * D_RNN), const2),
        pl.BlockSpec((N, D_RNN), const2),
    ]
    scratch = [pltpu.VMEM((N, D), BF16), pltpu.VMEM((N, D_RNN), F32), pltpu.VMEM((N, D), F32)]
    return pl.pallas_call(
        _proj_rnn_sample_kernel,
        grid=(N_COL_BLOCKS,),
        in_specs=in_specs,
        out_specs=out_specs,
        out_shape=out_shape,
        scratch_shapes=scratch,
        compiler_params=pltpu.CompilerParams(dimension_semantics=("arbitrary",), vmem_limit_bytes=VMEM_LIMIT),
        name="proj_rnn_sample",
    )(x, w_in, gpre, cw, cb, wa, wx, ba, bx, lam, wro, sconv, h0)


def _attn_kernel(q_ref, kc_ref, vc_ref, kp_ref, vp_ref, o_ref, st_ref, bias_s, *, group, chunk, n_res):
    dilation = ATTN_GROUPS[group][1]
    c = pl.program_id(2)
    nt = (((1,), (1,)), ((), ()))

    @pl.when((pl.program_id(0) == 0) & (pl.program_id(1) == 0) & (c == 0))
    def _init_bias():
        qi = lax.broadcasted_iota(jnp.int32, (Q_BLOCK, Q_BLOCK), 0)
        kj = lax.broadcasted_iota(jnp.int32, (Q_BLOCK, Q_BLOCK), 1)
        steps_prev = Q_BLOCK + qi - kj
        steps_cur = qi - kj
        dist_prev = (steps_prev * dilation).astype(F32)
        dist_cur = (steps_cur * dilation).astype(F32)
        for h in range(H_G):
            slope = float(_SLOPES[group, h])
            bias_s[h] = jnp.where(steps_cur >= 0, -slope * dist_cur, NEG_INF)
            bias_s[H_G + h] = jnp.where(steps_prev <= Q_BLOCK, -slope * dist_prev, NEG_INF)
            bias_s[2 * H_G + h] = jnp.full((Q_BLOCK, Q_BLOCK), NEG_INF, F32)

    lane = lax.broadcasted_iota(jnp.int32, (Q_BLOCK, LANES), 1)
    low = lane < HEAD_DIM
    lane2 = lax.broadcasted_iota(jnp.int32, (2 * Q_BLOCK, LANES), 1)
    low2 = lane2 < HEAD_DIM
    ones_lo = jnp.where(low2, 1.0, 0.0).astype(BF16)
    ones_hi = jnp.where(low2, 0.0, 1.0).astype(BF16)
    stat_lane = lax.broadcasted_iota(jnp.int32, (Q_BLOCK, STAT_LANES), 1)

    def keys_of(res, block, cur_ref, prev_ref, sl):
        if block == 0:
            return jnp.concatenate([prev_ref[res, :, sl], cur_ref[res, 0:Q_BLOCK, sl]], axis=0)
        return cur_ref[res, (block - 1) * Q_BLOCK:(block + 1) * Q_BLOCK, sl]

    for res, i in [(res, i) for res in range(n_res) for i in range(chunk // Q_BLOCK)]:
        r0 = i * Q_BLOCK
        first = jnp.where(c == 0, H_G, 0) if i == 0 else 0
        stats = jnp.zeros((Q_BLOCK, STAT_LANES), F32)
        for p in range(H_G // 2):
            sl = slice(LANES * p, LANES * (p + 1))
            qp = q_ref[res, r0:r0 + Q_BLOCK, sl] * (HEAD_DIM ** -0.5)
            kp = keys_of(res, i, kc_ref, kp_ref, sl)
            vp = keys_of(res, i, vc_ref, vp_ref, sl)
            es, ms = [], []
            for hh in range(2):
                h = 2 * p + hh
                msk = low if hh == 0 else jnp.logical_not(low)
                qm = jnp.where(msk, qp, jnp.zeros_like(qp))
                s = lax.dot_general(qm, kp, nt, preferred_element_type=F32)
                s_p = s[:, 0:Q_BLOCK] + bias_s[H_G + h + first]
                s_c = s[:, Q_BLOCK:2 * Q_BLOCK] + bias_s[h]
                mx = jnp.maximum(jnp.max(s_p, axis=-1, keepdims=True), jnp.max(s_c, axis=-1, keepdims=True))
                es.append(jnp.exp(s_p - mx).astype(BF16))
                es.append(jnp.exp(s_c - mx).astype(BF16))
                ms.append(mx)
            vm0 = jnp.where(low2, vp, jnp.zeros_like(vp))
            vm1 = jnp.where(low2, jnp.zeros_like(vp), vp)
            w = jnp.concatenate([jnp.concatenate([vm0, ones_lo], axis=1),
                                 jnp.concatenate([vm1, ones_hi], axis=1)], axis=0)
            acc = jnp.dot(jnp.concatenate(es, axis=1), w, preferred_element_type=F32)
            l_pair = acc[:, LANES:2 * LANES]
            o_ref[res, r0:r0 + Q_BLOCK, sl] = (acc[:, 0:LANES] / l_pair).astype(o_ref.dtype)
            lse_pair = jnp.where(low, ms[0], ms[1]) + jnp.log(l_pair)
            keep = ((stat_lane % (STAT_LANES // 2)) // STAT_LANES_PER_HEAD) == p
            stats = jnp.where(keep, lse_pair, stats)
        st_ref[res, r0:r0 + Q_BLOCK, :] = stats


def _attn_group(q, k, v, group):
    B, dil, L, _ = q.shape
    chunk = min(ATTN_CHUNK, L)
    n_res = min(ATTN_CHUNK // chunk, dil)
    assert L % chunk == 0 and chunk % Q_BLOCK == 0 and dil % n_res == 0
    cur = lambda b, r, c: (b, r, c, 0)
    prev = lambda b, r, c: (b, r, jnp.maximum(c * (chunk // Q_BLOCK) - 1, 0), 0)
    blk = (None, n_res, chunk, D_ATTN)
    pblk = (None, n_res, Q_BLOCK, D_ATTN)
    return pl.pallas_call(
        functools.partial(_attn_kernel, group=group, chunk=chunk, n_res=n_res),
        grid=(B, dil // n_res, L // chunk),
        in_specs=[pl.BlockSpec(blk, cur), pl.BlockSpec(blk, cur), pl.BlockSpec(blk, cur),
                  pl.BlockSpec(pblk, prev), pl.BlockSpec(pblk, prev)],
        out_specs=[pl.BlockSpec(blk, cur), pl.BlockSpec((None, n_res, chunk, STAT_LANES), cur)],
        out_shape=[jax.ShapeDtypeStruct((B, dil, L, D_ATTN), BF16),
                   jax.ShapeDtypeStruct((B, dil, L, STAT_LANES), F32)],
        scratch_shapes=[pltpu.VMEM((3 * H_G, Q_BLOCK, Q_BLOCK), F32)],
        compiler_params=pltpu.CompilerParams(
            dimension_semantics=("arbitrary", "arbitrary", "arbitrary"), vmem_limit_bytes=VMEM_LIMIT),
        name=f"attn_group{group}",
    )(q, k, v, k, v)


def _finish(att, z_ref, sga_ref, mixr_ref, x_ref, wao_ref, wo_ref, gpost_ref, y_ref):
    z = z_ref[...].astype(F32)
    ya_in = (att * (z * _sigmoid(z))).astype(BF16)
    ya = jnp.dot(ya_in, wao_ref[...], preferred_element_type=F32)
    mixed = mixr_ref[...].astype(F32) + sga_ref[...].astype(F32) * ya
    out = jnp.dot(mixed.astype(BF16), wo_ref[...], preferred_element_type=F32)
    ms = jnp.mean(out * out, axis=-1, keepdims=True)
    y_ref[...] = x_ref[...] + (out * lax.rsqrt(ms + EPS)) * gpost_ref[...]


def _out_sample_kernel(att_ref, z_ref, sga_ref, mixr_ref, x_ref, wao_ref, wo_ref, gpost_ref,
                       y_ref, waob_ref, wob_ref):
    waob_ref[...] = wao_ref[...].astype(BF16)
    wob_ref[...] = wo_ref[...].astype(BF16)
    _finish(att_ref[...], z_ref, sga_ref, mixr_ref, x_ref, waob_ref, wob_ref, gpost_ref, y_ref)


def _merge_out_kernel(o0_ref, o1_ref, o2_ref, s0_ref, s1_ref, s2_ref, z_ref, sga_ref, mixr_ref, x_ref,
                      wao_ref, wo_ref, gpost_ref, y_ref, nat1_s, nat2_s, st1_s, st2_s, *, tm):
    for o_ref, s_ref, nat_s, stn_s, dil in ((o1_ref, s1_ref, nat1_s, st1_s, ATTN_GROUPS[1][1]),
                                            (o2_ref, s2_ref, nat2_s, st2_s, ATTN_GROUPS[2][1])):
        p = tm // dil
        for r in range(dil):
            stn_s[pl.ds(r, p, stride=dil), :] = s_ref[r]
            for c in range(D_ATTN // LANES):
                nat_s[c, pl.ds(r, p, stride=dil), :] = o_ref[r, :, c * LANES:(c + 1) * LANES].astype(F32)

    lse = [s0_ref[...], st1_s[...], st2_s[...]]
    mm = jnp.maximum(jnp.maximum(lse[0], lse[1]), lse[2])
    ws = [jnp.exp(x - mm) for x in lse]
    den = ws[0] + ws[1] + ws[2]
    src = lax.broadcasted_iota(jnp.int32, (STAT_LANES, D_ATTN), 0)
    dst_head = lax.broadcasted_iota(jnp.int32, (STAT_LANES, D_ATTN), 1) // HEAD_DIM
    expand = jnp.where(src == (dst_head % 2) * (STAT_LANES // 2) + (dst_head // 2) * STAT_LANES_PER_HEAD,
                       1.0, 0.0).astype(BF16)
    outs = [o0_ref[...].astype(F32),
            jnp.concatenate([nat1_s[c] for c in range(D_ATTN // LANES)], axis=1),
            jnp.concatenate([nat2_s[c] for c in range(D_ATTN // LANES)], axis=1)]
    att = jnp.zeros((tm, D_ATTN), F32)
    for g in range(N_GROUPS):
        wexp = jnp.dot((ws[g] / den).astype(BF16), expand, preferred_element_type=F32)
        att = att + wexp * outs[g]
    _finish(att, z_ref, sga_ref, mixr_ref, x_ref, wao_ref, wo_ref, gpost_ref, y_ref)


def _merge_out_prompt(os, sts, zg, sga, mixr, x, wao, wo, gpost):
    B, S, D = x.shape
    tm = TM_OUT
    assert S % tm == 0
    tile3 = lambda b, m: (b, m, 0)
    tile4 = lambda b, m: (b, 0, m, 0)
    const = lambda b, m: (0, 0)
    in_specs = []
    for (_, dil), width in [(g, D_ATTN) for g in ATTN_GROUPS] + [(g, STAT_LANES) for g in ATTN_GROUPS]:
        if dil == 1:
            in_specs.append(pl.BlockSpec((None, None, tm, width), tile4))
        else:
            in_specs.append(pl.BlockSpec((None, dil, tm // dil, width), tile4))
    in_specs += [
        pl.BlockSpec((None, tm, D_ATTN), tile3),
        pl.BlockSpec((None, tm, D), tile3),
        pl.BlockSpec((None, tm, D), tile3),
        pl.BlockSpec((None, tm, D), tile3),
        pl.BlockSpec((D_ATTN, D), const),
        pl.BlockSpec((D, D), const),
        pl.BlockSpec((1, D), const),
    ]
    scratch = [
        pltpu.VMEM((D_ATTN // LANES, tm, LANES), F32),
        pltpu.VMEM((D_ATTN // LANES, tm, LANES), F32),
        pltpu.VMEM((tm, STAT_LANES), F32),
        pltpu.VMEM((tm, STAT_LANES), F32),
    ]
    return pl.pallas_call(
        functools.partial(_merge_out_kernel, tm=tm),
        grid=(B, S // tm),
        in_specs=in_specs,
        out_specs=pl.BlockSpec((None, tm, D), tile3),
        out_shape=jax.ShapeDtypeStruct((B, S, D), F32),
        scratch_shapes=scratch,
        compiler_params=pltpu.CompilerParams(
            dimension_semantics=("arbitrary", "arbitrary"), vmem_limit_bytes=VMEM_LIMIT),
        name="merge_out_prompt",
    )(*os, *sts, zg, sga, mixr, x, wao, wo, gpost)


def _out_sample(att, z, sga, mixr, x, wao, wo, gpost):
    N, D = x.shape
    full = lambda shape: pl.BlockSpec(shape, lambda i: (0, 0))
    return pl.pallas_call(
        _out_sample_kernel,
        grid=(1,),
        in_specs=[full((N, D_ATTN)), full((N, D_ATTN)), full((N, D)), full((N, D)), full((N, D)),
                  full((D_ATTN, D)), full((D, D)), full((1, D))],
        out_specs=(full((N, D)), full((D_ATTN, D)), full((D, D))),
        out_shape=(jax.ShapeDtypeStruct((N, D), F32), jax.ShapeDtypeStruct((D_ATTN, D), BF16),
                   jax.ShapeDtypeStruct((D, D), BF16)),
        compiler_params=pltpu.CompilerParams(dimension_semantics=("arbitrary",), vmem_limit_bytes=VMEM_LIMIT),
        name="out_sample",
    )(att, z, sga, mixr, x, wao, wo, gpost)


def _block_diag_chunks(w):
    per = GATE_BLOCK // RNN_BLOCK
    w = w.reshape(N_GATE_BLOCKS, per, RNN_BLOCK, RNN_BLOCK)
    eye = jnp.eye(per, dtype=w.dtype)
    dense = w[:, :, :, None, :] * eye[None, :, None, :, None]
    return dense.reshape(N_GATE_BLOCKS, GATE_BLOCK, GATE_BLOCK)


def _layer(layer, yp, ys, sconv, h0, caches, norm_pre, norm_post, w_in, conv_w, conv_b, lru_w_a, lru_b_a, lru_w_x,
           lru_b_x, lru_lambda, w_rnn_out, w_attn_out, w_out):
    B, S, D = yp.shape
    N = ys.shape[0]
    row = lambda v: v.reshape(1, -1)
    wa = _block_diag_chunks(lru_w_a).astype(BF16)
    wx = _block_diag_chunks(lru_w_x).astype(BF16)
    params = (row(norm_pre), conv_w, row(conv_b), wa, wx, row(lru_b_a), row(lru_b_x), row(lru_lambda))

    xs = ys.reshape(N, D)
    att_in, mixr_s, sga_s, conv_s, h_s, w_in_b, wro = _proj_rnn_sample(
        xs, w_in, *params, w_rnn_out, sconv.reshape(N, (CONV_WIDTH - 1) * D_RNN), h0)
    shared = (w_in_b,) + params + (wro,)
    n_qkv = 3 * N_GROUPS
    z_s = att_in[:, n_qkv * D_ATTN:]
    qkv_s = att_in[:, :n_qkv * D_ATTN].reshape(N, n_qkv, H_G, HEAD_DIM)

    outs = _proj_rnn_prompt(yp, *shared, qkv_s, caches, layer)
    qkv, (zg, mixr, sga, kv0, kv1, kv2, conv_p, h_p, att_s) = outs[:9], outs[9:]
    os, sts = [], []
    for g in range(N_GROUPS):
        o, st = _attn_group(qkv[3 * g], qkv[3 * g + 1], qkv[3 * g + 2], g)
        os.append(o)
        sts.append(st)
    y_s, wao, wo = _out_sample(att_s.reshape(N, D_ATTN), z_s, sga_s, mixr_s, xs, w_attn_out, w_out, row(norm_post))
    y_p = _merge_out_prompt(os, sts, zg, sga, mixr, yp, wao, wo, row(norm_post))
    kv_p = [kv.reshape(B, kv.shape[1], 2, H_G, HEAD_DIM) for kv in (kv0, kv1, kv2)]
    kv_s = [jnp.stack([qkv_s[:, N_GROUPS + g], qkv_s[:, 2 * N_GROUPS + g]], axis=1).reshape(
        N, 1, 2, H_G, HEAD_DIM) for g in range(N_GROUPS)]

    return (y_p, y_s.reshape(N, 1, D), conv_p, conv_s.reshape(N, CONV_WIDTH - 1, D_RNN),
            h_p.reshape(B, D_RNN), h_s, kv_p, kv_s)


def kernel(x_prompt, x_sample, state_conv, state_h, cache_kv_w128, cache_kv_w512, cache_kv_w2048, norm_pre, norm_post, w_in, conv_w, conv_b, lru_w_a, lru_b_a, lru_w_x, lru_b_x, lru_lambda, w_rnn_out, w_attn_out, w_out):
    depth = norm_pre.shape[0]
    caches = (cache_kv_w128, cache_kv_w512, cache_kv_w2048)
    yp, ys = x_prompt, x_sample
    conv_p, conv_s, h_p, h_s = [], [], [], []
    kvp = ([], [], [])
    kvs = ([], [], [])
    for l in range(depth):
        yp, ys, cp, cs, hp, hs, kv_p, kv_s = _layer(
            l, yp, ys, state_conv[l], state_h[l], caches,
            norm_pre[l], norm_post[l], w_in[l], conv_w[l], conv_b[l], lru_w_a[l], lru_b_a[l], lru_w_x[l],
            lru_b_x[l], lru_lambda[l], w_rnn_out[l], w_attn_out[l], w_out[l])
        conv_p.append(cp)
        conv_s.append(cs)
        h_p.append(hp)
        h_s.append(hs)
        for g in range(N_GROUPS):
            kvp[g].append(kv_p[g])
            kvs[g].append(kv_s[g])
    return (yp, ys, jnp.stack(conv_p), jnp.stack(conv_s), jnp.stack(h_p), jnp.stack(h_s),
            jnp.stack(kvp[0]), jnp.stack(kvs[0]), jnp.stack(kvp[1]), jnp.stack(kvs[1]),
            jnp.stack(kvp[2]), jnp.stack(kvs[2]))
```

```python
import functools

import numpy as np
import jax
import jax.numpy as jnp
from jax import lax
from jax.experimental import pallas as pl
from jax.experimental.pallas import tpu as pltpu

F32 = jnp.float32
BF16 = jnp.bfloat16

D_MODEL = 1024
D_RNN = 1024
N_RNN_BLOCKS = 16
RNN_BLOCK = D_RNN // N_RNN_BLOCKS
CONV_WIDTH = 4
LRU_C = 8.0
HEAD_DIM = 64
H_G = 8
ATTN_GROUPS = ((128, 1), (512, 4), (2048, 16))
N_GROUPS = 3
D_ATTN = H_G * HEAD_DIM
Q_BLOCK = 128
ALIBI_MAX = 8.0
EPS = 1e-6
NEG_INF = -1e30
D_IN = 2 * D_RNN + 3 * N_GROUPS * D_ATTN + D_ATTN + 2 * D_MODEL

LANES = 128
TN = 1024
N_COL_BLOCKS = D_IN // TN
COL_U = 0
COL_Z_RNN = COL_U + D_RNN
COL_Q = COL_Z_RNN + D_RNN
COL_K = COL_Q + N_GROUPS * D_ATTN
COL_V = COL_K + N_GROUPS * D_ATTN
COL_Z_ATTN = COL_V + N_GROUPS * D_ATTN
COL_G_RNN = COL_Z_ATTN + D_ATTN
COL_G_ATTN = COL_G_RNN + D_MODEL
GATE_BLOCK = 256
N_GATE_BLOCKS = D_RNN // GATE_BLOCK
STAT_LANES = LANES
STAT_LANES_PER_HEAD = STAT_LANES // H_G

TM_PROMPT = 256
TILES_PER_SAMPLE = 2
SEGMENTS = 8
TM_OUT = 1024
ATTN_CHUNK = 2048
GATE_ROWS = 128
VMEM_LIMIT = 56 * 1024 * 1024


def _alibi_slopes():
    n = N_GROUPS * H_G
    s = np.float32(2.0) ** (np.float32(-ALIBI_MAX) * np.arange(1, n + 1, dtype=np.float32) / np.float32(n))
    return s.reshape(N_GROUPS, H_G)


_SLOPES = _alibi_slopes()


def _softplus(y):
    return jnp.maximum(y, 0.0) + jnp.log1p(jnp.exp(-jnp.abs(y)))


def _sigmoid(x):
    return 0.5 * jnp.tanh(0.5 * x) + 0.5


def _lru_gates(xc, c, wa_ref, wx_ref, ba_ref, bx_ref, lam_ref):
    c0 = c * GATE_BLOCK
    xcb = xc.astype(BF16)
    r = _sigmoid(jnp.dot(xcb, wa_ref[c], preferred_element_type=F32) + ba_ref[:, c0:c0 + GATE_BLOCK])
    i = _sigmoid(jnp.dot(xcb, wx_ref[c], preferred_element_type=F32) + bx_ref[:, c0:c0 + GATE_BLOCK])
    log_a = (-LRU_C * r) * _softplus(-lam_ref[:, c0:c0 + GATE_BLOCK])
    a = jnp.exp(log_a)
    v = 1.0 - a * a
    b = jnp.where(v > 0.0, v * lax.rsqrt(v), 0.0) * i * xc
    return a, b


def _sample_heads(local_heads, first_head, slope_of, qs_ref, cache_refs, att_ref):
    scale = HEAD_DIM ** -0.5
    ri = lax.broadcasted_iota(jnp.int32, (HEAD_DIM, HEAD_DIM), 0)
    ci = lax.broadcasted_iota(jnp.int32, (HEAD_DIM, HEAD_DIM), 1)
    eye = ri == ci

    def to_col(row):
        return jnp.sum(jnp.where(eye, jnp.broadcast_to(row, (HEAD_DIM, HEAD_DIM)), 0.0), axis=-1, keepdims=True)

    def to_row(col):
        return jnp.sum(jnp.where(eye, jnp.broadcast_to(col, (HEAD_DIM, HEAD_DIM)), 0.0), axis=0, keepdims=True)

    for i in local_heads:
        h = first_head + i
        outs, lses = [], []
        for g, (window, dilation) in enumerate(ATTN_GROUPS):
            c_ref = cache_refs[g]
            pos = lax.broadcasted_iota(jnp.int32, (1, window), 1)
            dist = (window - pos).astype(F32)
            bias = jnp.where((pos % dilation) == 0, -slope_of(g, i) * dist, NEG_INF)
            q = qs_ref[g, pl.ds(h, 1), :] * scale
            s_new = jnp.sum(qs_ref[N_GROUPS + g, pl.ds(h, 1), :] * q, axis=-1, keepdims=True)
            v_col = to_col(qs_ref[2 * N_GROUPS + g, pl.ds(h, 1), :])
            s = jnp.sum(c_ref[0, i] * to_col(q), axis=0, keepdims=True) + bias
            mx = jnp.maximum(jnp.max(s, axis=-1, keepdims=True), s_new)
            e = jnp.exp(s - mx)
            e_new = jnp.exp(s_new - mx)
            l = jnp.sum(e, axis=-1, keepdims=True) + e_new
            o = jnp.sum(c_ref[1, i] * e, axis=-1, keepdims=True) + e_new * v_col
            outs.append(o / l)
            lses.append(mx + jnp.log(l))
        mm = jnp.maximum(jnp.maximum(lses[0], lses[1]), lses[2])
        ws = [jnp.exp(x - mm) for x in lses]
        col = (outs[0] * ws[0] + outs[1] * ws[1] + outs[2] * ws[2]) / (ws[0] + ws[1] + ws[2])
        att_ref[pl.ds(h, 1), :] = to_row(col)


def _proj_rnn_kernel(x_ref, w_ref, gpre_ref, cw_ref, cb_ref, wa_ref, wx_ref, ba_ref, bx_ref, lam_ref, wro_ref,
                     qs_ref, c0_ref, c1_ref, c2_ref,
                     q0_ref, k0_ref, v0_ref, q1_ref, k1_ref, v1_ref, q2_ref, k2_ref, v2_ref,
                     zg_ref, mixr_ref, sga_ref, kv0_ref, kv1_ref, kv2_ref, conv_ref, h_ref, att_ref,
                     xn_s, xnf_s, xseg_s, xnseg_s, xn4_s, xn16_s, u_s, a_s, b_s, cprev_s, carry_s,
                     *, tm, n_tiles):
    m = pl.program_id(1)
    tile = pl.program_id(0) * n_tiles + m
    half = D_ATTN
    d1, d2 = ATTN_GROUPS[1][1], ATTN_GROUPS[2][1]
    seg = tm // SEGMENTS
    n_slab = D_MODEL // LANES
    heads_per_tile = H_G // TILES_PER_SAMPLE

    def project(lhs_s, col, width=half):
        return jnp.dot(lhs_s[...], w_ref[:, col:col + width], preferred_element_type=F32)

    def emit(out_ref, val, dilation):
        p = tm // dilation
        for r in range(dilation):
            out_ref[r] = val[r * p:(r + 1) * p, :].astype(BF16)

    def natural_tail(out_ref, out_c0, val, dilation):
        p = tm // dilation
        for c in range(half // LANES):
            for r in range(dilation):
                xnf_s[c, pl.ds(r, p, stride=dilation), :] = val[r * p:(r + 1) * p, c * LANES:(c + 1) * LANES]
            out_ref[:, out_c0 + c * LANES:out_c0 + (c + 1) * LANES] = xnf_s[c]

    def sample_heads():
        part = tile % TILES_PER_SAMPLE
        first = part * heads_per_tile

        def slope_of(g, i):
            table = [float(_SLOPES[g, t * heads_per_tile + i]) for t in range(TILES_PER_SAMPLE)]
            out = jnp.float32(table[0])
            for t in range(1, TILES_PER_SAMPLE):
                out = jnp.where(part == t, jnp.float32(table[t]), out)
            return out

        _sample_heads(range(heads_per_tile), first, slope_of, qs_ref, (c0_ref, c1_ref, c2_ref), att_ref)

    def _recurrent_blocks():
        @pl.when(m == 0)
        def _reset():
            cprev_s[...] = jnp.zeros((8, D_RNN), F32)
            carry_s[...] = jnp.zeros((1, D_RNN), F32)

        for s in range(SEGMENTS):
            r0 = s * seg
            xv = x_ref[r0:r0 + seg, :]
            ms = jnp.mean(xv * xv, axis=-1, keepdims=True)
            xn = (xv * lax.rsqrt(ms + EPS)) * gpre_ref[...]
            xn_s[r0:r0 + seg, :] = xn.astype(BF16)
            for c in range(n_slab):
                piece = xn[:, c * LANES:(c + 1) * LANES]
                xnf_s[c, r0:r0 + seg, :] = piece
                xseg_s[c, pl.ds(s, seg, stride=SEGMENTS), :] = piece

        for c in range(n_slab):
            xnseg_s[:, c * LANES:(c + 1) * LANES] = xseg_s[c].astype(BF16)

        u_s[...] = project(xnseg_s, COL_U, D_RNN)

        sub = lax.broadcasted_iota(jnp.int32, (SEGMENTS, GATE_BLOCK), 0)
        gate_rows = min(GATE_ROWS, tm)
        for c in range(N_GATE_BLOCKS):
            cols = slice(c * GATE_BLOCK, (c + 1) * GATE_BLOCK)

            def wrapped(t):
                v = pltpu.roll(u_s[SEGMENTS * (seg + t):SEGMENTS * (seg + t + 1), cols], 1, axis=0)
                return jnp.where(sub == 0, cprev_s[8 + t:9 + t, cols], v)

            wrap = {t: wrapped(t) for t in range(1 - CONV_WIDTH, 0)}
            for rc in range(tm // gate_rows):
                r0 = rc * gate_rows
                xc = cb_ref[:, cols] + u_s[r0:r0 + gate_rows, cols] * cw_ref[CONV_WIDTH - 1:CONV_WIDTH, cols]
                for sh in range(1, CONV_WIDTH):
                    if r0 == 0:
                        head = [wrap[t - sh] for t in range(sh)]
                        ush = jnp.concatenate(head + [u_s[0:gate_rows - SEGMENTS * sh, cols]], axis=0)
                    else:
                        ush = u_s[r0 - SEGMENTS * sh:r0 - SEGMENTS * sh + gate_rows, cols]
                    xc = xc + ush * cw_ref[CONV_WIDTH - 1 - sh:CONV_WIDTH - sh, cols]
                a, b = _lru_gates(xc, c, wa_ref, wx_ref, ba_ref, bx_ref, lam_ref)
                a_s[r0:r0 + gate_rows, cols] = a
                b_s[r0:r0 + gate_rows, cols] = b

        hloc = jnp.zeros((SEGMENTS, D_RNN), F32)
        cum = jnp.ones((SEGMENTS, D_RNN), F32)
        for t in range(seg):
            rows = slice(t * SEGMENTS, (t + 1) * SEGMENTS)
            av = a_s[rows, :]
            hloc = av * hloc + b_s[rows, :]
            cum = av * cum
            b_s[rows, :] = hloc
            a_s[rows, :] = cum
        state = carry_s[...]
        enter = []
        for s in range(SEGMENTS):
            enter.append(state)
            state = hloc[s:s + 1, :] + cum[s:s + 1, :] * state
        enter = jnp.concatenate(enter, axis=0)
        carry_s[...] = state
        h_ref[...] = state
        for k in range(1, CONV_WIDTH):
            last = u_s[tm - SEGMENTS * (k - 1) - 1:tm - SEGMENTS * (k - 1), :]
            conv_ref[CONV_WIDTH - 1 - k:CONV_WIDTH - k, :] = last
            cprev_s[8 - k:9 - k, :] = last

        z = project(xnseg_s, COL_Z_RNN, D_RNN)
        split = (seg, SEGMENTS, D_RNN)
        h = b_s[...].reshape(split) + a_s[...].reshape(split) * enter[None]
        hz = (h.reshape(tm, D_RNN) * (z * _sigmoid(z))).astype(BF16)
        yr = jnp.dot(hz, wro_ref[...], preferred_element_type=F32)

        mix = _sigmoid(project(xnseg_s, COL_G_RNN, D_MODEL)) * yr
        for c in range(n_slab):
            xseg_s[c] = mix[:, c * LANES:(c + 1) * LANES]
        for s in range(SEGMENTS):
            for c in range(n_slab):
                mixr_ref[s * seg:(s + 1) * seg, c * LANES:(c + 1) * LANES] = \
                    xseg_s[c, pl.ds(s, seg, stride=SEGMENTS), :].astype(BF16)

    def _attention_blocks():
        for dil, dst in ((d1, xn4_s), (d2, xn16_s)):
            p = tm // dil
            for r in range(dil):
                for c in range(n_slab):
                    dst[r * p:(r + 1) * p, c * LANES:(c + 1) * LANES] = \
                        xnf_s[c, pl.ds(r, p, stride=dil), :].astype(BF16)

        sga_ref[...] = _sigmoid(project(xn_s, COL_G_ATTN, D_MODEL)).astype(BF16)
        zg_ref[...] = project(xn_s, COL_Z_ATTN).astype(BF16)

        lhs = (xn_s, xn4_s, xn16_s)
        qkv_refs = ((q0_ref, k0_ref, v0_ref), (q1_ref, k1_ref, v1_ref), (q2_ref, k2_ref, v2_ref))
        for g, (_, dil) in enumerate(ATTN_GROUPS):
            q_ref, k_ref, v_ref = qkv_refs[g]
            emit(q_ref, project(lhs[g], COL_Q + g * D_ATTN), dil)
            for col, out_ref, c0 in ((COL_K, k_ref, 0), (COL_V, v_ref, half)):
                r = project(lhs[g], col + g * D_ATTN)
                emit(out_ref, r, dil)
                if g == 0:
                    kv0_ref[:, c0:c0 + half] = r[tm - ATTN_GROUPS[0][0]:tm, :]
                else:
                    natural_tail(kv1_ref if g == 1 else kv2_ref, c0, r, dil)
        sample_heads()

    _recurrent_blocks()
    _attention_blocks()


def _proj_rnn_prompt(x, w_in, gpre, cw, cb, wa, wx, ba, bx, lam, wro, qkv_s, caches, layer):
    B, S, D = x.shape
    tm = TM_PROMPT
    n_tiles = S // tm
    N = qkv_s.shape[0]
    heads_per_tile = H_G // TILES_PER_SAMPLE
    assert S % tm == 0 and tm % (SEGMENTS * 16) == 0 and tm >= ATTN_GROUPS[0][0]
    assert all(w % tm == 0 and S >= w for w, _ in ATTN_GROUPS[1:])
    assert B * n_tiles == N * TILES_PER_SAMPLE and n_tiles % TILES_PER_SAMPLE == 0
    cache_views = [jnp.transpose(c, (0, 1, 3, 4, 5, 2)) for c in caches]

    const2 = lambda b, m: (0, 0)
    const3 = lambda b, m: (0, 0, 0)
    tile3 = lambda b, m: (b, m, 0)
    tile4 = lambda b, m: (b, 0, m, 0)
    per_seq = lambda b, m: (b, 0, 0)
    sample_of = lambda b, m: (b * n_tiles + m) // TILES_PER_SAMPLE
    head_block = lambda b, m: (b * n_tiles + m) % TILES_PER_SAMPLE

    def last_rows(window):
        first_tile = n_tiles - window // tm
        return lambda b, m: (b, jnp.maximum(m - first_tile, 0), 0)

    resident = dict(pipeline_mode=pl.Buffered(1))
    qkv_shapes, qkv_specs = [], []
    for _, dil in ATTN_GROUPS:
        for _ in range(3):
            qkv_shapes.append(jax.ShapeDtypeStruct((B, dil, S // dil, D_ATTN), BF16))
            qkv_specs.append(pl.BlockSpec((None, dil, tm // dil, D_ATTN), tile4))
    out_shape = tuple(qkv_shapes) + (
        jax.ShapeDtypeStruct((B, S, D_ATTN), BF16),
        jax.ShapeDtypeStruct((B, S, D), BF16),
        jax.ShapeDtypeStruct((B, S, D), BF16),
        jax.ShapeDtypeStruct((B, ATTN_GROUPS[0][0], 2 * D_ATTN), F32),
        jax.ShapeDtypeStruct((B, ATTN_GROUPS[1][0], 2 * D_ATTN), F32),
        jax.ShapeDtypeStruct((B, ATTN_GROUPS[2][0], 2 * D_ATTN), F32),
        jax.ShapeDtypeStruct((B, CONV_WIDTH - 1, D_RNN), F32),
        jax.ShapeDtypeStruct((B, 1, D_RNN), F32),
        jax.ShapeDtypeStruct((N, H_G, HEAD_DIM), F32),
    )
    out_specs = tuple(qkv_specs) + (
        pl.BlockSpec((None, tm, D_ATTN), tile3),
        pl.BlockSpec((None, tm, D), tile3),
        pl.BlockSpec((None, tm, D), tile3),
        pl.BlockSpec((None, ATTN_GROUPS[0][0], 2 * D_ATTN), per_seq),
        pl.BlockSpec((None, tm, 2 * D_ATTN), last_rows(ATTN_GROUPS[1][0])),
        pl.BlockSpec((None, tm, 2 * D_ATTN), last_rows(ATTN_GROUPS[2][0])),
        pl.BlockSpec((None, CONV_WIDTH - 1, D_RNN), per_seq),
        pl.BlockSpec((None, 1, D_RNN), per_seq),
        pl.BlockSpec((None, H_G, HEAD_DIM), lambda b, m: (sample_of(b, m), 0, 0)),
    )
    in_specs = [
        pl.BlockSpec((None, tm, D), tile3),
        pl.BlockSpec((D, D_IN), const2, **resident),
        pl.BlockSpec((1, D), const2),
        pl.BlockSpec((CONV_WIDTH, D_RNN), const2),
        pl.BlockSpec((1, D_RNN), const2),
        pl.BlockSpec((N_GATE_BLOCKS, GATE_BLOCK, GATE_BLOCK), const3),
        pl.BlockSpec((N_GATE_BLOCKS, GATE_BLOCK, GATE_BLOCK), const3),
        pl.BlockSpec((1, D_RNN), const2),
        pl.BlockSpec((1, D_RNN), const2),
        pl.BlockSpec((1, D_RNN), const2),
        pl.BlockSpec((D_RNN, D), const2, **resident),
        pl.BlockSpec((None, 3 * N_GROUPS, H_G, HEAD_DIM), lambda b, m: (sample_of(b, m), 0, 0, 0)),
    ]
    in_specs += [pl.BlockSpec((None, None, 2, heads_per_tile, HEAD_DIM, window),
                              lambda b, m: (layer, sample_of(b, m), 0, head_block(b, m), 0, 0))
                 for window, _ in ATTN_GROUPS]
    scratch = [
        pltpu.VMEM((tm, D), BF16),
        pltpu.VMEM((D // LANES, tm, LANES), F32),
        pltpu.VMEM((D // LANES, tm, LANES), F32),
        pltpu.VMEM((tm, D), BF16),
        pltpu.VMEM((tm, D), BF16),
        pltpu.VMEM((tm, D), BF16),
        pltpu.VMEM((tm, D_RNN), F32),
        pltpu.VMEM((tm, D_RNN), F32),
        pltpu.VMEM((tm, D_RNN), F32),
        pltpu.VMEM((8, D_RNN), F32),
        pltpu.VMEM((1, D_RNN), F32),
    ]
    return pl.pallas_call(
        functools.partial(_proj_rnn_kernel, tm=tm, n_tiles=n_tiles),
        grid=(B, n_tiles),
        in_specs=in_specs,
        out_specs=out_specs,
        out_shape=out_shape,
        scratch_shapes=scratch,
        compiler_params=pltpu.CompilerParams(
            dimension_semantics=("arbitrary", "arbitrary"), vmem_limit_bytes=VMEM_LIMIT),
        name="proj_rnn_prompt",
    )(x, w_in, gpre, cw, cb, wa, wx, ba, bx, lam, wro, qkv_s, *cache_views)


def _proj_rnn_sample_kernel(x_ref, w_ref, gpre_ref, cw_ref, cb_ref, wa_ref, wx_ref, ba_ref, bx_ref, lam_ref, wro_ref,
                            sc_ref, h0_ref,
                            att_ref, mixr_ref, sga_ref, conv_ref, h_ref, wb_ref, wrob_ref,
                            xn_s, h_s, yr_s):
    j = pl.program_id(0)

    def project():
        wb_ref[...] = w_ref[...].astype(BF16)
        return jnp.dot(xn_s[...], wb_ref[...], preferred_element_type=F32)

    @pl.when(j == 0)
    def _u_block():
        xv = x_ref[...]
        ms = jnp.mean(xv * xv, axis=-1, keepdims=True)
        xn_s[...] = ((xv * lax.rsqrt(ms + EPS)) * gpre_ref[...]).astype(BF16)
        u = project()
        taps = [sc_ref[:, k * D_RNN:(k + 1) * D_RNN] for k in range(CONV_WIDTH - 1)] + [u]
        for c in range(N_GATE_BLOCKS):
            c0 = c * GATE_BLOCK
            xc = cb_ref[:, c0:c0 + GATE_BLOCK]
            for tap in range(CONV_WIDTH):
                xc = xc + taps[tap][:, c0:c0 + GATE_BLOCK] * cw_ref[tap:tap + 1, c0:c0 + GATE_BLOCK]
            a, b = _lru_gates(xc, c, wa_ref, wx_ref, ba_ref, bx_ref, lam_ref)
            h = a * h0_ref[:, c0:c0 + GATE_BLOCK] + b
            h_s[:, c0:c0 + GATE_BLOCK] = h
            h_ref[:, c0:c0 + GATE_BLOCK] = h
        for k in range(1, CONV_WIDTH):
            conv_ref[:, (k - 1) * D_RNN:k * D_RNN] = taps[k]

    @pl.when(j == 1)
    def _z_rnn_block():
        z = project()
        hz = (h_s[...] * (z * _sigmoid(z))).astype(BF16)
        wrob_ref[...] = wro_ref[...].astype(BF16)
        yr_s[...] = jnp.dot(hz, wrob_ref[...], preferred_element_type=F32)

    @pl.when(j == 2)
    def _g_rnn_block():
        mixr_ref[...] = _sigmoid(project()) * yr_s[...]

    @pl.when(j == 3)
    def _g_attn_block():
        sga_ref[...] = _sigmoid(project())

    @pl.when(j >= 4)
    def _attn_blocks():
        att_ref[...] = project()


def _proj_rnn_sample(x, w_in, gpre, cw, cb, wa, wx, ba, bx, lam, wro, sconv, h0):
    N, D = x.shape
    const2 = lambda j: (0, 0)
    const3 = lambda j: (0, 0, 0)
    n_att = N_COL_BLOCKS - 4
    first_att, g_rnn_block = COL_Q // TN, COL_G_RNN // TN
    assert COL_Q % TN == 0 and COL_G_RNN % TN == 0 and COL_G_RNN - COL_Q == n_att * TN
    w_block = lambda j: jnp.where(j < 2, j, jnp.where(j < 4, j + g_rnn_block - 2, j - 4 + first_att))
    out_shape = (
        jax.ShapeDtypeStruct((N, n_att * TN), F32),
        jax.ShapeDtypeStruct((N, D), F32),
        jax.ShapeDtypeStruct((N, D), F32),
        jax.ShapeDtypeStruct((N, (CONV_WIDTH - 1) * D_RNN), F32),
        jax.ShapeDtypeStruct((N, D_RNN), F32),
        jax.ShapeDtypeStruct((D, D_IN), BF16),
        jax.ShapeDtypeStruct((D_RNN, D), BF16),
    )
    out_specs = (
        pl.BlockSpec((N, TN), lambda j: (0, jnp.maximum(j - 4, 0))),
        pl.BlockSpec((N, D), const2),
        pl.BlockSpec((N, D), const2),
        pl.BlockSpec((N, (CONV_WIDTH - 1) * D_RNN), const2),
        pl.BlockSpec((N, D_RNN), const2),
        pl.BlockSpec((D, TN), lambda j: (0, w_block(j))),
        pl.BlockSpec((D_RNN, D), const2),
    )
    in_specs = [
        pl.BlockSpec((N, D), const2),
        pl.BlockSpec((D, TN), lambda j: (0, w_block(j))),
        pl.BlockSpec((1, D), const2),
        pl.BlockSpec((CONV_WIDTH, D_RNN), const2),
        pl.BlockSpec((1, D_RNN), const2),
        pl.BlockSpec((N_GATE_BLOCKS, GATE_BLOCK, GATE_BLOCK), const3),
        pl.BlockSpec((N_GATE_BLOCKS, GATE_BLOCK, GATE_BLOCK), const3),
        pl.BlockSpec((1, D_RNN), const2),
        pl.BlockSpec((1, D_RNN), const2),
        pl.BlockSpec((1, D_RNN), const2),
        pl.BlockSpec((D_RNN, D), const2),
        pl.BlockSpec((N, (CONV_WIDTH - 1) * D_RNN), const2),
        pl.BlockSpec((N, D_RNN), const2),
    ]
    scratch = [pltpu.VMEM((N, D), BF16), pltpu.VMEM((N, D_RNN), F32), pltpu.VMEM((N, D), F32)]
    return pl.pallas_call(
        _proj_rnn_sample_kernel,
        grid=(N_COL_BLOCKS,),
        in_specs=in_specs,
        out_specs=out_specs,
        out_shape=out_shape,
        scratch_shapes=scratch,
        compiler_params=pltpu.CompilerParams(dimension_semantics=("arbitrary",), vmem_limit_bytes=VMEM_LIMIT),
        name="proj_rnn_sample",
    )(x, w_in, gpre, cw, cb, wa, wx, ba, bx, lam, wro, sconv, h0)


def _attn_kernel(q_ref, kc_ref, vc_ref, kp_ref, vp_ref, o_ref, st_ref, bias_s, *, group, chunk, n_res):
    dilation = ATTN_GROUPS[group][1]
    c = pl.program_id(2)
    nt = (((1,), (1,)), ((), ()))

    @pl.when((pl.program_id(0) == 0) & (pl.program_id(1) == 0) & (c == 0))
    def _init_bias():
        qi = lax.broadcasted_iota(jnp.int32, (Q_BLOCK, Q_BLOCK), 0)
        kj = lax.broadcasted_iota(jnp.int32, (Q_BLOCK, Q_BLOCK), 1)
        steps_prev = Q_BLOCK + qi - kj
        steps_cur = qi - kj
        dist_prev = (steps_prev * dilation).astype(F32)
        dist_cur = (steps_cur * dilation).astype(F32)
        for h in range(H_G):
            slope = float(_SLOPES[group, h])
            bias_s[h] = jnp.where(steps_cur >= 0, -slope * dist_cur, NEG_INF)
            bias_s[H_G + h] = jnp.where(steps_prev <= Q_BLOCK, -slope * dist_prev, NEG_INF)
            bias_s[2 * H_G + h] = jnp.full((Q_BLOCK, Q_BLOCK), NEG_INF, F32)

    lane = lax.broadcasted_iota(jnp.int32, (Q_BLOCK, LANES), 1)
    low = lane < HEAD_DIM
    lane2 = lax.broadcasted_iota(jnp.int32, (2 * Q_BLOCK, LANES), 1)
    low2 = lane2 < HEAD_DIM
    ones_lo = jnp.where(low2, 1.0, 0.0).astype(BF16)
    ones_hi = jnp.where(low2, 0.0, 1.0).astype(BF16)
    stat_lane = lax.broadcasted_iota(jnp.int32, (Q_BLOCK, STAT_LANES), 1)

    def keys_of(res, block, cur_ref, prev_ref, sl):
        if block == 0:
            return jnp.concatenate([prev_ref[res, :, sl], cur_ref[res, 0:Q_BLOCK, sl]], axis=0)
        return cur_ref[res, (block - 1) * Q_BLOCK:(block + 1) * Q_BLOCK, sl]

    for res, i in [(res, i) for res in range(n_res) for i in range(chunk // Q_BLOCK)]:
        r0 = i * Q_BLOCK
        first = jnp.where(c == 0, H_G, 0) if i == 0 else 0
        stats = jnp.zeros((Q_BLOCK, STAT_LANES), F32)
        for p in range(H_G // 2):
            sl = slice(LANES * p, LANES * (p + 1))
            qp = q_ref[res, r0:r0 + Q_BLOCK, sl] * (HEAD_DIM ** -0.5)
            kp = keys_of(res, i, kc_ref, kp_ref, sl)
            vp = keys_of(res, i, vc_ref, vp_ref, sl)
            es, ms = [], []
            for hh in range(2):
                h = 2 * p + hh
                msk = low if hh == 0 else jnp.logical_not(low)
                qm = jnp.where(msk, qp, jnp.zeros_like(qp))
                s = lax.dot_general(qm, kp, nt, preferred_element_type=F32)
                s_p = s[:, 0:Q_BLOCK] + bias_s[H_G + h + first]
                s_c = s[:, Q_BLOCK:2 * Q_BLOCK] + bias_s[h]
                mx = jnp.maximum(jnp.max(s_p, axis=-1, keepdims=True), jnp.max(s_c, axis=-1, keepdims=True))
                es.append(jnp.exp(s_p - mx).astype(BF16))
                es.append(jnp.exp(s_c - mx).astype(BF16))
                ms.append(mx)
            vm0 = jnp.where(low2, vp, jnp.zeros_like(vp))
            vm1 = jnp.where(low2, jnp.zeros_like(vp), vp)
            w = jnp.concatenate([jnp.concatenate([vm0, ones_lo], axis=1),
                                 jnp.concatenate([vm1, ones_hi], axis=1)], axis=0)
            acc = jnp.dot(jnp.concatenate(es, axis=1), w, preferred_element_type=F32)
            l_pair = acc[:, LANES:2 * LANES]
            o_ref[res, r0:r0 + Q_BLOCK, sl] = (acc[:, 0:LANES] / l_pair).astype(o_ref.dtype)
            lse_pair = jnp.where(low, ms[0], ms[1]) + jnp.log(l_pair)
            keep = ((stat_lane % (STAT_LANES // 2)) // STAT_LANES_PER_HEAD) == p
            stats = jnp.where(keep, lse_pair, stats)
        st_ref[res, r0:r0 + Q_BLOCK, :] = stats


def _attn_group(q, k, v, group):
    B, dil, L, _ = q.shape
    chunk = min(ATTN_CHUNK, L)
    n_res = min(ATTN_CHUNK // chunk, dil)
    assert L % chunk == 0 and chunk % Q_BLOCK == 0 and dil % n_res == 0
    cur = lambda b, r, c: (b, r, c, 0)
    prev = lambda b, r, c: (b, r, jnp.maximum(c * (chunk // Q_BLOCK) - 1, 0), 0)
    blk = (None, n_res, chunk, D_ATTN)
    pblk = (None, n_res, Q_BLOCK, D_ATTN)
    return pl.pallas_call(
        functools.partial(_attn_kernel, group=group, chunk=chunk, n_res=n_res),
        grid=(B, dil // n_res, L // chunk),
        in_specs=[pl.BlockSpec(blk, cur), pl.BlockSpec(blk, cur), pl.BlockSpec(blk, cur),
                  pl.BlockSpec(pblk, prev), pl.BlockSpec(pblk, prev)],
        out_specs=[pl.BlockSpec(blk, cur), pl.BlockSpec((None, n_res, chunk, STAT_LANES), cur)],
        out_shape=[jax.ShapeDtypeStruct((B, dil, L, D_ATTN), BF16),
                   jax.ShapeDtypeStruct((B, dil, L, STAT_LANES), F32)],
        scratch_shapes=[pltpu.VMEM((3 * H_G, Q_BLOCK, Q_BLOCK), F32)],
        compiler_params=pltpu.CompilerParams(
            dimension_semantics=("arbitrary", "arbitrary", "arbitrary"), vmem_limit_bytes=VMEM_LIMIT),
        name=f"attn_group{group}",
    )(q, k, v, k, v)


def _finish(att, z_ref, sga_ref, mixr_ref, x_ref, wao_ref, wo_ref, gpost_ref, y_ref):
    z = z_ref[...].astype(F32)
    ya_in = (att * (z * _sigmoid(z))).astype(BF16)
    ya = jnp.dot(ya_in, wao_ref[...], preferred_element_type=F32)
    mixed = mixr_ref[...].astype(F32) + sga_ref[...].astype(F32) * ya
    out = jnp.dot(mixed.astype(BF16), wo_ref[...], preferred_element_type=F32)
    ms = jnp.mean(out * out, axis=-1, keepdims=True)
    y_ref[...] = x_ref[...] + (out * lax.rsqrt(ms + EPS)) * gpost_ref[...]


def _out_sample_kernel(att_ref, z_ref, sga_ref, mixr_ref, x_ref, wao_ref, wo_ref, gpost_ref,
                       y_ref, waob_ref, wob_ref):
    waob_ref[...] = wao_ref[...].astype(BF16)
    wob_ref[...] = wo_ref[...].astype(BF16)
    _finish(att_ref[...], z_ref, sga_ref, mixr_ref, x_ref, waob_ref, wob_ref, gpost_ref, y_ref)


def _merge_out_kernel(o0_ref, o1_ref, o2_ref, s0_ref, s1_ref, s2_ref, z_ref, sga_ref, mixr_ref, x_ref,
                      wao_ref, wo_ref, gpost_ref, y_ref, nat1_s, nat2_s, st1_s, st2_s, *, tm):
    for o_ref, s_ref, nat_s, stn_s, dil in ((o1_ref, s1_ref, nat1_s, st1_s, ATTN_GROUPS[1][1]),
                                            (o2_ref, s2_ref, nat2_s, st2_s, ATTN_GROUPS[2][1])):
        p = tm // dil
        for r in range(dil):
            stn_s[pl.ds(r, p, stride=dil), :] = s_ref[r]
            for c in range(D_ATTN // LANES):
                nat_s[c, pl.ds(r, p, stride=dil), :] = o_ref[r, :, c * LANES:(c + 1) * LANES].astype(F32)

    lse = [s0_ref[...], st1_s[...], st2_s[...]]
    mm = jnp.maximum(jnp.maximum(lse[0], lse[1]), lse[2])
    ws = [jnp.exp(x - mm) for x in lse]
    den = ws[0] + ws[1] + ws[2]
    src = lax.broadcasted_iota(jnp.int32, (STAT_LANES, D_ATTN), 0)
    dst_head = lax.broadcasted_iota(jnp.int32, (STAT_LANES, D_ATTN), 1) // HEAD_DIM
    expand = jnp.where(src == (dst_head % 2) * (STAT_LANES // 2) + (dst_head // 2) * STAT_LANES_PER_HEAD,
                       1.0, 0.0).astype(BF16)
    outs = [o0_ref[...].astype(F32),
            jnp.concatenate([nat1_s[c] for c in range(D_ATTN // LANES)], axis=1),
            jnp.concatenate([nat2_s[c] for c in range(D_ATTN // LANES)], axis=1)]
    att = jnp.zeros((tm, D_ATTN), F32)
    for g in range(N_GROUPS):
        wexp = jnp.dot((ws[g] / den).astype(BF16), expand, preferred_element_type=F32)
        att = att + wexp * outs[g]
    _finish(att, z_ref, sga_ref, mixr_ref, x_ref, wao_ref, wo_ref, gpost_ref, y_ref)


def _merge_out_prompt(os, sts, zg, sga, mixr, x, wao, wo, gpost):
    B, S, D = x.shape
    tm = TM_OUT
    assert S % tm == 0
    tile3 = lambda b, m: (b, m, 0)
    tile4 = lambda b, m: (b, 0, m, 0)
    const = lambda b, m: (0, 0)
    in_specs = []
    for (_, dil), width in [(g, D_ATTN) for g in ATTN_GROUPS] + [(g, STAT_LANES) for g in ATTN_GROUPS]:
        if dil == 1:
            in_specs.append(pl.BlockSpec((None, None, tm, width), tile4))
        else:
            in_specs.append(pl.BlockSpec((None, dil, tm // dil, width), tile4))
    in_specs += [
        pl.BlockSpec((None, tm, D_ATTN), tile3),
        pl.BlockSpec((None, tm, D), tile3),
        pl.BlockSpec((None, tm, D), tile3),
        pl.BlockSpec((None, tm, D), tile3),
        pl.BlockSpec((D_ATTN, D), const),
        pl.BlockSpec((D, D), const),
        pl.BlockSpec((1, D), const),
    ]
    scratch = [
        pltpu.VMEM((D_ATTN // LANES, tm, LANES), F32),
        pltpu.VMEM((D_ATTN // LANES, tm, LANES), F32),
        pltpu.VMEM((tm, STAT_LANES), F32),
        pltpu.VMEM((tm, STAT_LANES), F32),
    ]
    return pl.pallas_call(
        functools.partial(_merge_out_kernel, tm=tm),
        grid=(B, S // tm),
        in_specs=in_specs,
        out_specs=pl.BlockSpec((None, tm, D), tile3),
        out_shape=jax.ShapeDtypeStruct((B, S, D), F32),
        scratch_shapes=scratch,
        compiler_params=pltpu.CompilerParams(
            dimension_semantics=("arbitrary", "arbitrary"), vmem_limit_bytes=VMEM_LIMIT),
        name="merge_out_prompt",
    )(*os, *sts, zg, sga, mixr, x, wao, wo, gpost)


def _out_sample(att, z, sga, mixr, x, wao, wo, gpost):
    N, D = x.shape
    full = lambda shape: pl.BlockSpec(shape, lambda i: (0, 0))
    return pl.pallas_call(
        _out_sample_kernel,
        grid=(1,),
        in_specs=[full((N, D_ATTN)), full((N, D_ATTN)), full((N, D)), full((N, D)), full((N, D)),
                  full((D_ATTN, D)), full((D, D)), full((1, D))],
        out_specs=(full((N, D)), full((D_ATTN, D)), full((D, D))),
        out_shape=(jax.ShapeDtypeStruct((N, D), F32), jax.ShapeDtypeStruct((D_ATTN, D), BF16),
                   jax.ShapeDtypeStruct((D, D), BF16)),
        compiler_params=pltpu.CompilerParams(dimension_semantics=("arbitrary",), vmem_limit_bytes=VMEM_LIMIT),
        name="out_sample",
    )(att, z, sga, mixr, x, wao, wo, gpost)


def _block_diag_chunks(w):
    per = GATE_BLOCK // RNN_BLOCK
    w = w.reshape(N_GATE_BLOCKS, per, RNN_BLOCK, RNN_BLOCK)
    eye = jnp.eye(per, dtype=w.dtype)
    dense = w[:, :, :, None, :] * eye[None, :, None, :, None]
    return dense.reshape(N_GATE_BLOCKS, GATE_BLOCK, GATE_BLOCK)


def _layer(layer, yp, ys, sconv, h0, caches, norm_pre, norm_post, w_in, conv_w, conv_b, lru_w_a, lru_b_a, lru_w_x,
           lru_b_x, lru_lambda, w_rnn_out, w_attn_out, w_out):
    B, S, D = yp.shape
    N = ys.shape[0]
    row = lambda v: v.reshape(1, -1)
    wa = _block_diag_chunks(lru_w_a).astype(BF16)
    wx = _block_diag_chunks(lru_w_x).astype(BF16)
    params = (row(norm_pre), conv_w, row(conv_b), wa, wx, row(lru_b_a), row(lru_b_x), row(lru_lambda))

    xs = ys.reshape(N, D)
    att_in, mixr_s, sga_s, conv_s, h_s, w_in_b, wro = _proj_rnn_sample(
        xs, w_in, *params, w_rnn_out, sconv.reshape(N, (CONV_WIDTH - 1) * D_RNN), h0)
    shared = (w_in_b,) + params + (wro,)
    n_qkv = 3 * N_GROUPS
    z_s = att_in[:, n_qkv * D_ATTN:]
    qkv_s = att_in[:, :n_qkv * D_ATTN].reshape(N, n_qkv, H_G, HEAD_DIM)

    outs = _proj_rnn_prompt(yp, *shared, qkv_s, caches, layer)
    qkv, (zg, mixr, sga, kv0, kv1, kv2, conv_p, h_p, att_s) = outs[:9], outs[9:]
    os, sts = [], []
    for g in range(N_GROUPS):
        o, st = _attn_group(qkv[3 * g], qkv[3 * g + 1], qkv[3 * g + 2], g)
        os.append(o)
        sts.append(st)
    y_s, wao, wo = _out_sample(att_s.reshape(N, D_ATTN), z_s, sga_s, mixr_s, xs, w_attn_out, w_out, row(norm_post))
    y_p = _merge_out_prompt(os, sts, zg, sga, mixr, yp, wao, wo, row(norm_post))
    kv_p = [kv.reshape(B, kv.shape[1], 2, H_G, HEAD_DIM) for kv in (kv0, kv1, kv2)]
    kv_s = [jnp.stack([qkv_s[:, N_GROUPS + g], qkv_s[:, 2 * N_GROUPS + g]], axis=1).reshape(
        N, 1, 2, H_G, HEAD_DIM) for g in range(N_GROUPS)]

    return (y_p, y_s.reshape(N, 1, D), conv_p, conv_s.reshape(N, CONV_WIDTH - 1, D_RNN),
            h_p.reshape(B, D_RNN), h_s, kv_p, kv_s)


def kernel(x_prompt, x_sample, state_conv, state_h, cache_kv_w128, cache_kv_w512, cache_kv_w2048, norm_pre, norm_post, w_in, conv_w, conv_b, lru_w_a, lru_b_a, lru_w_x, lru_b_x, lru_lambda, w_rnn_out, w_attn_out, w_out):
    depth = norm_pre.shape[0]
    caches = (cache_kv_w128, cache_kv_w512, cache_kv_w2048)
    yp, ys = x_prompt, x_sample
    conv_p, conv_s, h_p, h_s = [], [], [], []
    kvp = ([], [], [])
    kvs = ([], [], [])
    for l in range(depth):
        yp, ys, cp, cs, hp, hs, kv_p, kv_s = _layer(
            l, yp, ys, state_conv[l], state_h[l], caches,
            norm_pre[l], norm_post[l], w_in[l], conv_w[l], conv_b[l], lru_w_a[l], lru_b_a[l], lru_w_x[l],
            lru_b_x[l], lru_lambda[l], w_rnn_out[l], w_attn_out[l], w_out[l])
        conv_p.append(cp)
        conv_s.append(cs)
        h_p.append(hp)
        h_s.append(hs)
        for g in range(N_GROUPS):
            kvp[g].append(kv_p[g])
            kvs[g].append(kv_s[g])
    return (yp, ys, jnp.stack(conv_p), jnp.stack(conv_s), jnp.stack(h_p), jnp.stack(h_s),
            jnp.stack(kvp[0]), jnp.stack(kvs[0]), jnp.stack(kvp[1]), jnp.stack(kvs[1]),
            jnp.stack(kvp[2]), jnp.stack(kvs[2]))
```
